```python
import math
import jax, jax.numpy as jnp
from jax import lax
import numpy as np

D_MODEL = 4096
BATCH = 2
SEQ = 4096
DEPTH = 1

HEAD_DIM = 128
N_HEADS_MLA = D_MODEL // (2 * HEAD_DIM)
N_HEADS_DIL = D_MODEL // (2 * HEAD_DIM)
DIL_WIDTH = N_HEADS_DIL * HEAD_DIM
Q_LORA_RANK = 1536
KV_LORA_RANK = 512
QK_NOPE_DIM = 128
QK_ROPE_DIM = 64
V_HEAD_DIM = 128
ROPE_THETA = 10000.0
DIL_PATTERNS = ((128, 1), (512, 4), (2048, 16))
BLOCK = 128
REL_BUCKETS = 32
REL_MAX_EXACT = REL_BUCKETS // 2
REL_MAX_DISTANCE = 2048
N_EXPERTS = 32
TOP_K = 4
EXPERT_FF = 1536
SWIGLU_LIMIT = 7.0
SWIGLU_ALPHA = 1.702
LN_EPS = 1e-5
RMS_EPS = 1e-6
DEEPNORM_ALPHA = (2.0 * DEPTH) ** 0.25
DEEPNORM_BETA = (8.0 * DEPTH) ** -0.25
IN_SPLITS = (Q_LORA_RANK, KV_LORA_RANK, QK_ROPE_DIM, DIL_WIDTH, DIL_WIDTH, DIL_WIDTH)
IN_COLS = sum(IN_SPLITS)

kernel_name = "hybrid_mla_dilated_moe_deepnorm_adaln"


def _layer_norm(x, g, b):
    xf = x.astype(jnp.float32)
    mu = jnp.mean(xf, axis=-1, keepdims=True)
    var = jnp.mean(jnp.square(xf - mu), axis=-1, keepdims=True)
    y = (xf - mu) * lax.rsqrt(var + LN_EPS)
    return (y * g.astype(jnp.float32) + b.astype(jnp.float32)).astype(x.dtype)


def _rms_norm(x, g):
    xf = x.astype(jnp.float32)
    y = xf * lax.rsqrt(jnp.mean(jnp.square(xf), axis=-1, keepdims=True) + RMS_EPS)
    return (y * g.astype(jnp.float32)).astype(x.dtype)


def _rope(x, cos, sin):
    xf = x.astype(jnp.float32)
    x1, x2 = jnp.split(xf, 2, axis=-1)
    return jnp.concatenate([x1 * cos - x2 * sin, x1 * sin + x2 * cos], axis=-1).astype(x.dtype)


def _t5_bucket(dist):
    is_small = dist < REL_MAX_EXACT
    nf = jnp.maximum(dist, REL_MAX_EXACT).astype(jnp.float32)
    large = REL_MAX_EXACT + (jnp.log(nf / REL_MAX_EXACT) / math.log(REL_MAX_DISTANCE / REL_MAX_EXACT)
                             * (REL_BUCKETS - REL_MAX_EXACT)).astype(jnp.int32)
    large = jnp.minimum(large, REL_BUCKETS - 1)
    return jnp.where(is_small, dist, large)


def _causal_block_attention(q, k, v):
    B, S, H, E = q.shape
    nb = S // BLOCK
    scale = E ** -0.5
    qb = q.reshape(B, nb, BLOCK, H, E).swapaxes(0, 1)
    kpos = jnp.arange(S)

    def one_block(args):
        q_blk, i = args
        qpos = i * BLOCK + jnp.arange(BLOCK)
        s = jnp.einsum('bqhe,bkhe->bhqk', q_blk, k).astype(jnp.float32) * scale
        s = jnp.where(kpos[None, :] <= qpos[:, None], s, -jnp.inf)
        p = jax.nn.softmax(s, axis=-1).astype(v.dtype)
        return jnp.einsum('bhqk,bkhd->bqhd', p, v)

    o = lax.map(one_block, (qb, jnp.arange(nb)))
    return o.swapaxes(0, 1).reshape(B, S, H, v.shape[-1])


def _dilated_branch(q, k, v, rel_bias, window, dilation):
    B, S, H, E = q.shape
    d = dilation
    w = window // d
    L = -(-S // d)
    Lp = -(-L // BLOCK) * BLOCK
    Sp = Lp * d
    nb = Lp // BLOCK
    pad = ((0, 0), (0, Sp - S), (0, 0), (0, 0))

    def to_blocks(t):
        return jnp.pad(t, pad).reshape(B, nb, BLOCK, d, H, t.shape[-1])

    def with_prev(t):
        prev = jnp.pad(t, ((0, 0), (1, 0), (0, 0), (0, 0), (0, 0), (0, 0)))[:, :nb]
        return jnp.concatenate([prev, t], axis=2)

    qb = to_blocks(q)
    kk = with_prev(to_blocks(k))
    vv = with_prev(to_blocks(v))

    qi = jnp.arange(BLOCK)[:, None]
    ki = jnp.arange(2 * BLOCK)[None, :]
    rel = qi + BLOCK - ki
    blk = jnp.arange(nb)[:, None, None]
    valid = (rel >= 0) & (rel <= w) & (blk * BLOCK + qi - rel >= 0)
    bucket = _t5_bucket(jnp.clip(rel, 0) * d)
    bias = jnp.transpose(rel_bias[bucket].astype(jnp.float32), (2, 0, 1))

    s = jnp.einsum('bnqrhe,bnkrhe->bnrhqk', qb, kk).astype(jnp.float32) * (E ** -0.5) + bias
    s = jnp.where(valid[None, :, None, None], s, -jnp.inf)
    m = jnp.max(s, axis=-1, keepdims=True)
    p = jnp.exp(s - m)
    l = jnp.sum(p, axis=-1)
    o = jnp.einsum('bnrhqk,bnkrhe->bnqrhe', p.astype(v.dtype), vv).astype(jnp.float32)
    l_t = jnp.transpose(l, (0, 1, 4, 2, 3))
    o = o / l_t[..., None]
    lse = jnp.transpose(m[..., 0], (0, 1, 4, 2, 3)) + jnp.log(l_t)
    o = o.reshape(B, Sp, H, E)[:, :S]
    lse = lse.reshape(B, Sp, H)[:, :S]
    return o, lse


def _dilated_mixture(q, k, v, rel_bias):
    outs, lses = [], []
    for window, dilation in DIL_PATTERNS:
        o, lse = _dilated_branch(q, k, v, rel_bias, window, dilation)
        outs.append(o)
        lses.append(lse)
    wts = jax.nn.softmax(jnp.stack(lses, axis=0), axis=0)
    o = jnp.sum(wts[..., None] * jnp.stack(outs, axis=0), axis=0)
    return o.astype(q.dtype)


def _token_mixer(u, w_in, g_q, g_kv, w_uq, w_ukv, rel_bias, w_o, cos, sin):
    B, S, _ = u.shape
    h = u @ w_in
    idx = list(np.cumsum(IN_SPLITS)[:-1])
    c_q, c_kv, k_r, q_d, k_d, v_d = jnp.split(h, idx, axis=-1)
    q = (_rms_norm(c_q, g_q) @ w_uq).reshape(B, S, N_HEADS_MLA, QK_NOPE_DIM + QK_ROPE_DIM)
    q_nope, q_rope = q[..., :QK_NOPE_DIM], q[..., QK_NOPE_DIM:]
    q_rope = _rope(q_rope, cos[:, :, None], sin[:, :, None])
    k_r = _rope(k_r, cos, sin)
    kv = (_rms_norm(c_kv, g_kv) @ w_ukv).reshape(B, S, N_HEADS_MLA, QK_NOPE_DIM + V_HEAD_DIM)
    k_nope, v_a = kv[..., :QK_NOPE_DIM], kv[..., QK_NOPE_DIM:]
    q_a = jnp.concatenate([q_nope, q_rope], axis=-1)
    k_a = jnp.concatenate([k_nope, jnp.broadcast_to(k_r[:, :, None], (B, S, N_HEADS_MLA, QK_ROPE_DIM))], axis=-1)
    o_a = _causal_block_attention(q_a, k_a, v_a)
    shp = (B, S, N_HEADS_DIL, HEAD_DIM)
    o_b = _dilated_mixture(q_d.reshape(shp), k_d.reshape(shp), v_d.reshape(shp), rel_bias)
    o = jnp.concatenate([o_a.reshape(B, S, -1), o_b.reshape(B, S, -1)], axis=-1)
    return o @ w_o


def _moe(u, w_router, b_router, w_gate, b_gate, w_up, b_up, w_down, b_down):
    shape = u.shape
    t = u.reshape(-1, shape[-1])
    logits = (t @ w_router + b_router).astype(jnp.float32)
    vals, idx = lax.top_k(logits, TOP_K)
    wts = jax.nn.softmax(vals, axis=-1)
    gates = jnp.einsum('tk,tke->te', wts, jax.nn.one_hot(idx, N_EXPERTS, dtype=jnp.float32)).astype(u.dtype)
    out = jnp.zeros_like(t)
    for e in range(N_EXPERTS):
        glu = jnp.minimum(t @ w_gate[e] + b_gate[e], SWIGLU_LIMIT)
        lin = jnp.clip(t @ w_up[e] + b_up[e], -SWIGLU_LIMIT, SWIGLU_LIMIT)
        act = glu * jax.nn.sigmoid(SWIGLU_ALPHA * glu) * (lin + 1.0)
        out = out + gates[:, e:e + 1] * (act @ w_down[e] + b_down[e])
    return out.reshape(shape)


def setup_inputs(seed: int = 0) -> dict:
    key = jax.random.key(seed)
    ks = jax.random.split(key, 24)
    f32 = jnp.float32
    nrm = lambda k, s, sc: jax.random.normal(k, s, f32) * sc
    D, E, F = D_MODEL, N_EXPERTS, EXPERT_FF
    start = jax.random.randint(ks[2], (BATCH, 1), 0, 1024)
    positions = (start + jnp.arange(SEQ)[None, :]).astype(jnp.int32)
    return {
        "x": nrm(ks[0], (BATCH, SEQ, D), 1.0),
        "c": nrm(ks[1], (BATCH, D), 1.0),
        "positions": positions,
        "w_ada": nrm(ks[3], (DEPTH, D, 6 * D), 0.5 * D ** -0.5),
        "b_ada": nrm(ks[4], (DEPTH, 6 * D), 0.02),
        "w_in": nrm(ks[5], (DEPTH, D, IN_COLS), D ** -0.5),
        "g_q": 1.0 + nrm(ks[6], (DEPTH, Q_LORA_RANK), 0.02),
        "g_kv": 1.0 + nrm(ks[7], (DEPTH, KV_LORA_RANK), 0.02),
        "w_uq": nrm(ks[8], (DEPTH, Q_LORA_RANK, N_HEADS_MLA * (QK_NOPE_DIM + QK_ROPE_DIM)), Q_LORA_RANK ** -0.5),
        "w_ukv": nrm(ks[9], (DEPTH, KV_LORA_RANK, N_HEADS_MLA * (QK_NOPE_DIM + V_HEAD_DIM)), KV_LORA_RANK ** -0.5),
        "rel_bias": nrm(ks[10], (REL_BUCKETS, N_HEADS_DIL), 0.3),
        "w_o": nrm(ks[11], (DEPTH, D, D), D ** -0.5 * DEEPNORM_BETA),
        "ln1_g": 1.0 + nrm(ks[12], (DEPTH, D), 0.02),
        "ln1_b": nrm(ks[13], (DEPTH, D), 0.02),
        "w_router": nrm(ks[14], (DEPTH, D, E), D ** -0.5),
        "b_router": nrm(ks[15], (DEPTH, E), 0.01),
        "w_gate": nrm(ks[16], (DEPTH, E, D, F), D ** -0.5),
        "b_gate": nrm(ks[17], (DEPTH, E, F), 0.02),
        "w_up": nrm(ks[18], (DEPTH, E, D, F), D ** -0.5),
        "b_up": nrm(ks[19], (DEPTH, E, F), 0.02),
        "w_down": nrm(ks[20], (DEPTH, E, F, D), F ** -0.5 * DEEPNORM_BETA),
        "b_down": nrm(ks[21], (DEPTH, E, D), 0.02),
        "ln2_g": 1.0 + nrm(ks[22], (DEPTH, D), 0.02),
        "ln2_b": nrm(ks[23], (DEPTH, D), 0.02),
    }


def reference(x, c, positions, w_ada, b_ada, w_in, g_q, g_kv, w_uq, w_ukv, rel_bias, w_o,
              ln1_g, ln1_b, w_router, b_router, w_gate, b_gate, w_up, b_up, w_down, b_down,
              ln2_g, ln2_b):
    inv_freq = 1.0 / (ROPE_THETA ** (jnp.arange(0, QK_ROPE_DIM, 2, dtype=jnp.float32) / QK_ROPE_DIM))
    ang = positions.astype(jnp.float32)[..., None] * inv_freq
    cos, sin = jnp.cos(ang), jnp.sin(ang)
    cond = jax.nn.silu(c)
    for l in range(DEPTH):
        mod = cond @ w_ada[l] + b_ada[l]
        sh1, sc1, gt1, sh2, sc2, gt2 = [m[:, None, :] for m in jnp.split(mod, 6, axis=-1)]
        u = x * (1.0 + sc1) + sh1
        mix = _token_mixer(u, w_in[l], g_q[l], g_kv[l], w_uq[l], w_ukv[l], rel_bias, w_o[l], cos, sin)
        x = _layer_norm(DEEPNORM_ALPHA * x + gt1 * mix, ln1_g[l], ln1_b[l])
        u = x * (1.0 + sc2) + sh2
        ffn = _moe(u, w_router[l], b_router[l], w_gate[l], b_gate[l], w_up[l], b_up[l], w_down[l], b_down[l])
        x = _layer_norm(DEEPNORM_ALPHA * x + gt2 * ffn, ln2_g[l], ln2_b[l])
    return x
```

```python
import functools
import math

import jax
import jax.numpy as jnp
from jax import lax
from jax.experimental import pallas as pl
from jax.experimental.pallas import tpu as pltpu

F32 = jnp.float32
BF16 = jnp.bfloat16

HEAD_DIM = 128
Q_LORA_RANK = 1536
KV_LORA_RANK = 512
QK_NOPE_DIM = 128
QK_ROPE_DIM = 64
V_HEAD_DIM = 128
ROPE_THETA = 10000.0
DIL_PATTERNS = ((128, 1), (512, 4), (2048, 16))
BLOCK = 128
REL_BUCKETS = 32
REL_MAX_EXACT = REL_BUCKETS // 2
REL_MAX_DISTANCE = 2048
N_EXPERTS = 32
TOP_K = 4
EXPERT_FF = 1536
SWIGLU_LIMIT = 7.0
SWIGLU_ALPHA = 1.702
LN_EPS = 1e-5
RMS_EPS = 1e-6
NEG_BIG = -1e30

LANES = 128
QK_PAD = 256
VMEM_LIMIT = 56 * 1024 * 1024
MOE_ROWS = 1280
MOE_SUB = 256


def _cparams(sem):
    return pltpu.CompilerParams(dimension_semantics=sem, vmem_limit_bytes=VMEM_LIMIT)


def _ada_kernel(c_ref, w_ref, b_ref, o_ref):
    c = c_ref[...]
    cond = (c * jax.nn.sigmoid(c)).astype(BF16)
    o_ref[...] = jnp.dot(cond, w_ref[...].astype(BF16), preferred_element_type=F32) + b_ref[...]


def _ada(c_pad, w_ada, b_ada3, l, tn=512):
    rows, d = c_pad.shape
    n = w_ada.shape[-1]
    return pl.pallas_call(
        _ada_kernel,
        grid=(n // tn,),
        in_specs=[
            pl.BlockSpec((rows, d), lambda j: (0, 0)),
            pl.BlockSpec((None, d, tn), lambda j: (l, 0, j)),
            pl.BlockSpec((None, 1, tn), lambda j: (l, 0, j)),
        ],
        out_specs=pl.BlockSpec((rows, tn), lambda j: (0, j)),
        out_shape=jax.ShapeDtypeStruct((rows, n), F32),
        compiler_params=_cparams(("arbitrary",)),
        name="ada_mod",
    )(c_pad, w_ada, b_ada3)


def _modulate_kernel(x_ref, mod_ref, o_ref, *, shift_row, scale_row):
    sh = mod_ref[shift_row:shift_row + 1, :]
    sc = mod_ref[scale_row:scale_row + 1, :]
    o_ref[...] = (x_ref[...] * (1.0 + sc) + sh).astype(BF16)


def _modulate(x2, mod3, seq, shift_row, scale_row, tm=512):
    t, d = x2.shape
    per_b = seq // tm
    return pl.pallas_call(
        functools.partial(_modulate_kernel, shift_row=shift_row, scale_row=scale_row),
        grid=(t // tm,),
        in_specs=[
            pl.BlockSpec((tm, d), lambda i: (i, 0)),
            pl.BlockSpec((None, 6, d), lambda i: (i // per_b, 0, 0)),
        ],
        out_specs=pl.BlockSpec((tm, d), lambda i: (i, 0)),
        out_shape=jax.ShapeDtypeStruct((t, d), BF16),
        compiler_params=_cparams(("arbitrary",)),
        name="modulate",
    )(x2, mod3)


def _rope_block(x, cos_t, sin_t):
    lane = lax.broadcasted_iota(jnp.int32, x.shape, 1)
    half = QK_ROPE_DIM // 2
    swapped = jnp.where(lane < half, pltpu.roll(x, LANES - half, 1), pltpu.roll(x, half, 1))
    return x * cos_t + swapped * sin_t


def _proj_plain_kernel(a_ref, w_ref, o_ref):
    o_ref[...] = jnp.dot(a_ref[...], w_ref[...], preferred_element_type=F32).astype(o_ref.dtype)


def _proj_heads_kernel(a_ref, w_ref, o_ref, *, heads_per_tile):
    res = jnp.dot(a_ref[...], w_ref[...], preferred_element_type=F32)
    for hh in range(heads_per_tile):
        o_ref[hh] = res[:, hh * HEAD_DIM:(hh + 1) * HEAD_DIM].astype(o_ref.dtype)


def _proj_rope_kernel(a_ref, w_ref, cos_ref, sin_ref, o_ref):
    res = jnp.dot(a_ref[...], w_ref[...], preferred_element_type=F32)
    o_ref[...] = _rope_block(res, cos_ref[...], sin_ref[...]).astype(o_ref.dtype)


def _in_proj_mla(u, w_p, n_cols, tm=512, tn=512):
    t, d = u.shape
    return pl.pallas_call(
        _proj_plain_kernel,
        grid=(n_cols // tn, t // tm),
        in_specs=[
            pl.BlockSpec((tm, d), lambda j, i: (i, 0)),
            pl.BlockSpec((d, tn), lambda j, i: (0, j)),
        ],
        out_specs=pl.BlockSpec((tm, tn), lambda j, i: (i, j)),
        out_shape=jax.ShapeDtypeStruct((t, n_cols), BF16),
        compiler_params=_cparams(("arbitrary", "arbitrary")),
        name="in_proj_mla",
    )(u, w_p)


def _in_proj_dil(u, w_p, col0, batch, seq, n_heads, tm=512, tn=512):
    t, d = u.shape
    hpt = tn // HEAD_DIM
    tiles_per_mat = n_heads // hpt
    per_b = seq // tm
    off = col0 // tn
    return pl.pallas_call(
        functools.partial(_proj_heads_kernel, heads_per_tile=hpt),
        grid=(3 * tiles_per_mat, t // tm),
        in_specs=[
            pl.BlockSpec((tm, d), lambda j, i: (i, 0)),
            pl.BlockSpec((d, tn), lambda j, i: (0, j + off)),
        ],
        out_specs=pl.BlockSpec(
            (None, None, hpt, tm, HEAD_DIM),
            lambda j, i: (j // tiles_per_mat, i // per_b, j % tiles_per_mat, i % per_b, 0)),
        out_shape=jax.ShapeDtypeStruct((3, batch, n_heads, seq, HEAD_DIM), BF16),
        compiler_params=_cparams(("arbitrary", "arbitrary")),
        name="in_proj_dil",
    )(u, w_p)


def _in_proj_krope(u, w_p, col0, cos_t, sin_t, tm=1024):
    t, d = u.shape
    off = col0 // LANES
    return pl.pallas_call(
        _proj_rope_kernel,
        grid=(t // tm,),
        in_specs=[
            pl.BlockSpec((tm, d), lambda i: (i, 0)),
            pl.BlockSpec((d, LANES), lambda i: (0, off)),
            pl.BlockSpec((tm, LANES), lambda i: (i, 0)),
            pl.BlockSpec((tm, LANES), lambda i: (i, 0)),
        ],
        out_specs=pl.BlockSpec((tm, LANES), lambda i: (i, 0)),
        out_shape=jax.ShapeDtypeStruct((t, LANES), BF16),
        compiler_params=_cparams(("arbitrary",)),
        name="in_proj_krope",
    )(u, w_p, cos_t, sin_t)


def _rms_bf16(h_ref, g_ref):
    hf = h_ref[...].astype(F32)
    y = hf * lax.rsqrt(jnp.mean(hf * hf, axis=-1, keepdims=True) + RMS_EPS)
    return (y * g_ref[...]).astype(BF16)


def _q_up_kernel(h_ref, g_ref, w_ref, cos_ref, sin_ref, o_ref, *, n_heads, scale):
    yb = _rms_bf16(h_ref, g_ref)
    cos_t = cos_ref[...]
    sin_t = sin_ref[...]
    group = 4
    for h0 in range(0, n_heads, group):
        res = jnp.dot(yb, w_ref[:, h0 * QK_PAD:(h0 + group) * QK_PAD], preferred_element_type=F32)
        for hh in range(group):
            c0 = hh * QK_PAD
            nope = res[:, c0:c0 + QK_NOPE_DIM]
            rp = _rope_block(res[:, c0 + QK_NOPE_DIM:c0 + QK_PAD], cos_t, sin_t)
            o0 = (h0 + hh) * QK_PAD
            o_ref[:, o0:o0 + QK_NOPE_DIM] = (nope * scale).astype(BF16)
            o_ref[:, o0 + QK_NOPE_DIM:o0 + QK_PAD] = (rp * scale).astype(BF16)


def _q_up(hm, g3, l, w_uq_p, cos_t, sin_t, n_heads, tm=256):
    t = hm.shape[0]
    scale = float(QK_NOPE_DIM + QK_ROPE_DIM) ** -0.5
    return pl.pallas_call(
        functools.partial(_q_up_kernel, n_heads=n_heads, scale=scale),
        grid=(t // tm,),
        in_specs=[
            pl.BlockSpec((tm, Q_LORA_RANK), lambda i: (i, 0)),
            pl.BlockSpec((None, 1, Q_LORA_RANK), lambda i: (l, 0, 0)),
            pl.BlockSpec((Q_LORA_RANK, n_heads * QK_PAD), lambda i: (0, 0)),
            pl.BlockSpec((tm, LANES), lambda i: (i, 0)),
            pl.BlockSpec((tm, LANES), lambda i: (i, 0)),
        ],
        out_specs=pl.BlockSpec((tm, n_heads * QK_PAD), lambda i: (i, 0)),
        out_shape=jax.ShapeDtypeStruct((t, n_heads * QK_PAD), BF16),
        compiler_params=_cparams(("arbitrary",)),
        name="mla_q_up",
    )(hm, g3, w_uq_p, cos_t, sin_t)


def _kv_up_kernel(h_ref, g_ref, wk_ref, wv_ref, kr_ref, k_ref, v_ref, *, n_heads):
    yb = _rms_bf16(h_ref, g_ref)
    kr = kr_ref[...]
    group = 4
    for h0 in range(0, n_heads, group):
        res = jnp.dot(yb, wk_ref[:, h0 * QK_NOPE_DIM:(h0 + group) * QK_NOPE_DIM],
                      preferred_element_type=F32)
        for hh in range(group):
            o0 = (h0 + hh) * QK_PAD
            k_ref[:, o0:o0 + QK_NOPE_DIM] = res[:, hh * QK_NOPE_DIM:(hh + 1) * QK_NOPE_DIM].astype(BF16)
            k_ref[:, o0 + QK_NOPE_DIM:o0 + QK_PAD] = kr
    v_ref[...] = jnp.dot(yb, wv_ref[...], preferred_element_type=F32).astype(BF16)


def _kv_up(hm, g3, l, w_uk, w_uv, krr, n_heads, tm=256):
    t = hm.shape[0]
    col_blk = Q_LORA_RANK // KV_LORA_RANK
    return pl.pallas_call(
        functools.partial(_kv_up_kernel, n_heads=n_heads),
        grid=(t // tm,),
        in_specs=[
            pl.BlockSpec((tm, KV_LORA_RANK), lambda i: (i, col_blk)),
            pl.BlockSpec((None, 1, KV_LORA_RANK), lambda i: (l, 0, 0)),
            pl.BlockSpec((KV_LORA_RANK, n_heads * QK_NOPE_DIM), lambda i: (0, 0)),
            pl.BlockSpec((KV_LORA_RANK, n_heads * V_HEAD_DIM), lambda i: (0, 0)),
            pl.BlockSpec((tm, LANES), lambda i: (i, 0)),
        ],
        out_specs=[
            pl.BlockSpec((tm, n_heads * QK_PAD), lambda i: (i, 0)),
            pl.BlockSpec((tm, n_heads * V_HEAD_DIM), lambda i: (i, 0)),
        ],
        out_shape=[
            jax.ShapeDtypeStruct((t, n_heads * QK_PAD), BF16),
            jax.ShapeDtypeStruct((t, n_heads * V_HEAD_DIM), BF16),
        ],
        compiler_params=_cparams(("arbitrary",)),
        name="mla_kv_up",
    )(hm, g3, w_uk, w_uv, krr)


def _mla_attn_kernel(q_ref, k_ref, v_ref, o_ref, *, seq, tq):
    nq = seq // tq
    row = lax.broadcasted_iota(jnp.int32, (tq, tq), 0)
    col = lax.broadcasted_iota(jnp.int32, (tq, tq), 1)
    causal = col <= row

    def kv_step(q, j, carry, masked):
        m, l, acc = carry
        off = pl.multiple_of(j * tq, tq)
        k = k_ref[pl.ds(off, tq), :]
        v = v_ref[pl.ds(off, tq), :]
        s = lax.dot_general(q, k, (((1,), (1,)), ((), ())), preferred_element_type=F32)
        if masked:
            s = jnp.where(causal, s, NEG_BIG)
        m_new = jnp.maximum(m, jnp.max(s, axis=-1, keepdims=True))
        p = jnp.exp(s - m_new)
        alpha = jnp.exp(m - m_new)
        l_new = alpha * l + jnp.sum(p, axis=-1, keepdims=True)
        acc_new = alpha * acc + jnp.dot(p.astype(BF16), v, preferred_element_type=F32)
        return m_new, l_new, acc_new

    def q_loop(i, _):
        qoff = pl.multiple_of(i * tq, tq)
        q = q_ref[pl.ds(qoff, tq), :]
        init = (jnp.full((tq, 1), NEG_BIG, F32), jnp.zeros((tq, 1), F32),
                jnp.zeros((tq, V_HEAD_DIM), F32))
        carry = lax.fori_loop(0, i, lambda j, c: kv_step(q, j, c, False), init)
        m, l, acc = kv_step(q, i, carry, True)
        o_ref[pl.ds(qoff, tq), :] = (acc / l).astype(o_ref.dtype)
        return 0

    lax.fori_loop(0, nq, q_loop, 0)


def _mla_attn(q, k, v, batch, seq, n_heads, tq=512):
    t = q.shape[0]
    return pl.pallas_call(
        functools.partial(_mla_attn_kernel, seq=seq, tq=tq),
        grid=(batch, n_heads),
        in_specs=[
            pl.BlockSpec((seq, QK_PAD), lambda b, h: (b, h)),
            pl.BlockSpec((seq, QK_PAD), lambda b, h: (b, h)),
            pl.BlockSpec((seq, V_HEAD_DIM), lambda b, h: (b, h)),
        ],
        out_specs=pl.BlockSpec((seq, V_HEAD_DIM), lambda b, h: (b, h)),
        out_shape=jax.ShapeDtypeStruct((t, n_heads * V_HEAD_DIM), BF16),
        compiler_params=_cparams(("arbitrary", "arbitrary")),
        name="mla_attn",
    )(q, k, v)


def _dil_attn_kernel(v1_ref, v4_ref, v16_ref, bias_ref, o_ref, m_s, l_s, acc_s, *, seq):
    scale = float(HEAD_DIM) ** -0.5
    views = (v1_ref, v4_ref, v16_ref)

    def block(ref, bi, d, r, i, first):
        lanes = slice(r * HEAD_DIM, (r + 1) * HEAD_DIM)
        qoff = pl.multiple_of(i * BLOCK, BLOCK)
        q = ref[0, pl.ds(qoff, BLOCK), lanes]
        if first:
            kk = ref[1, pl.ds(0, BLOCK), lanes]
            vv = ref[2, pl.ds(0, BLOCK), lanes]
            bias = bias_ref[bi, :, BLOCK:2 * BLOCK]
        else:
            koff = pl.multiple_of(i * BLOCK - BLOCK, BLOCK)
            kk = ref[1, pl.ds(koff, 2 * BLOCK), lanes]
            vv = ref[2, pl.ds(koff, 2 * BLOCK), lanes]
            bias = bias_ref[bi]
        s = lax.dot_general(q, kk, (((1,), (1,)), ((), ())), preferred_element_type=F32)
        s = s * scale + bias
        m_b = jnp.max(s, axis=-1, keepdims=True)
        p = jnp.exp(s - m_b)
        l_b = jnp.sum(p, axis=-1, keepdims=True)
        a_b = jnp.dot(p.astype(BF16), vv, preferred_element_type=F32)
        m_b = jnp.broadcast_to(m_b, (BLOCK, HEAD_DIM))
        l_b = jnp.broadcast_to(l_b, (BLOCK, HEAD_DIM))
        if d == 1:
            rows = pl.ds(qoff, BLOCK)
            m_s[rows, :] = m_b
            l_s[rows, :] = l_b
            acc_s[rows, :] = a_b
        else:
            rows = pl.ds(i * (BLOCK * d) + r, BLOCK, stride=d)
            m_o = m_s[rows, :]
            m_n = jnp.maximum(m_o, m_b)
            e_o = jnp.exp(m_o - m_n)
            e_b = jnp.exp(m_b - m_n)
            l_s[rows, :] = e_o * l_s[rows, :] + e_b * l_b
            acc_s[rows, :] = e_o * acc_s[rows, :] + e_b * a_b
            m_s[rows, :] = m_n

    for bi, (_, d) in enumerate(DIL_PATTERNS):
        ref = views[bi]
        nblk = seq // d // BLOCK
        for r in range(d):
            block(ref, bi, d, r, 0, True)

            def body(i, _, ref=ref, bi=bi, d=d, r=r):
                block(ref, bi, d, r, i, False)
                return 0

            lax.fori_loop(1, nblk, body, 0)

    o_ref[...] = (acc_s[...] / l_s[...]).astype(o_ref.dtype)


def _dil_attn(qkv, bias_tab, batch, seq, n_heads):
    t = batch * seq
    views = []
    specs = []
    for _, d in DIL_PATTERNS:
        views.append(qkv.reshape(3, batch, n_heads, seq // d, d * HEAD_DIM))
        specs.append(pl.BlockSpec((3, None, None, seq // d, d * HEAD_DIM),
                                  lambda b, h: (0, b, h, 0, 0)))
    specs.append(pl.BlockSpec((len(DIL_PATTERNS), None, BLOCK, 2 * BLOCK), lambda b, h: (0, h, 0, 0)))
    return pl.pallas_call(
        functools.partial(_dil_attn_kernel, seq=seq),
        grid=(batch, n_heads),
        in_specs=specs,
        out_specs=pl.BlockSpec((seq, HEAD_DIM), lambda b, h: (b, h)),
        out_shape=jax.ShapeDtypeStruct((t, n_heads * HEAD_DIM), BF16),
        scratch_shapes=[pltpu.VMEM((seq, HEAD_DIM), F32)] * 3,
        compiler_params=_cparams(("arbitrary", "arbitrary")),
        name="dil_attn",
    )(*views, bias_tab)


def _out_proj_kernel(a1_ref, a2_ref, w_ref, o_ref, wb_ref):
    @pl.when(pl.program_id(1) == 0)
    def _():
        wb_ref[...] = w_ref[...].astype(BF16)

    k1 = a1_ref.shape[1]
    acc = jnp.dot(a1_ref[...], wb_ref[:k1, :], preferred_element_type=F32)
    acc = acc + jnp.dot(a2_ref[...], wb_ref[k1:, :], preferred_element_type=F32)
    o_ref[...] = acc.astype(o_ref.dtype)


def _out_proj(o_a, o_b, w_o, l, tm=512, tn=512):
    t, k1 = o_a.shape
    k2 = o_b.shape[1]
    n = w_o.shape[-1]
    return pl.pallas_call(
        _out_proj_kernel,
        grid=(n // tn, t // tm),
        in_specs=[
            pl.BlockSpec((tm, k1), lambda j, i: (i, 0)),
            pl.BlockSpec((tm, k2), lambda j, i: (i, 0)),
            pl.BlockSpec((None, k1 + k2, tn), lambda j, i: (l, 0, j)),
        ],
        out_specs=pl.BlockSpec((tm, tn), lambda j, i: (i, j)),
        out_shape=jax.ShapeDtypeStruct((t, n), BF16),
        scratch_shapes=[pltpu.VMEM((k1 + k2, tn), BF16)],
        compiler_params=_cparams(("arbitrary", "arbitrary")),
        name="out_proj",
    )(o_a, o_b, w_o)


def _layer_norm_rows(z, g, b):
    mu = jnp.mean(z, axis=-1, keepdims=True)
    zc = z - mu
    var = jnp.mean(zc * zc, axis=-1, keepdims=True)
    return zc * lax.rsqrt(var + LN_EPS) * g + b


def _ln1_router_kernel(x_ref, mix_ref, mod_ref, g_ref, b_ref, wr_ref, br_ref,
                       x1_ref, u2_ref, ids_ref, gates_ref, *, alpha):
    gt1 = mod_ref[2:3, :]
    sh2 = mod_ref[3:4, :]
    sc2 = mod_ref[4:5, :]
    z = alpha * x_ref[...] + gt1 * mix_ref[...].astype(F32)
    x1 = _layer_norm_rows(z, g_ref[...], b_ref[...])
    x1_ref[...] = x1
    u2 = (x1 * (1.0 + sc2) + sh2).astype(BF16)
    u2_ref[...] = u2
    logits = jnp.dot(u2, wr_ref[...].astype(BF16), preferred_element_type=F32) + br_ref[...]
    lane = lax.broadcasted_iota(jnp.int32, logits.shape, 1)
    lane_f = lane.astype(F32)
    vals = []
    ids = []
    for _ in range(TOP_K):
        mk = jnp.max(logits, axis=-1, keepdims=True)
        idx_f = jnp.min(jnp.where(logits == mk, lane_f, float(LANES)), axis=-1, keepdims=True)
        vals.append(mk)
        ids.append(idx_f.astype(jnp.int32))
        logits = jnp.where(lane_f == idx_f, -jnp.inf, logits)
    exps = [jnp.exp(v - vals[0]) for v in vals]
    denom = exps[0]
    for e in exps[1:]:
        denom = denom + e
    ids_out = jnp.zeros(lane.shape, jnp.int32)
    gates_out = jnp.zeros(lane.shape, F32)
    for k in range(TOP_K):
        ids_out = jnp.where(lane == k, ids[k], ids_out)
        gates_out = jnp.where(lane == k, exps[k] / denom, gates_out)
    ids_ref[...] = ids_out
    gates_ref[...] = gates_out


def _ln1_router(x2, mix, mod3, g3, b3, wr_p, br_p, l, seq, alpha, tm=256):
    t, d = x2.shape
    per_b = seq // tm
    return pl.pallas_call(
        functools.partial(_ln1_router_kernel, alpha=alpha),
        grid=(t // tm,),
        in_specs=[
            pl.BlockSpec((tm, d), lambda i: (i, 0)),
            pl.BlockSpec((tm, d), lambda i: (i, 0)),
            pl.BlockSpec((None, 6, d), lambda i: (i // per_b, 0, 0)),
            pl.BlockSpec((None, 1, d), lambda i: (l, 0, 0)),
            pl.BlockSpec((None, 1, d), lambda i: (l, 0, 0)),
            pl.BlockSpec((d, LANES), lambda i: (0, 0)),
            pl.BlockSpec((1, LANES), lambda i: (0, 0)),
        ],
        out_specs=[
            pl.BlockSpec((tm, d), lambda i: (i, 0)),
            pl.BlockSpec((tm, d), lambda i: (i, 0)),
            pl.BlockSpec((tm, LANES), lambda i: (i, 0)),
            pl.BlockSpec((tm, LANES), lambda i: (i, 0)),
        ],
        out_shape=[
            jax.ShapeDtypeStruct((t, d), F32),
            jax.ShapeDtypeStruct((t, d), BF16),
            jax.ShapeDtypeStruct((t, LANES), jnp.int32),
            jax.ShapeDtypeStruct((t, LANES), F32),
        ],
        compiler_params=_cparams(("arbitrary",)),
        name="ln1_router",
    )(x2, mix, mod3, g3, b3, wr_p, br_p)


def _moe_up_kernel(item_e, item_blk, item_sub, x_ref, wg_ref, wu_ref, bg_ref, bu_ref,
                   h_ref, wg_s, wu_s):
    w = pl.program_id(0)
    nsub = item_sub[w]

    @pl.when(nsub > 0)
    def _():
        wg_s[...] = wg_ref[...].astype(BF16)
        wu_s[...] = wu_ref[...].astype(BF16)
        bg = bg_ref[...]
        bu = bu_ref[...]

        def body(j, _):
            off = pl.multiple_of(j * MOE_SUB, MOE_SUB)
            xt = x_ref[pl.ds(off, MOE_SUB), :]
            glu = jnp.dot(xt, wg_s[...], preferred_element_type=F32) + bg
            lin = jnp.dot(xt, wu_s[...], preferred_element_type=F32) + bu
            glu = jnp.minimum(glu, SWIGLU_LIMIT)
            lin = jnp.clip(lin, -SWIGLU_LIMIT, SWIGLU_LIMIT)
            act = glu * jax.nn.sigmoid(SWIGLU_ALPHA * glu) * (lin + 1.0)
            h_ref[pl.ds(off, MOE_SUB), :] = act.astype(h_ref.dtype)
            return 0

        lax.fori_loop(0, nsub, body, 0)


def _moe_up(xs, w_gate, w_up, b_gate4, b_up4, item_e, item_blk, item_sub, l, tf=256):
    r, d = xs.shape
    n_items = r // MOE_ROWS
    ff = w_gate.shape[-1]
    n_chunks = ff // tf

    def cmap(w, c, isub):
        return jnp.where(isub[w] > 0, c, n_chunks - 1)

    def wmap(w, c, ie, ib, isub):
        return (l, ie[w], 0, cmap(w, c, isub))

    grid_spec = pltpu.PrefetchScalarGridSpec(
        num_scalar_prefetch=3,
        grid=(n_items, n_chunks),
        in_specs=[
            pl.BlockSpec((MOE_ROWS, d), lambda w, c, ie, ib, isub: (ib[w], 0)),
            pl.BlockSpec((None, None, d, tf), wmap),
            pl.BlockSpec((None, None, d, tf), wmap),
            pl.BlockSpec((None, None, 1, tf), wmap),
            pl.BlockSpec((None, None, 1, tf), wmap),
        ],
        out_specs=pl.BlockSpec(
            (MOE_ROWS, tf), lambda w, c, ie, ib, isub: (ib[w], cmap(w, c, isub))),
        scratch_shapes=[pltpu.VMEM((d, tf), BF16), pltpu.VMEM((d, tf), BF16)],
    )
    return pl.pallas_call(
        _moe_up_kernel,
        grid_spec=grid_spec,
        out_shape=jax.ShapeDtypeStruct((r, ff), BF16),
        compiler_params=_cparams(("arbitrary", "arbitrary")),
        name="moe_up",
    )(item_e, item_blk, item_sub, xs, w_gate, w_up, b_gate4, b_up4)


def _moe_down_kernel(item_e, item_blk, item_sub, h_ref, wd_ref, bd_ref, y_ref, wd_s):
    w = pl.program_id(0)
    nsub = item_sub[w]

    @pl.when(nsub > 0)
    def _():
        wd_s[...] = wd_ref[...].astype(BF16)
        bd = bd_ref[...]

        def body(j, _):
            off = pl.multiple_of(j * MOE_SUB, MOE_SUB)
            ht = h_ref[pl.ds(off, MOE_SUB), :]
            y = jnp.dot(ht, wd_s[...], preferred_element_type=F32) + bd
            y_ref[pl.ds(off, MOE_SUB), :] = y.astype(y_ref.dtype)
            return 0

        lax.fori_loop(0, nsub, body, 0)


def _moe_down(hs, w_down, b_down4, item_e, item_blk, item_sub, l, tn=1024):
    r, ff = hs.shape
    n_items = r // MOE_ROWS
    d = w_down.shape[-1]
    tn = min(tn, d)
    n_chunks = d // tn

    def cmap(w, c, isub):
        return jnp.where(isub[w] > 0, c, n_chunks - 1)

    def wmap(w, c, ie, ib, isub):
        return (l, ie[w], 0, cmap(w, c, isub))

    grid_spec = pltpu.PrefetchScalarGridSpec(
        num_scalar_prefetch=3,
        grid=(n_items, n_chunks),
        in_specs=[
            pl.BlockSpec((MOE_ROWS, ff), lambda w, c, ie, ib, isub: (ib[w], 0)),
            pl.BlockSpec((None, None, ff, tn), wmap),
            pl.BlockSpec((None, None, 1, tn), wmap),
        ],
        out_specs=pl.BlockSpec(
            (MOE_ROWS, tn), lambda w, c, ie, ib, isub: (ib[w], cmap(w, c, isub))),
        scratch_shapes=[pltpu.VMEM((ff, tn), BF16)],
    )
    return pl.pallas_call(
        _moe_down_kernel,
        grid_spec=grid_spec,
        out_shape=jax.ShapeDtypeStruct((r, d), BF16),
        compiler_params=_cparams(("arbitrary", "arbitrary")),
        name="moe_down",
    )(item_e, item_blk, item_sub, hs, w_down, b_down4)


def _combine_ln2_kernel(x1_ref, y_ref, gates_ref, mod_ref, g_ref, b_ref, o_ref, *, alpha, d):
    gt2 = mod_ref[5:6, :]
    gates = gates_ref[...]
    ffn = gates[:, 0:1] * y_ref[:, 0:d].astype(F32)
    for k in range(1, TOP_K):
        ffn = ffn + gates[:, k:k + 1] * y_ref[:, k * d:(k + 1) * d].astype(F32)
    z = alpha * x1_ref[...] + gt2 * ffn
    o_ref[...] = _layer_norm_rows(z, g_ref[...], b_ref[...])


def _combine_ln2(x1, yg, gates, mod3, g3, b3, l, seq, alpha, tm=256):
    t, d = x1.shape
    per_b = seq // tm
    return pl.pallas_call(
        functools.partial(_combine_ln2_kernel, alpha=alpha, d=d),
        grid=(t // tm,),
        in_specs=[
            pl.BlockSpec((tm, d), lambda i: (i, 0)),
            pl.BlockSpec((tm, TOP_K * d), lambda i: (i, 0)),
            pl.BlockSpec((tm, LANES), lambda i: (i, 0)),
            pl.BlockSpec((None, 6, d), lambda i: (i // per_b, 0, 0)),
            pl.BlockSpec((None, 1, d), lambda i: (l, 0, 0)),
            pl.BlockSpec((None, 1, d), lambda i: (l, 0, 0)),
        ],
        out_specs=pl.BlockSpec((tm, d), lambda i: (i, 0)),
        out_shape=jax.ShapeDtypeStruct((t, d), F32),
        compiler_params=_cparams(("arbitrary",)),
        name="combine_ln2",
    )(x1, yg, gates, mod3, g3, b3)


def _t5_bucket(dist):
    is_small = dist < REL_MAX_EXACT
    nf = jnp.maximum(dist, REL_MAX_EXACT).astype(F32)
    large = REL_MAX_EXACT + (jnp.log(nf / REL_MAX_EXACT) / math.log(REL_MAX_DISTANCE / REL_MAX_EXACT)
                             * (REL_BUCKETS - REL_MAX_EXACT)).astype(jnp.int32)
    large = jnp.minimum(large, REL_BUCKETS - 1)
    return jnp.where(is_small, dist, large)


def _dilated_bias_table(rel_bias):
    qi = jnp.arange(BLOCK)[:, None]
    ki = jnp.arange(2 * BLOCK)[None, :]
    rel = qi + BLOCK - ki
    tabs = []
    for window, d in DIL_PATTERNS:
        w = window // d
        valid = (rel >= 0) & (rel <= w)
        bucket = _t5_bucket(jnp.clip(rel, 0) * d)
        bias = jnp.transpose(rel_bias[bucket].astype(F32), (2, 0, 1))
        tabs.append(jnp.where(valid[None], bias, NEG_BIG))
    return jnp.stack(tabs, axis=0)


def _routing_tables(ids4, n_items_max):
    e = ids4.reshape(-1)
    onehot = (e[:, None] == jnp.arange(N_EXPERTS, dtype=jnp.int32)[None, :]).astype(jnp.int32)
    csum = jnp.cumsum(onehot, axis=0)
    rank = jnp.sum(onehot * csum, axis=1) - 1
    counts = csum[-1]
    nit = (counts + MOE_ROWS - 1) // MOE_ROWS
    cum_items = jnp.cumsum(nit)
    first_item = cum_items - nit
    slot = first_item[e] * MOE_ROWS + rank
    total = cum_items[-1]
    w = jnp.arange(n_items_max, dtype=jnp.int32)
    e_w = jnp.minimum(jnp.searchsorted(cum_items, w, side="right"), N_EXPERTS - 1).astype(jnp.int32)
    j_w = w - first_item[e_w]
    rows_w = jnp.clip(counts[e_w] - j_w * MOE_ROWS, 0, MOE_ROWS)
    valid = w < total
    item_sub = jnp.where(valid, (rows_w + MOE_SUB - 1) // MOE_SUB, 0).astype(jnp.int32)
    last = total - 1
    item_e = jnp.where(valid, e_w, e_w[last]).astype(jnp.int32)
    item_blk = jnp.where(valid, w, last).astype(jnp.int32)
    return slot.astype(jnp.int32), item_e, item_blk, item_sub


def kernel(x, c, positions, w_ada, b_ada, w_in, g_q, g_kv, w_uq, w_ukv, rel_bias, w_o,
           ln1_g, ln1_b, w_router, b_router, w_gate, b_gate, w_up, b_up, w_down, b_down,
           ln2_g, ln2_b):
    batch, seq, d = x.shape
    depth = w_ada.shape[0]
    t = batch * seq
    n_heads = d // (2 * HEAD_DIM)
    dil_w = n_heads * HEAD_DIM
    alpha = (2.0 * depth) ** 0.25
    assert seq % (BLOCK * max(dd for _, dd in DIL_PATTERNS)) == 0
    assert all(win // dd == BLOCK for win, dd in DIL_PATTERNS)

    inv_freq = 1.0 / (ROPE_THETA ** (jnp.arange(0, QK_ROPE_DIM, 2, dtype=F32) / QK_ROPE_DIM))
    ang = positions.astype(F32)[..., None] * inv_freq
    cos, sin = jnp.cos(ang).reshape(t, -1), jnp.sin(ang).reshape(t, -1)
    zpad = jnp.zeros((t, LANES - QK_ROPE_DIM), F32)
    cos_t = jnp.concatenate([cos, cos, zpad], axis=1)
    sin_t = jnp.concatenate([-sin, sin, zpad], axis=1)

    bias_tab = _dilated_bias_table(rel_bias)
    c_pad = jnp.zeros((16, d), F32).at[:batch].set(c)

    x2 = x.reshape(t, d)
    n_items_max = (t * TOP_K) // MOE_ROWS + N_EXPERTS
    mla_cols = Q_LORA_RANK + KV_LORA_RANK

    for l in range(depth):
        mod = _ada(c_pad, w_ada, b_ada.reshape(depth, 1, -1), l)[:batch]
        mod3 = mod.reshape(batch, 6, d)

        wl = w_in[l]
        w_in_p = jnp.concatenate(
            [wl[:, :mla_cols], wl[:, mla_cols + QK_ROPE_DIM:],
             wl[:, mla_cols:mla_cols + QK_ROPE_DIM],
             jnp.zeros((d, LANES - QK_ROPE_DIM), wl.dtype)], axis=1).astype(BF16)
        wq = w_uq[l].reshape(Q_LORA_RANK, n_heads, QK_NOPE_DIM + QK_ROPE_DIM)
        w_uq_p = jnp.pad(wq, ((0, 0), (0, 0), (0, QK_PAD - QK_NOPE_DIM - QK_ROPE_DIM))
                         ).reshape(Q_LORA_RANK, n_heads * QK_PAD).astype(BF16)
        wkv = w_ukv[l].reshape(KV_LORA_RANK, n_heads, QK_NOPE_DIM + V_HEAD_DIM)
        w_uk = wkv[:, :, :QK_NOPE_DIM].reshape(KV_LORA_RANK, -1).astype(BF16)
        w_uv = wkv[:, :, QK_NOPE_DIM:].reshape(KV_LORA_RANK, -1).astype(BF16)

        u1 = _modulate(x2, mod3, seq, 0, 1)
        hm = _in_proj_mla(u1, w_in_p, mla_cols)
        qkv_d = _in_proj_dil(u1, w_in_p, mla_cols, batch, seq, n_heads)
        krr = _in_proj_krope(u1, w_in_p, mla_cols + 3 * dil_w, cos_t, sin_t)
        q_a = _q_up(hm, g_q.reshape(depth, 1, -1), l, w_uq_p, cos_t, sin_t, n_heads)
        k_a, v_a = _kv_up(hm, g_kv.reshape(depth, 1, -1), l, w_uk, w_uv, krr, n_heads)
        o_a = _mla_attn(q_a, k_a, v_a, batch, seq, n_heads)
        o_b = _dil_attn(qkv_d, bias_tab, batch, seq, n_heads)
        mix = _out_proj(o_a, o_b, w_o, l)

        wr_p = jnp.pad(w_router[l], ((0, 0), (0, LANES - N_EXPERTS)))
        br_p = jnp.concatenate([b_router[l], jnp.full((LANES - N_EXPERTS,), NEG_BIG, F32)])[None, :]
        x1, u2, ids, gates = _ln1_router(
            x2, mix, mod3, ln1_g.reshape(depth, 1, -1), ln1_b.reshape(depth, 1, -1),
            wr_p, br_p, l, seq, alpha)

        slot, item_e, item_blk, item_sub = _routing_tables(ids[:, :TOP_K], n_items_max)
        tok = jnp.arange(t * TOP_K, dtype=jnp.int32) // TOP_K
        row_tok = jnp.zeros((n_items_max * MOE_ROWS,), jnp.int32).at[slot].set(tok)
        xs = jnp.take(u2, row_tok, axis=0)
        hs = _moe_up(xs, w_gate, w_up, b_gate.reshape(depth, N_EXPERTS, 1, -1),
                     b_up.reshape(depth, N_EXPERTS, 1, -1), item_e, item_blk, item_sub, l)
        ys = _moe_down(hs, w_down, b_down.reshape(depth, N_EXPERTS, 1, -1),
                       item_e, item_blk, item_sub, l)
        yg = jnp.take(ys, slot, axis=0).reshape(t, TOP_K * d)
        x2 = _combine_ln2(x1, yg, gates, mod3, ln2_g.reshape(depth, 1, -1),
                          ln2_b.reshape(depth, 1, -1), l, seq, alpha)

    return x2.reshape(batch, seq, d)
```

```python
import functools
import math

import jax
import jax.numpy as jnp
from jax import lax
from jax.experimental import pallas as pl
from jax.experimental.pallas import tpu as pltpu

F32 = jnp.float32
BF16 = jnp.bfloat16

HEAD_DIM = 128
Q_LORA_RANK = 1536
KV_LORA_RANK = 512
QK_NOPE_DIM = 128
QK_ROPE_DIM = 64
V_HEAD_DIM = 128
ROPE_THETA = 10000.0
DIL_PATTERNS = ((128, 1), (512, 4), (2048, 16))
BLOCK = 128
REL_BUCKETS = 32
REL_MAX_EXACT = REL_BUCKETS // 2
REL_MAX_DISTANCE = 2048
N_EXPERTS = 32
TOP_K = 4
EXPERT_FF = 1536
SWIGLU_LIMIT = 7.0
SWIGLU_ALPHA = 1.702
LN_EPS = 1e-5
RMS_EPS = 1e-6
NEG_BIG = -1e30

LANES = 128
QK_PAD = 256
VMEM_LIMIT = 56 * 1024 * 1024
MOE_ROWS = 1280
MOE_SUB = 256


def _cparams(sem):
    return pltpu.CompilerParams(dimension_semantics=sem, vmem_limit_bytes=VMEM_LIMIT)


def _ada_kernel(c_ref, w_ref, b_ref, o_ref):
    c = c_ref[...]
    cond = (c * jax.nn.sigmoid(c)).astype(BF16)
    o_ref[...] = jnp.dot(cond, w_ref[...].astype(BF16), preferred_element_type=F32) + b_ref[...]


def _ada(c_pad, w_ada, b_ada3, l, tn=512):
    rows, d = c_pad.shape
    n = w_ada.shape[-1]
    return pl.pallas_call(
        _ada_kernel,
        grid=(n // tn,),
        in_specs=[
            pl.BlockSpec((rows, d), lambda j: (0, 0)),
            pl.BlockSpec((None, d, tn), lambda j: (l, 0, j)),
            pl.BlockSpec((None, 1, tn), lambda j: (l, 0, j)),
        ],
        out_specs=pl.BlockSpec((rows, tn), lambda j: (0, j)),
        out_shape=jax.ShapeDtypeStruct((rows, n), F32),
        compiler_params=_cparams(("arbitrary",)),
        name="ada_mod",
    )(c_pad, w_ada, b_ada3)


def _modulate_kernel(x_ref, mod_ref, o_ref, *, shift_row, scale_row):
    sh = mod_ref[shift_row:shift_row + 1, :]
    sc = mod_ref[scale_row:scale_row + 1, :]
    o_ref[...] = (x_ref[...] * (1.0 + sc) + sh).astype(BF16)


def _modulate(x2, mod3, seq, shift_row, scale_row, tm=512):
    t, d = x2.shape
    per_b = seq // tm
    return pl.pallas_call(
        functools.partial(_modulate_kernel, shift_row=shift_row, scale_row=scale_row),
        grid=(t // tm,),
        in_specs=[
            pl.BlockSpec((tm, d), lambda i: (i, 0)),
            pl.BlockSpec((None, 6, d), lambda i: (i // per_b, 0, 0)),
        ],
        out_specs=pl.BlockSpec((tm, d), lambda i: (i, 0)),
        out_shape=jax.ShapeDtypeStruct((t, d), BF16),
        compiler_params=_cparams(("arbitrary",)),
        name="modulate",
    )(x2, mod3)


def _rope_block(x, cos_t, sin_t):
    lane = lax.broadcasted_iota(jnp.int32, x.shape, 1)
    half = QK_ROPE_DIM // 2
    swapped = jnp.where(lane < half, pltpu.roll(x, LANES - half, 1), pltpu.roll(x, half, 1))
    return x * cos_t + swapped * sin_t


def _proj_plain_kernel(a_ref, w_ref, o_ref):
    o_ref[...] = jnp.dot(a_ref[...], w_ref[...], preferred_element_type=F32).astype(o_ref.dtype)


def _proj_heads_kernel(a_ref, w_ref, o_ref, *, heads_per_tile):
    res = jnp.dot(a_ref[...], w_ref[...], preferred_element_type=F32)
    for hh in range(heads_per_tile):
        o_ref[hh] = res[:, hh * HEAD_DIM:(hh + 1) * HEAD_DIM].astype(o_ref.dtype)


def _proj_rope_kernel(a_ref, w_ref, cos_ref, sin_ref, o_ref):
    res = jnp.dot(a_ref[...], w_ref[...], preferred_element_type=F32)
    o_ref[...] = _rope_block(res, cos_ref[...], sin_ref[...]).astype(o_ref.dtype)


def _in_proj_mla(u, w_p, n_cols, tm=512, tn=512):
    t, d = u.shape
    return pl.pallas_call(
        _proj_plain_kernel,
        grid=(n_cols // tn, t // tm),
        in_specs=[
            pl.BlockSpec((tm, d), lambda j, i: (i, 0)),
            pl.BlockSpec((d, tn), lambda j, i: (0, j)),
        ],
        out_specs=pl.BlockSpec((tm, tn), lambda j, i: (i, j)),
        out_shape=jax.ShapeDtypeStruct((t, n_cols), BF16),
        compiler_params=_cparams(("arbitrary", "arbitrary")),
        name="in_proj_mla",
    )(u, w_p)


def _in_proj_dil(u, w_p, col0, batch, seq, n_heads, tm=512, tn=512):
    t, d = u.shape
    hpt = tn // HEAD_DIM
    tiles_per_mat = n_heads // hpt
    per_b = seq // tm
    off = col0 // tn
    return pl.pallas_call(
        functools.partial(_proj_heads_kernel, heads_per_tile=hpt),
        grid=(3 * tiles_per_mat, t // tm),
        in_specs=[
            pl.BlockSpec((tm, d), lambda j, i: (i, 0)),
            pl.BlockSpec((d, tn), lambda j, i: (0, j + off)),
        ],
        out_specs=pl.BlockSpec(
            (None, None, hpt, tm, HEAD_DIM),
            lambda j, i: (j // tiles_per_mat, i // per_b, j % tiles_per_mat, i % per_b, 0)),
        out_shape=jax.ShapeDtypeStruct((3, batch, n_heads, seq, HEAD_DIM), BF16),
        compiler_params=_cparams(("arbitrary", "arbitrary")),
        name="in_proj_dil",
    )(u, w_p)


def _in_proj_krope(u, w_p, col0, cos_t, sin_t, tm=1024):
    t, d = u.shape
    off = col0 // LANES
    return pl.pallas_call(
        _proj_rope_kernel,
        grid=(t // tm,),
        in_specs=[
            pl.BlockSpec((tm, d), lambda i: (i, 0)),
            pl.BlockSpec((d, LANES), lambda i: (0, off)),
            pl.BlockSpec((tm, LANES), lambda i: (i, 0)),
            pl.BlockSpec((tm, LANES), lambda i: (i, 0)),
        ],
        out_specs=pl.BlockSpec((tm, LANES), lambda i: (i, 0)),
        out_shape=jax.ShapeDtypeStruct((t, LANES), BF16),
        compiler_params=_cparams(("arbitrary",)),
        name="in_proj_krope",
    )(u, w_p, cos_t, sin_t)


def _rms_bf16(h_ref, g_ref):
    hf = h_ref[...].astype(F32)
    y = hf * lax.rsqrt(jnp.mean(hf * hf, axis=-1, keepdims=True) + RMS_EPS)
    return (y * g_ref[...]).astype(BF16)


def _q_up_kernel(h_ref, g_ref, w_ref, cos_ref, sin_ref, o_ref, *, n_heads, scale):
    yb = _rms_bf16(h_ref, g_ref)
    cos_t = cos_ref[...]
    sin_t = sin_ref[...]
    group = 4
    for h0 in range(0, n_heads, group):
        res = jnp.dot(yb, w_ref[:, h0 * QK_PAD:(h0 + group) * QK_PAD], preferred_element_type=F32)
        for hh in range(group):
            c0 = hh * QK_PAD
            nope = res[:, c0:c0 + QK_NOPE_DIM]
            rp = _rope_block(res[:, c0 + QK_NOPE_DIM:c0 + QK_PAD], cos_t, sin_t)
            o0 = (h0 + hh) * QK_PAD
            o_ref[:, o0:o0 + QK_NOPE_DIM] = (nope * scale).astype(BF16)
            o_ref[:, o0 + QK_NOPE_DIM:o0 + QK_PAD] = (rp * scale).astype(BF16)


def _q_up(hm, g3, l, w_uq_p, cos_t, sin_t, n_heads, tm=256):
    t = hm.shape[0]
    scale = float(QK_NOPE_DIM + QK_ROPE_DIM) ** -0.5
    return pl.pallas_call(
        functools.partial(_q_up_kernel, n_heads=n_heads, scale=scale),
        grid=(t // tm,),
        in_specs=[
            pl.BlockSpec((tm, Q_LORA_RANK), lambda i: (i, 0)),
            pl.BlockSpec((None, 1, Q_LORA_RANK), lambda i: (l, 0, 0)),
            pl.BlockSpec((Q_LORA_RANK, n_heads * QK_PAD), lambda i: (0, 0)),
            pl.BlockSpec((tm, LANES), lambda i: (i, 0)),
            pl.BlockSpec((tm, LANES), lambda i: (i, 0)),
        ],
        out_specs=pl.BlockSpec((tm, n_heads * QK_PAD), lambda i: (i, 0)),
        out_shape=jax.ShapeDtypeStruct((t, n_heads * QK_PAD), BF16),
        compiler_params=_cparams(("arbitrary",)),
        name="mla_q_up",
    )(hm, g3, w_uq_p, cos_t, sin_t)


def _kv_up_kernel(h_ref, g_ref, wk_ref, wv_ref, kr_ref, k_ref, v_ref, *, n_heads):
    yb = _rms_bf16(h_ref, g_ref)
    kr = kr_ref[...]
    group = 4
    for h0 in range(0, n_heads, group):
        res = jnp.dot(yb, wk_ref[:, h0 * QK_NOPE_DIM:(h0 + group) * QK_NOPE_DIM],
                      preferred_element_type=F32)
        for hh in range(group):
            o0 = (h0 + hh) * QK_PAD
            k_ref[:, o0:o0 + QK_NOPE_DIM] = res[:, hh * QK_NOPE_DIM:(hh + 1) * QK_NOPE_DIM].astype(BF16)
            k_ref[:, o0 + QK_NOPE_DIM:o0 + QK_PAD] = kr
    v_ref[...] = jnp.dot(yb, wv_ref[...], preferred_element_type=F32).astype(BF16)


def _kv_up(hm, g3, l, w_uk, w_uv, krr, n_heads, tm=256):
    t = hm.shape[0]
    col_blk = Q_LORA_RANK // KV_LORA_RANK
    return pl.pallas_call(
        functools.partial(_kv_up_kernel, n_heads=n_heads),
        grid=(t // tm,),
        in_specs=[
            pl.BlockSpec((tm, KV_LORA_RANK), lambda i: (i, col_blk)),
            pl.BlockSpec((None, 1, KV_LORA_RANK), lambda i: (l, 0, 0)),
            pl.BlockSpec((KV_LORA_RANK, n_heads * QK_NOPE_DIM), lambda i: (0, 0)),
            pl.BlockSpec((KV_LORA_RANK, n_heads * V_HEAD_DIM), lambda i: (0, 0)),
            pl.BlockSpec((tm, LANES), lambda i: (i, 0)),
        ],
        out_specs=[
            pl.BlockSpec((tm, n_heads * QK_PAD), lambda i: (i, 0)),
            pl.BlockSpec((tm, n_heads * V_HEAD_DIM), lambda i: (i, 0)),
        ],
        out_shape=[
            jax.ShapeDtypeStruct((t, n_heads * QK_PAD), BF16),
            jax.ShapeDtypeStruct((t, n_heads * V_HEAD_DIM), BF16),
        ],
        compiler_params=_cparams(("arbitrary",)),
        name="mla_kv_up",
    )(hm, g3, w_uk, w_uv, krr)


def _mla_attn_kernel(q_ref, k_ref, v_ref, o_ref, *, seq, tq):
    nq = seq // tq
    row = lax.broadcasted_iota(jnp.int32, (tq, tq), 0)
    col = lax.broadcasted_iota(jnp.int32, (tq, tq), 1)
    causal = col <= row

    def kv_step(q, j, carry, masked):
        m, l, acc = carry
        off = pl.multiple_of(j * tq, tq)
        k = k_ref[pl.ds(off, tq), :]
        v = v_ref[pl.ds(off, tq), :]
        s = lax.dot_general(q, k, (((1,), (1,)), ((), ())), preferred_element_type=F32)
        if masked:
            s = jnp.where(causal, s, NEG_BIG)
        m_new = jnp.maximum(m, jnp.max(s, axis=-1, keepdims=True))
        p = jnp.exp(s - m_new)
        alpha = jnp.exp(m - m_new)
        l_new = alpha * l + jnp.sum(p, axis=-1, keepdims=True)
        acc_new = alpha * acc + jnp.dot(p.astype(BF16), v, preferred_element_type=F32)
        return m_new, l_new, acc_new

    def q_loop(i, _):
        qoff = pl.multiple_of(i * tq, tq)
        q = q_ref[pl.ds(qoff, tq), :]
        init = (jnp.full((tq, 1), NEG_BIG, F32), jnp.zeros((tq, 1), F32),
                jnp.zeros((tq, V_HEAD_DIM), F32))
        carry = lax.fori_loop(0, i, lambda j, c: kv_step(q, j, c, False), init)
        m, l, acc = kv_step(q, i, carry, True)
        o_ref[pl.ds(qoff, tq), :] = (acc / l).astype(o_ref.dtype)
        return 0

    lax.fori_loop(0, nq, q_loop, 0)


def _mla_attn(q, k, v, batch, seq, n_heads, tq=512):
    t = q.shape[0]
    return pl.pallas_call(
        functools.partial(_mla_attn_kernel, seq=seq, tq=tq),
        grid=(batch, n_heads),
        in_specs=[
            pl.BlockSpec((seq, QK_PAD), lambda b, h: (b, h)),
            pl.BlockSpec((seq, QK_PAD), lambda b, h: (b, h)),
            pl.BlockSpec((seq, V_HEAD_DIM), lambda b, h: (b, h)),
        ],
        out_specs=pl.BlockSpec((seq, V_HEAD_DIM), lambda b, h: (b, h)),
        out_shape=jax.ShapeDtypeStruct((t, n_heads * V_HEAD_DIM), BF16),
        compiler_params=_cparams(("arbitrary", "arbitrary")),
        name="mla_attn",
    )(q, k, v)


def _dil_attn_kernel(v1_ref, v4_ref, v16_ref, bias_ref, o_ref, m_s, l_s, acc_s, *, seq):
    scale = float(HEAD_DIM) ** -0.5
    views = (v1_ref, v4_ref, v16_ref)

    def block(ref, bi, d, r, i, first):
        lanes = slice(r * HEAD_DIM, (r + 1) * HEAD_DIM)
        qoff = pl.multiple_of(i * BLOCK, BLOCK)
        q = ref[0, pl.ds(qoff, BLOCK), lanes]
        if first:
            kk = ref[1, pl.ds(0, BLOCK), lanes]
            vv = ref[2, pl.ds(0, BLOCK), lanes]
            bias = bias_ref[bi, :, BLOCK:2 * BLOCK]
        else:
            koff = pl.multiple_of(i * BLOCK - BLOCK, BLOCK)
            kk = ref[1, pl.ds(koff, 2 * BLOCK), lanes]
            vv = ref[2, pl.ds(koff, 2 * BLOCK), lanes]
            bias = bias_ref[bi]
        s = lax.dot_general(q, kk, (((1,), (1,)), ((), ())), preferred_element_type=F32)
        s = s * scale + bias
        m_b = jnp.max(s, axis=-1, keepdims=True)
        p = jnp.exp(s - m_b)
        l_b = jnp.sum(p, axis=-1, keepdims=True)
        a_b = jnp.dot(p.astype(BF16), vv, preferred_element_type=F32)
        m_b = jnp.broadcast_to(m_b, (BLOCK, HEAD_DIM))
        l_b = jnp.broadcast_to(l_b, (BLOCK, HEAD_DIM))
        if d == 1:
            rows = pl.ds(qoff, BLOCK)
            m_s[rows, :] = m_b
            l_s[rows, :] = l_b
            acc_s[rows, :] = a_b
        else:
            rows = pl.ds(i * (BLOCK * d) + r, BLOCK, stride=d)
            m_o = m_s[rows, :]
            m_n = jnp.maximum(m_o, m_b)
            e_o = jnp.exp(m_o - m_n)
            e_b = jnp.exp(m_b - m_n)
            l_s[rows, :] = e_o * l_s[rows, :] + e_b * l_b
            acc_s[rows, :] = e_o * acc_s[rows, :] + e_b * a_b
            m_s[rows, :] = m_n

    for bi, (_, d) in enumerate(DIL_PATTERNS):
        ref = views[bi]
        nblk = seq // d // BLOCK
        for r in range(d):
            block(ref, bi, d, r, 0, True)

            def body(i, _, ref=ref, bi=bi, d=d, r=r):
                block(ref, bi, d, r, i, False)
                return 0

            lax.fori_loop(1, nblk, body, 0)

    o_ref[...] = (acc_s[...] / l_s[...]).astype(o_ref.dtype)


def _dil_attn(qkv, bias_tab, batch, seq, n_heads):
    t = batch * seq
    views = []
    specs = []
    for _, d in DIL_PATTERNS:
        views.append(qkv.reshape(3, batch, n_heads, seq // d, d * HEAD_DIM))
        specs.append(pl.BlockSpec((3, None, None, seq // d, d * HEAD_DIM),
                                  lambda b, h: (0, b, h, 0, 0)))
    specs.append(pl.BlockSpec((len(DIL_PATTERNS), None, BLOCK, 2 * BLOCK), lambda b, h: (0, h, 0, 0)))
    return pl.pallas_call(
        functools.partial(_dil_attn_kernel, seq=seq),
        grid=(batch, n_heads),
        in_specs=specs,
        out_specs=pl.BlockSpec((seq, HEAD_DIM), lambda b, h: (b, h)),
        out_shape=jax.ShapeDtypeStruct((t, n_heads * HEAD_DIM), BF16),
        scratch_shapes=[pltpu.VMEM((seq, HEAD_DIM), F32)] * 3,
        compiler_params=_cparams(("arbitrary", "arbitrary")),
        name="dil_attn",
    )(*views, bias_tab)


def _out_proj_kernel(a1_ref, a2_ref, w_ref, o_ref, wb_ref):
    @pl.when(pl.program_id(1) == 0)
    def _():
        wb_ref[...] = w_ref[...].astype(BF16)

    k1 = a1_ref.shape[1]
    acc = jnp.dot(a1_ref[...], wb_ref[:k1, :], preferred_element_type=F32)
    acc = acc + jnp.dot(a2_ref[...], wb_ref[k1:, :], preferred_element_type=F32)
    o_ref[...] = acc.astype(o_ref.dtype)


def _out_proj(o_a, o_b, w_o, l, tm=512, tn=512):
    t, k1 = o_a.shape
    k2 = o_b.shape[1]
    n = w_o.shape[-1]
    return pl.pallas_call(
        _out_proj_kernel,
        grid=(n // tn, t // tm),
        in_specs=[
            pl.BlockSpec((tm, k1), lambda j, i: (i, 0)),
            pl.BlockSpec((tm, k2), lambda j, i: (i, 0)),
            pl.BlockSpec((None, k1 + k2, tn), lambda j, i: (l, 0, j)),
        ],
        out_specs=pl.BlockSpec((tm, tn), lambda j, i: (i, j)),
        out_shape=jax.ShapeDtypeStruct((t, n), BF16),
        scratch_shapes=[pltpu.VMEM((k1 + k2, tn), BF16)],
        compiler_params=_cparams(("arbitrary", "arbitrary")),
        name="out_proj",
    )(o_a, o_b, w_o)


def _layer_norm_rows(z, g, b):
    mu = jnp.mean(z, axis=-1, keepdims=True)
    zc = z - mu
    var = jnp.mean(zc * zc, axis=-1, keepdims=True)
    return zc * lax.rsqrt(var + LN_EPS) * g + b


SLAB_CHUNK = 1024
SLAB_Q = SLAB_CHUNK // (2 * LANES)


def _slab_rows(width):
    return width // (2 * LANES)


def _pack_chunk(vals):
    bits = lax.bitcast_convert_type(vals.astype(BF16).astype(F32), jnp.uint32)
    half = SLAB_CHUNK // 2
    return [(bits[:, q * LANES:(q + 1) * LANES] >> 16)
            | bits[:, half + q * LANES:half + (q + 1) * LANES] for q in range(SLAB_Q)]


def _unpack_words(words):
    lo = lax.bitcast_convert_type(words << 16, F32)
    hi = lax.bitcast_convert_type(words & jnp.uint32(0xFFFF0000), F32)
    return lo, hi


def _ln1_router_kernel(x_ref, mix_ref, mod_ref, g_ref, b_ref, wr_ref, br_ref,
                       x1_ref, u2p_ref, ids_ref, gates_ref, *, alpha):
    gt1 = mod_ref[2:3, :]
    sh2 = mod_ref[3:4, :]
    sc2 = mod_ref[4:5, :]
    z = alpha * x_ref[...] + gt1 * mix_ref[...].astype(F32)
    x1 = _layer_norm_rows(z, g_ref[...], b_ref[...])
    x1_ref[...] = x1
    u2 = (x1 * (1.0 + sc2) + sh2).astype(BF16)
    tm, d = u2.shape
    sr = _slab_rows(d)
    for g in range(d // SLAB_CHUNK):
        for q, words in enumerate(_pack_chunk(u2[:, g * SLAB_CHUNK:(g + 1) * SLAB_CHUNK])):
            u2p_ref[pl.ds(g * SLAB_Q + q, tm, stride=sr), :] = words
    logits = jnp.dot(u2, wr_ref[...].astype(BF16), preferred_element_type=F32) + br_ref[...]
    lane = lax.broadcasted_iota(jnp.int32, logits.shape, 1)
    lane_f = lane.astype(F32)
    vals = []
    ids = []
    for _ in range(TOP_K):
        mk = jnp.max(logits, axis=-1, keepdims=True)
        idx_f = jnp.min(jnp.where(logits == mk, lane_f, float(LANES)), axis=-1, keepdims=True)
        vals.append(mk)
        ids.append(idx_f.astype(jnp.int32))
        logits = jnp.where(lane_f == idx_f, -jnp.inf, logits)
    exps = [jnp.exp(v - vals[0]) for v in vals]
    denom = exps[0]
    for e in exps[1:]:
        denom = denom + e
    ids_out = jnp.zeros(lane.shape, jnp.int32)
    gates_out = jnp.zeros(lane.shape, F32)
    for k in range(TOP_K):
        ids_out = jnp.where(lane == k, ids[k], ids_out)
        gates_out = jnp.where(lane == k, exps[k] / denom, gates_out)
    ids_ref[...] = ids_out
    gates_ref[...] = gates_out


def _ln1_router(x2, mix, mod3, g3, b3, wr_p, br_p, l, seq, alpha, tm=256):
    t, d = x2.shape
    per_b = seq // tm
    return pl.pallas_call(
        functools.partial(_ln1_router_kernel, alpha=alpha),
        grid=(t // tm,),
        in_specs=[
            pl.BlockSpec((tm, d), lambda i: (i, 0)),
            pl.BlockSpec((tm, d), lambda i: (i, 0)),
            pl.BlockSpec((None, 6, d), lambda i: (i // per_b, 0, 0)),
            pl.BlockSpec((None, 1, d), lambda i: (l, 0, 0)),
            pl.BlockSpec((None, 1, d), lambda i: (l, 0, 0)),
            pl.BlockSpec((d, LANES), lambda i: (0, 0)),
            pl.BlockSpec((1, LANES), lambda i: (0, 0)),
        ],
        out_specs=[
            pl.BlockSpec((tm, d), lambda i: (i, 0)),
            pl.BlockSpec((tm * _slab_rows(d), LANES), lambda i: (i, 0)),
            pl.BlockSpec((tm, LANES), lambda i: (i, 0)),
            pl.BlockSpec((tm, LANES), lambda i: (i, 0)),
        ],
        out_shape=[
            jax.ShapeDtypeStruct((t, d), F32),
            jax.ShapeDtypeStruct((t * _slab_rows(d), LANES), jnp.uint32),
            jax.ShapeDtypeStruct((t, LANES), jnp.int32),
            jax.ShapeDtypeStruct((t, LANES), F32),
        ],
        compiler_params=_cparams(("arbitrary",)),
        name="ln1_router",
    )(x2, mix, mod3, g3, b3, wr_p, br_p)


def _moe_up_kernel(item_e, item_blk, item_sub, x_ref, wg_ref, wu_ref, bg_ref, bu_ref,
                   h_ref, wg_s, wu_s):
    w = pl.program_id(0)
    nsub = item_sub[w]

    @pl.when(nsub > 0)
    def _():
        wg_s[...] = wg_ref[...].astype(BF16)
        wu_s[...] = wu_ref[...].astype(BF16)
        bg = bg_ref[...]
        bu = bu_ref[...]

        def body(j, _):
            off = pl.multiple_of(j * MOE_SUB, MOE_SUB)
            xt = x_ref[pl.ds(off, MOE_SUB), :]
            glu = jnp.dot(xt, wg_s[...], preferred_element_type=F32) + bg
            lin = jnp.dot(xt, wu_s[...], preferred_element_type=F32) + bu
            glu = jnp.minimum(glu, SWIGLU_LIMIT)
            lin = jnp.clip(lin, -SWIGLU_LIMIT, SWIGLU_LIMIT)
            act = glu * jax.nn.sigmoid(SWIGLU_ALPHA * glu) * (lin + 1.0)
            h_ref[pl.ds(off, MOE_SUB), :] = act.astype(h_ref.dtype)
            return 0

        lax.fori_loop(0, nsub, body, 0)


def _moe_up(xs, w_gate, w_up, b_gate4, b_up4, item_e, item_blk, item_sub, l, tf=256):
    r, d = xs.shape
    n_items = r // MOE_ROWS
    ff = w_gate.shape[-1]
    n_chunks = ff // tf

    def cmap(w, c, isub):
        return jnp.where(isub[w] > 0, c, n_chunks - 1)

    def wmap(w, c, ie, ib, isub):
        return (l, ie[w], 0, cmap(w, c, isub))

    grid_spec = pltpu.PrefetchScalarGridSpec(
        num_scalar_prefetch=3,
        grid=(n_items, n_chunks),
        in_specs=[
            pl.BlockSpec((MOE_ROWS, d), lambda w, c, ie, ib, isub: (ib[w], 0)),
            pl.BlockSpec((None, None, d, tf), wmap),
            pl.BlockSpec((None, None, d, tf), wmap),
            pl.BlockSpec((None, None, 1, tf), wmap),
            pl.BlockSpec((None, None, 1, tf), wmap),
        ],
        out_specs=pl.BlockSpec(
            (MOE_ROWS, tf), lambda w, c, ie, ib, isub: (ib[w], cmap(w, c, isub))),
        scratch_shapes=[pltpu.VMEM((d, tf), BF16), pltpu.VMEM((d, tf), BF16)],
    )
    return pl.pallas_call(
        _moe_up_kernel,
        grid_spec=grid_spec,
        out_shape=jax.ShapeDtypeStruct((r, ff), BF16),
        compiler_params=_cparams(("arbitrary", "arbitrary")),
        name="moe_up",
    )(item_e, item_blk, item_sub, xs, w_gate, w_up, b_gate4, b_up4)


def _moe_down_kernel(item_e, item_blk, item_sub, h_ref, wd_ref, bd_ref, y_ref, wd_s, *, sr):
    w = pl.program_id(0)
    c = pl.program_id(1)
    nsub = item_sub[w]

    @pl.when(nsub > 0)
    def _():
        wd_s[...] = wd_ref[...].astype(BF16)
        bd = bd_ref[...]

        def body(j, _):
            off = pl.multiple_of(j * MOE_SUB, MOE_SUB)
            ht = h_ref[pl.ds(off, MOE_SUB), :]
            y = jnp.dot(ht, wd_s[...], preferred_element_type=F32) + bd
            for q, words in enumerate(_pack_chunk(y)):
                y_ref[pl.ds(off * sr + c * SLAB_Q + q, MOE_SUB, stride=sr), :] = words
            return 0

        lax.fori_loop(0, nsub, body, 0)


def _moe_down(hs, w_down, b_down4, item_e, item_blk, item_sub, l):
    r, ff = hs.shape
    n_items = r // MOE_ROWS
    d = w_down.shape[-1]
    tn = SLAB_CHUNK
    n_chunks = d // tn
    sr = _slab_rows(d)

    def cmap(w, c, isub):
        return jnp.where(isub[w] > 0, c, n_chunks - 1)

    def wmap(w, c, ie, ib, isub):
        return (l, ie[w], 0, cmap(w, c, isub))

    grid_spec = pltpu.PrefetchScalarGridSpec(
        num_scalar_prefetch=3,
        grid=(n_items, n_chunks),
        in_specs=[
            pl.BlockSpec((MOE_ROWS, ff), lambda w, c, ie, ib, isub: (ib[w], 0)),
            pl.BlockSpec((None, None, ff, tn), wmap),
            pl.BlockSpec((None, None, 1, tn), wmap),
        ],
        out_specs=pl.BlockSpec((MOE_ROWS * sr, LANES), lambda w, c, ie, ib, isub: (ib[w], 0)),
        scratch_shapes=[pltpu.VMEM((ff, tn), BF16)],
    )
    return pl.pallas_call(
        functools.partial(_moe_down_kernel, sr=sr),
        grid_spec=grid_spec,
        out_shape=jax.ShapeDtypeStruct((r * sr, LANES), jnp.uint32),
        compiler_params=_cparams(("arbitrary", "arbitrary")),
        name="moe_down",
    )(item_e, item_blk, item_sub, hs, w_down, b_down4)


GATHER_UNROLL = 8


def _dispatch_kernel(blk_valid, blk_out, tok_ref, src_ref, o_ref, buf, sem, *, rows, sr):
    i = pl.program_id(0)

    @pl.when(blk_valid[i] > 0)
    def _():
        def copy(r):
            src_row = pl.multiple_of(tok_ref[0, r] * sr, sr)
            dst_row = pl.multiple_of(r * sr, sr)
            return pltpu.make_async_copy(src_ref.at[pl.ds(src_row, sr), :],
                                         buf.at[pl.ds(dst_row, sr), :], sem)

        def issue(r0, _):
            for u in range(GATHER_UNROLL):
                copy(r0 * GATHER_UNROLL + u).start()
            return 0

        def drain(r0, _):
            for u in range(GATHER_UNROLL):
                copy(r0 * GATHER_UNROLL + u).wait()
            return 0

        lax.fori_loop(0, rows // GATHER_UNROLL, issue, 0)
        lax.fori_loop(0, rows // GATHER_UNROLL, drain, 0)
        half = SLAB_CHUNK // 2
        for g in range(sr // SLAB_Q):
            for q in range(SLAB_Q):
                lo, hi = _unpack_words(buf[pl.ds(g * SLAB_Q + q, rows, stride=sr), :])
                c0 = g * SLAB_CHUNK + q * LANES
                o_ref[:, c0:c0 + LANES] = lo.astype(BF16)
                o_ref[:, c0 + half:c0 + half + LANES] = hi.astype(BF16)


def _dispatch(u2p, d, row_tok, blk_valid, blk_out, rows=MOE_SUB):
    sr = _slab_rows(d)
    r = row_tok.shape[0]
    n_blk = r // rows
    grid_spec = pltpu.PrefetchScalarGridSpec(
        num_scalar_prefetch=2,
        grid=(n_blk,),
        in_specs=[
            pl.BlockSpec((None, 1, rows), lambda i, bv, bo: (i, 0, 0), memory_space=pltpu.SMEM),
            pl.BlockSpec(memory_space=pl.ANY),
        ],
        out_specs=pl.BlockSpec((rows, d), lambda i, bv, bo: (bo[i], 0)),
        scratch_shapes=[pltpu.VMEM((rows * sr, LANES), jnp.uint32),
                        pltpu.SemaphoreType.DMA],
    )
    return pl.pallas_call(
        functools.partial(_dispatch_kernel, rows=rows, sr=sr),
        grid_spec=grid_spec,
        out_shape=jax.ShapeDtypeStruct((r, d), BF16),
        compiler_params=_cparams(("arbitrary",)),
        name="moe_dispatch",
    )(blk_valid, blk_out, row_tok.reshape(n_blk, 1, rows), u2p)


def _combine_ln2_kernel(slot_ref, x1_ref, ys_ref, gates_ref, mod_ref, g_ref, b_ref, o_ref,
                        buf, ffn_s, sem, *, alpha, tm, sr):
    def copy(i, k):
        src_row = pl.multiple_of(slot_ref[0, i * TOP_K + k] * sr, sr)
        dst_row = pl.multiple_of((k * tm + i) * sr, sr)
        return pltpu.make_async_copy(ys_ref.at[pl.ds(src_row, sr), :],
                                     buf.at[pl.ds(dst_row, sr), :], sem)

    def issue(i, _):
        for k in range(TOP_K):
            copy(i, k).start()
        return 0

    def drain(i, _):
        for k in range(TOP_K):
            copy(i, k).wait()
        return 0

    lax.fori_loop(0, tm, issue, 0)
    lax.fori_loop(0, tm, drain, 0)

    gates = gates_ref[...]
    gk = [jnp.broadcast_to(gates[:, k:k + 1], (tm, LANES)) for k in range(TOP_K)]
    half = SLAB_CHUNK // 2
    for g in range(sr // SLAB_Q):
        for q in range(SLAB_Q):
            lo_acc = None
            hi_acc = None
            for k in range(TOP_K):
                lo, hi = _unpack_words(buf[pl.ds(k * tm * sr + g * SLAB_Q + q, tm, stride=sr), :])
                lo_acc = gk[k] * lo if lo_acc is None else lo_acc + gk[k] * lo
                hi_acc = gk[k] * hi if hi_acc is None else hi_acc + gk[k] * hi
            c0 = g * SLAB_CHUNK + q * LANES
            ffn_s[:, c0:c0 + LANES] = lo_acc
            ffn_s[:, c0 + half:c0 + half + LANES] = hi_acc
    gt2 = mod_ref[5:6, :]
    z = alpha * x1_ref[...] + gt2 * ffn_s[...]
    o_ref[...] = _layer_norm_rows(z, g_ref[...], b_ref[...])


def _combine_ln2(x1, ys, slot, gates, mod3, g3, b3, l, seq, alpha, tm=256):
    t, d = x1.shape
    per_b = seq // tm
    sr = _slab_rows(d)
    return pl.pallas_call(
        functools.partial(_combine_ln2_kernel, alpha=alpha, tm=tm, sr=sr),
        grid=(t // tm,),
        in_specs=[
            pl.BlockSpec((None, 1, tm * TOP_K), lambda i: (i, 0, 0), memory_space=pltpu.SMEM),
            pl.BlockSpec((tm, d), lambda i: (i, 0)),
            pl.BlockSpec(memory_space=pl.ANY),
            pl.BlockSpec((tm, LANES), lambda i: (i, 0)),
            pl.BlockSpec((None, 6, d), lambda i: (i // per_b, 0, 0)),
            pl.BlockSpec((None, 1, d), lambda i: (l, 0, 0)),
            pl.BlockSpec((None, 1, d), lambda i: (l, 0, 0)),
        ],
        out_specs=pl.BlockSpec((tm, d), lambda i: (i, 0)),
        out_shape=jax.ShapeDtypeStruct((t, d), F32),
        scratch_shapes=[pltpu.VMEM((TOP_K * tm * sr, LANES), jnp.uint32),
                        pltpu.VMEM((tm, d), F32),
                        pltpu.SemaphoreType.DMA],
        compiler_params=_cparams(("arbitrary",)),
        name="combine_ln2",
    )(slot.reshape(t // tm, 1, tm * TOP_K), x1, ys, gates, mod3, g3, b3)


def _t5_bucket(dist):
    is_small = dist < REL_MAX_EXACT
    nf = jnp.maximum(dist, REL_MAX_EXACT).astype(F32)
    large = REL_MAX_EXACT + (jnp.log(nf / REL_MAX_EXACT) / math.log(REL_MAX_DISTANCE / REL_MAX_EXACT)
                             * (REL_BUCKETS - REL_MAX_EXACT)).astype(jnp.int32)
    large = jnp.minimum(large, REL_BUCKETS - 1)
    return jnp.where(is_small, dist, large)


def _dilated_bias_table(rel_bias):
    qi = jnp.arange(BLOCK)[:, None]
    ki = jnp.arange(2 * BLOCK)[None, :]
    rel = qi + BLOCK - ki
    tabs = []
    for window, d in DIL_PATTERNS:
        w = window // d
        valid = (rel >= 0) & (rel <= w)
        bucket = _t5_bucket(jnp.clip(rel, 0) * d)
        bias = jnp.transpose(rel_bias[bucket].astype(F32), (2, 0, 1))
        tabs.append(jnp.where(valid[None], bias, NEG_BIG))
    return jnp.stack(tabs, axis=0)


def _routing_tables(ids4, n_items_max):
    e = ids4.reshape(-1)
    onehot = (e[:, None] == jnp.arange(N_EXPERTS, dtype=jnp.int32)[None, :]).astype(jnp.int32)
    csum = jnp.cumsum(onehot, axis=0)
    rank = jnp.sum(onehot * csum, axis=1) - 1
    counts = csum[-1]
    nit = (counts + MOE_ROWS - 1) // MOE_ROWS
    cum_items = jnp.cumsum(nit)
    first_item = cum_items - nit
    slot = first_item[e] * MOE_ROWS + rank
    total = cum_items[-1]
    w = jnp.arange(n_items_max, dtype=jnp.int32)
    e_w = jnp.minimum(jnp.searchsorted(cum_items, w, side="right"), N_EXPERTS - 1).astype(jnp.int32)
    j_w = w - first_item[e_w]
    rows_w = jnp.clip(counts[e_w] - j_w * MOE_ROWS, 0, MOE_ROWS)
    valid = w < total
    item_sub = jnp.where(valid, (rows_w + MOE_SUB - 1) // MOE_SUB, 0).astype(jnp.int32)
    last = total - 1
    item_e = jnp.where(valid, e_w, e_w[last]).astype(jnp.int32)
    item_blk = jnp.where(valid, w, last).astype(jnp.int32)
    per_item = MOE_ROWS // MOE_SUB
    blk = jnp.arange(n_items_max * per_item, dtype=jnp.int32)
    blk_valid = ((blk % per_item) < item_sub[blk // per_item]).astype(jnp.int32)
    blk_out = lax.cummax(jnp.where(blk_valid > 0, blk, 0), axis=0).astype(jnp.int32)
    return slot.astype(jnp.int32), item_e, item_blk, item_sub, blk_valid, blk_out


def kernel(x, c, positions, w_ada, b_ada, w_in, g_q, g_kv, w_uq, w_ukv, rel_bias, w_o,
           ln1_g, ln1_b, w_router, b_router, w_gate, b_gate, w_up, b_up, w_down, b_down,
           ln2_g, ln2_b):
    batch, seq, d = x.shape
    depth = w_ada.shape[0]
    t = batch * seq
    n_heads = d // (2 * HEAD_DIM)
    dil_w = n_heads * HEAD_DIM
    alpha = (2.0 * depth) ** 0.25
    assert seq % (BLOCK * max(dd for _, dd in DIL_PATTERNS)) == 0
    assert all(win // dd == BLOCK for win, dd in DIL_PATTERNS)

    inv_freq = 1.0 / (ROPE_THETA ** (jnp.arange(0, QK_ROPE_DIM, 2, dtype=F32) / QK_ROPE_DIM))
    ang = positions.astype(F32)[..., None] * inv_freq
    cos, sin = jnp.cos(ang).reshape(t, -1), jnp.sin(ang).reshape(t, -1)
    zpad = jnp.zeros((t, LANES - QK_ROPE_DIM), F32)
    cos_t = jnp.concatenate([cos, cos, zpad], axis=1)
    sin_t = jnp.concatenate([-sin, sin, zpad], axis=1)

    bias_tab = _dilated_bias_table(rel_bias)
    c_pad = jnp.zeros((16, d), F32).at[:batch].set(c)

    x2 = x.reshape(t, d)
    n_items_max = (t * TOP_K) // MOE_ROWS + N_EXPERTS
    mla_cols = Q_LORA_RANK + KV_LORA_RANK

    for l in range(depth):
        mod = _ada(c_pad, w_ada, b_ada.reshape(depth, 1, -1), l)[:batch]
        mod3 = mod.reshape(batch, 6, d)

        wl = w_in[l]
        w_in_p = jnp.concatenate(
            [wl[:, :mla_cols], wl[:, mla_cols + QK_ROPE_DIM:],
             wl[:, mla_cols:mla_cols + QK_ROPE_DIM],
             jnp.zeros((d, LANES - QK_ROPE_DIM), wl.dtype)], axis=1).astype(BF16)
        wq = w_uq[l].reshape(Q_LORA_RANK, n_heads, QK_NOPE_DIM + QK_ROPE_DIM)
        w_uq_p = jnp.pad(wq, ((0, 0), (0, 0), (0, QK_PAD - QK_NOPE_DIM - QK_ROPE_DIM))
                         ).reshape(Q_LORA_RANK, n_heads * QK_PAD).astype(BF16)
        wkv = w_ukv[l].reshape(KV_LORA_RANK, n_heads, QK_NOPE_DIM + V_HEAD_DIM)
        w_uk = wkv[:, :, :QK_NOPE_DIM].reshape(KV_LORA_RANK, -1).astype(BF16)
        w_uv = wkv[:, :, QK_NOPE_DIM:].reshape(KV_LORA_RANK, -1).astype(BF16)

        u1 = _modulate(x2, mod3, seq, 0, 1)
        hm = _in_proj_mla(u1, w_in_p, mla_cols)
        qkv_d = _in_proj_dil(u1, w_in_p, mla_cols, batch, seq, n_heads)
        krr = _in_proj_krope(u1, w_in_p, mla_cols + 3 * dil_w, cos_t, sin_t)
        q_a = _q_up(hm, g_q.reshape(depth, 1, -1), l, w_uq_p, cos_t, sin_t, n_heads)
        k_a, v_a = _kv_up(hm, g_kv.reshape(depth, 1, -1), l, w_uk, w_uv, krr, n_heads)
        o_a = _mla_attn(q_a, k_a, v_a, batch, seq, n_heads)
        o_b = _dil_attn(qkv_d, bias_tab, batch, seq, n_heads)
        mix = _out_proj(o_a, o_b, w_o, l)

        wr_p = jnp.pad(w_router[l], ((0, 0), (0, LANES - N_EXPERTS)))
        br_p = jnp.concatenate([b_router[l], jnp.full((LANES - N_EXPERTS,), NEG_BIG, F32)])[None, :]
        x1, u2p, ids, gates = _ln1_router(
            x2, mix, mod3, ln1_g.reshape(depth, 1, -1), ln1_b.reshape(depth, 1, -1),
            wr_p, br_p, l, seq, alpha)

        slot, item_e, item_blk, item_sub, blk_valid, blk_out = _routing_tables(
            ids[:, :TOP_K], n_items_max)
        tok = jnp.arange(t * TOP_K, dtype=jnp.int32) // TOP_K
        row_tok = jnp.zeros((n_items_max * MOE_ROWS,), jnp.int32).at[slot].set(tok)
        xs = _dispatch(u2p, d, row_tok, blk_valid, blk_out)
        hs = _moe_up(xs, w_gate, w_up, b_gate.reshape(depth, N_EXPERTS, 1, -1),
                     b_up.reshape(depth, N_EXPERTS, 1, -1), item_e, item_blk, item_sub, l)
        ys = _moe_down(hs, w_down, b_down.reshape(depth, N_EXPERTS, 1, -1),
                       item_e, item_blk, item_sub, l)
        x2 = _combine_ln2(x1, ys, slot, gates, mod3, ln2_g.reshape(depth, 1, -1),
                          ln2_b.reshape(depth, 1, -1), l, seq, alpha)

    return x2.reshape(batch, seq, d)
```

```python
import functools
import math

import jax
import jax.numpy as jnp
from jax import lax
from jax.experimental import pallas as pl
from jax.experimental.pallas import tpu as pltpu

F32 = jnp.float32
BF16 = jnp.bfloat16

HEAD_DIM = 128
Q_LORA_RANK = 1536
KV_LORA_RANK = 512
QK_NOPE_DIM = 128
QK_ROPE_DIM = 64
V_HEAD_DIM = 128
ROPE_THETA = 10000.0
DIL_PATTERNS = ((128, 1), (512, 4), (2048, 16))
BLOCK = 128
REL_BUCKETS = 32
REL_MAX_EXACT = REL_BUCKETS // 2
REL_MAX_DISTANCE = 2048
N_EXPERTS = 32
TOP_K = 4
EXPERT_FF = 1536
SWIGLU_LIMIT = 7.0
SWIGLU_ALPHA = 1.702
LN_EPS = 1e-5
RMS_EPS = 1e-6
NEG_BIG = -1e30

LANES = 128
QK_PAD = 256
VMEM_LIMIT = 56 * 1024 * 1024
MOE_ROWS = 1280
MOE_SUB = 256


def _cparams(sem):
    return pltpu.CompilerParams(dimension_semantics=sem, vmem_limit_bytes=VMEM_LIMIT)


def _ada_kernel(c_ref, w_ref, b_ref, o_ref):
    c = c_ref[...]
    cond = (c * jax.nn.sigmoid(c)).astype(BF16)
    o_ref[...] = jnp.dot(cond, w_ref[...].astype(BF16), preferred_element_type=F32) + b_ref[...]


def _ada(c_pad, w_ada, b_ada3, l, tn=512):
    rows, d = c_pad.shape
    n = w_ada.shape[-1]
    return pl.pallas_call(
        _ada_kernel,
        grid=(n // tn,),
        in_specs=[
            pl.BlockSpec((rows, d), lambda j: (0, 0)),
            pl.BlockSpec((None, d, tn), lambda j: (l, 0, j)),
            pl.BlockSpec((None, 1, tn), lambda j: (l, 0, j)),
        ],
        out_specs=pl.BlockSpec((rows, tn), lambda j: (0, j)),
        out_shape=jax.ShapeDtypeStruct((rows, n), F32),
        compiler_params=_cparams(("arbitrary",)),
        name="ada_mod",
    )(c_pad, w_ada, b_ada3)


def _modulate_kernel(x_ref, mod_ref, o_ref, *, shift_row, scale_row):
    sh = mod_ref[shift_row:shift_row + 1, :]
    sc = mod_ref[scale_row:scale_row + 1, :]
    o_ref[...] = (x_ref[...] * (1.0 + sc) + sh).astype(BF16)


def _modulate(x2, mod3, seq, shift_row, scale_row, tm=512):
    t, d = x2.shape
    per_b = seq // tm
    return pl.pallas_call(
        functools.partial(_modulate_kernel, shift_row=shift_row, scale_row=scale_row),
        grid=(t // tm,),
        in_specs=[
            pl.BlockSpec((tm, d), lambda i: (i, 0)),
            pl.BlockSpec((None, 6, d), lambda i: (i // per_b, 0, 0)),
        ],
        out_specs=pl.BlockSpec((tm, d), lambda i: (i, 0)),
        out_shape=jax.ShapeDtypeStruct((t, d), BF16),
        compiler_params=_cparams(("arbitrary",)),
        name="modulate",
    )(x2, mod3)


def _rope_block(x, cos_t, sin_t):
    lane = lax.broadcasted_iota(jnp.int32, x.shape, 1)
    half = QK_ROPE_DIM // 2
    swapped = jnp.where(lane < half, pltpu.roll(x, LANES - half, 1), pltpu.roll(x, half, 1))
    return x * cos_t + swapped * sin_t


def _w_in_layout_kernel(w_ref, o_ref, *, mla_cols, dil_cols):
    rope0 = mla_cols
    dil0 = mla_cols + QK_ROPE_DIM
    o_ref[:, :mla_cols] = w_ref[:, :mla_cols].astype(BF16)
    o_ref[:, mla_cols:mla_cols + dil_cols] = w_ref[:, dil0:dil0 + dil_cols].astype(BF16)
    tail = w_ref[:, rope0:rope0 + LANES]
    lane = lax.broadcasted_iota(jnp.int32, tail.shape, 1)
    o_ref[:, mla_cols + dil_cols:] = jnp.where(lane < QK_ROPE_DIM, tail, 0.0).astype(BF16)


def _w_in_layout(w_in, l, mla_cols, dil_cols, tk=256):
    _, d, n = w_in.shape
    n_out = mla_cols + dil_cols + LANES
    return pl.pallas_call(
        functools.partial(_w_in_layout_kernel, mla_cols=mla_cols, dil_cols=dil_cols),
        grid=(d // tk,),
        in_specs=[pl.BlockSpec((None, tk, n), lambda i: (l, i, 0))],
        out_specs=pl.BlockSpec((tk, n_out), lambda i: (i, 0)),
        out_shape=jax.ShapeDtypeStruct((d, n_out), BF16),
        compiler_params=_cparams(("arbitrary",)),
        name="w_in_layout",
    )(w_in)


def _proj_plain_kernel(a_ref, w_ref, o_ref):
    o_ref[...] = jnp.dot(a_ref[...], w_ref[...], preferred_element_type=F32).astype(o_ref.dtype)


def _proj_heads_kernel(a_ref, w_ref, o_ref, *, heads_per_tile):
    res = jnp.dot(a_ref[...], w_ref[...], preferred_element_type=F32)
    for hh in range(heads_per_tile):
        o_ref[hh] = res[:, hh * HEAD_DIM:(hh + 1) * HEAD_DIM].astype(o_ref.dtype)


def _proj_rope_kernel(a_ref, w_ref, cos_ref, sin_ref, o_ref):
    res = jnp.dot(a_ref[...], w_ref[...], preferred_element_type=F32)
    o_ref[...] = _rope_block(res, cos_ref[...], sin_ref[...]).astype(o_ref.dtype)


def _in_proj_mla(u, w_p, n_cols, tm=512, tn=512):
    t, d = u.shape
    return pl.pallas_call(
        _proj_plain_kernel,
        grid=(n_cols // tn, t // tm),
        in_specs=[
            pl.BlockSpec((tm, d), lambda j, i: (i, 0)),
            pl.BlockSpec((d, tn), lambda j, i: (0, j)),
        ],
        out_specs=pl.BlockSpec((tm, tn), lambda j, i: (i, j)),
        out_shape=jax.ShapeDtypeStruct((t, n_cols), BF16),
        compiler_params=_cparams(("arbitrary", "arbitrary")),
        name="in_proj_mla",
    )(u, w_p)


def _in_proj_dil(u, w_p, col0, batch, seq, n_heads, tm=512, tn=512):
    t, d = u.shape
    hpt = tn // HEAD_DIM
    tiles_per_mat = n_heads // hpt
    per_b = seq // tm
    off = col0 // tn
    return pl.pallas_call(
        functools.partial(_proj_heads_kernel, heads_per_tile=hpt),
        grid=(3 * tiles_per_mat, t // tm),
        in_specs=[
            pl.BlockSpec((tm, d), lambda j, i: (i, 0)),
            pl.BlockSpec((d, tn), lambda j, i: (0, j + off)),
        ],
        out_specs=pl.BlockSpec(
            (None, None, hpt, tm, HEAD_DIM),
            lambda j, i: (j // tiles_per_mat, i // per_b, j % tiles_per_mat, i % per_b, 0)),
        out_shape=jax.ShapeDtypeStruct((3, batch, n_heads, seq, HEAD_DIM), BF16),
        compiler_params=_cparams(("arbitrary", "arbitrary")),
        name="in_proj_dil",
    )(u, w_p)


def _in_proj_krope(u, w_p, col0, cos_t, sin_t, tm=1024):
    t, d = u.shape
    off = col0 // LANES
    return pl.pallas_call(
        _proj_rope_kernel,
        grid=(t // tm,),
        in_specs=[
            pl.BlockSpec((tm, d), lambda i: (i, 0)),
            pl.BlockSpec((d, LANES), lambda i: (0, off)),
            pl.BlockSpec((tm, LANES), lambda i: (i, 0)),
            pl.BlockSpec((tm, LANES), lambda i: (i, 0)),
        ],
        out_specs=pl.BlockSpec((tm, LANES), lambda i: (i, 0)),
        out_shape=jax.ShapeDtypeStruct((t, LANES), BF16),
        compiler_params=_cparams(("arbitrary",)),
        name="in_proj_krope",
    )(u, w_p, cos_t, sin_t)


def _rms_bf16(h_ref, g_ref):
    hf = h_ref[...].astype(F32)
    y = hf * lax.rsqrt(jnp.mean(hf * hf, axis=-1, keepdims=True) + RMS_EPS)
    return (y * g_ref[...]).astype(BF16)


def _q_up_kernel(h_ref, g_ref, w_ref, cos_ref, sin_ref, o_ref, *, n_heads, scale):
    yb = _rms_bf16(h_ref, g_ref)
    cos_t = cos_ref[...]
    sin_t = sin_ref[...]
    group = 4
    for h0 in range(0, n_heads, group):
        res = jnp.dot(yb, w_ref[:, h0 * QK_PAD:(h0 + group) * QK_PAD], preferred_element_type=F32)
        for hh in range(group):
            c0 = hh * QK_PAD
            nope = res[:, c0:c0 + QK_NOPE_DIM]
            rp = _rope_block(res[:, c0 + QK_NOPE_DIM:c0 + QK_PAD], cos_t, sin_t)
            o0 = (h0 + hh) * QK_PAD
            o_ref[:, o0:o0 + QK_NOPE_DIM] = (nope * scale).astype(BF16)
            o_ref[:, o0 + QK_NOPE_DIM:o0 + QK_PAD] = (rp * scale).astype(BF16)


def _q_up(hm, g3, l, w_uq_p, cos_t, sin_t, n_heads, tm=256):
    t = hm.shape[0]
    scale = float(QK_NOPE_DIM + QK_ROPE_DIM) ** -0.5
    return pl.pallas_call(
        functools.partial(_q_up_kernel, n_heads=n_heads, scale=scale),
        grid=(t // tm,),
        in_specs=[
            pl.BlockSpec((tm, Q_LORA_RANK), lambda i: (i, 0)),
            pl.BlockSpec((None, 1, Q_LORA_RANK), lambda i: (l, 0, 0)),
            pl.BlockSpec((Q_LORA_RANK, n_heads * QK_PAD), lambda i: (0, 0)),
            pl.BlockSpec((tm, LANES), lambda i: (i, 0)),
            pl.BlockSpec((tm, LANES), lambda i: (i, 0)),
        ],
        out_specs=pl.BlockSpec((tm, n_heads * QK_PAD), lambda i: (i, 0)),
        out_shape=jax.ShapeDtypeStruct((t, n_heads * QK_PAD), BF16),
        compiler_params=_cparams(("arbitrary",)),
        name="mla_q_up",
    )(hm, g3, w_uq_p, cos_t, sin_t)


def _kv_up_kernel(h_ref, g_ref, wk_ref, wv_ref, kr_ref, k_ref, v_ref, *, n_heads):
    yb = _rms_bf16(h_ref, g_ref)
    kr = kr_ref[...]
    group = 4
    for h0 in range(0, n_heads, group):
        res = jnp.dot(yb, wk_ref[:, h0 * QK_NOPE_DIM:(h0 + group) * QK_NOPE_DIM],
                      preferred_element_type=F32)
        for hh in range(group):
            o0 = (h0 + hh) * QK_PAD
            k_ref[:, o0:o0 + QK_NOPE_DIM] = res[:, hh * QK_NOPE_DIM:(hh + 1) * QK_NOPE_DIM].astype(BF16)
            k_ref[:, o0 + QK_NOPE_DIM:o0 + QK_PAD] = kr
    v_ref[...] = jnp.dot(yb, wv_ref[...], preferred_element_type=F32).astype(BF16)


def _kv_up(hm, g3, l, w_uk, w_uv, krr, n_heads, tm=256):
    t = hm.shape[0]
    col_blk = Q_LORA_RANK // KV_LORA_RANK
    return pl.pallas_call(
        functools.partial(_kv_up_kernel, n_heads=n_heads),
        grid=(t // tm,),
        in_specs=[
            pl.BlockSpec((tm, KV_LORA_RANK), lambda i: (i, col_blk)),
            pl.BlockSpec((None, 1, KV_LORA_RANK), lambda i: (l, 0, 0)),
            pl.BlockSpec((KV_LORA_RANK, n_heads * QK_NOPE_DIM), lambda i: (0, 0)),
            pl.BlockSpec((KV_LORA_RANK, n_heads * V_HEAD_DIM), lambda i: (0, 0)),
            pl.BlockSpec((tm, LANES), lambda i: (i, 0)),
        ],
        out_specs=[
            pl.BlockSpec((tm, n_heads * QK_PAD), lambda i: (i, 0)),
            pl.BlockSpec((tm, n_heads * V_HEAD_DIM), lambda i: (i, 0)),
        ],
        out_shape=[
            jax.ShapeDtypeStruct((t, n_heads * QK_PAD), BF16),
            jax.ShapeDtypeStruct((t, n_heads * V_HEAD_DIM), BF16),
        ],
        compiler_params=_cparams(("arbitrary",)),
        name="mla_kv_up",
    )(hm, g3, w_uk, w_uv, krr)


MLA_HEADS_PER_STEP = 2


def _mla_attn_kernel(q_ref, k_ref, v_ref, o_ref, *, seq, tq):
    nq = seq // tq
    hp = MLA_HEADS_PER_STEP
    row = lax.broadcasted_iota(jnp.int32, (tq, tq), 0)
    col = lax.broadcasted_iota(jnp.int32, (tq, tq), 1)
    causal = col <= row

    def kv_step(qs, j, carry, masked):
        off = pl.multiple_of(j * tq, tq)
        out = []
        for hh in range(hp):
            m, l, acc = carry[hh]
            k = k_ref[pl.ds(off, tq), hh * QK_PAD:(hh + 1) * QK_PAD]
            v = v_ref[pl.ds(off, tq), hh * V_HEAD_DIM:(hh + 1) * V_HEAD_DIM]
            s = lax.dot_general(qs[hh], k, (((1,), (1,)), ((), ())), preferred_element_type=F32)
            if masked:
                s = jnp.where(causal, s, NEG_BIG)
            m_new = jnp.maximum(m, jnp.max(s, axis=-1, keepdims=True))
            p = jnp.exp(s - m_new)
            alpha = jnp.exp(m - m_new)
            l_new = alpha * l + jnp.sum(p, axis=-1, keepdims=True)
            acc_new = alpha * acc + jnp.dot(p.astype(BF16), v, preferred_element_type=F32)
            out.append((m_new, l_new, acc_new))
        return tuple(out)

    def q_loop(i, _):
        qoff = pl.multiple_of(i * tq, tq)
        qs = [q_ref[pl.ds(qoff, tq), hh * QK_PAD:(hh + 1) * QK_PAD] for hh in range(hp)]
        init = tuple((jnp.full((tq, 1), NEG_BIG, F32), jnp.zeros((tq, 1), F32),
                      jnp.zeros((tq, V_HEAD_DIM), F32)) for _ in range(hp))
        carry = lax.fori_loop(0, i, lambda j, c: kv_step(qs, j, c, False), init)
        carry = kv_step(qs, i, carry, True)
        for hh in range(hp):
            m, l, acc = carry[hh]
            o_ref[pl.ds(qoff, tq), hh * V_HEAD_DIM:(hh + 1) * V_HEAD_DIM] = (acc / l).astype(o_ref.dtype)
        return 0

    lax.fori_loop(0, nq, q_loop, 0)


def _mla_attn(q, k, v, batch, seq, n_heads, tq=512):
    t = q.shape[0]
    hp = MLA_HEADS_PER_STEP
    return pl.pallas_call(
        functools.partial(_mla_attn_kernel, seq=seq, tq=tq),
        grid=(batch, n_heads // hp),
        in_specs=[
            pl.BlockSpec((seq, hp * QK_PAD), lambda b, h: (b, h)),
            pl.BlockSpec((seq, hp * QK_PAD), lambda b, h: (b, h)),
            pl.BlockSpec((seq, hp * V_HEAD_DIM), lambda b, h: (b, h)),
        ],
        out_specs=pl.BlockSpec((seq, hp * V_HEAD_DIM), lambda b, h: (b, h)),
        out_shape=jax.ShapeDtypeStruct((t, n_heads * V_HEAD_DIM), BF16),
        compiler_params=_cparams(("arbitrary", "arbitrary")),
        name="mla_attn",
    )(q, k, v)


DIL_GROUP = 4


def _dil_attn_kernel(v1_ref, v4_ref, v16_ref, bvec_ref, o_ref, m_s, l_s, acc_s, bias_s, *, seq):
    scale = float(HEAD_DIM) ** -0.5
    views = (v1_ref, v4_ref, v16_ref)

    for bi in range(len(DIL_PATTERNS)):
        full = jnp.broadcast_to(bvec_ref[bi], (BLOCK, 2 * BLOCK))
        bias_s[bi] = pltpu.roll(full, 0, 1, stride=1, stride_axis=0)

    def scores(ref, bi, d, r, i, first):
        lanes = slice(r * HEAD_DIM, (r + 1) * HEAD_DIM)
        qoff = pl.multiple_of(i * BLOCK, BLOCK)
        q = ref[0, pl.ds(qoff, BLOCK), lanes]
        if first:
            kk = ref[1, pl.ds(0, BLOCK), lanes]
            vv = ref[2, pl.ds(0, BLOCK), lanes]
            bias = bias_s[bi, :, BLOCK:2 * BLOCK]
        else:
            koff = pl.multiple_of(i * BLOCK - BLOCK, BLOCK)
            kk = ref[1, pl.ds(koff, 2 * BLOCK), lanes]
            vv = ref[2, pl.ds(koff, 2 * BLOCK), lanes]
            bias = bias_s[bi]
        s = lax.dot_general(q, kk, (((1,), (1,)), ((), ())), preferred_element_type=F32)
        s = s * scale + bias
        m_b = jnp.max(s, axis=-1, keepdims=True)
        p = jnp.exp(s - m_b)
        l_b = jnp.sum(p, axis=-1, keepdims=True)
        a_b = jnp.dot(p.astype(BF16), vv, preferred_element_type=F32)
        m_b = jnp.broadcast_to(m_b, (BLOCK, HEAD_DIM))
        l_b = jnp.broadcast_to(l_b, (BLOCK, HEAD_DIM))
        if d == 1:
            rows = pl.ds(qoff, BLOCK)
        else:
            rows = pl.ds(i * (BLOCK * d) + r, BLOCK, stride=d)
        return m_b, l_b, a_b, rows

    def group(ref, bi, d, blocks):
        parts = [scores(ref, bi, d, r, i, first) for r, i, first in blocks]
        if d == 1:
            for m_b, l_b, a_b, rows in parts:
                m_s[rows, :] = m_b
                l_s[rows, :] = l_b
                acc_s[rows, :] = a_b
            return
        old = [(m_s[rows, :], l_s[rows, :], acc_s[rows, :]) for _, _, _, rows in parts]
        new = []
        for (m_b, l_b, a_b, rows), (m_o, l_o, a_o) in zip(parts, old):
            m_n = jnp.maximum(m_o, m_b)
            e_o = jnp.exp(m_o - m_n)
            e_b = jnp.exp(m_b - m_n)
            new.append((m_n, e_o * l_o + e_b * l_b, e_o * a_o + e_b * a_b, rows))
        for m_n, l_n, a_n, rows in new:
            m_s[rows, :] = m_n
            l_s[rows, :] = l_n
            acc_s[rows, :] = a_n

    for bi, (_, d) in enumerate(DIL_PATTERNS):
        ref = views[bi]
        nblk = seq // d // BLOCK
        if d == 1:
            assert nblk % DIL_GROUP == 0
            group(ref, bi, d, [(0, u, u == 0) for u in range(DIL_GROUP)])

            def body1(g, _, ref=ref, bi=bi, d=d):
                group(ref, bi, d, [(0, g * DIL_GROUP + u, False) for u in range(DIL_GROUP)])
                return 0

            lax.fori_loop(1, nblk // DIL_GROUP, body1, 0)
        else:
            assert d % DIL_GROUP == 0
            for r0 in range(0, d, DIL_GROUP):
                group(ref, bi, d, [(r0 + u, 0, True) for u in range(DIL_GROUP)])

                def body(i, _, ref=ref, bi=bi, d=d, r0=r0):
                    group(ref, bi, d, [(r0 + u, i, False) for u in range(DIL_GROUP)])
                    return 0

                lax.fori_loop(1, nblk, body, 0)

    o_ref[...] = (acc_s[...] / l_s[...]).astype(o_ref.dtype)


def _dil_attn(qkv, bias_vec, batch, seq, n_heads):
    t = batch * seq
    views = []
    specs = []
    for _, d in DIL_PATTERNS:
        views.append(qkv.reshape(3, batch, n_heads, seq // d, d * HEAD_DIM))
        specs.append(pl.BlockSpec((3, None, None, seq // d, d * HEAD_DIM),
                                  lambda b, h: (0, b, h, 0, 0)))
    nbr = len(DIL_PATTERNS)
    specs.append(pl.BlockSpec((nbr, None, 1, 2 * BLOCK), lambda b, h: (0, h, 0, 0)))
    return pl.pallas_call(
        functools.partial(_dil_attn_kernel, seq=seq),
        grid=(batch, n_heads),
        in_specs=specs,
        out_specs=pl.BlockSpec((seq, HEAD_DIM), lambda b, h: (b, h)),
        out_shape=jax.ShapeDtypeStruct((t, n_heads * HEAD_DIM), BF16),
        scratch_shapes=[pltpu.VMEM((seq, HEAD_DIM), F32)] * 3
        + [pltpu.VMEM((nbr, BLOCK, 2 * BLOCK), F32)],
        compiler_params=_cparams(("arbitrary", "arbitrary")),
        name="dil_attn",
    )(*views, bias_vec)


def _out_proj_kernel(a1_ref, a2_ref, w_ref, o_ref, wb_ref):
    @pl.when(pl.program_id(1) == 0)
    def _():
        wb_ref[...] = w_ref[...].astype(BF16)

    k1 = a1_ref.shape[1]
    acc = jnp.dot(a1_ref[...], wb_ref[:k1, :], preferred_element_type=F32)
    acc = acc + jnp.dot(a2_ref[...], wb_ref[k1:, :], preferred_element_type=F32)
    o_ref[...] = acc.astype(o_ref.dtype)


def _out_proj(o_a, o_b, w_o, l, tm=512, tn=512):
    t, k1 = o_a.shape
    k2 = o_b.shape[1]
    n = w_o.shape[-1]
    return pl.pallas_call(
        _out_proj_kernel,
        grid=(n // tn, t // tm),
        in_specs=[
            pl.BlockSpec((tm, k1), lambda j, i: (i, 0)),
            pl.BlockSpec((tm, k2), lambda j, i: (i, 0)),
            pl.BlockSpec((None, k1 + k2, tn), lambda j, i: (l, 0, j)),
        ],
        out_specs=pl.BlockSpec((tm, tn), lambda j, i: (i, j)),
        out_shape=jax.ShapeDtypeStruct((t, n), BF16),
        scratch_shapes=[pltpu.VMEM((k1 + k2, tn), BF16)],
        compiler_params=_cparams(("arbitrary", "arbitrary")),
        name="out_proj",
    )(o_a, o_b, w_o)


def _layer_norm_rows(z, g, b):
    mu = jnp.mean(z, axis=-1, keepdims=True)
    zc = z - mu
    var = jnp.mean(zc * zc, axis=-1, keepdims=True)
    return zc * lax.rsqrt(var + LN_EPS) * g + b


SLAB_CHUNK = 1024
SLAB_Q = SLAB_CHUNK // (2 * LANES)


def _slab_rows(width):
    return width // (2 * LANES)


def _pack_chunk(vals):
    bits = lax.bitcast_convert_type(vals.astype(BF16).astype(F32), jnp.uint32)
    half = SLAB_CHUNK // 2
    return [(bits[:, q * LANES:(q + 1) * LANES] >> 16)
            | bits[:, half + q * LANES:half + (q + 1) * LANES] for q in range(SLAB_Q)]


def _unpack_words(words):
    lo = lax.bitcast_convert_type(words << 16, F32)
    hi = lax.bitcast_convert_type(words & jnp.uint32(0xFFFF0000), F32)
    return lo, hi


def _ln1_router_kernel(x_ref, mix_ref, mod_ref, g_ref, b_ref, wr_ref, br_ref,
                       x1_ref, u2p_ref, ids_ref, gates_ref, *, alpha):
    gt1 = mod_ref[2:3, :]
    sh2 = mod_ref[3:4, :]
    sc2 = mod_ref[4:5, :]
    z = alpha * x_ref[...] + gt1 * mix_ref[...].astype(F32)
    x1 = _layer_norm_rows(z, g_ref[...], b_ref[...])
    x1_ref[...] = x1
    u2 = (x1 * (1.0 + sc2) + sh2).astype(BF16)
    tm, d = u2.shape
    sr = _slab_rows(d)
    for g in range(d // SLAB_CHUNK):
        for q, words in enumerate(_pack_chunk(u2[:, g * SLAB_CHUNK:(g + 1) * SLAB_CHUNK])):
            u2p_ref[pl.ds(g * SLAB_Q + q, tm, stride=sr), :] = words
    logits = jnp.dot(u2, wr_ref[...].astype(BF16), preferred_element_type=F32) + br_ref[...]
    lane = lax.broadcasted_iota(jnp.int32, logits.shape, 1)
    lane_f = lane.astype(F32)
    vals = []
    ids = []
    for _ in range(TOP_K):
        mk = jnp.max(logits, axis=-1, keepdims=True)
        idx_f = jnp.min(jnp.where(logits == mk, lane_f, float(LANES)), axis=-1, keepdims=True)
        vals.append(mk)
        ids.append(idx_f.astype(jnp.int32))
        logits = jnp.where(lane_f == idx_f, -jnp.inf, logits)
    exps = [jnp.exp(v - vals[0]) for v in vals]
    denom = exps[0]
    for e in exps[1:]:
        denom = denom + e
    ids_out = jnp.zeros(lane.shape, jnp.int32)
    gates_out = jnp.zeros(lane.shape, F32)
    for k in range(TOP_K):
        ids_out = jnp.where(lane == k, ids[k], ids_out)
        gates_out = jnp.where(lane == k, exps[k] / denom, gates_out)
    ids_ref[...] = ids_out
    gates_ref[...] = gates_out


def _ln1_router(x2, mix, mod3, g3, b3, wr_p, br_p, l, seq, alpha, tm=256):
    t, d = x2.shape
    per_b = seq // tm
    return pl.pallas_call(
        functools.partial(_ln1_router_kernel, alpha=alpha),
        grid=(t // tm,),
        in_specs=[
            pl.BlockSpec((tm, d), lambda i: (i, 0)),
            pl.BlockSpec((tm, d), lambda i: (i, 0)),
            pl.BlockSpec((None, 6, d), lambda i: (i // per_b, 0, 0)),
            pl.BlockSpec((None, 1, d), lambda i: (l, 0, 0)),
            pl.BlockSpec((None, 1, d), lambda i: (l, 0, 0)),
            pl.BlockSpec((d, LANES), lambda i: (0, 0)),
            pl.BlockSpec((1, LANES), lambda i: (0, 0)),
        ],
        out_specs=[
            pl.BlockSpec((tm, d), lambda i: (i, 0)),
            pl.BlockSpec((tm * _slab_rows(d), LANES), lambda i: (i, 0)),
            pl.BlockSpec((tm, LANES), lambda i: (i, 0)),
            pl.BlockSpec((tm, LANES), lambda i: (i, 0)),
        ],
        out_shape=[
            jax.ShapeDtypeStruct((t, d), F32),
            jax.ShapeDtypeStruct((t * _slab_rows(d), LANES), jnp.uint32),
            jax.ShapeDtypeStruct((t, LANES), jnp.int32),
            jax.ShapeDtypeStruct((t, LANES), F32),
        ],
        compiler_params=_cparams(("arbitrary",)),
        name="ln1_router",
    )(x2, mix, mod3, g3, b3, wr_p, br_p)


def _moe_up_kernel(item_e, item_blk, item_sub, x_ref, wg_ref, wu_ref, bg_ref, bu_ref,
                   h_ref, wg_s, wu_s):
    w = pl.program_id(0)
    nsub = item_sub[w]

    @pl.when(nsub > 0)
    def _():
        wg_s[...] = wg_ref[...].astype(BF16)
        wu_s[...] = wu_ref[...].astype(BF16)
        bg = bg_ref[...]
        bu = bu_ref[...]

        def body(j, _):
            off = pl.multiple_of(j * MOE_SUB, MOE_SUB)
            xt = x_ref[pl.ds(off, MOE_SUB), :]
            glu = jnp.dot(xt, wg_s[...], preferred_element_type=F32) + bg
            lin = jnp.dot(xt, wu_s[...], preferred_element_type=F32) + bu
            glu = jnp.minimum(glu, SWIGLU_LIMIT)
            lin = jnp.clip(lin, -SWIGLU_LIMIT, SWIGLU_LIMIT)
            act = glu * jax.nn.sigmoid(SWIGLU_ALPHA * glu) * (lin + 1.0)
            h_ref[pl.ds(off, MOE_SUB), :] = act.astype(h_ref.dtype)
            return 0

        lax.fori_loop(0, nsub, body, 0)


def _moe_up(xs, w_gate, w_up, b_gate4, b_up4, item_e, item_blk, item_sub, l, tf=256):
    r, d = xs.shape
    n_items = r // MOE_ROWS
    ff = w_gate.shape[-1]
    n_chunks = ff // tf

    def cmap(w, c, isub):
        return jnp.where(isub[w] > 0, c, n_chunks - 1)

    def wmap(w, c, ie, ib, isub):
        return (l, ie[w], 0, cmap(w, c, isub))

    grid_spec = pltpu.PrefetchScalarGridSpec(
        num_scalar_prefetch=3,
        grid=(n_items, n_chunks),
        in_specs=[
            pl.BlockSpec((MOE_ROWS, d), lambda w, c, ie, ib, isub: (ib[w], 0)),
            pl.BlockSpec((None, None, d, tf), wmap),
            pl.BlockSpec((None, None, d, tf), wmap),
            pl.BlockSpec((None, None, 1, tf), wmap),
            pl.BlockSpec((None, None, 1, tf), wmap),
        ],
        out_specs=pl.BlockSpec(
            (MOE_ROWS, tf), lambda w, c, ie, ib, isub: (ib[w], cmap(w, c, isub))),
        scratch_shapes=[pltpu.VMEM((d, tf), BF16), pltpu.VMEM((d, tf), BF16)],
    )
    return pl.pallas_call(
        _moe_up_kernel,
        grid_spec=grid_spec,
        out_shape=jax.ShapeDtypeStruct((r, ff), BF16),
        compiler_params=_cparams(("arbitrary", "arbitrary")),
        name="moe_up",
    )(item_e, item_blk, item_sub, xs, w_gate, w_up, b_gate4, b_up4)


def _moe_down_kernel(item_e, item_blk, item_sub, h_ref, wd_ref, bd_ref, y_ref, wd_s, *, sr):
    w = pl.program_id(0)
    c = pl.program_id(1)
    nsub = item_sub[w]

    @pl.when(nsub > 0)
    def _():
        wd_s[...] = wd_ref[...].astype(BF16)
        bd = bd_ref[...]

        def body(j, _):
            off = pl.multiple_of(j * MOE_SUB, MOE_SUB)
            ht = h_ref[pl.ds(off, MOE_SUB), :]
            y = jnp.dot(ht, wd_s[...], preferred_element_type=F32) + bd
            for q, words in enumerate(_pack_chunk(y)):
                y_ref[pl.ds(off * sr + c * SLAB_Q + q, MOE_SUB, stride=sr), :] = words
            return 0

        lax.fori_loop(0, nsub, body, 0)


def _moe_down(hs, w_down, b_down4, item_e, item_blk, item_sub, l):
    r, ff = hs.shape
    n_items = r // MOE_ROWS
    d = w_down.shape[-1]
    tn = SLAB_CHUNK
    n_chunks = d // tn
    sr = _slab_rows(d)

    def cmap(w, c, isub):
        return jnp.where(isub[w] > 0, c, n_chunks - 1)

    def wmap(w, c, ie, ib, isub):
        return (l, ie[w], 0, cmap(w, c, isub))

    grid_spec = pltpu.PrefetchScalarGridSpec(
        num_scalar_prefetch=3,
        grid=(n_items, n_chunks),
        in_specs=[
            pl.BlockSpec((MOE_ROWS, ff), lambda w, c, ie, ib, isub: (ib[w], 0)),
            pl.BlockSpec((None, None, ff, tn), wmap),
            pl.BlockSpec((None, None, 1, tn), wmap),
        ],
        out_specs=pl.BlockSpec((MOE_ROWS * sr, LANES), lambda w, c, ie, ib, isub: (ib[w], 0)),
        scratch_shapes=[pltpu.VMEM((ff, tn), BF16)],
    )
    return pl.pallas_call(
        functools.partial(_moe_down_kernel, sr=sr),
        grid_spec=grid_spec,
        out_shape=jax.ShapeDtypeStruct((r * sr, LANES), jnp.uint32),
        compiler_params=_cparams(("arbitrary", "arbitrary")),
        name="moe_down",
    )(item_e, item_blk, item_sub, hs, w_down, b_down4)


GATHER_UNROLL = 8


def _slab_gather(idx_ref, src_ref, buf, sem, base_row, n, sr, wait):
    def copy(r):
        src_row = pl.multiple_of(idx_ref[0, r] * sr, sr)
        dst_row = pl.multiple_of((base_row + r) * sr, sr)
        return pltpu.make_async_copy(src_ref.at[pl.ds(src_row, sr), :],
                                     buf.at[pl.ds(dst_row, sr), :], sem)

    def body(r0, _):
        for u in range(GATHER_UNROLL):
            c = copy(r0 * GATHER_UNROLL + u)
            if wait:
                c.wait()
            else:
                c.start()
        return 0

    lax.fori_loop(0, n // GATHER_UNROLL, body, 0)


def _dispatch_kernel(blk_valid, blk_out, tok_ref, tok_next_ref, src_ref, o_ref, buf, sems,
                     *, rows, sr):
    i = pl.program_id(0)
    n = pl.num_programs(0)
    slot = i % 2
    nxt = jnp.minimum(i + 1, n - 1)

    @pl.when(i == 0)
    def _():
        _slab_gather(tok_ref, src_ref, buf, sems.at[0], 0, rows, sr, wait=False)

    @pl.when((i + 1 < n) & (blk_valid[nxt] > 0))
    def _():
        _slab_gather(tok_next_ref, src_ref, buf, sems.at[1 - slot], (1 - slot) * rows, rows, sr,
                     wait=False)

    @pl.when((blk_valid[i] > 0) | (i == 0))
    def _():
        _slab_gather(tok_ref, src_ref, buf, sems.at[slot], slot * rows, rows, sr, wait=True)
        half = SLAB_CHUNK // 2
        for g in range(sr // SLAB_Q):
            for q in range(SLAB_Q):
                start = slot * (rows * sr) + (g * SLAB_Q + q)
                lo, hi = _unpack_words(buf[pl.ds(start, rows, stride=sr), :])
                c0 = g * SLAB_CHUNK + q * LANES
                o_ref[:, c0:c0 + LANES] = lo.astype(BF16)
                o_ref[:, c0 + half:c0 + half + LANES] = hi.astype(BF16)


def _dispatch(u2p, d, row_tok, blk_valid, blk_out, rows=MOE_SUB):
    sr = _slab_rows(d)
    r = row_tok.shape[0]
    n_blk = r // rows
    tok3 = row_tok.reshape(n_blk, 1, rows)
    grid_spec = pltpu.PrefetchScalarGridSpec(
        num_scalar_prefetch=2,
        grid=(n_blk,),
        in_specs=[
            pl.BlockSpec((None, 1, rows), lambda i, bv, bo: (i, 0, 0), memory_space=pltpu.SMEM),
            pl.BlockSpec((None, 1, rows), lambda i, bv, bo: (jnp.minimum(i + 1, n_blk - 1), 0, 0),
                         memory_space=pltpu.SMEM),
            pl.BlockSpec(memory_space=pl.ANY),
        ],
        out_specs=pl.BlockSpec((rows, d), lambda i, bv, bo: (bo[i], 0)),
        scratch_shapes=[pltpu.VMEM((2 * rows * sr, LANES), jnp.uint32),
                        pltpu.SemaphoreType.DMA((2,))],
    )
    return pl.pallas_call(
        functools.partial(_dispatch_kernel, rows=rows, sr=sr),
        grid_spec=grid_spec,
        out_shape=jax.ShapeDtypeStruct((r, d), BF16),
        compiler_params=_cparams(("arbitrary",)),
        name="moe_dispatch",
    )(blk_valid, blk_out, tok3, tok3, u2p)


def _combine_ln2_kernel(slot_ref, slot_next_ref, x1_ref, ys_ref, gates_ref, mod_ref, g_ref, b_ref,
                        o_ref, buf, ffn_s, sems, *, alpha, tm, sr):
    i = pl.program_id(0)
    n = pl.num_programs(0)
    cur = i % 2
    n_rows = TOP_K * tm

    @pl.when(i == 0)
    def _():
        _slab_gather(slot_ref, ys_ref, buf, sems.at[0], 0, n_rows, sr, wait=False)

    @pl.when(i + 1 < n)
    def _():
        _slab_gather(slot_next_ref, ys_ref, buf, sems.at[1 - cur], (1 - cur) * n_rows, n_rows, sr,
                     wait=False)

    _slab_gather(slot_ref, ys_ref, buf, sems.at[cur], cur * n_rows, n_rows, sr, wait=True)

    gates = gates_ref[...]
    gk = [jnp.broadcast_to(gates[:, k:k + 1], (tm, LANES)) for k in range(TOP_K)]
    half = SLAB_CHUNK // 2
    for g in range(sr // SLAB_Q):
        for q in range(SLAB_Q):
            lo_acc = None
            hi_acc = None
            for k in range(TOP_K):
                start = (cur * n_rows + k * tm) * sr + g * SLAB_Q + q
                lo, hi = _unpack_words(buf[pl.ds(start, tm, stride=sr), :])
                lo_acc = gk[k] * lo if lo_acc is None else lo_acc + gk[k] * lo
                hi_acc = gk[k] * hi if hi_acc is None else hi_acc + gk[k] * hi
            c0 = g * SLAB_CHUNK + q * LANES
            ffn_s[:, c0:c0 + LANES] = lo_acc
            ffn_s[:, c0 + half:c0 + half + LANES] = hi_acc
    gt2 = mod_ref[5:6, :]
    z = alpha * x1_ref[...] + gt2 * ffn_s[...]
    o_ref[...] = _layer_norm_rows(z, g_ref[...], b_ref[...])


def _combine_ln2(x1, ys, slot, gates, mod3, g3, b3, l, seq, alpha, tm=256):
    t, d = x1.shape
    per_b = seq // tm
    sr = _slab_rows(d)
    n_tiles = t // tm
    slot3 = slot.reshape(n_tiles, tm, TOP_K).transpose(0, 2, 1).reshape(n_tiles, 1, TOP_K * tm)
    return pl.pallas_call(
        functools.partial(_combine_ln2_kernel, alpha=alpha, tm=tm, sr=sr),
        grid=(n_tiles,),
        in_specs=[
            pl.BlockSpec((None, 1, tm * TOP_K), lambda i: (i, 0, 0), memory_space=pltpu.SMEM),
            pl.BlockSpec((None, 1, tm * TOP_K), lambda i: (jnp.minimum(i + 1, n_tiles - 1), 0, 0),
                         memory_space=pltpu.SMEM),
            pl.BlockSpec((tm, d), lambda i: (i, 0)),
            pl.BlockSpec(memory_space=pl.ANY),
            pl.BlockSpec((tm, LANES), lambda i: (i, 0)),
            pl.BlockSpec((None, 6, d), lambda i: (i // per_b, 0, 0)),
            pl.BlockSpec((None, 1, d), lambda i: (l, 0, 0)),
            pl.BlockSpec((None, 1, d), lambda i: (l, 0, 0)),
        ],
        out_specs=pl.BlockSpec((tm, d), lambda i: (i, 0)),
        out_shape=jax.ShapeDtypeStruct((t, d), F32),
        scratch_shapes=[pltpu.VMEM((2 * TOP_K * tm * sr, LANES), jnp.uint32),
                        pltpu.VMEM((tm, d), F32),
                        pltpu.SemaphoreType.DMA((2,))],
        compiler_params=_cparams(("arbitrary",)),
        name="combine_ln2",
    )(slot3, slot3, x1, ys, gates, mod3, g3, b3)


def _t5_bucket(dist):
    is_small = dist < REL_MAX_EXACT
    nf = jnp.maximum(dist, REL_MAX_EXACT).astype(F32)
    large = REL_MAX_EXACT + (jnp.log(nf / REL_MAX_EXACT) / math.log(REL_MAX_DISTANCE / REL_MAX_EXACT)
                             * (REL_BUCKETS - REL_MAX_EXACT)).astype(jnp.int32)
    large = jnp.minimum(large, REL_BUCKETS - 1)
    return jnp.where(is_small, dist, large)


def _dilated_bias_vectors(rel_bias):
    m = jnp.arange(2 * BLOCK)
    rel = BLOCK - m
    vecs = []
    for window, d in DIL_PATTERNS:
        valid = (rel >= 0) & (rel <= window // d)
        bucket = _t5_bucket(jnp.clip(rel, 0) * d)
        bias = jnp.transpose(rel_bias[bucket].astype(F32), (1, 0))
        vecs.append(jnp.where(valid[None, :], bias, NEG_BIG))
    return jnp.stack(vecs, axis=0)[:, :, None, :]


def _routing_tables(ids4, n_items_max):
    e = ids4.reshape(-1)
    onehot = (e[:, None] == jnp.arange(N_EXPERTS, dtype=jnp.int32)[None, :]).astype(jnp.int32)
    csum = jnp.cumsum(onehot, axis=0)
    rank = jnp.sum(onehot * csum, axis=1) - 1
    counts = csum[-1]
    nit = (counts + MOE_ROWS - 1) // MOE_ROWS
    cum_items = jnp.cumsum(nit)
    first_item = cum_items - nit
    slot = first_item[e] * MOE_ROWS + rank
    total = cum_items[-1]
    w = jnp.arange(n_items_max, dtype=jnp.int32)
    e_w = jnp.minimum(jnp.searchsorted(cum_items, w, side="right"), N_EXPERTS - 1).astype(jnp.int32)
    j_w = w - first_item[e_w]
    rows_w = jnp.clip(counts[e_w] - j_w * MOE_ROWS, 0, MOE_ROWS)
    valid = w < total
    item_sub = jnp.where(valid, (rows_w + MOE_SUB - 1) // MOE_SUB, 0).astype(jnp.int32)
    last = total - 1
    item_e = jnp.where(valid, e_w, e_w[last]).astype(jnp.int32)
    item_blk = jnp.where(valid, w, last).astype(jnp.int32)
    per_item = MOE_ROWS // MOE_SUB
    blk = jnp.arange(n_items_max * per_item, dtype=jnp.int32)
    blk_valid = ((blk % per_item) < item_sub[blk // per_item]).astype(jnp.int32)
    blk_out = lax.cummax(jnp.where(blk_valid > 0, blk, 0), axis=0).astype(jnp.int32)
    return slot.astype(jnp.int32), item_e, item_blk, item_sub, blk_valid, blk_out


def kernel(x, c, positions, w_ada, b_ada, w_in, g_q, g_kv, w_uq, w_ukv, rel_bias, w_o,
           ln1_g, ln1_b, w_router, b_router, w_gate, b_gate, w_up, b_up, w_down, b_down,
           ln2_g, ln2_b):
    batch, seq, d = x.shape
    depth = w_ada.shape[0]
    t = batch * seq
    n_heads = d // (2 * HEAD_DIM)
    dil_w = n_heads * HEAD_DIM
    alpha = (2.0 * depth) ** 0.25
    assert seq % (BLOCK * max(dd for _, dd in DIL_PATTERNS)) == 0
    assert all(win // dd == BLOCK for win, dd in DIL_PATTERNS)

    inv_freq = 1.0 / (ROPE_THETA ** (jnp.arange(0, QK_ROPE_DIM, 2, dtype=F32) / QK_ROPE_DIM))
    ang = positions.astype(F32)[..., None] * inv_freq
    cos, sin = jnp.cos(ang).reshape(t, -1), jnp.sin(ang).reshape(t, -1)
    zpad = jnp.zeros((t, LANES - QK_ROPE_DIM), F32)
    cos_t = jnp.concatenate([cos, cos, zpad], axis=1)
    sin_t = jnp.concatenate([-sin, sin, zpad], axis=1)

    bias_vec = _dilated_bias_vectors(rel_bias)
    c_pad = jnp.zeros((16, d), F32).at[:batch].set(c)

    x2 = x.reshape(t, d)
    n_items_max = (t * TOP_K) // MOE_ROWS + N_EXPERTS
    mla_cols = Q_LORA_RANK + KV_LORA_RANK

    for l in range(depth):
        mod = _ada(c_pad, w_ada, b_ada.reshape(depth, 1, -1), l)[:batch]
        mod3 = mod.reshape(batch, 6, d)

        w_in_p = _w_in_layout(w_in, l, mla_cols, 3 * dil_w)
        wq = w_uq[l].reshape(Q_LORA_RANK, n_heads, QK_NOPE_DIM + QK_ROPE_DIM)
        w_uq_p = jnp.pad(wq, ((0, 0), (0, 0), (0, QK_PAD - QK_NOPE_DIM - QK_ROPE_DIM))
                         ).reshape(Q_LORA_RANK, n_heads * QK_PAD).astype(BF16)
        wkv = w_ukv[l].reshape(KV_LORA_RANK, n_heads, QK_NOPE_DIM + V_HEAD_DIM)
        w_uk = wkv[:, :, :QK_NOPE_DIM].reshape(KV_LORA_RANK, -1).astype(BF16)
        w_uv = wkv[:, :, QK_NOPE_DIM:].reshape(KV_LORA_RANK, -1).astype(BF16)

        u1 = _modulate(x2, mod3, seq, 0, 1)
        hm = _in_proj_mla(u1, w_in_p, mla_cols)
        qkv_d = _in_proj_dil(u1, w_in_p, mla_cols, batch, seq, n_heads)
        krr = _in_proj_krope(u1, w_in_p, mla_cols + 3 * dil_w, cos_t, sin_t)
        q_a = _q_up(hm, g_q.reshape(depth, 1, -1), l, w_uq_p, cos_t, sin_t, n_heads)
        k_a, v_a = _kv_up(hm, g_kv.reshape(depth, 1, -1), l, w_uk, w_uv, krr, n_heads)
        o_a = _mla_attn(q_a, k_a, v_a, batch, seq, n_heads)
        o_b = _dil_attn(qkv_d, bias_vec, batch, seq, n_heads)
        mix = _out_proj(o_a, o_b, w_o, l)

        wr_p = jnp.pad(w_router[l], ((0, 0), (0, LANES - N_EXPERTS)))
        br_p = jnp.concatenate([b_router[l], jnp.full((LANES - N_EXPERTS,), NEG_BIG, F32)])[None, :]
        x1, u2p, ids, gates = _ln1_router(
            x2, mix, mod3, ln1_g.reshape(depth, 1, -1), ln1_b.reshape(depth, 1, -1),
            wr_p, br_p, l, seq, alpha)

        slot, item_e, item_blk, item_sub, blk_valid, blk_out = _routing_tables(
            ids[:, :TOP_K], n_items_max)
        tok = jnp.arange(t * TOP_K, dtype=jnp.int32) // TOP_K
        row_tok = jnp.zeros((n_items_max * MOE_ROWS,), jnp.int32).at[slot].set(tok)
        xs = _dispatch(u2p, d, row_tok, blk_valid, blk_out)
        hs = _moe_up(xs, w_gate, w_up, b_gate.reshape(depth, N_EXPERTS, 1, -1),
                     b_up.reshape(depth, N_EXPERTS, 1, -1), item_e, item_blk, item_sub, l)
        ys = _moe_down(hs, w_down, b_down.reshape(depth, N_EXPERTS, 1, -1),
                       item_e, item_blk, item_sub, l)
        x2 = _combine_ln2(x1, ys, slot, gates, mod3, ln2_g.reshape(depth, 1, -1),
                          ln2_b.reshape(depth, 1, -1), l, seq, alpha)

    return x2.reshape(batch, seq, d)
```

```python
import functools
import math

import jax
import jax.numpy as jnp
from jax import lax
from jax.experimental import pallas as pl
from jax.experimental.pallas import tpu as pltpu

F32 = jnp.float32
BF16 = jnp.bfloat16

HEAD_DIM = 128
Q_LORA_RANK = 1536
KV_LORA_RANK = 512
QK_NOPE_DIM = 128
QK_ROPE_DIM = 64
V_HEAD_DIM = 128
ROPE_THETA = 10000.0
DIL_PATTERNS = ((128, 1), (512, 4), (2048, 16))
BLOCK = 128
REL_BUCKETS = 32
REL_MAX_EXACT = REL_BUCKETS // 2
REL_MAX_DISTANCE = 2048
N_EXPERTS = 32
TOP_K = 4
EXPERT_FF = 1536
SWIGLU_LIMIT = 7.0
SWIGLU_ALPHA = 1.702
LN_EPS = 1e-5
RMS_EPS = 1e-6
NEG_BIG = -1e30

LANES = 128
QK_PAD = 256
VMEM_LIMIT = 56 * 1024 * 1024
MOE_ROWS = 1280
MOE_SUB = 256


def _cparams(sem):
    return pltpu.CompilerParams(dimension_semantics=sem, vmem_limit_bytes=VMEM_LIMIT)


def _ada_kernel(c_ref, w_ref, b_ref, o_ref):
    c = c_ref[...]
    cond = (c * jax.nn.sigmoid(c)).astype(BF16)
    o_ref[...] = jnp.dot(cond, w_ref[...].astype(BF16), preferred_element_type=F32) + b_ref[...]


def _ada(c_pad, w_ada, b_ada3, l, tn=512):
    rows, d = c_pad.shape
    n = w_ada.shape[-1]
    return pl.pallas_call(
        _ada_kernel,
        grid=(n // tn,),
        in_specs=[
            pl.BlockSpec((rows, d), lambda j: (0, 0)),
            pl.BlockSpec((None, d, tn), lambda j: (l, 0, j)),
            pl.BlockSpec((None, 1, tn), lambda j: (l, 0, j)),
        ],
        out_specs=pl.BlockSpec((rows, tn), lambda j: (0, j)),
        out_shape=jax.ShapeDtypeStruct((rows, n), F32),
        compiler_params=_cparams(("arbitrary",)),
        name="ada_mod",
    )(c_pad, w_ada, b_ada3)


def _modulate_kernel(x_ref, mod_ref, o_ref, *, shift_row, scale_row):
    sh = mod_ref[shift_row:shift_row + 1, :]
    sc = mod_ref[scale_row:scale_row + 1, :]
    o_ref[...] = (x_ref[...] * (1.0 + sc) + sh).astype(BF16)


def _modulate(x2, mod3, seq, shift_row, scale_row, tm=512):
    t, d = x2.shape
    per_b = seq // tm
    return pl.pallas_call(
        functools.partial(_modulate_kernel, shift_row=shift_row, scale_row=scale_row),
        grid=(t // tm,),
        in_specs=[
            pl.BlockSpec((tm, d), lambda i: (i, 0)),
            pl.BlockSpec((None, 6, d), lambda i: (i // per_b, 0, 0)),
        ],
        out_specs=pl.BlockSpec((tm, d), lambda i: (i, 0)),
        out_shape=jax.ShapeDtypeStruct((t, d), BF16),
        compiler_params=_cparams(("arbitrary",)),
        name="modulate",
    )(x2, mod3)


def _rope_block(x, cos_t, sin_t):
    lane = lax.broadcasted_iota(jnp.int32, x.shape, 1)
    half = QK_ROPE_DIM // 2
    swapped = jnp.where(lane < half, pltpu.roll(x, LANES - half, 1), pltpu.roll(x, half, 1))
    return x * cos_t + swapped * sin_t


def _w_in_layout_kernel(w_ref, o_ref, *, mla_cols, dil_cols):
    rope0 = mla_cols
    dil0 = mla_cols + QK_ROPE_DIM
    o_ref[:, :mla_cols] = w_ref[:, :mla_cols].astype(BF16)
    o_ref[:, mla_cols:mla_cols + dil_cols] = w_ref[:, dil0:dil0 + dil_cols].astype(BF16)
    tail = w_ref[:, rope0:rope0 + LANES]
    lane = lax.broadcasted_iota(jnp.int32, tail.shape, 1)
    o_ref[:, mla_cols + dil_cols:] = jnp.where(lane < QK_ROPE_DIM, tail, 0.0).astype(BF16)


def _w_in_layout(w_in, l, mla_cols, dil_cols, tk=256):
    _, d, n = w_in.shape
    n_out = mla_cols + dil_cols + LANES
    return pl.pallas_call(
        functools.partial(_w_in_layout_kernel, mla_cols=mla_cols, dil_cols=dil_cols),
        grid=(d // tk,),
        in_specs=[pl.BlockSpec((None, tk, n), lambda i: (l, i, 0))],
        out_specs=pl.BlockSpec((tk, n_out), lambda i: (i, 0)),
        out_shape=jax.ShapeDtypeStruct((d, n_out), BF16),
        compiler_params=_cparams(("arbitrary",)),
        name="w_in_layout",
    )(w_in)


def _proj_plain_kernel(a_ref, w_ref, o_ref):
    o_ref[...] = jnp.dot(a_ref[...], w_ref[...], preferred_element_type=F32).astype(o_ref.dtype)


def _proj_heads_kernel(a_ref, w_ref, o_ref, *, heads_per_tile):
    res = jnp.dot(a_ref[...], w_ref[...], preferred_element_type=F32)
    for hh in range(heads_per_tile):
        o_ref[hh] = res[:, hh * HEAD_DIM:(hh + 1) * HEAD_DIM].astype(o_ref.dtype)


def _proj_rope_kernel(a_ref, w_ref, cos_ref, sin_ref, o_ref):
    res = jnp.dot(a_ref[...], w_ref[...], preferred_element_type=F32)
    o_ref[...] = _rope_block(res, cos_ref[...], sin_ref[...]).astype(o_ref.dtype)


def _in_proj_mla(u, w_p, n_cols, tm=512, tn=512):
    t, d = u.shape
    return pl.pallas_call(
        _proj_plain_kernel,
        grid=(n_cols // tn, t // tm),
        in_specs=[
            pl.BlockSpec((tm, d), lambda j, i: (i, 0)),
            pl.BlockSpec((d, tn), lambda j, i: (0, j)),
        ],
        out_specs=pl.BlockSpec((tm, tn), lambda j, i: (i, j)),
        out_shape=jax.ShapeDtypeStruct((t, n_cols), BF16),
        compiler_params=_cparams(("arbitrary", "arbitrary")),
        name="in_proj_mla",
    )(u, w_p)


def _in_proj_dil(u, w_p, col0, batch, seq, n_heads, tm=512, tn=512):
    t, d = u.shape
    hpt = tn // HEAD_DIM
    tiles_per_mat = n_heads // hpt
    per_b = seq // tm
    off = col0 // tn
    return pl.pallas_call(
        functools.partial(_proj_heads_kernel, heads_per_tile=hpt),
        grid=(3 * tiles_per_mat, t // tm),
        in_specs=[
            pl.BlockSpec((tm, d), lambda j, i: (i, 0)),
            pl.BlockSpec((d, tn), lambda j, i: (0, j + off)),
        ],
        out_specs=pl.BlockSpec(
            (None, None, hpt, tm, HEAD_DIM),
            lambda j, i: (j // tiles_per_mat, i // per_b, j % tiles_per_mat, i % per_b, 0)),
        out_shape=jax.ShapeDtypeStruct((3, batch, n_heads, seq, HEAD_DIM), BF16),
        compiler_params=_cparams(("arbitrary", "arbitrary")),
        name="in_proj_dil",
    )(u, w_p)


def _in_proj_krope(u, w_p, col0, cos_t, sin_t, tm=1024):
    t, d = u.shape
    off = col0 // LANES
    return pl.pallas_call(
        _proj_rope_kernel,
        grid=(t // tm,),
        in_specs=[
            pl.BlockSpec((tm, d), lambda i: (i, 0)),
            pl.BlockSpec((d, LANES), lambda i: (0, off)),
            pl.BlockSpec((tm, LANES), lambda i: (i, 0)),
            pl.BlockSpec((tm, LANES), lambda i: (i, 0)),
        ],
        out_specs=pl.BlockSpec((tm, LANES), lambda i: (i, 0)),
        out_shape=jax.ShapeDtypeStruct((t, LANES), BF16),
        compiler_params=_cparams(("arbitrary",)),
        name="in_proj_krope",
    )(u, w_p, cos_t, sin_t)


def _rms_bf16(h_ref, g_ref):
    hf = h_ref[...].astype(F32)
    y = hf * lax.rsqrt(jnp.mean(hf * hf, axis=-1, keepdims=True) + RMS_EPS)
    return (y * g_ref[...]).astype(BF16)


def _q_up_kernel(h_ref, g_ref, w_ref, cos_ref, sin_ref, o_ref, *, n_heads, scale):
    yb = _rms_bf16(h_ref, g_ref)
    cos_t = cos_ref[...]
    sin_t = sin_ref[...]
    group = 4
    for h0 in range(0, n_heads, group):
        res = jnp.dot(yb, w_ref[:, h0 * QK_PAD:(h0 + group) * QK_PAD], preferred_element_type=F32)
        for hh in range(group):
            c0 = hh * QK_PAD
            nope = res[:, c0:c0 + QK_NOPE_DIM]
            rp = _rope_block(res[:, c0 + QK_NOPE_DIM:c0 + QK_PAD], cos_t, sin_t)
            o0 = (h0 + hh) * QK_PAD
            o_ref[:, o0:o0 + QK_NOPE_DIM] = (nope * scale).astype(BF16)
            o_ref[:, o0 + QK_NOPE_DIM:o0 + QK_PAD] = (rp * scale).astype(BF16)


def _q_up(hm, g3, l, w_uq_p, cos_t, sin_t, n_heads, tm=256):
    t = hm.shape[0]
    scale = float(QK_NOPE_DIM + QK_ROPE_DIM) ** -0.5
    return pl.pallas_call(
        functools.partial(_q_up_kernel, n_heads=n_heads, scale=scale),
        grid=(t // tm,),
        in_specs=[
            pl.BlockSpec((tm, Q_LORA_RANK), lambda i: (i, 0)),
            pl.BlockSpec((None, 1, Q_LORA_RANK), lambda i: (l, 0, 0)),
            pl.BlockSpec((Q_LORA_RANK, n_heads * QK_PAD), lambda i: (0, 0)),
            pl.BlockSpec((tm, LANES), lambda i: (i, 0)),
            pl.BlockSpec((tm, LANES), lambda i: (i, 0)),
        ],
        out_specs=pl.BlockSpec((tm, n_heads * QK_PAD), lambda i: (i, 0)),
        out_shape=jax.ShapeDtypeStruct((t, n_heads * QK_PAD), BF16),
        compiler_params=_cparams(("arbitrary",)),
        name="mla_q_up",
    )(hm, g3, w_uq_p, cos_t, sin_t)


def _kv_up_kernel(h_ref, g_ref, wk_ref, wv_ref, kr_ref, k_ref, v_ref, *, n_heads):
    yb = _rms_bf16(h_ref, g_ref)
    kr = kr_ref[...]
    group = 4
    for h0 in range(0, n_heads, group):
        res = jnp.dot(yb, wk_ref[:, h0 * QK_NOPE_DIM:(h0 + group) * QK_NOPE_DIM],
                      preferred_element_type=F32)
        for hh in range(group):
            o0 = (h0 + hh) * QK_PAD
            k_ref[:, o0:o0 + QK_NOPE_DIM] = res[:, hh * QK_NOPE_DIM:(hh + 1) * QK_NOPE_DIM].astype(BF16)
            k_ref[:, o0 + QK_NOPE_DIM:o0 + QK_PAD] = kr
    v_ref[...] = jnp.dot(yb, wv_ref[...], preferred_element_type=F32).astype(BF16)


def _kv_up(hm, g3, l, w_uk, w_uv, krr, n_heads, tm=256):
    t = hm.shape[0]
    col_blk = Q_LORA_RANK // KV_LORA_RANK
    return pl.pallas_call(
        functools.partial(_kv_up_kernel, n_heads=n_heads),
        grid=(t // tm,),
        in_specs=[
            pl.BlockSpec((tm, KV_LORA_RANK), lambda i: (i, col_blk)),
            pl.BlockSpec((None, 1, KV_LORA_RANK), lambda i: (l, 0, 0)),
            pl.BlockSpec((KV_LORA_RANK, n_heads * QK_NOPE_DIM), lambda i: (0, 0)),
            pl.BlockSpec((KV_LORA_RANK, n_heads * V_HEAD_DIM), lambda i: (0, 0)),
            pl.BlockSpec((tm, LANES), lambda i: (i, 0)),
        ],
        out_specs=[
            pl.BlockSpec((tm, n_heads * QK_PAD), lambda i: (i, 0)),
            pl.BlockSpec((tm, n_heads * V_HEAD_DIM), lambda i: (i, 0)),
        ],
        out_shape=[
            jax.ShapeDtypeStruct((t, n_heads * QK_PAD), BF16),
            jax.ShapeDtypeStruct((t, n_heads * V_HEAD_DIM), BF16),
        ],
        compiler_params=_cparams(("arbitrary",)),
        name="mla_kv_up",
    )(hm, g3, w_uk, w_uv, krr)


MLA_HEADS_PER_STEP = 2


def _mla_attn_kernel(q_ref, k_ref, v_ref, o_ref, *, seq, tq):
    nq = seq // tq
    hp = MLA_HEADS_PER_STEP
    row = lax.broadcasted_iota(jnp.int32, (tq, tq), 0)
    col = lax.broadcasted_iota(jnp.int32, (tq, tq), 1)
    causal = col <= row

    def kv_step(qs, j, carry, masked):
        off = pl.multiple_of(j * tq, tq)
        out = []
        for hh in range(hp):
            m, l, acc = carry[hh]
            k = k_ref[pl.ds(off, tq), hh * QK_PAD:(hh + 1) * QK_PAD]
            v = v_ref[pl.ds(off, tq), hh * V_HEAD_DIM:(hh + 1) * V_HEAD_DIM]
            s = lax.dot_general(qs[hh], k, (((1,), (1,)), ((), ())), preferred_element_type=F32)
            if masked:
                s = jnp.where(causal, s, NEG_BIG)
            m_new = jnp.maximum(m, jnp.max(s, axis=-1, keepdims=True))
            p = jnp.exp(s - m_new)
            alpha = jnp.exp(m - m_new)
            l_new = alpha * l + jnp.sum(p, axis=-1, keepdims=True)
            acc_new = alpha * acc + jnp.dot(p.astype(BF16), v, preferred_element_type=F32)
            out.append((m_new, l_new, acc_new))
        return tuple(out)

    def q_loop(i, _):
        qoff = pl.multiple_of(i * tq, tq)
        qs = [q_ref[pl.ds(qoff, tq), hh * QK_PAD:(hh + 1) * QK_PAD] for hh in range(hp)]
        init = tuple((jnp.full((tq, 1), NEG_BIG, F32), jnp.zeros((tq, 1), F32),
                      jnp.zeros((tq, V_HEAD_DIM), F32)) for _ in range(hp))
        carry = lax.fori_loop(0, i, lambda j, c: kv_step(qs, j, c, False), init)
        carry = kv_step(qs, i, carry, True)
        for hh in range(hp):
            m, l, acc = carry[hh]
            o_ref[pl.ds(qoff, tq), hh * V_HEAD_DIM:(hh + 1) * V_HEAD_DIM] = (acc / l).astype(o_ref.dtype)
        return 0

    lax.fori_loop(0, nq, q_loop, 0)


def _mla_attn(q, k, v, batch, seq, n_heads, tq=512):
    t = q.shape[0]
    hp = MLA_HEADS_PER_STEP
    return pl.pallas_call(
        functools.partial(_mla_attn_kernel, seq=seq, tq=tq),
        grid=(batch, n_heads // hp),
        in_specs=[
            pl.BlockSpec((seq, hp * QK_PAD), lambda b, h: (b, h)),
            pl.BlockSpec((seq, hp * QK_PAD), lambda b, h: (b, h)),
            pl.BlockSpec((seq, hp * V_HEAD_DIM), lambda b, h: (b, h)),
        ],
        out_specs=pl.BlockSpec((seq, hp * V_HEAD_DIM), lambda b, h: (b, h)),
        out_shape=jax.ShapeDtypeStruct((t, n_heads * V_HEAD_DIM), BF16),
        compiler_params=_cparams(("arbitrary", "arbitrary")),
        name="mla_attn",
    )(q, k, v)


DIL_GROUP = 4


def _dil_attn_kernel(qkv_ref, bvec_ref, o_ref, m_s, l_s, acc_s, bias_s, nat_f, *view_s, seq):
    scale = float(HEAD_DIM) ** -0.5
    views = (qkv_ref,) + tuple(view_s)

    for bi in range(len(DIL_PATTERNS)):
        full = jnp.broadcast_to(bvec_ref[bi], (BLOCK, 2 * BLOCK))
        bias_s[bi] = pltpu.roll(full, 0, 1, stride=1, stride_axis=0)

    for which in range(3):
        nat_f[...] = qkv_ref[which].astype(F32)
        for bi, (_, d) in enumerate(DIL_PATTERNS):
            if d == 1:
                continue
            for r in range(d):
                views[bi][which, :, r * HEAD_DIM:(r + 1) * HEAD_DIM] = (
                    nat_f[pl.ds(r, seq // d, stride=d), :].astype(BF16))

    def scores(ref, bi, d, r, i, first):
        lanes = slice(r * HEAD_DIM, (r + 1) * HEAD_DIM)
        qoff = pl.multiple_of(i * BLOCK, BLOCK)
        q = ref[0, pl.ds(qoff, BLOCK), lanes]
        if first:
            kk = ref[1, pl.ds(0, BLOCK), lanes]
            vv = ref[2, pl.ds(0, BLOCK), lanes]
            bias = bias_s[bi, :, BLOCK:2 * BLOCK]
        else:
            koff = pl.multiple_of(i * BLOCK - BLOCK, BLOCK)
            kk = ref[1, pl.ds(koff, 2 * BLOCK), lanes]
            vv = ref[2, pl.ds(koff, 2 * BLOCK), lanes]
            bias = bias_s[bi]
        s = lax.dot_general(q, kk, (((1,), (1,)), ((), ())), preferred_element_type=F32)
        s = s * scale + bias
        m_b = jnp.max(s, axis=-1, keepdims=True)
        p = jnp.exp(s - m_b)
        l_b = jnp.sum(p, axis=-1, keepdims=True)
        a_b = jnp.dot(p.astype(BF16), vv, preferred_element_type=F32)
        m_b = jnp.broadcast_to(m_b, (BLOCK, HEAD_DIM))
        l_b = jnp.broadcast_to(l_b, (BLOCK, HEAD_DIM))
        if d == 1:
            rows = pl.ds(qoff, BLOCK)
        else:
            rows = pl.ds(i * (BLOCK * d) + r, BLOCK, stride=d)
        return m_b, l_b, a_b, rows

    def group(ref, bi, d, blocks):
        parts = [scores(ref, bi, d, r, i, first) for r, i, first in blocks]
        if d == 1:
            for m_b, l_b, a_b, rows in parts:
                m_s[rows, :] = m_b
                l_s[rows, :] = l_b
                acc_s[rows, :] = a_b
            return
        old = [(m_s[rows, :], l_s[rows, :], acc_s[rows, :]) for _, _, _, rows in parts]
        new = []
        for (m_b, l_b, a_b, rows), (m_o, l_o, a_o) in zip(parts, old):
            m_n = jnp.maximum(m_o, m_b)
            e_o = jnp.exp(m_o - m_n)
            e_b = jnp.exp(m_b - m_n)
            new.append((m_n, e_o * l_o + e_b * l_b, e_o * a_o + e_b * a_b, rows))
        for m_n, l_n, a_n, rows in new:
            m_s[rows, :] = m_n
            l_s[rows, :] = l_n
            acc_s[rows, :] = a_n

    for bi, (_, d) in enumerate(DIL_PATTERNS):
        ref = views[bi]
        nblk = seq // d // BLOCK
        if d == 1:
            assert nblk % DIL_GROUP == 0
            group(ref, bi, d, [(0, u, u == 0) for u in range(DIL_GROUP)])

            def body1(g, _, ref=ref, bi=bi, d=d):
                group(ref, bi, d, [(0, g * DIL_GROUP + u, False) for u in range(DIL_GROUP)])
                return 0

            lax.fori_loop(1, nblk // DIL_GROUP, body1, 0)
        else:
            assert d % DIL_GROUP == 0
            for r0 in range(0, d, DIL_GROUP):
                group(ref, bi, d, [(r0 + u, 0, True) for u in range(DIL_GROUP)])

                def body(i, _, ref=ref, bi=bi, d=d, r0=r0):
                    group(ref, bi, d, [(r0 + u, i, False) for u in range(DIL_GROUP)])
                    return 0

                lax.fori_loop(1, nblk, body, 0)

    o_ref[...] = (acc_s[...] / l_s[...]).astype(o_ref.dtype)


def _dil_attn(qkv, bias_vec, batch, seq, n_heads):
    t = batch * seq
    nbr = len(DIL_PATTERNS)
    assert DIL_PATTERNS[0][1] == 1
    view_scratch = [pltpu.VMEM((3, seq // d, d * HEAD_DIM), BF16) for _, d in DIL_PATTERNS[1:]]
    return pl.pallas_call(
        functools.partial(_dil_attn_kernel, seq=seq),
        grid=(batch, n_heads),
        in_specs=[
            pl.BlockSpec((3, None, None, seq, HEAD_DIM), lambda b, h: (0, b, h, 0, 0)),
            pl.BlockSpec((nbr, None, 1, 2 * BLOCK), lambda b, h: (0, h, 0, 0)),
        ],
        out_specs=pl.BlockSpec((seq, HEAD_DIM), lambda b, h: (b, h)),
        out_shape=jax.ShapeDtypeStruct((t, n_heads * HEAD_DIM), BF16),
        scratch_shapes=[pltpu.VMEM((seq, HEAD_DIM), F32)] * 3
        + [pltpu.VMEM((nbr, BLOCK, 2 * BLOCK), F32), pltpu.VMEM((seq, HEAD_DIM), F32)]
        + view_scratch,
        compiler_params=_cparams(("arbitrary", "arbitrary")),
        name="dil_attn",
    )(qkv, bias_vec)


def _out_proj_kernel(a1_ref, a2_ref, w_ref, o_ref, wb_ref):
    @pl.when(pl.program_id(1) == 0)
    def _():
        wb_ref[...] = w_ref[...].astype(BF16)

    k1 = a1_ref.shape[1]
    acc = jnp.dot(a1_ref[...], wb_ref[:k1, :], preferred_element_type=F32)
    acc = acc + jnp.dot(a2_ref[...], wb_ref[k1:, :], preferred_element_type=F32)
    o_ref[...] = acc.astype(o_ref.dtype)


def _out_proj(o_a, o_b, w_o, l, tm=512, tn=512):
    t, k1 = o_a.shape
    k2 = o_b.shape[1]
    n = w_o.shape[-1]
    return pl.pallas_call(
        _out_proj_kernel,
        grid=(n // tn, t // tm),
        in_specs=[
            pl.BlockSpec((tm, k1), lambda j, i: (i, 0)),
            pl.BlockSpec((tm, k2), lambda j, i: (i, 0)),
            pl.BlockSpec((None, k1 + k2, tn), lambda j, i: (l, 0, j)),
        ],
        out_specs=pl.BlockSpec((tm, tn), lambda j, i: (i, j)),
        out_shape=jax.ShapeDtypeStruct((t, n), BF16),
        scratch_shapes=[pltpu.VMEM((k1 + k2, tn), BF16)],
        compiler_params=_cparams(("arbitrary", "arbitrary")),
        name="out_proj",
    )(o_a, o_b, w_o)


def _layer_norm_rows(z, g, b):
    mu = jnp.mean(z, axis=-1, keepdims=True)
    zc = z - mu
    var = jnp.mean(zc * zc, axis=-1, keepdims=True)
    return zc * lax.rsqrt(var + LN_EPS) * g + b


SLAB_CHUNK = 1024
SLAB_Q = SLAB_CHUNK // (2 * LANES)


def _slab_sub(width):
    return width // (2 * LANES)


def _slab_index(j, first_tok, n_tok, sub):
    return (pl.ds(first_tok * sub + j, n_tok, stride=sub), slice(None))


def _pack_chunk(vals):
    bits = lax.bitcast_convert_type(vals.astype(BF16).astype(F32), jnp.uint32)
    half = SLAB_CHUNK // 2
    return [(bits[:, q * LANES:(q + 1) * LANES] >> 16)
            | bits[:, half + q * LANES:half + (q + 1) * LANES] for q in range(SLAB_Q)]


def _unpack_words(words):
    lo = lax.bitcast_convert_type(words << 16, F32)
    hi = lax.bitcast_convert_type(words & jnp.uint32(0xFFFF0000), F32)
    return lo, hi


def _ln1_router_kernel(x_ref, mix_ref, mod_ref, g_ref, b_ref, wr_ref, br_ref,
                       x1_ref, u2p_ref, ids_ref, gates_ref, *, alpha):
    gt1 = mod_ref[2:3, :]
    sh2 = mod_ref[3:4, :]
    sc2 = mod_ref[4:5, :]
    z = alpha * x_ref[...] + gt1 * mix_ref[...].astype(F32)
    x1 = _layer_norm_rows(z, g_ref[...], b_ref[...])
    x1_ref[...] = x1
    u2 = (x1 * (1.0 + sc2) + sh2).astype(BF16)
    tm, d = u2.shape
    for g in range(d // SLAB_CHUNK):
        for q, words in enumerate(_pack_chunk(u2[:, g * SLAB_CHUNK:(g + 1) * SLAB_CHUNK])):
            u2p_ref[_slab_index(g * SLAB_Q + q, 0, tm, _slab_sub(d))] = words
    logits = jnp.dot(u2, wr_ref[...].astype(BF16), preferred_element_type=F32) + br_ref[...]
    lane = lax.broadcasted_iota(jnp.int32, logits.shape, 1)
    lane_f = lane.astype(F32)
    vals = []
    ids = []
    for _ in range(TOP_K):
        mk = jnp.max(logits, axis=-1, keepdims=True)
        idx_f = jnp.min(jnp.where(logits == mk, lane_f, float(LANES)), axis=-1, keepdims=True)
        vals.append(mk)
        ids.append(idx_f.astype(jnp.int32))
        logits = jnp.where(lane_f == idx_f, -jnp.inf, logits)
    exps = [jnp.exp(v - vals[0]) for v in vals]
    denom = exps[0]
    for e in exps[1:]:
        denom = denom + e
    ids_out = jnp.zeros(lane.shape, jnp.int32)
    gates_out = jnp.zeros(lane.shape, F32)
    for k in range(TOP_K):
        ids_out = jnp.where(lane == k, ids[k], ids_out)
        gates_out = jnp.where(lane == k, exps[k] / denom, gates_out)
    ids_ref[...] = ids_out
    gates_ref[...] = gates_out


def _ln1_router(x2, mix, mod3, g3, b3, wr_p, br_p, l, seq, alpha, tm=256):
    t, d = x2.shape
    per_b = seq // tm
    return pl.pallas_call(
        functools.partial(_ln1_router_kernel, alpha=alpha),
        grid=(t // tm,),
        in_specs=[
            pl.BlockSpec((tm, d), lambda i: (i, 0)),
            pl.BlockSpec((tm, d), lambda i: (i, 0)),
            pl.BlockSpec((None, 6, d), lambda i: (i // per_b, 0, 0)),
            pl.BlockSpec((None, 1, d), lambda i: (l, 0, 0)),
            pl.BlockSpec((None, 1, d), lambda i: (l, 0, 0)),
            pl.BlockSpec((d, LANES), lambda i: (0, 0)),
            pl.BlockSpec((1, LANES), lambda i: (0, 0)),
        ],
        out_specs=[
            pl.BlockSpec((tm, d), lambda i: (i, 0)),
            pl.BlockSpec((tm * _slab_sub(d), LANES), lambda i: (i, 0)),
            pl.BlockSpec((tm, LANES), lambda i: (i, 0)),
            pl.BlockSpec((tm, LANES), lambda i: (i, 0)),
        ],
        out_shape=[
            jax.ShapeDtypeStruct((t, d), F32),
            jax.ShapeDtypeStruct((t * _slab_sub(d), LANES), jnp.uint32),
            jax.ShapeDtypeStruct((t, LANES), jnp.int32),
            jax.ShapeDtypeStruct((t, LANES), F32),
        ],
        compiler_params=_cparams(("arbitrary",)),
        name="ln1_router",
    )(x2, mix, mod3, g3, b3, wr_p, br_p)


def _moe_up_kernel(item_e, item_blk, item_sub, x_ref, wg_ref, wu_ref, bg_ref, bu_ref,
                   h_ref, wg_s, wu_s):
    w = pl.program_id(0)
    nsub = item_sub[w]

    @pl.when(nsub > 0)
    def _():
        wg_s[...] = wg_ref[...].astype(BF16)
        wu_s[...] = wu_ref[...].astype(BF16)
        bg = bg_ref[...]
        bu = bu_ref[...]

        def body(j, _):
            off = pl.multiple_of(j * MOE_SUB, MOE_SUB)
            xt = x_ref[pl.ds(off, MOE_SUB), :]
            glu = jnp.dot(xt, wg_s[...], preferred_element_type=F32) + bg
            lin = jnp.dot(xt, wu_s[...], preferred_element_type=F32) + bu
            glu = jnp.minimum(glu, SWIGLU_LIMIT)
            lin = jnp.clip(lin, -SWIGLU_LIMIT, SWIGLU_LIMIT)
            act = glu * jax.nn.sigmoid(SWIGLU_ALPHA * glu) * (lin + 1.0)
            h_ref[pl.ds(off, MOE_SUB), :] = act.astype(h_ref.dtype)
            return 0

        lax.fori_loop(0, nsub, body, 0)


def _moe_up(xs, w_gate, w_up, b_gate4, b_up4, item_e, item_blk, item_sub, l, tf=256):
    r, d = xs.shape
    n_items = r // MOE_ROWS
    ff = w_gate.shape[-1]
    n_chunks = ff // tf

    def cmap(w, c, isub):
        return jnp.where(isub[w] > 0, c, n_chunks - 1)

    def wmap(w, c, ie, ib, isub):
        return (l, ie[w], 0, cmap(w, c, isub))

    grid_spec = pltpu.PrefetchScalarGridSpec(
        num_scalar_prefetch=3,
        grid=(n_items, n_chunks),
        in_specs=[
            pl.BlockSpec((MOE_ROWS, d), lambda w, c, ie, ib, isub: (ib[w], 0)),
            pl.BlockSpec((None, None, d, tf), wmap),
            pl.BlockSpec((None, None, d, tf), wmap),
            pl.BlockSpec((None, None, 1, tf), wmap),
            pl.BlockSpec((None, None, 1, tf), wmap),
        ],
        out_specs=pl.BlockSpec(
            (MOE_ROWS, tf), lambda w, c, ie, ib, isub: (ib[w], cmap(w, c, isub))),
        scratch_shapes=[pltpu.VMEM((d, tf), BF16), pltpu.VMEM((d, tf), BF16)],
    )
    return pl.pallas_call(
        _moe_up_kernel,
        grid_spec=grid_spec,
        out_shape=jax.ShapeDtypeStruct((r, ff), BF16),
        compiler_params=_cparams(("arbitrary", "arbitrary")),
        name="moe_up",
    )(item_e, item_blk, item_sub, xs, w_gate, w_up, b_gate4, b_up4)


def _moe_down_kernel(item_e, item_blk, item_sub, h_ref, wd_ref, bd_ref, y_ref, wd_s,
                     *, sub):
    w = pl.program_id(0)
    c = pl.program_id(1)
    nsub = item_sub[w]

    @pl.when(nsub > 0)
    def _():
        wd_s[...] = wd_ref[...].astype(BF16)
        bd = bd_ref[...]

        def body(j, _):
            off = pl.multiple_of(j * MOE_SUB, MOE_SUB)
            ht = h_ref[pl.ds(off, MOE_SUB), :]
            y = jnp.dot(ht, wd_s[...], preferred_element_type=F32) + bd
            for q, words in enumerate(_pack_chunk(y)):
                y_ref[_slab_index(c * SLAB_Q + q, off, MOE_SUB, sub)] = words
            return 0

        lax.fori_loop(0, nsub, body, 0)


def _moe_down(hs, w_down, b_down4, item_e, item_blk, item_sub, l):
    r, ff = hs.shape
    n_items = r // MOE_ROWS
    d = w_down.shape[-1]
    tn = SLAB_CHUNK
    n_chunks = d // tn
    sub = _slab_sub(d)

    def cmap(w, c, isub):
        return jnp.where(isub[w] > 0, c, n_chunks - 1)

    def wmap(w, c, ie, ib, isub):
        return (l, ie[w], 0, cmap(w, c, isub))

    grid_spec = pltpu.PrefetchScalarGridSpec(
        num_scalar_prefetch=3,
        grid=(n_items, n_chunks),
        in_specs=[
            pl.BlockSpec((MOE_ROWS, ff), lambda w, c, ie, ib, isub: (ib[w], 0)),
            pl.BlockSpec((None, None, ff, tn), wmap),
            pl.BlockSpec((None, None, 1, tn), wmap),
        ],
        out_specs=pl.BlockSpec((MOE_ROWS * sub, LANES), lambda w, c, ie, ib, isub: (ib[w], 0)),
        scratch_shapes=[pltpu.VMEM((ff, tn), BF16)],
    )
    return pl.pallas_call(
        functools.partial(_moe_down_kernel, sub=sub),
        grid_spec=grid_spec,
        out_shape=jax.ShapeDtypeStruct((r * sub, LANES), jnp.uint32),
        compiler_params=_cparams(("arbitrary", "arbitrary")),
        name="moe_down",
    )(item_e, item_blk, item_sub, hs, w_down, b_down4)


GATHER_UNROLL = 8


def _slab_gather(idx_ref, src_ref, buf, sem, base_slab, n, sub, wait):
    def copy(r):
        src_row = pl.multiple_of(idx_ref[0, r] * sub, sub)
        dst_row = pl.multiple_of((base_slab + r) * sub, sub)
        return pltpu.make_async_copy(src_ref.at[pl.ds(src_row, sub), :],
                                     buf.at[pl.ds(dst_row, sub), :], sem)

    def body(r0, _):
        for u in range(GATHER_UNROLL):
            c = copy(r0 * GATHER_UNROLL + u)
            if wait:
                c.wait()
            else:
                c.start(priority=u % 2)
        return 0

    lax.fori_loop(0, n // GATHER_UNROLL, body, 0)


def _dispatch_kernel(vblk, nvalid, tok_ref, tok_next_ref, src_ref, o_ref, buf, sems, *, rows, d):
    s = pl.program_id(0)
    slot = s % 2
    sub = _slab_sub(d)

    @pl.when(s == 0)
    def _():
        _slab_gather(tok_ref, src_ref, buf, sems.at[0], 0, rows, sub, wait=False)

    @pl.when(s + 1 < nvalid[0])
    def _():
        _slab_gather(tok_next_ref, src_ref, buf, sems.at[1 - slot], (1 - slot) * rows, rows, sub,
                     wait=False)

    @pl.when((s < nvalid[0]) | (s == 0))
    def _():
        _slab_gather(tok_ref, src_ref, buf, sems.at[slot], slot * rows, rows, sub, wait=True)
        half = SLAB_CHUNK // 2
        for g in range(d // SLAB_CHUNK):
            for q in range(SLAB_Q):
                lo, hi = _unpack_words(buf[_slab_index(g * SLAB_Q + q, slot * rows, rows, sub)])
                c0 = g * SLAB_CHUNK + q * LANES
                o_ref[:, c0:c0 + LANES] = lo.astype(BF16)
                o_ref[:, c0 + half:c0 + half + LANES] = hi.astype(BF16)


def _dispatch(u2p, d, row_tok, vblk, nvalid, rows=MOE_SUB):
    r = row_tok.shape[0]
    n_blk = r // rows
    n_steps = vblk.shape[0]
    tok3 = row_tok.reshape(n_blk, 1, rows)
    grid_spec = pltpu.PrefetchScalarGridSpec(
        num_scalar_prefetch=2,
        grid=(n_steps,),
        in_specs=[
            pl.BlockSpec((None, 1, rows), lambda s, vb, nv: (vb[s], 0, 0), memory_space=pltpu.SMEM),
            pl.BlockSpec((None, 1, rows), lambda s, vb, nv: (vb[jnp.minimum(s + 1, n_steps - 1)], 0, 0),
                         memory_space=pltpu.SMEM),
            pl.BlockSpec(memory_space=pl.ANY),
        ],
        out_specs=pl.BlockSpec((rows, d), lambda s, vb, nv: (vb[s], 0)),
        scratch_shapes=[pltpu.VMEM((2 * rows * _slab_sub(d), LANES), jnp.uint32),
                        pltpu.SemaphoreType.DMA((2,))],
    )
    return pl.pallas_call(
        functools.partial(_dispatch_kernel, rows=rows, d=d),
        grid_spec=grid_spec,
        out_shape=jax.ShapeDtypeStruct((r, d), BF16),
        compiler_params=_cparams(("arbitrary",)),
        name="moe_dispatch",
    )(vblk, nvalid, tok3, tok3, u2p)


def _combine_ln2_kernel(slot_ref, slot_next_ref, x1_ref, ys_ref, gates_ref, mod_ref, g_ref, b_ref,
                        o_ref, buf, ffn_s, sems, *, alpha, tm):
    i = pl.program_id(0)
    n = pl.num_programs(0)
    cur = i % 2
    n_rows = TOP_K * tm
    d = ffn_s.shape[1]
    sub = _slab_sub(d)

    @pl.when(i == 0)
    def _():
        _slab_gather(slot_ref, ys_ref, buf, sems.at[0], 0, n_rows, sub, wait=False)

    @pl.when(i + 1 < n)
    def _():
        _slab_gather(slot_next_ref, ys_ref, buf, sems.at[1 - cur], (1 - cur) * n_rows, n_rows, sub,
                     wait=False)

    _slab_gather(slot_ref, ys_ref, buf, sems.at[cur], cur * n_rows, n_rows, sub, wait=True)

    gates = gates_ref[...]
    gk = [jnp.broadcast_to(gates[:, k:k + 1], (tm, LANES)) for k in range(TOP_K)]
    half = SLAB_CHUNK // 2
    for g in range(d // SLAB_CHUNK):
        for q in range(SLAB_Q):
            lo_acc = None
            hi_acc = None
            for k in range(TOP_K):
                idx = _slab_index(g * SLAB_Q + q, cur * n_rows + k * tm, tm, sub)
                lo, hi = _unpack_words(buf[idx])
                lo_acc = gk[k] * lo if lo_acc is None else lo_acc + gk[k] * lo
                hi_acc = gk[k] * hi if hi_acc is None else hi_acc + gk[k] * hi
            c0 = g * SLAB_CHUNK + q * LANES
            ffn_s[:, c0:c0 + LANES] = lo_acc
            ffn_s[:, c0 + half:c0 + half + LANES] = hi_acc
    gt2 = mod_ref[5:6, :]
    z = alpha * x1_ref[...] + gt2 * ffn_s[...]
    o_ref[...] = _layer_norm_rows(z, g_ref[...], b_ref[...])


def _combine_ln2(x1, ys, slot, gates, mod3, g3, b3, l, seq, alpha, tm=256):
    t, d = x1.shape
    per_b = seq // tm
    n_tiles = t // tm
    slot3 = slot.reshape(n_tiles, tm, TOP_K).transpose(0, 2, 1).reshape(n_tiles, 1, TOP_K * tm)
    return pl.pallas_call(
        functools.partial(_combine_ln2_kernel, alpha=alpha, tm=tm),
        grid=(n_tiles,),
        in_specs=[
            pl.BlockSpec((None, 1, tm * TOP_K), lambda i: (i, 0, 0), memory_space=pltpu.SMEM),
            pl.BlockSpec((None, 1, tm * TOP_K), lambda i: (jnp.minimum(i + 1, n_tiles - 1), 0, 0),
                         memory_space=pltpu.SMEM),
            pl.BlockSpec((tm, d), lambda i: (i, 0)),
            pl.BlockSpec(memory_space=pl.ANY),
            pl.BlockSpec((tm, LANES), lambda i: (i, 0)),
            pl.BlockSpec((None, 6, d), lambda i: (i // per_b, 0, 0)),
            pl.BlockSpec((None, 1, d), lambda i: (l, 0, 0)),
            pl.BlockSpec((None, 1, d), lambda i: (l, 0, 0)),
        ],
        out_specs=pl.BlockSpec((tm, d), lambda i: (i, 0)),
        out_shape=jax.ShapeDtypeStruct((t, d), F32),
        scratch_shapes=[pltpu.VMEM((2 * TOP_K * tm * _slab_sub(d), LANES), jnp.uint32),
                        pltpu.VMEM((tm, d), F32),
                        pltpu.SemaphoreType.DMA((2,))],
        compiler_params=_cparams(("arbitrary",)),
        name="combine_ln2",
    )(slot3, slot3, x1, ys, gates, mod3, g3, b3)


def _t5_bucket(dist):
    is_small = dist < REL_MAX_EXACT
    nf = jnp.maximum(dist, REL_MAX_EXACT).astype(F32)
    large = REL_MAX_EXACT + (jnp.log(nf / REL_MAX_EXACT) / math.log(REL_MAX_DISTANCE / REL_MAX_EXACT)
                             * (REL_BUCKETS - REL_MAX_EXACT)).astype(jnp.int32)
    large = jnp.minimum(large, REL_BUCKETS - 1)
    return jnp.where(is_small, dist, large)


def _dilated_bias_vectors(rel_bias):
    m = jnp.arange(2 * BLOCK)
    rel = BLOCK - m
    vecs = []
    for window, d in DIL_PATTERNS:
        valid = (rel >= 0) & (rel <= window // d)
        bucket = _t5_bucket(jnp.clip(rel, 0) * d)
        bias = jnp.transpose(rel_bias[bucket].astype(F32), (1, 0))
        vecs.append(jnp.where(valid[None, :], bias, NEG_BIG))
    return jnp.stack(vecs, axis=0)[:, :, None, :]


def _routing_tables(ids4, n_items_max):
    e = ids4.reshape(-1)
    n_rows_real = e.shape[0]
    onehot =(e[:, None] == jnp.arange(N_EXPERTS, dtype=jnp.int32)[None, :]).astype(jnp.int32)
    csum = jnp.cumsum(onehot, axis=0)
    rank = jnp.sum(onehot * csum, axis=1) - 1
    counts = csum[-1]
    nit = (counts + MOE_ROWS - 1) // MOE_ROWS
    cum_items = jnp.cumsum(nit)
    first_item = cum_items - nit
    slot = first_item[e] * MOE_ROWS + rank
    total = cum_items[-1]
    w = jnp.arange(n_items_max, dtype=jnp.int32)
    e_w = jnp.minimum(jnp.searchsorted(cum_items, w, side="right"), N_EXPERTS - 1).astype(jnp.int32)
    j_w = w - first_item[e_w]
    rows_w = jnp.clip(counts[e_w] - j_w * MOE_ROWS, 0, MOE_ROWS)
    valid = w < total
    item_sub = jnp.where(valid, (rows_w + MOE_SUB - 1) // MOE_SUB, 0).astype(jnp.int32)
    last = total - 1
    item_e = jnp.where(valid, e_w, e_w[last]).astype(jnp.int32)
    item_blk = jnp.where(valid, w, last).astype(jnp.int32)
    per_item = MOE_ROWS // MOE_SUB
    blk = jnp.arange(n_items_max * per_item, dtype=jnp.int32)
    blk_valid = ((blk % per_item) < item_sub[blk // per_item]).astype(jnp.int32)
    cum_valid = jnp.cumsum(blk_valid)
    nvalid = cum_valid[-1]
    n_steps = n_rows_real // MOE_SUB + n_items_max
    want = jnp.minimum(jnp.arange(n_steps, dtype=jnp.int32) + 1, nvalid)
    vblk = jnp.searchsorted(cum_valid, want, side="left").astype(jnp.int32)
    return (slot.astype(jnp.int32), item_e, item_blk, item_sub, vblk,
            nvalid.astype(jnp.int32).reshape(1))


def kernel(x, c, positions, w_ada, b_ada, w_in, g_q, g_kv, w_uq, w_ukv, rel_bias, w_o,
           ln1_g, ln1_b, w_router, b_router, w_gate, b_gate, w_up, b_up, w_down, b_down,
           ln2_g, ln2_b):
    batch, seq, d = x.shape
    depth = w_ada.shape[0]
    t = batch * seq
    n_heads = d // (2 * HEAD_DIM)
    dil_w = n_heads * HEAD_DIM
    alpha = (2.0 * depth) ** 0.25
    assert seq % (BLOCK * max(dd for _, dd in DIL_PATTERNS)) == 0
    assert all(win // dd == BLOCK for win, dd in DIL_PATTERNS)

    inv_freq = 1.0 / (ROPE_THETA ** (jnp.arange(0, QK_ROPE_DIM, 2, dtype=F32) / QK_ROPE_DIM))
    ang = positions.astype(F32)[..., None] * inv_freq
    cos, sin = jnp.cos(ang).reshape(t, -1), jnp.sin(ang).reshape(t, -1)
    zpad = jnp.zeros((t, LANES - QK_ROPE_DIM), F32)
    cos_t = jnp.concatenate([cos, cos, zpad], axis=1)
    sin_t = jnp.concatenate([-sin, sin, zpad], axis=1)

    bias_vec = _dilated_bias_vectors(rel_bias)
    c_pad = jnp.zeros((16, d), F32).at[:batch].set(c)

    x2 = x.reshape(t, d)
    n_items_max = (t * TOP_K) // MOE_ROWS + N_EXPERTS
    mla_cols = Q_LORA_RANK + KV_LORA_RANK

    for l in range(depth):
        mod = _ada(c_pad, w_ada, b_ada.reshape(depth, 1, -1), l)[:batch]
        mod3 = mod.reshape(batch, 6, d)

        w_in_p = _w_in_layout(w_in, l, mla_cols, 3 * dil_w)
        wq = w_uq[l].reshape(Q_LORA_RANK, n_heads, QK_NOPE_DIM + QK_ROPE_DIM)
        w_uq_p = jnp.pad(wq, ((0, 0), (0, 0), (0, QK_PAD - QK_NOPE_DIM - QK_ROPE_DIM))
                         ).reshape(Q_LORA_RANK, n_heads * QK_PAD).astype(BF16)
        wkv = w_ukv[l].reshape(KV_LORA_RANK, n_heads, QK_NOPE_DIM + V_HEAD_DIM)
        w_uk = wkv[:, :, :QK_NOPE_DIM].reshape(KV_LORA_RANK, -1).astype(BF16)
        w_uv = wkv[:, :, QK_NOPE_DIM:].reshape(KV_LORA_RANK, -1).astype(BF16)

        u1 = _modulate(x2, mod3, seq, 0, 1)
        hm = _in_proj_mla(u1, w_in_p, mla_cols)
        qkv_d = _in_proj_dil(u1, w_in_p, mla_cols, batch, seq, n_heads)
        krr = _in_proj_krope(u1, w_in_p, mla_cols + 3 * dil_w, cos_t, sin_t)
        q_a = _q_up(hm, g_q.reshape(depth, 1, -1), l, w_uq_p, cos_t, sin_t, n_heads)
        k_a, v_a = _kv_up(hm, g_kv.reshape(depth, 1, -1), l, w_uk, w_uv, krr, n_heads)
        o_a = _mla_attn(q_a, k_a, v_a, batch, seq, n_heads)
        o_b = _dil_attn(qkv_d, bias_vec, batch, seq, n_heads)
        mix = _out_proj(o_a, o_b, w_o, l)

        wr_p = jnp.pad(w_router[l], ((0, 0), (0, LANES - N_EXPERTS)))
        br_p = jnp.concatenate([b_router[l], jnp.full((LANES - N_EXPERTS,), NEG_BIG, F32)])[None, :]
        x1, u2p, ids, gates = _ln1_router(
            x2, mix, mod3, ln1_g.reshape(depth, 1, -1), ln1_b.reshape(depth, 1, -1),
            wr_p, br_p, l, seq, alpha)

        slot, item_e, item_blk, item_sub, vblk, nvalid = _routing_tables(
            ids[:, :TOP_K], n_items_max)
        tok = jnp.arange(t * TOP_K, dtype=jnp.int32) // TOP_K
        row_tok = jnp.zeros((n_items_max * MOE_ROWS,), jnp.int32).at[slot].set(tok)
        xs = _dispatch(u2p, d, row_tok, vblk, nvalid)
        hs = _moe_up(xs, w_gate, w_up, b_gate.reshape(depth, N_EXPERTS, 1, -1),
                     b_up.reshape(depth, N_EXPERTS, 1, -1), item_e, item_blk, item_sub, l)
        ys = _moe_down(hs, w_down, b_down.reshape(depth, N_EXPERTS, 1, -1),
                       item_e, item_blk, item_sub, l)
        x2 = _combine_ln2(x1, ys, slot, gates, mod3, ln2_g.reshape(depth, 1, -1),
                          ln2_b.reshape(depth, 1, -1), l, seq, alpha)

    return x2.reshape(batch, seq, d)
```

```python
import functools
import math

import jax
import jax.numpy as jnp
from jax import lax
from jax.experimental import pallas as pl
from jax.experimental.pallas import tpu as pltpu

F32 = jnp.float32
BF16 = jnp.bfloat16

HEAD_DIM = 128
Q_LORA_RANK = 1536
KV_LORA_RANK = 512
QK_NOPE_DIM = 128
QK_ROPE_DIM = 64
V_HEAD_DIM = 128
ROPE_THETA = 10000.0
DIL_PATTERNS = ((128, 1), (512, 4), (2048, 16))
BLOCK = 128
REL_BUCKETS = 32
REL_MAX_EXACT = REL_BUCKETS // 2
REL_MAX_DISTANCE = 2048
N_EXPERTS = 32
TOP_K = 4
EXPERT_FF = 1536
SWIGLU_LIMIT = 7.0
SWIGLU_ALPHA = 1.702
LN_EPS = 1e-5
RMS_EPS = 1e-6
NEG_BIG = -1e30

LANES = 128
QK_PAD = 256
VMEM_LIMIT = 56 * 1024 * 1024
MOE_ROWS = 1280
MOE_SUB = 512
MOE_TAIL = 128
MOE_BLK = 256


def _cparams(sem):
    return pltpu.CompilerParams(dimension_semantics=sem, vmem_limit_bytes=VMEM_LIMIT)


def _ada_kernel(c_ref, w_ref, b_ref, o_ref):
    c = c_ref[...]
    cond = (c * jax.nn.sigmoid(c)).astype(BF16)
    o_ref[...] = jnp.dot(cond, w_ref[...].astype(BF16), preferred_element_type=F32) + b_ref[...]


def _ada(c_pad, w_ada, b_ada3, l, tn=512):
    rows, d = c_pad.shape
    n = w_ada.shape[-1]
    return pl.pallas_call(
        _ada_kernel,
        grid=(n // tn,),
        in_specs=[
            pl.BlockSpec((rows, d), lambda j: (0, 0)),
            pl.BlockSpec((None, d, tn), lambda j: (l, 0, j)),
            pl.BlockSpec((None, 1, tn), lambda j: (l, 0, j)),
        ],
        out_specs=pl.BlockSpec((rows, tn), lambda j: (0, j)),
        out_shape=jax.ShapeDtypeStruct((rows, n), F32),
        compiler_params=_cparams(("arbitrary",)),
        name="ada_mod",
    )(c_pad, w_ada, b_ada3)


def _modulate_kernel(x_ref, mod_ref, o_ref, *, shift_row, scale_row):
    sh = mod_ref[shift_row:shift_row + 1, :]
    sc = mod_ref[scale_row:scale_row + 1, :]
    o_ref[...] = (x_ref[...] * (1.0 + sc) + sh).astype(BF16)


def _modulate(x2, mod3, seq, shift_row, scale_row, tm=512):
    t, d = x2.shape
    per_b = seq // tm
    return pl.pallas_call(
        functools.partial(_modulate_kernel, shift_row=shift_row, scale_row=scale_row),
        grid=(t // tm,),
        in_specs=[
            pl.BlockSpec((tm, d), lambda i: (i, 0)),
            pl.BlockSpec((None, 6, d), lambda i: (i // per_b, 0, 0)),
        ],
        out_specs=pl.BlockSpec((tm, d), lambda i: (i, 0)),
        out_shape=jax.ShapeDtypeStruct((t, d), BF16),
        compiler_params=_cparams(("arbitrary",)),
        name="modulate",
    )(x2, mod3)


def _rope_block(x, cos_t, sin_t):
    lane = lax.broadcasted_iota(jnp.int32, x.shape, 1)
    half = QK_ROPE_DIM // 2
    swapped = jnp.where(lane < half, pltpu.roll(x, LANES - half, 1), pltpu.roll(x, half, 1))
    return x * cos_t + swapped * sin_t


def _w_in_layout_kernel(w_ref, o_ref, *, mla_cols, dil_cols):
    rope0 = mla_cols
    dil0 = mla_cols + QK_ROPE_DIM
    o_ref[:, :mla_cols] = w_ref[:, :mla_cols].astype(BF16)
    o_ref[:, mla_cols:mla_cols + dil_cols] = w_ref[:, dil0:dil0 + dil_cols].astype(BF16)
    tail = w_ref[:, rope0:rope0 + LANES]
    lane = lax.broadcasted_iota(jnp.int32, tail.shape, 1)
    o_ref[:, mla_cols + dil_cols:] = jnp.where(lane < QK_ROPE_DIM, tail, 0.0).astype(BF16)


def _w_in_layout(w_in, l, mla_cols, dil_cols, tk=256):
    _, d, n = w_in.shape
    n_out = mla_cols + dil_cols + LANES
    return pl.pallas_call(
        functools.partial(_w_in_layout_kernel, mla_cols=mla_cols, dil_cols=dil_cols),
        grid=(d // tk,),
        in_specs=[pl.BlockSpec((None, tk, n), lambda i: (l, i, 0))],
        out_specs=pl.BlockSpec((tk, n_out), lambda i: (i, 0)),
        out_shape=jax.ShapeDtypeStruct((d, n_out), BF16),
        compiler_params=_cparams(("arbitrary",)),
        name="w_in_layout",
    )(w_in)


def _proj_plain_kernel(a_ref, w_ref, o_ref):
    o_ref[...] = jnp.dot(a_ref[...], w_ref[...], preferred_element_type=F32).astype(o_ref.dtype)


def _proj_heads_kernel(a_ref, w_ref, o_ref, *, heads_per_tile):
    res = jnp.dot(a_ref[...], w_ref[...], preferred_element_type=F32)
    for hh in range(heads_per_tile):
        o_ref[hh] = res[:, hh * HEAD_DIM:(hh + 1) * HEAD_DIM].astype(o_ref.dtype)


def _proj_rope_kernel(a_ref, w_ref, cos_ref, sin_ref, o_ref):
    res = jnp.dot(a_ref[...], w_ref[...], preferred_element_type=F32)
    o_ref[...] = _rope_block(res, cos_ref[...], sin_ref[...]).astype(o_ref.dtype)


def _in_proj_mla(u, w_p, n_cols, tm=512, tn=512):
    t, d = u.shape
    return pl.pallas_call(
        _proj_plain_kernel,
        grid=(n_cols // tn, t // tm),
        in_specs=[
            pl.BlockSpec((tm, d), lambda j, i: (i, 0)),
            pl.BlockSpec((d, tn), lambda j, i: (0, j)),
        ],
        out_specs=pl.BlockSpec((tm, tn), lambda j, i: (i, j)),
        out_shape=jax.ShapeDtypeStruct((t, n_cols), BF16),
        compiler_params=_cparams(("arbitrary", "arbitrary")),
        name="in_proj_mla",
    )(u, w_p)


def _in_proj_dil(u, w_p, col0, batch, seq, n_heads, tm=512, tn=512):
    t, d = u.shape
    hpt = tn // HEAD_DIM
    tiles_per_mat = n_heads // hpt
    per_b = seq // tm
    off = col0 // tn
    return pl.pallas_call(
        functools.partial(_proj_heads_kernel, heads_per_tile=hpt),
        grid=(3 * tiles_per_mat, t // tm),
        in_specs=[
            pl.BlockSpec((tm, d), lambda j, i: (i, 0)),
            pl.BlockSpec((d, tn), lambda j, i: (0, j + off)),
        ],
        out_specs=pl.BlockSpec(
            (None, None, hpt, tm, HEAD_DIM),
            lambda j, i: (j // tiles_per_mat, i // per_b, j % tiles_per_mat, i % per_b, 0)),
        out_shape=jax.ShapeDtypeStruct((3, batch, n_heads, seq, HEAD_DIM), BF16),
        compiler_params=_cparams(("arbitrary", "arbitrary")),
        name="in_proj_dil",
    )(u, w_p)


def _in_proj_krope(u, w_p, col0, cos_t, sin_t, tm=1024):
    t, d = u.shape
    off = col0 // LANES
    return pl.pallas_call(
        _proj_rope_kernel,
        grid=(t // tm,),
        in_specs=[
            pl.BlockSpec((tm, d), lambda i: (i, 0)),
            pl.BlockSpec((d, LANES), lambda i: (0, off)),
            pl.BlockSpec((tm, LANES), lambda i: (i, 0)),
            pl.BlockSpec((tm, LANES), lambda i: (i, 0)),
        ],
        out_specs=pl.BlockSpec((tm, LANES), lambda i: (i, 0)),
        out_shape=jax.ShapeDtypeStruct((t, LANES), BF16),
        compiler_params=_cparams(("arbitrary",)),
        name="in_proj_krope",
    )(u, w_p, cos_t, sin_t)


def _rms_bf16(h_ref, g_ref):
    hf = h_ref[...].astype(F32)
    y = hf * lax.rsqrt(jnp.mean(hf * hf, axis=-1, keepdims=True) + RMS_EPS)
    return (y * g_ref[...]).astype(BF16)


def _q_up_kernel(h_ref, g_ref, w_ref, cos_ref, sin_ref, o_ref, *, n_heads, scale):
    yb = _rms_bf16(h_ref, g_ref)
    cos_t = cos_ref[...]
    sin_t = sin_ref[...]
    group = 4
    for h0 in range(0, n_heads, group):
        res = jnp.dot(yb, w_ref[:, h0 * QK_PAD:(h0 + group) * QK_PAD], preferred_element_type=F32)
        for hh in range(group):
            c0 = hh * QK_PAD
            nope = res[:, c0:c0 + QK_NOPE_DIM]
            rp = _rope_block(res[:, c0 + QK_NOPE_DIM:c0 + QK_PAD], cos_t, sin_t)
            o0 = (h0 + hh) * QK_PAD
            o_ref[:, o0:o0 + QK_NOPE_DIM] = (nope * scale).astype(BF16)
            o_ref[:, o0 + QK_NOPE_DIM:o0 + QK_PAD] = (rp * scale).astype(BF16)


def _q_up(hm, g3, l, w_uq_p, cos_t, sin_t, n_heads, tm=256):
    t = hm.shape[0]
    scale = float(QK_NOPE_DIM + QK_ROPE_DIM) ** -0.5
    return pl.pallas_call(
        functools.partial(_q_up_kernel, n_heads=n_heads, scale=scale),
        grid=(t // tm,),
        in_specs=[
            pl.BlockSpec((tm, Q_LORA_RANK), lambda i: (i, 0)),
            pl.BlockSpec((None, 1, Q_LORA_RANK), lambda i: (l, 0, 0)),
            pl.BlockSpec((Q_LORA_RANK, n_heads * QK_PAD), lambda i: (0, 0)),
            pl.BlockSpec((tm, LANES), lambda i: (i, 0)),
            pl.BlockSpec((tm, LANES), lambda i: (i, 0)),
        ],
        out_specs=pl.BlockSpec((tm, n_heads * QK_PAD), lambda i: (i, 0)),
        out_shape=jax.ShapeDtypeStruct((t, n_heads * QK_PAD), BF16),
        compiler_params=_cparams(("arbitrary",)),
        name="mla_q_up",
    )(hm, g3, w_uq_p, cos_t, sin_t)


def _kv_up_kernel(h_ref, g_ref, wk_ref, wv_ref, kr_ref, k_ref, v_ref, *, n_heads):
    yb = _rms_bf16(h_ref, g_ref)
    kr = kr_ref[...]
    group = 4
    for h0 in range(0, n_heads, group):
        res = jnp.dot(yb, wk_ref[:, h0 * QK_NOPE_DIM:(h0 + group) * QK_NOPE_DIM],
                      preferred_element_type=F32)
        for hh in range(group):
            o0 = (h0 + hh) * QK_PAD
            k_ref[:, o0:o0 + QK_NOPE_DIM] = res[:, hh * QK_NOPE_DIM:(hh + 1) * QK_NOPE_DIM].astype(BF16)
            k_ref[:, o0 + QK_NOPE_DIM:o0 + QK_PAD] = kr
    v_ref[...] = jnp.dot(yb, wv_ref[...], preferred_element_type=F32).astype(BF16)


def _kv_up(hm, g3, l, w_uk, w_uv, krr, n_heads, tm=256):
    t = hm.shape[0]
    col_blk = Q_LORA_RANK // KV_LORA_RANK
    return pl.pallas_call(
        functools.partial(_kv_up_kernel, n_heads=n_heads),
        grid=(t // tm,),
        in_specs=[
            pl.BlockSpec((tm, KV_LORA_RANK), lambda i: (i, col_blk)),
            pl.BlockSpec((None, 1, KV_LORA_RANK), lambda i: (l, 0, 0)),
            pl.BlockSpec((KV_LORA_RANK, n_heads * QK_NOPE_DIM), lambda i: (0, 0)),
            pl.BlockSpec((KV_LORA_RANK, n_heads * V_HEAD_DIM), lambda i: (0, 0)),
            pl.BlockSpec((tm, LANES), lambda i: (i, 0)),
        ],
        out_specs=[
            pl.BlockSpec((tm, n_heads * QK_PAD), lambda i: (i, 0)),
            pl.BlockSpec((tm, n_heads * V_HEAD_DIM), lambda i: (i, 0)),
        ],
        out_shape=[
            jax.ShapeDtypeStruct((t, n_heads * QK_PAD), BF16),
            jax.ShapeDtypeStruct((t, n_heads * V_HEAD_DIM), BF16),
        ],
        compiler_params=_cparams(("arbitrary",)),
        name="mla_kv_up",
    )(hm, g3, w_uk, w_uv, krr)


MLA_HEADS_PER_STEP = 2


def _mla_attn_kernel(q_ref, k_ref, v_ref, o_ref, *, seq, tq):
    nq = seq // tq
    hp = MLA_HEADS_PER_STEP
    row = lax.broadcasted_iota(jnp.int32, (tq, tq), 0)
    col = lax.broadcasted_iota(jnp.int32, (tq, tq), 1)
    causal = col <= row

    def kv_step(qs, j, carry, masked):
        off = pl.multiple_of(j * tq, tq)
        out = []
        for hh in range(hp):
            m, l, acc = carry[hh]
            k = k_ref[pl.ds(off, tq), hh * QK_PAD:(hh + 1) * QK_PAD]
            v = v_ref[pl.ds(off, tq), hh * V_HEAD_DIM:(hh + 1) * V_HEAD_DIM]
            s = lax.dot_general(qs[hh], k, (((1,), (1,)), ((), ())), preferred_element_type=F32)
            if masked:
                s = jnp.where(causal, s, NEG_BIG)
            m_new = jnp.maximum(m, jnp.max(s, axis=-1, keepdims=True))
            p = jnp.exp(s - m_new)
            alpha = jnp.exp(m - m_new)
            l_new = alpha * l + jnp.sum(p, axis=-1, keepdims=True)
            acc_new = alpha * acc + jnp.dot(p.astype(BF16), v, preferred_element_type=F32)
            out.append((m_new, l_new, acc_new))
        return tuple(out)

    def q_loop(i, _):
        qoff = pl.multiple_of(i * tq, tq)
        qs = [q_ref[pl.ds(qoff, tq), hh * QK_PAD:(hh + 1) * QK_PAD] for hh in range(hp)]
        init = tuple((jnp.full((tq, 1), NEG_BIG, F32), jnp.zeros((tq, 1), F32),
                      jnp.zeros((tq, V_HEAD_DIM), F32)) for _ in range(hp))
        carry = lax.fori_loop(0, i, lambda j, c: kv_step(qs, j, c, False), init)
        carry = kv_step(qs, i, carry, True)
        for hh in range(hp):
            m, l, acc = carry[hh]
            o_ref[pl.ds(qoff, tq), hh * V_HEAD_DIM:(hh + 1) * V_HEAD_DIM] = (acc / l).astype(o_ref.dtype)
        return 0

    lax.fori_loop(0, nq, q_loop, 0)


def _mla_attn(q, k, v, batch, seq, n_heads, tq=512):
    t = q.shape[0]
    hp = MLA_HEADS_PER_STEP
    return pl.pallas_call(
        functools.partial(_mla_attn_kernel, seq=seq, tq=tq),
        grid=(batch, n_heads // hp),
        in_specs=[
            pl.BlockSpec((seq, hp * QK_PAD), lambda b, h: (b, h)),
            pl.BlockSpec((seq, hp * QK_PAD), lambda b, h: (b, h)),
            pl.BlockSpec((seq, hp * V_HEAD_DIM), lambda b, h: (b, h)),
        ],
        out_specs=pl.BlockSpec((seq, hp * V_HEAD_DIM), lambda b, h: (b, h)),
        out_shape=jax.ShapeDtypeStruct((t, n_heads * V_HEAD_DIM), BF16),
        compiler_params=_cparams(("arbitrary", "arbitrary")),
        name="mla_attn",
    )(q, k, v)


DIL_GROUP = 4


def _dil_attn_kernel(qkv_ref, bvec_ref, o_ref, m_s, l_s, acc_s, bias_s, nat_f, *view_s, seq):
    scale = float(HEAD_DIM) ** -0.5
    views = (qkv_ref,) + tuple(view_s)

    for bi in range(len(DIL_PATTERNS)):
        full = jnp.broadcast_to(bvec_ref[bi], (BLOCK, 2 * BLOCK))
        bias_s[bi] = pltpu.roll(full, 0, 1, stride=1, stride_axis=0)

    for which in range(3):
        nat_f[...] = qkv_ref[which].astype(F32)
        for bi, (_, d) in enumerate(DIL_PATTERNS):
            if d == 1:
                continue
            for r in range(d):
                views[bi][which, :, r * HEAD_DIM:(r + 1) * HEAD_DIM] = (
                    nat_f[pl.ds(r, seq // d, stride=d), :].astype(BF16))

    def scores(ref, bi, d, r, i, first):
        lanes = slice(r * HEAD_DIM, (r + 1) * HEAD_DIM)
        qoff = pl.multiple_of(i * BLOCK, BLOCK)
        q = ref[0, pl.ds(qoff, BLOCK), lanes]
        if first:
            kk = ref[1, pl.ds(0, BLOCK), lanes]
            vv = ref[2, pl.ds(0, BLOCK), lanes]
            bias = bias_s[bi, :, BLOCK:2 * BLOCK]
        else:
            koff = pl.multiple_of(i * BLOCK - BLOCK, BLOCK)
            kk = ref[1, pl.ds(koff, 2 * BLOCK), lanes]
            vv = ref[2, pl.ds(koff, 2 * BLOCK), lanes]
            bias = bias_s[bi]
        s = lax.dot_general(q, kk, (((1,), (1,)), ((), ())), preferred_element_type=F32)
        s = s * scale + bias
        m_b = jnp.max(s, axis=-1, keepdims=True)
        p = jnp.exp(s - m_b)
        l_b = jnp.sum(p, axis=-1, keepdims=True)
        a_b = jnp.dot(p.astype(BF16), vv, preferred_element_type=F32)
        m_b = jnp.broadcast_to(m_b, (BLOCK, HEAD_DIM))
        l_b = jnp.broadcast_to(l_b, (BLOCK, HEAD_DIM))
        if d == 1:
            rows = pl.ds(qoff, BLOCK)
        else:
            rows = pl.ds(i * (BLOCK * d) + r, BLOCK, stride=d)
        return m_b, l_b, a_b, rows

    def group(ref, bi, d, blocks):
        parts = [scores(ref, bi, d, r, i, first) for r, i, first in blocks]
        if d == 1:
            for m_b, l_b, a_b, rows in parts:
                m_s[rows, :] = m_b
                l_s[rows, :] = l_b
                acc_s[rows, :] = a_b
            return
        old = [(m_s[rows, :], l_s[rows, :], acc_s[rows, :]) for _, _, _, rows in parts]
        new = []
        for (m_b, l_b, a_b, rows), (m_o, l_o, a_o) in zip(parts, old):
            m_n = jnp.maximum(m_o, m_b)
            e_o = jnp.exp(m_o - m_n)
            e_b = jnp.exp(m_b - m_n)
            new.append((m_n, e_o * l_o + e_b * l_b, e_o * a_o + e_b * a_b, rows))
        for m_n, l_n, a_n, rows in new:
            m_s[rows, :] = m_n
            l_s[rows, :] = l_n
            acc_s[rows, :] = a_n

    for bi, (_, d) in enumerate(DIL_PATTERNS):
        ref = views[bi]
        nblk = seq // d // BLOCK
        if d == 1:
            assert nblk % DIL_GROUP == 0
            group(ref, bi, d, [(0, u, u == 0) for u in range(DIL_GROUP)])

            def body1(g, _, ref=ref, bi=bi, d=d):
                group(ref, bi, d, [(0, g * DIL_GROUP + u, False) for u in range(DIL_GROUP)])
                return 0

            lax.fori_loop(1, nblk // DIL_GROUP, body1, 0)
        else:
            assert d % DIL_GROUP == 0
            for r0 in range(0, d, DIL_GROUP):
                group(ref, bi, d, [(r0 + u, 0, True) for u in range(DIL_GROUP)])

                def body(i, _, ref=ref, bi=bi, d=d, r0=r0):
                    group(ref, bi, d, [(r0 + u, i, False) for u in range(DIL_GROUP)])
                    return 0

                lax.fori_loop(1, nblk, body, 0)

    o_ref[...] = (acc_s[...] / l_s[...]).astype(o_ref.dtype)


def _dil_attn(qkv, bias_vec, batch, seq, n_heads):
    t = batch * seq
    nbr = len(DIL_PATTERNS)
    assert DIL_PATTERNS[0][1] == 1
    view_scratch = [pltpu.VMEM((3, seq // d, d * HEAD_DIM), BF16) for _, d in DIL_PATTERNS[1:]]
    return pl.pallas_call(
        functools.partial(_dil_attn_kernel, seq=seq),
        grid=(batch, n_heads),
        in_specs=[
            pl.BlockSpec((3, None, None, seq, HEAD_DIM), lambda b, h: (0, b, h, 0, 0)),
            pl.BlockSpec((nbr, None, 1, 2 * BLOCK), lambda b, h: (0, h, 0, 0)),
        ],
        out_specs=pl.BlockSpec((seq, HEAD_DIM), lambda b, h: (b, h)),
        out_shape=jax.ShapeDtypeStruct((t, n_heads * HEAD_DIM), BF16),
        scratch_shapes=[pltpu.VMEM((seq, HEAD_DIM), F32)] * 3
        + [pltpu.VMEM((nbr, BLOCK, 2 * BLOCK), F32), pltpu.VMEM((seq, HEAD_DIM), F32)]
        + view_scratch,
        compiler_params=_cparams(("arbitrary", "arbitrary")),
        name="dil_attn",
    )(qkv, bias_vec)


def _out_proj_kernel(a1_ref, a2_ref, w_ref, o_ref, wb_ref):
    @pl.when(pl.program_id(1) == 0)
    def _():
        wb_ref[...] = w_ref[...].astype(BF16)

    k1 = a1_ref.shape[1]
    acc = jnp.dot(a1_ref[...], wb_ref[:k1, :], preferred_element_type=F32)
    acc = acc + jnp.dot(a2_ref[...], wb_ref[k1:, :], preferred_element_type=F32)
    o_ref[...] = acc.astype(o_ref.dtype)


def _out_proj(o_a, o_b, w_o, l, tm=512, tn=512):
    t, k1 = o_a.shape
    k2 = o_b.shape[1]
    n = w_o.shape[-1]
    return pl.pallas_call(
        _out_proj_kernel,
        grid=(n // tn, t // tm),
        in_specs=[
            pl.BlockSpec((tm, k1), lambda j, i: (i, 0)),
            pl.BlockSpec((tm, k2), lambda j, i: (i, 0)),
            pl.BlockSpec((None, k1 + k2, tn), lambda j, i: (l, 0, j)),
        ],
        out_specs=pl.BlockSpec((tm, tn), lambda j, i: (i, j)),
        out_shape=jax.ShapeDtypeStruct((t, n), BF16),
        scratch_shapes=[pltpu.VMEM((k1 + k2, tn), BF16)],
        compiler_params=_cparams(("arbitrary", "arbitrary")),
        name="out_proj",
    )(o_a, o_b, w_o)


def _layer_norm_rows(z, g, b):
    mu = jnp.mean(z, axis=-1, keepdims=True)
    zc = z - mu
    var = jnp.mean(zc * zc, axis=-1, keepdims=True)
    return zc * lax.rsqrt(var + LN_EPS) * g + b


SLAB_CHUNK = 1024
SLAB_Q = SLAB_CHUNK // (2 * LANES)


def _slab_sub(width):
    return width // (2 * LANES)


def _slab_index(j, first_tok, n_tok, sub):
    return (pl.ds(first_tok * sub + j, n_tok, stride=sub), slice(None))


def _pack_chunk(vals):
    bits = lax.bitcast_convert_type(vals.astype(BF16).astype(F32), jnp.uint32)
    half = SLAB_CHUNK // 2
    return [(bits[:, q * LANES:(q + 1) * LANES] >> 16)
            | bits[:, half + q * LANES:half + (q + 1) * LANES] for q in range(SLAB_Q)]


def _unpack_words(words):
    lo = lax.bitcast_convert_type(words << 16, F32)
    hi = lax.bitcast_convert_type(words & jnp.uint32(0xFFFF0000), F32)
    return lo, hi


def _ln1_router_kernel(x_ref, mix_ref, mod_ref, g_ref, b_ref, wr_ref, br_ref,
                       x1_ref, u2p_ref, ids_ref, gates_ref, *, alpha):
    gt1 = mod_ref[2:3, :]
    sh2 = mod_ref[3:4, :]
    sc2 = mod_ref[4:5, :]
    z = alpha * x_ref[...] + gt1 * mix_ref[...].astype(F32)
    x1 = _layer_norm_rows(z, g_ref[...], b_ref[...])
    x1_ref[...] = x1
    u2 = (x1 * (1.0 + sc2) + sh2).astype(BF16)
    tm, d = u2.shape
    for g in range(d // SLAB_CHUNK):
        for q, words in enumerate(_pack_chunk(u2[:, g * SLAB_CHUNK:(g + 1) * SLAB_CHUNK])):
            u2p_ref[_slab_index(g * SLAB_Q + q, 0, tm, _slab_sub(d))] = words
    logits = jnp.dot(u2, wr_ref[...].astype(BF16), preferred_element_type=F32) + br_ref[...]
    lane = lax.broadcasted_iota(jnp.int32, logits.shape, 1)
    lane_f = lane.astype(F32)
    vals = []
    ids = []
    for _ in range(TOP_K):
        mk = jnp.max(logits, axis=-1, keepdims=True)
        idx_f = jnp.min(jnp.where(logits == mk, lane_f, float(LANES)), axis=-1, keepdims=True)
        vals.append(mk)
        ids.append(idx_f.astype(jnp.int32))
        logits = jnp.where(lane_f == idx_f, -jnp.inf, logits)
    exps = [jnp.exp(v - vals[0]) for v in vals]
    denom = exps[0]
    for e in exps[1:]:
        denom = denom + e
    ids_out = jnp.zeros(lane.shape, jnp.int32)
    gates_out = jnp.zeros(lane.shape, F32)
    for k in range(TOP_K):
        ids_out = jnp.where(lane == k, ids[k], ids_out)
        gates_out = jnp.where(lane == k, exps[k] / denom, gates_out)
    ids_ref[...] = ids_out
    gates_ref[...] = gates_out


def _ln1_router(x2, mix, mod3, g3, b3, wr_p, br_p, l, seq, alpha, tm=256):
    t, d = x2.shape
    per_b = seq // tm
    return pl.pallas_call(
        functools.partial(_ln1_router_kernel, alpha=alpha),
        grid=(t // tm,),
        in_specs=[
            pl.BlockSpec((tm, d), lambda i: (i, 0)),
            pl.BlockSpec((tm, d), lambda i: (i, 0)),
            pl.BlockSpec((None, 6, d), lambda i: (i // per_b, 0, 0)),
            pl.BlockSpec((None, 1, d), lambda i: (l, 0, 0)),
            pl.BlockSpec((None, 1, d), lambda i: (l, 0, 0)),
            pl.BlockSpec((d, LANES), lambda i: (0, 0)),
            pl.BlockSpec((1, LANES), lambda i: (0, 0)),
        ],
        out_specs=[
            pl.BlockSpec((tm, d), lambda i: (i, 0)),
            pl.BlockSpec((tm * _slab_sub(d), LANES), lambda i: (i, 0)),
            pl.BlockSpec((tm, LANES), lambda i: (i, 0)),
            pl.BlockSpec((tm, LANES), lambda i: (i, 0)),
        ],
        out_shape=[
            jax.ShapeDtypeStruct((t, d), F32),
            jax.ShapeDtypeStruct((t * _slab_sub(d), LANES), jnp.uint32),
            jax.ShapeDtypeStruct((t, LANES), jnp.int32),
            jax.ShapeDtypeStruct((t, LANES), F32),
        ],
        compiler_params=_cparams(("arbitrary",)),
        name="ln1_router",
    )(x2, mix, mod3, g3, b3, wr_p, br_p)


def _moe_row_tiles(n_full, n_tail, tile_fn):
    def full(j, _):
        tile_fn(pl.multiple_of(j * MOE_SUB, MOE_SUB), MOE_SUB)
        return 0

    def tail(k, _):
        tile_fn(pl.multiple_of(n_full * MOE_SUB + k * MOE_TAIL, MOE_TAIL), MOE_TAIL)
        return 0

    lax.fori_loop(0, n_full, full, 0)
    lax.fori_loop(0, n_tail, tail, 0)


def _moe_up_kernel(item_e, item_blk, item_full, item_tail, x_ref, wg_ref, wu_ref, bg_ref, bu_ref,
                   h_ref, wg_s, wu_s):
    w = pl.program_id(0)
    n_full = item_full[w]
    n_tail = item_tail[w]

    @pl.when(n_full + n_tail > 0)
    def _():
        wg_s[...] = wg_ref[...].astype(BF16)
        wu_s[...] = wu_ref[...].astype(BF16)
        bg = bg_ref[...]
        bu = bu_ref[...]

        def tile(off, rows):
            xt = x_ref[pl.ds(off, rows), :]
            glu = jnp.dot(xt, wg_s[...], preferred_element_type=F32) + bg
            lin = jnp.dot(xt, wu_s[...], preferred_element_type=F32) + bu
            glu = jnp.minimum(glu, SWIGLU_LIMIT)
            lin = jnp.clip(lin, -SWIGLU_LIMIT, SWIGLU_LIMIT)
            act = glu * jax.nn.sigmoid(SWIGLU_ALPHA * glu) * (lin + 1.0)
            h_ref[pl.ds(off, rows), :] = act.astype(h_ref.dtype)

        _moe_row_tiles(n_full, n_tail, tile)


def _moe_up(xs, w_gate, w_up, b_gate4, b_up4, item_e, item_blk, item_full, item_tail, l, tf=256):
    r, d = xs.shape
    n_items = r // MOE_ROWS
    ff = w_gate.shape[-1]
    n_chunks = ff // tf

    def cmap(w, c, ifull, itail):
        return jnp.where(ifull[w] + itail[w] > 0, c, n_chunks - 1)

    def wmap(w, c, ie, ib, ifull, itail):
        return (l, ie[w], 0, cmap(w, c, ifull, itail))

    grid_spec = pltpu.PrefetchScalarGridSpec(
        num_scalar_prefetch=4,
        grid=(n_items, n_chunks),
        in_specs=[
            pl.BlockSpec((MOE_ROWS, d), lambda w, c, ie, ib, ifull, itail: (ib[w], 0)),
            pl.BlockSpec((None, None, d, tf), wmap),
            pl.BlockSpec((None, None, d, tf), wmap),
            pl.BlockSpec((None, None, 1, tf), wmap),
            pl.BlockSpec((None, None, 1, tf), wmap),
        ],
        out_specs=pl.BlockSpec(
            (MOE_ROWS, tf),
            lambda w, c, ie, ib, ifull, itail: (ib[w], cmap(w, c, ifull, itail))),
        scratch_shapes=[pltpu.VMEM((d, tf), BF16), pltpu.VMEM((d, tf), BF16)],
    )
    return pl.pallas_call(
        _moe_up_kernel,
        grid_spec=grid_spec,
        out_shape=jax.ShapeDtypeStruct((r, ff), BF16),
        compiler_params=_cparams(("arbitrary", "arbitrary")),
        name="moe_up",
    )(item_e, item_blk, item_full, item_tail, xs, w_gate, w_up, b_gate4, b_up4)


def _moe_down_kernel(item_e, item_blk, item_full, item_tail, h_ref, wd_ref, bd_ref, y_ref, wd_s,
                     *, sub):
    w = pl.program_id(0)
    c = pl.program_id(1)
    n_full = item_full[w]
    n_tail = item_tail[w]

    @pl.when(n_full + n_tail > 0)
    def _():
        wd_s[...] = wd_ref[...].astype(BF16)
        bd = bd_ref[...]

        def tile(off, rows):
            ht = h_ref[pl.ds(off, rows), :]
            y = jnp.dot(ht, wd_s[...], preferred_element_type=F32) + bd
            for q, words in enumerate(_pack_chunk(y)):
                y_ref[_slab_index(c * SLAB_Q + q, off, rows, sub)] = words

        _moe_row_tiles(n_full, n_tail, tile)


def _moe_down(hs, w_down, b_down4, item_e, item_blk, item_full, item_tail, l):
    r, ff = hs.shape
    n_items = r // MOE_ROWS
    d = w_down.shape[-1]
    tn = SLAB_CHUNK
    n_chunks = d // tn
    sub = _slab_sub(d)

    def cmap(w, c, ifull, itail):
        return jnp.where(ifull[w] + itail[w] > 0, c, n_chunks - 1)

    def wmap(w, c, ie, ib, ifull, itail):
        return (l, ie[w], 0, cmap(w, c, ifull, itail))

    grid_spec = pltpu.PrefetchScalarGridSpec(
        num_scalar_prefetch=4,
        grid=(n_items, n_chunks),
        in_specs=[
            pl.BlockSpec((MOE_ROWS, ff), lambda w, c, ie, ib, ifull, itail: (ib[w], 0)),
            pl.BlockSpec((None, None, ff, tn), wmap),
            pl.BlockSpec((None, None, 1, tn), wmap),
        ],
        out_specs=pl.BlockSpec((MOE_ROWS * sub, LANES),
                               lambda w, c, ie, ib, ifull, itail: (ib[w], 0)),
        scratch_shapes=[pltpu.VMEM((ff, tn), BF16)],
    )
    return pl.pallas_call(
        functools.partial(_moe_down_kernel, sub=sub),
        grid_spec=grid_spec,
        out_shape=jax.ShapeDtypeStruct((r * sub, LANES), jnp.uint32),
        compiler_params=_cparams(("arbitrary", "arbitrary")),
        name="moe_down",
    )(item_e, item_blk, item_full, item_tail, hs, w_down, b_down4)


GATHER_UNROLL = 8


def _slab_gather(idx_ref, src_ref, buf, sem, base_slab, n, sub, wait):
    def copy(r):
        src_row = pl.multiple_of(idx_ref[0, r] * sub, sub)
        dst_row = pl.multiple_of((base_slab + r) * sub, sub)
        return pltpu.make_async_copy(src_ref.at[pl.ds(src_row, sub), :],
                                     buf.at[pl.ds(dst_row, sub), :], sem)

    def body(r0, _):
        for u in range(GATHER_UNROLL):
            c = copy(r0 * GATHER_UNROLL + u)
            if wait:
                c.wait()
            else:
                c.start(priority=u % 2)
        return 0

    lax.fori_loop(0, n // GATHER_UNROLL, body, 0)


def _dispatch_kernel(vblk, nvalid, tok_ref, tok_next_ref, src_ref, o_ref, buf, sems, *, rows, d):
    s = pl.program_id(0)
    slot = s % 2
    sub = _slab_sub(d)

    @pl.when(s == 0)
    def _():
        _slab_gather(tok_ref, src_ref, buf, sems.at[0], 0, rows, sub, wait=False)

    @pl.when(s + 1 < nvalid[0])
    def _():
        _slab_gather(tok_next_ref, src_ref, buf, sems.at[1 - slot], (1 - slot) * rows, rows, sub,
                     wait=False)

    @pl.when((s < nvalid[0]) | (s == 0))
    def _():
        _slab_gather(tok_ref, src_ref, buf, sems.at[slot], slot * rows, rows, sub, wait=True)
        half = SLAB_CHUNK // 2
        for g in range(d // SLAB_CHUNK):
            for q in range(SLAB_Q):
                lo, hi = _unpack_words(buf[_slab_index(g * SLAB_Q + q, slot * rows, rows, sub)])
                c0 = g * SLAB_CHUNK + q * LANES
                o_ref[:, c0:c0 + LANES] = lo.astype(BF16)
                o_ref[:, c0 + half:c0 + half + LANES] = hi.astype(BF16)


def _dispatch(u2p, d, row_tok, vblk, nvalid, rows=MOE_BLK):
    r = row_tok.shape[0]
    n_blk = r // rows
    n_steps = vblk.shape[0]
    tok3 = row_tok.reshape(n_blk, 1, rows)
    grid_spec = pltpu.PrefetchScalarGridSpec(
        num_scalar_prefetch=2,
        grid=(n_steps,),
        in_specs=[
            pl.BlockSpec((None, 1, rows), lambda s, vb, nv: (vb[s], 0, 0), memory_space=pltpu.SMEM),
            pl.BlockSpec((None, 1, rows), lambda s, vb, nv: (vb[jnp.minimum(s + 1, n_steps - 1)], 0, 0),
                         memory_space=pltpu.SMEM),
            pl.BlockSpec(memory_space=pl.ANY),
        ],
        out_specs=pl.BlockSpec((rows, d), lambda s, vb, nv: (vb[s], 0)),
        scratch_shapes=[pltpu.VMEM((2 * rows * _slab_sub(d), LANES), jnp.uint32),
                        pltpu.SemaphoreType.DMA((2,))],
    )
    return pl.pallas_call(
        functools.partial(_dispatch_kernel, rows=rows, d=d),
        grid_spec=grid_spec,
        out_shape=jax.ShapeDtypeStruct((r, d), BF16),
        compiler_params=_cparams(("arbitrary",)),
        name="moe_dispatch",
    )(vblk, nvalid, tok3, tok3, u2p)


def _combine_ln2_kernel(slot_ref, slot_next_ref, x1_ref, ys_ref, gates_ref, mod_ref, g_ref, b_ref,
                        o_ref, buf, ffn_s, sems, *, alpha, tm):
    i = pl.program_id(0)
    n = pl.num_programs(0)
    cur = i % 2
    n_rows = TOP_K * tm
    d = ffn_s.shape[1]
    sub = _slab_sub(d)

    @pl.when(i == 0)
    def _():
        _slab_gather(slot_ref, ys_ref, buf, sems.at[0], 0, n_rows, sub, wait=False)

    @pl.when(i + 1 < n)
    def _():
        _slab_gather(slot_next_ref, ys_ref, buf, sems.at[1 - cur], (1 - cur) * n_rows, n_rows, sub,
                     wait=False)

    _slab_gather(slot_ref, ys_ref, buf, sems.at[cur], cur * n_rows, n_rows, sub, wait=True)

    gates = gates_ref[...]
    gk = [jnp.broadcast_to(gates[:, k:k + 1], (tm, LANES)) for k in range(TOP_K)]
    half = SLAB_CHUNK // 2
    for g in range(d // SLAB_CHUNK):
        for q in range(SLAB_Q):
            lo_acc = None
            hi_acc = None
            for k in range(TOP_K):
                idx = _slab_index(g * SLAB_Q + q, cur * n_rows + k * tm, tm, sub)
                lo, hi = _unpack_words(buf[idx])
                lo_acc = gk[k] * lo if lo_acc is None else lo_acc + gk[k] * lo
                hi_acc = gk[k] * hi if hi_acc is None else hi_acc + gk[k] * hi
            c0 = g * SLAB_CHUNK + q * LANES
            ffn_s[:, c0:c0 + LANES] = lo_acc
            ffn_s[:, c0 + half:c0 + half + LANES] = hi_acc
    gt2 = mod_ref[5:6, :]
    z = alpha * x1_ref[...] + gt2 * ffn_s[...]
    o_ref[...] = _layer_norm_rows(z, g_ref[...], b_ref[...])


def _combine_ln2(x1, ys, slot, gates, mod3, g3, b3, l, seq, alpha, tm=256):
    t, d = x1.shape
    per_b = seq // tm
    n_tiles = t // tm
    slot3 = slot.reshape(n_tiles, tm, TOP_K).transpose(0, 2, 1).reshape(n_tiles, 1, TOP_K * tm)
    return pl.pallas_call(
        functools.partial(_combine_ln2_kernel, alpha=alpha, tm=tm),
        grid=(n_tiles,),
        in_specs=[
            pl.BlockSpec((None, 1, tm * TOP_K), lambda i: (i, 0, 0), memory_space=pltpu.SMEM),
            pl.BlockSpec((None, 1, tm * TOP_K), lambda i: (jnp.minimum(i + 1, n_tiles - 1), 0, 0),
                         memory_space=pltpu.SMEM),
            pl.BlockSpec((tm, d), lambda i: (i, 0)),
            pl.BlockSpec(memory_space=pl.ANY),
            pl.BlockSpec((tm, LANES), lambda i: (i, 0)),
            pl.BlockSpec((None, 6, d), lambda i: (i // per_b, 0, 0)),
            pl.BlockSpec((None, 1, d), lambda i: (l, 0, 0)),
            pl.BlockSpec((None, 1, d), lambda i: (l, 0, 0)),
        ],
        out_specs=pl.BlockSpec((tm, d), lambda i: (i, 0)),
        out_shape=jax.ShapeDtypeStruct((t, d), F32),
        scratch_shapes=[pltpu.VMEM((2 * TOP_K * tm * _slab_sub(d), LANES), jnp.uint32),
                        pltpu.VMEM((tm, d), F32),
                        pltpu.SemaphoreType.DMA((2,))],
        compiler_params=_cparams(("arbitrary",)),
        name="combine_ln2",
    )(slot3, slot3, x1, ys, gates, mod3, g3, b3)


def _t5_bucket(dist):
    is_small = dist < REL_MAX_EXACT
    nf = jnp.maximum(dist, REL_MAX_EXACT).astype(F32)
    large = REL_MAX_EXACT + (jnp.log(nf / REL_MAX_EXACT) / math.log(REL_MAX_DISTANCE / REL_MAX_EXACT)
                             * (REL_BUCKETS - REL_MAX_EXACT)).astype(jnp.int32)
    large = jnp.minimum(large, REL_BUCKETS - 1)
    return jnp.where(is_small, dist, large)


def _dilated_bias_vectors(rel_bias):
    m = jnp.arange(2 * BLOCK)
    rel = BLOCK - m
    vecs = []
    for window, d in DIL_PATTERNS:
        valid = (rel >= 0) & (rel <= window // d)
        bucket = _t5_bucket(jnp.clip(rel, 0) * d)
        bias = jnp.transpose(rel_bias[bucket].astype(F32), (1, 0))
        vecs.append(jnp.where(valid[None, :], bias, NEG_BIG))
    return jnp.stack(vecs, axis=0)[:, :, None, :]


def _routing_tables(ids4, n_items_max):
    e = ids4.reshape(-1)
    n_rows_real = e.shape[0]
    onehot =(e[:, None] == jnp.arange(N_EXPERTS, dtype=jnp.int32)[None, :]).astype(jnp.int32)
    csum = jnp.cumsum(onehot, axis=0)
    rank = jnp.sum(onehot * csum, axis=1) - 1
    counts = csum[-1]
    nit = (counts + MOE_ROWS - 1) // MOE_ROWS
    cum_items = jnp.cumsum(nit)
    first_item = cum_items - nit
    slot = first_item[e] * MOE_ROWS + rank
    total = cum_items[-1]
    w = jnp.arange(n_items_max, dtype=jnp.int32)
    e_w = jnp.minimum(jnp.searchsorted(cum_items, w, side="right"), N_EXPERTS - 1).astype(jnp.int32)
    j_w = w - first_item[e_w]
    rows_w = jnp.clip(counts[e_w] - j_w * MOE_ROWS, 0, MOE_ROWS)
    valid = w < total
    rows_w = jnp.where(valid, rows_w, 0)
    tails_per_full = MOE_SUB // MOE_TAIL
    n_full = rows_w // MOE_SUB
    n_tail = (rows_w - n_full * MOE_SUB + MOE_TAIL - 1) // MOE_TAIL
    item_full = jnp.where(n_tail == tails_per_full, n_full + 1, n_full).astype(jnp.int32)
    item_tail = jnp.where(n_tail == tails_per_full, 0, n_tail).astype(jnp.int32)
    last = total - 1
    item_e = jnp.where(valid, e_w, e_w[last]).astype(jnp.int32)
    item_blk = jnp.where(valid, w, last).astype(jnp.int32)
    per_item = MOE_ROWS // MOE_BLK
    blk = jnp.arange(n_items_max * per_item, dtype=jnp.int32)
    blk_valid = ((blk % per_item) * MOE_BLK < rows_w[blk // per_item]).astype(jnp.int32)
    cum_valid = jnp.cumsum(blk_valid)
    nvalid = cum_valid[-1]
    n_steps = n_rows_real // MOE_BLK + n_items_max
    want = jnp.minimum(jnp.arange(n_steps, dtype=jnp.int32) + 1, nvalid)
    vblk = jnp.searchsorted(cum_valid, want, side="left").astype(jnp.int32)
    return (slot.astype(jnp.int32), item_e, item_blk, item_full, item_tail, vblk,
            nvalid.astype(jnp.int32).reshape(1))


def kernel(x, c, positions, w_ada, b_ada, w_in, g_q, g_kv, w_uq, w_ukv, rel_bias, w_o,
           ln1_g, ln1_b, w_router, b_router, w_gate, b_gate, w_up, b_up, w_down, b_down,
           ln2_g, ln2_b):
    batch, seq, d = x.shape
    depth = w_ada.shape[0]
    t = batch * seq
    n_heads = d // (2 * HEAD_DIM)
    dil_w = n_heads * HEAD_DIM
    alpha = (2.0 * depth) ** 0.25
    assert seq % (BLOCK * max(dd for _, dd in DIL_PATTERNS)) == 0
    assert all(win // dd == BLOCK for win, dd in DIL_PATTERNS)

    inv_freq = 1.0 / (ROPE_THETA ** (jnp.arange(0, QK_ROPE_DIM, 2, dtype=F32) / QK_ROPE_DIM))
    ang = positions.astype(F32)[..., None] * inv_freq
    cos, sin = jnp.cos(ang).reshape(t, -1), jnp.sin(ang).reshape(t, -1)
    zpad = jnp.zeros((t, LANES - QK_ROPE_DIM), F32)
    cos_t = jnp.concatenate([cos, cos, zpad], axis=1)
    sin_t = jnp.concatenate([-sin, sin, zpad], axis=1)

    bias_vec = _dilated_bias_vectors(rel_bias)
    c_pad = jnp.zeros((16, d), F32).at[:batch].set(c)

    x2 = x.reshape(t, d)
    n_items_max = (t * TOP_K) // MOE_ROWS + N_EXPERTS
    mla_cols = Q_LORA_RANK + KV_LORA_RANK

    for l in range(depth):
        mod = _ada(c_pad, w_ada, b_ada.reshape(depth, 1, -1), l)[:batch]
        mod3 = mod.reshape(batch, 6, d)

        w_in_p = _w_in_layout(w_in, l, mla_cols, 3 * dil_w)
        wq = w_uq[l].reshape(Q_LORA_RANK, n_heads, QK_NOPE_DIM + QK_ROPE_DIM)
        w_uq_p = jnp.pad(wq, ((0, 0), (0, 0), (0, QK_PAD - QK_NOPE_DIM - QK_ROPE_DIM))
                         ).reshape(Q_LORA_RANK, n_heads * QK_PAD).astype(BF16)
        wkv = w_ukv[l].reshape(KV_LORA_RANK, n_heads, QK_NOPE_DIM + V_HEAD_DIM)
        w_uk = wkv[:, :, :QK_NOPE_DIM].reshape(KV_LORA_RANK, -1).astype(BF16)
        w_uv = wkv[:, :, QK_NOPE_DIM:].reshape(KV_LORA_RANK, -1).astype(BF16)

        u1 = _modulate(x2, mod3, seq, 0, 1)
        hm = _in_proj_mla(u1, w_in_p, mla_cols)
        qkv_d = _in_proj_dil(u1, w_in_p, mla_cols, batch, seq, n_heads)
        krr = _in_proj_krope(u1, w_in_p, mla_cols + 3 * dil_w, cos_t, sin_t)
        q_a = _q_up(hm, g_q.reshape(depth, 1, -1), l, w_uq_p, cos_t, sin_t, n_heads)
        k_a, v_a = _kv_up(hm, g_kv.reshape(depth, 1, -1), l, w_uk, w_uv, krr, n_heads)
        o_a = _mla_attn(q_a, k_a, v_a, batch, seq, n_heads)
        o_b = _dil_attn(qkv_d, bias_vec, batch, seq, n_heads)
        mix = _out_proj(o_a, o_b, w_o, l)

        wr_p = jnp.pad(w_router[l], ((0, 0), (0, LANES - N_EXPERTS)))
        br_p = jnp.concatenate([b_router[l], jnp.full((LANES - N_EXPERTS,), NEG_BIG, F32)])[None, :]
        x1, u2p, ids, gates = _ln1_router(
            x2, mix, mod3, ln1_g.reshape(depth, 1, -1), ln1_b.reshape(depth, 1, -1),
            wr_p, br_p, l, seq, alpha)

        slot, item_e, item_blk, item_full, item_tail, vblk, nvalid = _routing_tables(
            ids[:, :TOP_K], n_items_max)
        tok = jnp.arange(t * TOP_K, dtype=jnp.int32) // TOP_K
        row_tok = jnp.zeros((n_items_max * MOE_ROWS,), jnp.int32).at[slot].set(tok)
        xs = _dispatch(u2p, d, row_tok, vblk, nvalid)
        hs = _moe_up(xs, w_gate, w_up, b_gate.reshape(depth, N_EXPERTS, 1, -1),
                     b_up.reshape(depth, N_EXPERTS, 1, -1), item_e, item_blk, item_full,
                     item_tail, l)
        ys = _moe_down(hs, w_down, b_down.reshape(depth, N_EXPERTS, 1, -1),
                       item_e, item_blk, item_full, item_tail, l)
        x2 = _combine_ln2(x1, ys, slot, gates, mod3, ln2_g.reshape(depth, 1, -1),
                          ln2_b.reshape(depth, 1, -1), l, seq, alpha)

    return x2.reshape(batch, seq, d)
```

```python
import functools
import math

import jax
import jax.numpy as jnp
from jax import lax
from jax.experimental import pallas as pl
from jax.experimental.pallas import tpu as pltpu

F32 = jnp.float32
BF16 = jnp.bfloat16

HEAD_DIM = 128
Q_LORA_RANK = 1536
KV_LORA_RANK = 512
QK_NOPE_DIM = 128
QK_ROPE_DIM = 64
V_HEAD_DIM = 128
ROPE_THETA = 10000.0
DIL_PATTERNS = ((128, 1), (512, 4), (2048, 16))
BLOCK = 128
REL_BUCKETS = 32
REL_MAX_EXACT = REL_BUCKETS // 2
REL_MAX_DISTANCE = 2048
N_EXPERTS = 32
TOP_K = 4
EXPERT_FF = 1536
SWIGLU_LIMIT = 7.0
SWIGLU_ALPHA = 1.702
LN_EPS = 1e-5
RMS_EPS = 1e-6
NEG_BIG = -1e30
LOG2E = math.log2(math.e)

LANES = 128
QK_PAD = 256
VMEM_LIMIT = 56 * 1024 * 1024
MOE_ROWS = 1280
MOE_SUB = 512
MOE_TAIL = 128
MOE_BLK = 256


def _cparams(sem):
    return pltpu.CompilerParams(dimension_semantics=sem, vmem_limit_bytes=VMEM_LIMIT)


def _ada_kernel(c_ref, w_ref, b_ref, o_ref):
    c = c_ref[...]
    cond = (c * jax.nn.sigmoid(c)).astype(BF16)
    o_ref[...] = jnp.dot(cond, w_ref[...].astype(BF16), preferred_element_type=F32) + b_ref[...]


def _ada(c_pad, w_ada, b_ada3, l, tn=512):
    rows, d = c_pad.shape
    n = w_ada.shape[-1]
    return pl.pallas_call(
        _ada_kernel,
        grid=(n // tn,),
        in_specs=[
            pl.BlockSpec((rows, d), lambda j: (0, 0)),
            pl.BlockSpec((None, d, tn), lambda j: (l, 0, j)),
            pl.BlockSpec((None, 1, tn), lambda j: (l, 0, j)),
        ],
        out_specs=pl.BlockSpec((rows, tn), lambda j: (0, j)),
        out_shape=jax.ShapeDtypeStruct((rows, n), F32),
        compiler_params=_cparams(("arbitrary",)),
        name="ada_mod",
    )(c_pad, w_ada, b_ada3)


def _modulate_kernel(x_ref, mod_ref, o_ref, *, shift_row, scale_row):
    sh = mod_ref[shift_row:shift_row + 1, :]
    sc = mod_ref[scale_row:scale_row + 1, :]
    o_ref[...] = (x_ref[...] * (1.0 + sc) + sh).astype(BF16)


def _modulate(x2, mod3, seq, shift_row, scale_row, tm=512):
    t, d = x2.shape
    per_b = seq // tm
    return pl.pallas_call(
        functools.partial(_modulate_kernel, shift_row=shift_row, scale_row=scale_row),
        grid=(t // tm,),
        in_specs=[
            pl.BlockSpec((tm, d), lambda i: (i, 0)),
            pl.BlockSpec((None, 6, d), lambda i: (i // per_b, 0, 0)),
        ],
        out_specs=pl.BlockSpec((tm, d), lambda i: (i, 0)),
        out_shape=jax.ShapeDtypeStruct((t, d), BF16),
        compiler_params=_cparams(("arbitrary",)),
        name="modulate",
    )(x2, mod3)


def _rope_block(x, cos_t, sin_t):
    lane = lax.broadcasted_iota(jnp.int32, x.shape, 1)
    half = QK_ROPE_DIM // 2
    swapped = jnp.where(lane < half, pltpu.roll(x, LANES - half, 1), pltpu.roll(x, half, 1))
    return x * cos_t + swapped * sin_t


def _load_weight_rows(wt_ref, row0, wf_s, wb_s):
    n_rows = wf_s.shape[0]
    pltpu.sync_copy(wt_ref.at[pl.ds(pl.multiple_of(row0, 8), n_rows), :], wf_s)
    wb_s[0:n_rows, :] = wf_s[...].astype(BF16)


def _dot_nt(a, w):
    return lax.dot_general(a, w, (((1,), (1,)), ((), ())), preferred_element_type=F32)


def _proj_plain_kernel(a_ref, wt_ref, o_ref, wf_s, wb_s, *, row_base):
    @pl.when(pl.program_id(1) == 0)
    def _():
        _load_weight_rows(wt_ref, row_base + pl.program_id(0) * wf_s.shape[0], wf_s, wb_s)

    o_ref[...] = _dot_nt(a_ref[...], wb_s[...]).astype(o_ref.dtype)


def _proj_heads_kernel(a_ref, wt_ref, o_ref, wf_s, wb_s, *, row_base, heads_per_tile):
    @pl.when(pl.program_id(1) == 0)
    def _():
        _load_weight_rows(wt_ref, row_base + pl.program_id(0) * wf_s.shape[0], wf_s, wb_s)

    res = _dot_nt(a_ref[...], wb_s[...])
    for hh in range(heads_per_tile):
        o_ref[hh] = res[:, hh * HEAD_DIM:(hh + 1) * HEAD_DIM].astype(o_ref.dtype)


def _proj_rope_kernel(a_ref, wt_ref, cos_ref, sin_ref, o_ref, wf_s, wb_s, *, row_base):
    @pl.when(pl.program_id(0) == 0)
    def _():
        wb_s[...] = jnp.zeros(wb_s.shape, BF16)
        _load_weight_rows(wt_ref, row_base, wf_s, wb_s)

    res = _dot_nt(a_ref[...], wb_s[...])
    o_ref[...] = _rope_block(res, cos_ref[...], sin_ref[...]).astype(o_ref.dtype)


def _in_proj_mla(u, wt, n_cols, tm=512, tn=512):
    t, d = u.shape
    return pl.pallas_call(
        functools.partial(_proj_plain_kernel, row_base=0),
        grid=(n_cols // tn, t // tm),
        in_specs=[
            pl.BlockSpec((tm, d), lambda j, i: (i, 0)),
            pl.BlockSpec(memory_space=pl.ANY),
        ],
        out_specs=pl.BlockSpec((tm, tn), lambda j, i: (i, j)),
        out_shape=jax.ShapeDtypeStruct((t, n_cols), BF16),
        scratch_shapes=[pltpu.VMEM((tn, d), F32), pltpu.VMEM((tn, d), BF16)],
        compiler_params=_cparams(("arbitrary", "arbitrary")),
        name="in_proj_mla",
    )(u, wt)


def _in_proj_dil(u, wt, col0, batch, seq, n_heads, tm=512, tn=512):
    t, d = u.shape
    hpt = tn // HEAD_DIM
    tiles_per_mat = n_heads // hpt
    per_b = seq // tm
    return pl.pallas_call(
        functools.partial(_proj_heads_kernel, row_base=col0, heads_per_tile=hpt),
        grid=(3 * tiles_per_mat, t // tm),
        in_specs=[
            pl.BlockSpec((tm, d), lambda j, i: (i, 0)),
            pl.BlockSpec(memory_space=pl.ANY),
        ],
        out_specs=pl.BlockSpec(
            (None, None, hpt, tm, HEAD_DIM),
            lambda j, i: (j // tiles_per_mat, i // per_b, j % tiles_per_mat, i % per_b, 0)),
        out_shape=jax.ShapeDtypeStruct((3, batch, n_heads, seq, HEAD_DIM), BF16),
        scratch_shapes=[pltpu.VMEM((tn, d), F32), pltpu.VMEM((tn, d), BF16)],
        compiler_params=_cparams(("arbitrary", "arbitrary")),
        name="in_proj_dil",
    )(u, wt)


def _in_proj_krope(u, wt, col0, cos_t, sin_t, tm=1024):
    t, d = u.shape
    return pl.pallas_call(
        functools.partial(_proj_rope_kernel, row_base=col0),
        grid=(t // tm,),
        in_specs=[
            pl.BlockSpec((tm, d), lambda i: (i, 0)),
            pl.BlockSpec(memory_space=pl.ANY),
            pl.BlockSpec((tm, LANES), lambda i: (i, 0)),
            pl.BlockSpec((tm, LANES), lambda i: (i, 0)),
        ],
        out_specs=pl.BlockSpec((tm, LANES), lambda i: (i, 0)),
        out_shape=jax.ShapeDtypeStruct((t, LANES), BF16),
        scratch_shapes=[pltpu.VMEM((QK_ROPE_DIM, d), F32), pltpu.VMEM((LANES, d), BF16)],
        compiler_params=_cparams(("arbitrary",)),
        name="in_proj_krope",
    )(u, wt, cos_t, sin_t)


def _rms_bf16(h_ref, g_ref):
    hf = h_ref[...].astype(F32)
    y = hf * lax.rsqrt(jnp.mean(hf * hf, axis=-1, keepdims=True) + RMS_EPS)
    return (y * g_ref[...]).astype(BF16)


def _q_up_kernel(h_ref, g_ref, w_ref, cos_ref, sin_ref, o_ref, *, n_heads, scale):
    yb = _rms_bf16(h_ref, g_ref)
    cos_t = cos_ref[...]
    sin_t = sin_ref[...]
    group = 4
    for h0 in range(0, n_heads, group):
        res = jnp.dot(yb, w_ref[:, h0 * QK_PAD:(h0 + group) * QK_PAD], preferred_element_type=F32)
        for hh in range(group):
            c0 = hh * QK_PAD
            nope = res[:, c0:c0 + QK_NOPE_DIM]
            rp = _rope_block(res[:, c0 + QK_NOPE_DIM:c0 + QK_PAD], cos_t, sin_t)
            o0 = (h0 + hh) * QK_PAD
            o_ref[:, o0:o0 + QK_NOPE_DIM] = (nope * scale).astype(BF16)
            o_ref[:, o0 + QK_NOPE_DIM:o0 + QK_PAD] = (rp * scale).astype(BF16)


def _q_up(hm, g3, l, w_uq_p, cos_t, sin_t, n_heads, tm=256):
    t = hm.shape[0]
    scale = float(QK_NOPE_DIM + QK_ROPE_DIM) ** -0.5 * LOG2E
    return pl.pallas_call(
        functools.partial(_q_up_kernel, n_heads=n_heads, scale=scale),
        grid=(t // tm,),
        in_specs=[
            pl.BlockSpec((tm, Q_LORA_RANK), lambda i: (i, 0)),
            pl.BlockSpec((None, 1, Q_LORA_RANK), lambda i: (l, 0, 0)),
            pl.BlockSpec((Q_LORA_RANK, n_heads * QK_PAD), lambda i: (0, 0)),
            pl.BlockSpec((tm, LANES), lambda i: (i, 0)),
            pl.BlockSpec((tm, LANES), lambda i: (i, 0)),
        ],
        out_specs=pl.BlockSpec((tm, n_heads * QK_PAD), lambda i: (i, 0)),
        out_shape=jax.ShapeDtypeStruct((t, n_heads * QK_PAD), BF16),
        compiler_params=_cparams(("arbitrary",)),
        name="mla_q_up",
    )(hm, g3, w_uq_p, cos_t, sin_t)


def _kv_up_kernel(h_ref, g_ref, wk_ref, wv_ref, kr_ref, k_ref, v_ref, *, n_heads):
    yb = _rms_bf16(h_ref, g_ref)
    kr = kr_ref[...]
    group = 4
    for h0 in range(0, n_heads, group):
        res = jnp.dot(yb, wk_ref[:, h0 * QK_NOPE_DIM:(h0 + group) * QK_NOPE_DIM],
                      preferred_element_type=F32)
        for hh in range(group):
            o0 = (h0 + hh) * QK_PAD
            k_ref[:, o0:o0 + QK_NOPE_DIM] = res[:, hh * QK_NOPE_DIM:(hh + 1) * QK_NOPE_DIM].astype(BF16)
            k_ref[:, o0 + QK_NOPE_DIM:o0 + QK_PAD] = kr
    v_ref[...] = jnp.dot(yb, wv_ref[...], preferred_element_type=F32).astype(BF16)


def _kv_up(hm, g3, l, w_uk, w_uv, krr, n_heads, tm=256):
    t = hm.shape[0]
    col_blk = Q_LORA_RANK // KV_LORA_RANK
    return pl.pallas_call(
        functools.partial(_kv_up_kernel, n_heads=n_heads),
        grid=(t // tm,),
        in_specs=[
            pl.BlockSpec((tm, KV_LORA_RANK), lambda i: (i, col_blk)),
            pl.BlockSpec((None, 1, KV_LORA_RANK), lambda i: (l, 0, 0)),
            pl.BlockSpec((KV_LORA_RANK, n_heads * QK_NOPE_DIM), lambda i: (0, 0)),
            pl.BlockSpec((KV_LORA_RANK, n_heads * V_HEAD_DIM), lambda i: (0, 0)),
            pl.BlockSpec((tm, LANES), lambda i: (i, 0)),
        ],
        out_specs=[
            pl.BlockSpec((tm, n_heads * QK_PAD), lambda i: (i, 0)),
            pl.BlockSpec((tm, n_heads * V_HEAD_DIM), lambda i: (i, 0)),
        ],
        out_shape=[
            jax.ShapeDtypeStruct((t, n_heads * QK_PAD), BF16),
            jax.ShapeDtypeStruct((t, n_heads * V_HEAD_DIM), BF16),
        ],
        compiler_params=_cparams(("arbitrary",)),
        name="mla_kv_up",
    )(hm, g3, w_uk, w_uv, krr)


MLA_HEADS_PER_STEP = 2


def _mla_attn_kernel(q_ref, k_ref, v_ref, o_ref, *, seq, tq):
    nq = seq // tq
    hp = MLA_HEADS_PER_STEP
    row = lax.broadcasted_iota(jnp.int32, (tq, tq), 0)
    col = lax.broadcasted_iota(jnp.int32, (tq, tq), 1)
    causal = col <= row

    def kv_step(qs, j, carry, masked):
        off = pl.multiple_of(j * tq, tq)
        out = []
        for hh in range(hp):
            m, l, acc = carry[hh]
            k = k_ref[pl.ds(off, tq), hh * QK_PAD:(hh + 1) * QK_PAD]
            v = v_ref[pl.ds(off, tq), hh * V_HEAD_DIM:(hh + 1) * V_HEAD_DIM]
            s = lax.dot_general(qs[hh], k, (((1,), (1,)), ((), ())), preferred_element_type=F32)
            if masked:
                s = jnp.where(causal, s, NEG_BIG)
            m_new = jnp.maximum(m, jnp.max(s, axis=-1, keepdims=True))
            p = jnp.exp2(s - m_new)
            alpha = jnp.exp2(m - m_new)
            l_new = alpha * l + jnp.sum(p, axis=-1, keepdims=True)
            acc_new = alpha * acc + jnp.dot(p.astype(BF16), v, preferred_element_type=F32)
            out.append((m_new, l_new, acc_new))
        return tuple(out)

    def q_loop(i, _):
        qoff = pl.multiple_of(i * tq, tq)
        qs = [q_ref[pl.ds(qoff, tq), hh * QK_PAD:(hh + 1) * QK_PAD] for hh in range(hp)]
        init = tuple((jnp.full((tq, 1), NEG_BIG, F32), jnp.zeros((tq, 1), F32),
                      jnp.zeros((tq, V_HEAD_DIM), F32)) for _ in range(hp))
        carry = lax.fori_loop(0, i, lambda j, c: kv_step(qs, j, c, False), init)
        carry = kv_step(qs, i, carry, True)
        for hh in range(hp):
            m, l, acc = carry[hh]
            o_ref[pl.ds(qoff, tq), hh * V_HEAD_DIM:(hh + 1) * V_HEAD_DIM] = (acc / l).astype(o_ref.dtype)
        return 0

    lax.fori_loop(0, nq, q_loop, 0)


def _mla_attn(q, k, v, batch, seq, n_heads, tq=512):
    t = q.shape[0]
    hp = MLA_HEADS_PER_STEP
    return pl.pallas_call(
        functools.partial(_mla_attn_kernel, seq=seq, tq=tq),
        grid=(batch, n_heads // hp),
        in_specs=[
            pl.BlockSpec((seq, hp * QK_PAD), lambda b, h: (b, h)),
            pl.BlockSpec((seq, hp * QK_PAD), lambda b, h: (b, h)),
            pl.BlockSpec((seq, hp * V_HEAD_DIM), lambda b, h: (b, h)),
        ],
        out_specs=pl.BlockSpec((seq, hp * V_HEAD_DIM), lambda b, h: (b, h)),
        out_shape=jax.ShapeDtypeStruct((t, n_heads * V_HEAD_DIM), BF16),
        compiler_params=_cparams(("arbitrary", "arbitrary")),
        name="mla_attn",
    )(q, k, v)


DIL_GROUP = 4


def _dil_attn_kernel(qkv_ref, bvec_ref, o_ref, m_s, l_s, acc_s, bias_s, nat_f, *view_s, seq):
    scale = float(HEAD_DIM) ** -0.5 * LOG2E
    views = (qkv_ref,) + tuple(view_s)

    for bi in range(len(DIL_PATTERNS)):
        full = jnp.broadcast_to(bvec_ref[bi] * LOG2E, (BLOCK, 2 * BLOCK))
        bias_s[bi] = pltpu.roll(full, 0, 1, stride=1, stride_axis=0)

    for which in range(3):
        nat_f[...] = qkv_ref[which].astype(F32)
        for bi, (_, d) in enumerate(DIL_PATTERNS):
            if d == 1:
                continue
            for r in range(d):
                views[bi][which, :, r * HEAD_DIM:(r + 1) * HEAD_DIM] = (
                    nat_f[pl.ds(r, seq // d, stride=d), :].astype(BF16))

    def scores(ref, bi, d, r, i, first):
        lanes = slice(r * HEAD_DIM, (r + 1) * HEAD_DIM)
        qoff = pl.multiple_of(i * BLOCK, BLOCK)
        q = ref[0, pl.ds(qoff, BLOCK), lanes]
        if first:
            kk = ref[1, pl.ds(0, BLOCK), lanes]
            vv = ref[2, pl.ds(0, BLOCK), lanes]
            bias = bias_s[bi, :, BLOCK:2 * BLOCK]
        else:
            koff = pl.multiple_of(i * BLOCK - BLOCK, BLOCK)
            kk = ref[1, pl.ds(koff, 2 * BLOCK), lanes]
            vv = ref[2, pl.ds(koff, 2 * BLOCK), lanes]
            bias = bias_s[bi]
        s = lax.dot_general(q, kk, (((1,), (1,)), ((), ())), preferred_element_type=F32)
        s = s * scale + bias
        m_b = jnp.max(s, axis=-1, keepdims=True)
        p = jnp.exp2(s - m_b)
        l_b = jnp.sum(p, axis=-1, keepdims=True)
        a_b = jnp.dot(p.astype(BF16), vv, preferred_element_type=F32)
        m_b = jnp.broadcast_to(m_b, (BLOCK, HEAD_DIM))
        l_b = jnp.broadcast_to(l_b, (BLOCK, HEAD_DIM))
        if d == 1:
            rows = pl.ds(qoff, BLOCK)
        else:
            rows = pl.ds(i * (BLOCK * d) + r, BLOCK, stride=d)
        return m_b, l_b, a_b, rows

    def group(ref, bi, d, blocks):
        parts = [scores(ref, bi, d, r, i, first) for r, i, first in blocks]
        if d == 1:
            for m_b, l_b, a_b, rows in parts:
                m_s[rows, :] = m_b
                l_s[rows, :] = l_b
                acc_s[rows, :] = a_b
            return
        old = [(m_s[rows, :], l_s[rows, :], acc_s[rows, :]) for _, _, _, rows in parts]
        new = []
        for (m_b, l_b, a_b, rows), (m_o, l_o, a_o) in zip(parts, old):
            m_n = jnp.maximum(m_o, m_b)
            e_o = jnp.exp2(m_o - m_n)
            e_b = jnp.exp2(m_b - m_n)
            new.append((m_n, e_o * l_o + e_b * l_b, e_o * a_o + e_b * a_b, rows))
        for m_n, l_n, a_n, rows in new:
            m_s[rows, :] = m_n
            l_s[rows, :] = l_n
            acc_s[rows, :] = a_n

    for bi, (_, d) in enumerate(DIL_PATTERNS):
        ref = views[bi]
        nblk = seq // d // BLOCK
        if d == 1:
            assert nblk % DIL_GROUP == 0
            group(ref, bi, d, [(0, u, u == 0) for u in range(DIL_GROUP)])

            def body1(g, _, ref=ref, bi=bi, d=d):
                group(ref, bi, d, [(0, g * DIL_GROUP + u, False) for u in range(DIL_GROUP)])
                return 0

            lax.fori_loop(1, nblk // DIL_GROUP, body1, 0)
        else:
            assert d % DIL_GROUP == 0
            for r0 in range(0, d, DIL_GROUP):
                group(ref, bi, d, [(r0 + u, 0, True) for u in range(DIL_GROUP)])

                def body(i, _, ref=ref, bi=bi, d=d, r0=r0):
                    group(ref, bi, d, [(r0 + u, i, False) for u in range(DIL_GROUP)])
                    return 0

                lax.fori_loop(1, nblk, body, 0)

    o_ref[...] = (acc_s[...] / l_s[...]).astype(o_ref.dtype)


def _dil_attn(qkv, bias_vec, batch, seq, n_heads):
    t = batch * seq
    nbr = len(DIL_PATTERNS)
    assert DIL_PATTERNS[0][1] == 1
    view_scratch = [pltpu.VMEM((3, seq // d, d * HEAD_DIM), BF16) for _, d in DIL_PATTERNS[1:]]
    return pl.pallas_call(
        functools.partial(_dil_attn_kernel, seq=seq),
        grid=(batch, n_heads),
        in_specs=[
            pl.BlockSpec((3, None, None, seq, HEAD_DIM), lambda b, h: (0, b, h, 0, 0)),
            pl.BlockSpec((nbr, None, 1, 2 * BLOCK), lambda b, h: (0, h, 0, 0)),
        ],
        out_specs=pl.BlockSpec((seq, HEAD_DIM), lambda b, h: (b, h)),
        out_shape=jax.ShapeDtypeStruct((t, n_heads * HEAD_DIM), BF16),
        scratch_shapes=[pltpu.VMEM((seq, HEAD_DIM), F32)] * 3
        + [pltpu.VMEM((nbr, BLOCK, 2 * BLOCK), F32), pltpu.VMEM((seq, HEAD_DIM), F32)]
        + view_scratch,
        compiler_params=_cparams(("arbitrary", "arbitrary")),
        name="dil_attn",
    )(qkv, bias_vec)


def _out_proj_kernel(a1_ref, a2_ref, w_ref, o_ref, wb_ref):
    @pl.when(pl.program_id(1) == 0)
    def _():
        wb_ref[...] = w_ref[...].astype(BF16)

    k1 = a1_ref.shape[1]
    acc = jnp.dot(a1_ref[...], wb_ref[:k1, :], preferred_element_type=F32)
    acc = acc + jnp.dot(a2_ref[...], wb_ref[k1:, :], preferred_element_type=F32)
    o_ref[...] = acc.astype(o_ref.dtype)


def _out_proj(o_a, o_b, w_o, l, tm=512, tn=512):
    t, k1 = o_a.shape
    k2 = o_b.shape[1]
    n = w_o.shape[-1]
    return pl.pallas_call(
        _out_proj_kernel,
        grid=(n // tn, t // tm),
        in_specs=[
            pl.BlockSpec((tm, k1), lambda j, i: (i, 0)),
            pl.BlockSpec((tm, k2), lambda j, i: (i, 0)),
            pl.BlockSpec((None, k1 + k2, tn), lambda j, i: (l, 0, j)),
        ],
        out_specs=pl.BlockSpec((tm, tn), lambda j, i: (i, j)),
        out_shape=jax.ShapeDtypeStruct((t, n), BF16),
        scratch_shapes=[pltpu.VMEM((k1 + k2, tn), BF16)],
        compiler_params=_cparams(("arbitrary", "arbitrary")),
        name="out_proj",
    )(o_a, o_b, w_o)


def _layer_norm_rows(z, g, b):
    mu = jnp.mean(z, axis=-1, keepdims=True)
    zc = z - mu
    var = jnp.mean(zc * zc, axis=-1, keepdims=True)
    return zc * lax.rsqrt(var + LN_EPS) * g + b


SLAB_CHUNK = 1024
SLAB_Q = SLAB_CHUNK // (2 * LANES)


def _slab_sub(width):
    return width // (2 * LANES)


def _slab_index(j, first_tok, n_tok, sub):
    return (pl.ds(first_tok * sub + j, n_tok, stride=sub), slice(None))


def _pack_chunk(vals):
    bits = lax.bitcast_convert_type(vals.astype(BF16).astype(F32), jnp.uint32)
    half = SLAB_CHUNK // 2
    return [(bits[:, q * LANES:(q + 1) * LANES] >> 16)
            | bits[:, half + q * LANES:half + (q + 1) * LANES] for q in range(SLAB_Q)]


def _unpack_words(words):
    lo = lax.bitcast_convert_type(words << 16, F32)
    hi = lax.bitcast_convert_type(words & jnp.uint32(0xFFFF0000), F32)
    return lo, hi


def _ln1_router_kernel(x_ref, mix_ref, mod_ref, g_ref, b_ref, wr_ref, br_ref,
                       x1_ref, u2p_ref, ids_ref, gates_ref, *, alpha):
    gt1 = mod_ref[2:3, :]
    sh2 = mod_ref[3:4, :]
    sc2 = mod_ref[4:5, :]
    z = alpha * x_ref[...] + gt1 * mix_ref[...].astype(F32)
    x1 = _layer_norm_rows(z, g_ref[...], b_ref[...])
    x1_ref[...] = x1
    u2 = (x1 * (1.0 + sc2) + sh2).astype(BF16)
    tm, d = u2.shape
    for g in range(d // SLAB_CHUNK):
        for q, words in enumerate(_pack_chunk(u2[:, g * SLAB_CHUNK:(g + 1) * SLAB_CHUNK])):
            u2p_ref[_slab_index(g * SLAB_Q + q, 0, tm, _slab_sub(d))] = words
    logits = jnp.dot(u2, wr_ref[...].astype(BF16), preferred_element_type=F32) + br_ref[...]
    lane = lax.broadcasted_iota(jnp.int32, logits.shape, 1)
    lane_f = lane.astype(F32)
    vals = []
    ids = []
    for _ in range(TOP_K):
        mk = jnp.max(logits, axis=-1, keepdims=True)
        idx_f = jnp.min(jnp.where(logits == mk, lane_f, float(LANES)), axis=-1, keepdims=True)
        vals.append(mk)
        ids.append(idx_f.astype(jnp.int32))
        logits = jnp.where(lane_f == idx_f, -jnp.inf, logits)
    exps = [jnp.exp(v - vals[0]) for v in vals]
    denom = exps[0]
    for e in exps[1:]:
        denom = denom + e
    ids_out = jnp.zeros(lane.shape, jnp.int32)
    gates_out = jnp.zeros(lane.shape, F32)
    for k in range(TOP_K):
        ids_out = jnp.where(lane == k, ids[k], ids_out)
        gates_out = jnp.where(lane == k, exps[k] / denom, gates_out)
    ids_ref[...] = ids_out
    gates_ref[...] = gates_out


def _ln1_router(x2, mix, mod3, g3, b3, wr_p, br_p, l, seq, alpha, tm=256):
    t, d = x2.shape
    per_b = seq // tm
    return pl.pallas_call(
        functools.partial(_ln1_router_kernel, alpha=alpha),
        grid=(t // tm,),
        in_specs=[
            pl.BlockSpec((tm, d), lambda i: (i, 0)),
            pl.BlockSpec((tm, d), lambda i: (i, 0)),
            pl.BlockSpec((None, 6, d), lambda i: (i // per_b, 0, 0)),
            pl.BlockSpec((None, 1, d), lambda i: (l, 0, 0)),
            pl.BlockSpec((None, 1, d), lambda i: (l, 0, 0)),
            pl.BlockSpec((d, LANES), lambda i: (0, 0)),
            pl.BlockSpec((1, LANES), lambda i: (0, 0)),
        ],
        out_specs=[
            pl.BlockSpec((tm, d), lambda i: (i, 0)),
            pl.BlockSpec((tm * _slab_sub(d), LANES), lambda i: (i, 0)),
            pl.BlockSpec((tm, LANES), lambda i: (i, 0)),
            pl.BlockSpec((tm, LANES), lambda i: (i, 0)),
        ],
        out_shape=[
            jax.ShapeDtypeStruct((t, d), F32),
            jax.ShapeDtypeStruct((t * _slab_sub(d), LANES), jnp.uint32),
            jax.ShapeDtypeStruct((t, LANES), jnp.int32),
            jax.ShapeDtypeStruct((t, LANES), F32),
        ],
        compiler_params=_cparams(("arbitrary",)),
        name="ln1_router",
    )(x2, mix, mod3, g3, b3, wr_p, br_p)


def _moe_row_tiles(n_full, n_tail, tile_fn):
    def full(j, _):
        tile_fn(pl.multiple_of(j * MOE_SUB, MOE_SUB), MOE_SUB)
        return 0

    def tail(k, _):
        tile_fn(pl.multiple_of(n_full * MOE_SUB + k * MOE_TAIL, MOE_TAIL), MOE_TAIL)
        return 0

    lax.fori_loop(0, n_full, full, 0)
    lax.fori_loop(0, n_tail, tail, 0)


def _moe_up_kernel(item_e, item_blk, item_full, item_tail, x_ref, wg_ref, wu_ref, bg_ref, bu_ref,
                   h_ref, wg_s, wu_s):
    w = pl.program_id(0)
    n_full = item_full[w]
    n_tail = item_tail[w]

    @pl.when(n_full + n_tail > 0)
    def _():
        wg_s[...] = wg_ref[...].astype(BF16)
        wu_s[...] = wu_ref[...].astype(BF16)
        bg = bg_ref[...]
        bu = bu_ref[...]

        def tile(off, rows):
            xt = x_ref[pl.ds(off, rows), :]
            glu = jnp.dot(xt, wg_s[...], preferred_element_type=F32) + bg
            lin = jnp.dot(xt, wu_s[...], preferred_element_type=F32) + bu
            glu = jnp.minimum(glu, SWIGLU_LIMIT)
            lin = jnp.clip(lin, -SWIGLU_LIMIT, SWIGLU_LIMIT)
            act = glu * jax.nn.sigmoid(SWIGLU_ALPHA * glu) * (lin + 1.0)
            h_ref[pl.ds(off, rows), :] = act.astype(h_ref.dtype)

        _moe_row_tiles(n_full, n_tail, tile)


def _moe_up(xs, w_gate, w_up, b_gate4, b_up4, item_e, item_blk, item_full, item_tail, l, tf=256):
    r, d = xs.shape
    n_items = r // MOE_ROWS
    ff = w_gate.shape[-1]
    n_chunks = ff // tf

    def cmap(w, c, ifull, itail):
        return jnp.where(ifull[w] + itail[w] > 0, c, n_chunks - 1)

    def wmap(w, c, ie, ib, ifull, itail):
        return (l, ie[w], 0, cmap(w, c, ifull, itail))

    grid_spec = pltpu.PrefetchScalarGridSpec(
        num_scalar_prefetch=4,
        grid=(n_items, n_chunks),
        in_specs=[
            pl.BlockSpec((MOE_ROWS, d), lambda w, c, ie, ib, ifull, itail: (ib[w], 0)),
            pl.BlockSpec((None, None, d, tf), wmap),
            pl.BlockSpec((None, None, d, tf), wmap),
            pl.BlockSpec((None, None, 1, tf), wmap),
            pl.BlockSpec((None, None, 1, tf), wmap),
        ],
        out_specs=pl.BlockSpec(
            (MOE_ROWS, tf),
            lambda w, c, ie, ib, ifull, itail: (ib[w], cmap(w, c, ifull, itail))),
        scratch_shapes=[pltpu.VMEM((d, tf), BF16), pltpu.VMEM((d, tf), BF16)],
    )
    return pl.pallas_call(
        _moe_up_kernel,
        grid_spec=grid_spec,
        out_shape=jax.ShapeDtypeStruct((r, ff), BF16),
        compiler_params=_cparams(("arbitrary", "arbitrary")),
        name="moe_up",
    )(item_e, item_blk, item_full, item_tail, xs, w_gate, w_up, b_gate4, b_up4)


def _moe_down_kernel(item_e, item_blk, item_full, item_tail, h_ref, wd_ref, bd_ref, y_ref, wd_s,
                     *, sub):
    w = pl.program_id(0)
    c = pl.program_id(1)
    n_full = item_full[w]
    n_tail = item_tail[w]

    @pl.when(n_full + n_tail > 0)
    def _():
        wd_s[...] = wd_ref[...].astype(BF16)
        bd = bd_ref[...]

        def tile(off, rows):
            ht = h_ref[pl.ds(off, rows), :]
            y = jnp.dot(ht, wd_s[...], preferred_element_type=F32) + bd
            for q, words in enumerate(_pack_chunk(y)):
                y_ref[_slab_index(c * SLAB_Q + q, off, rows, sub)] = words

        _moe_row_tiles(n_full, n_tail, tile)


def _moe_down(hs, w_down, b_down4, item_e, item_blk, item_full, item_tail, l):
    r, ff = hs.shape
    n_items = r // MOE_ROWS
    d = w_down.shape[-1]
    tn = SLAB_CHUNK
    n_chunks = d // tn
    sub = _slab_sub(d)

    def cmap(w, c, ifull, itail):
        return jnp.where(ifull[w] + itail[w] > 0, c, n_chunks - 1)

    def wmap(w, c, ie, ib, ifull, itail):
        return (l, ie[w], 0, cmap(w, c, ifull, itail))

    grid_spec = pltpu.PrefetchScalarGridSpec(
        num_scalar_prefetch=4,
        grid=(n_items, n_chunks),
        in_specs=[
            pl.BlockSpec((MOE_ROWS, ff), lambda w, c, ie, ib, ifull, itail: (ib[w], 0)),
            pl.BlockSpec((None, None, ff, tn), wmap),
            pl.BlockSpec((None, None, 1, tn), wmap),
        ],
        out_specs=pl.BlockSpec((MOE_ROWS * sub, LANES),
                               lambda w, c, ie, ib, ifull, itail: (ib[w], 0)),
        scratch_shapes=[pltpu.VMEM((ff, tn), BF16)],
    )
    return pl.pallas_call(
        functools.partial(_moe_down_kernel, sub=sub),
        grid_spec=grid_spec,
        out_shape=jax.ShapeDtypeStruct((r * sub, LANES), jnp.uint32),
        compiler_params=_cparams(("arbitrary", "arbitrary")),
        name="moe_down",
    )(item_e, item_blk, item_full, item_tail, hs, w_down, b_down4)


GATHER_UNROLL = 8


def _slab_gather(idx_ref, src_ref, buf, sem, base_slab, n, sub, wait):
    def copy(r):
        src_row = pl.multiple_of(idx_ref[0, r] * sub, sub)
        dst_row = pl.multiple_of((base_slab + r) * sub, sub)
        return pltpu.make_async_copy(src_ref.at[pl.ds(src_row, sub), :],
                                     buf.at[pl.ds(dst_row, sub), :], sem)

    def body(r0, _):
        for u in range(GATHER_UNROLL):
            c = copy(r0 * GATHER_UNROLL + u)
            if wait:
                c.wait()
            else:
                c.start(priority=u % 2)
        return 0

    lax.fori_loop(0, n // GATHER_UNROLL, body, 0)


def _dispatch_kernel(vblk, nvalid, tok_ref, tok_next_ref, src_ref, o_ref, buf, sems, *, rows, d):
    s = pl.program_id(0)
    slot = s % 2
    sub = _slab_sub(d)

    @pl.when(s == 0)
    def _():
        _slab_gather(tok_ref, src_ref, buf, sems.at[0], 0, rows, sub, wait=False)

    @pl.when(s + 1 < nvalid[0])
    def _():
        _slab_gather(tok_next_ref, src_ref, buf, sems.at[1 - slot], (1 - slot) * rows, rows, sub,
                     wait=False)

    @pl.when((s < nvalid[0]) | (s == 0))
    def _():
        _slab_gather(tok_ref, src_ref, buf, sems.at[slot], slot * rows, rows, sub, wait=True)
        half = SLAB_CHUNK // 2
        for g in range(d // SLAB_CHUNK):
            for q in range(SLAB_Q):
                lo, hi = _unpack_words(buf[_slab_index(g * SLAB_Q + q, slot * rows, rows, sub)])
                c0 = g * SLAB_CHUNK + q * LANES
                o_ref[:, c0:c0 + LANES] = lo.astype(BF16)
                o_ref[:, c0 + half:c0 + half + LANES] = hi.astype(BF16)


def _dispatch(u2p, d, row_tok, vblk, nvalid, rows=MOE_BLK):
    r = row_tok.shape[0]
    n_blk = r // rows
    n_steps = vblk.shape[0]
    tok3 = row_tok.reshape(n_blk, 1, rows)
    grid_spec = pltpu.PrefetchScalarGridSpec(
        num_scalar_prefetch=2,
        grid=(n_steps,),
        in_specs=[
            pl.BlockSpec((None, 1, rows), lambda s, vb, nv: (vb[s], 0, 0), memory_space=pltpu.SMEM),
            pl.BlockSpec((None, 1, rows), lambda s, vb, nv: (vb[jnp.minimum(s + 1, n_steps - 1)], 0, 0),
                         memory_space=pltpu.SMEM),
            pl.BlockSpec(memory_space=pl.ANY),
        ],
        out_specs=pl.BlockSpec((rows, d), lambda s, vb, nv: (vb[s], 0)),
        scratch_shapes=[pltpu.VMEM((2 * rows * _slab_sub(d), LANES), jnp.uint32),
                        pltpu.SemaphoreType.DMA((2,))],
    )
    return pl.pallas_call(
        functools.partial(_dispatch_kernel, rows=rows, d=d),
        grid_spec=grid_spec,
        out_shape=jax.ShapeDtypeStruct((r, d), BF16),
        compiler_params=_cparams(("arbitrary",)),
        name="moe_dispatch",
    )(vblk, nvalid, tok3, tok3, u2p)


def _combine_ln2_kernel(slot_ref, slot_next_ref, x1_ref, ys_ref, gates_ref, mod_ref, g_ref, b_ref,
                        o_ref, buf, ffn_s, sems, *, alpha, tm):
    i = pl.program_id(0)
    n = pl.num_programs(0)
    cur = i % 2
    n_rows = TOP_K * tm
    d = ffn_s.shape[1]
    sub = _slab_sub(d)

    @pl.when(i == 0)
    def _():
        _slab_gather(slot_ref, ys_ref, buf, sems.at[0], 0, n_rows, sub, wait=False)

    @pl.when(i + 1 < n)
    def _():
        _slab_gather(slot_next_ref, ys_ref, buf, sems.at[1 - cur], (1 - cur) * n_rows, n_rows, sub,
                     wait=False)

    _slab_gather(slot_ref, ys_ref, buf, sems.at[cur], cur * n_rows, n_rows, sub, wait=True)

    gates = gates_ref[...]
    gk = [jnp.broadcast_to(gates[:, k:k + 1], (tm, LANES)) for k in range(TOP_K)]
    half = SLAB_CHUNK // 2
    for g in range(d // SLAB_CHUNK):
        for q in range(SLAB_Q):
            lo_acc = None
            hi_acc = None
            for k in range(TOP_K):
                idx = _slab_index(g * SLAB_Q + q, cur * n_rows + k * tm, tm, sub)
                lo, hi = _unpack_words(buf[idx])
                lo_acc = gk[k] * lo if lo_acc is None else lo_acc + gk[k] * lo
                hi_acc = gk[k] * hi if hi_acc is None else hi_acc + gk[k] * hi
            c0 = g * SLAB_CHUNK + q * LANES
            ffn_s[:, c0:c0 + LANES] = lo_acc
            ffn_s[:, c0 + half:c0 + half + LANES] = hi_acc
    gt2 = mod_ref[5:6, :]
    z = alpha * x1_ref[...] + gt2 * ffn_s[...]
    o_ref[...] = _layer_norm_rows(z, g_ref[...], b_ref[...])


def _combine_ln2(x1, ys, slot, gates, mod3, g3, b3, l, seq, alpha, tm=256):
    t, d = x1.shape
    per_b = seq // tm
    n_tiles = t // tm
    slot3 = slot.reshape(n_tiles, tm, TOP_K).transpose(0, 2, 1).reshape(n_tiles, 1, TOP_K * tm)
    return pl.pallas_call(
        functools.partial(_combine_ln2_kernel, alpha=alpha, tm=tm),
        grid=(n_tiles,),
        in_specs=[
            pl.BlockSpec((None, 1, tm * TOP_K), lambda i: (i, 0, 0), memory_space=pltpu.SMEM),
            pl.BlockSpec((None, 1, tm * TOP_K), lambda i: (jnp.minimum(i + 1, n_tiles - 1), 0, 0),
                         memory_space=pltpu.SMEM),
            pl.BlockSpec((tm, d), lambda i: (i, 0)),
            pl.BlockSpec(memory_space=pl.ANY),
            pl.BlockSpec((tm, LANES), lambda i: (i, 0)),
            pl.BlockSpec((None, 6, d), lambda i: (i // per_b, 0, 0)),
            pl.BlockSpec((None, 1, d), lambda i: (l, 0, 0)),
            pl.BlockSpec((None, 1, d), lambda i: (l, 0, 0)),
        ],
        out_specs=pl.BlockSpec((tm, d), lambda i: (i, 0)),
        out_shape=jax.ShapeDtypeStruct((t, d), F32),
        scratch_shapes=[pltpu.VMEM((2 * TOP_K * tm * _slab_sub(d), LANES), jnp.uint32),
                        pltpu.VMEM((tm, d), F32),
                        pltpu.SemaphoreType.DMA((2,))],
        compiler_params=_cparams(("arbitrary",)),
        name="combine_ln2",
    )(slot3, slot3, x1, ys, gates, mod3, g3, b3)


def _t5_bucket(dist):
    is_small = dist < REL_MAX_EXACT
    nf = jnp.maximum(dist, REL_MAX_EXACT).astype(F32)
    large = REL_MAX_EXACT + (jnp.log(nf / REL_MAX_EXACT) / math.log(REL_MAX_DISTANCE / REL_MAX_EXACT)
                             * (REL_BUCKETS - REL_MAX_EXACT)).astype(jnp.int32)
    large = jnp.minimum(large, REL_BUCKETS - 1)
    return jnp.where(is_small, dist, large)


def _dilated_bias_vectors(rel_bias):
    m = jnp.arange(2 * BLOCK)
    rel = BLOCK - m
    vecs = []
    for window, d in DIL_PATTERNS:
        valid = (rel >= 0) & (rel <= window // d)
        bucket = _t5_bucket(jnp.clip(rel, 0) * d)
        bias = jnp.transpose(rel_bias[bucket].astype(F32), (1, 0))
        vecs.append(jnp.where(valid[None, :], bias, NEG_BIG))
    return jnp.stack(vecs, axis=0)[:, :, None, :]


def _routing_tables(ids4, n_items_max):
    e = ids4.reshape(-1)
    n_rows_real = e.shape[0]
    onehot =(e[:, None] == jnp.arange(N_EXPERTS, dtype=jnp.int32)[None, :]).astype(jnp.int32)
    csum = jnp.cumsum(onehot, axis=0)
    rank = jnp.sum(onehot * csum, axis=1) - 1
    counts = csum[-1]
    nit = (counts + MOE_ROWS - 1) // MOE_ROWS
    cum_items = jnp.cumsum(nit)
    first_item = cum_items - nit
    slot = first_item[e] * MOE_ROWS + rank
    total = cum_items[-1]
    w = jnp.arange(n_items_max, dtype=jnp.int32)
    e_w = jnp.minimum(jnp.searchsorted(cum_items, w, side="right"), N_EXPERTS - 1).astype(jnp.int32)
    j_w = w - first_item[e_w]
    rows_w = jnp.clip(counts[e_w] - j_w * MOE_ROWS, 0, MOE_ROWS)
    valid = w < total
    rows_w = jnp.where(valid, rows_w, 0)
    tails_per_full = MOE_SUB // MOE_TAIL
    n_full = rows_w // MOE_SUB
    n_tail = (rows_w - n_full * MOE_SUB + MOE_TAIL - 1) // MOE_TAIL
    item_full = jnp.where(n_tail == tails_per_full, n_full + 1, n_full).astype(jnp.int32)
    item_tail = jnp.where(n_tail == tails_per_full, 0, n_tail).astype(jnp.int32)
    last = total - 1
    item_e = jnp.where(valid, e_w, e_w[last]).astype(jnp.int32)
    item_blk = jnp.where(valid, w, last).astype(jnp.int32)
    per_item = MOE_ROWS // MOE_BLK
    blk = jnp.arange(n_items_max * per_item, dtype=jnp.int32)
    blk_valid = ((blk % per_item) * MOE_BLK < rows_w[blk // per_item]).astype(jnp.int32)
    cum_valid = jnp.cumsum(blk_valid)
    nvalid = cum_valid[-1]
    n_steps = n_rows_real // MOE_BLK + n_items_max
    want = jnp.minimum(jnp.arange(n_steps, dtype=jnp.int32) + 1, nvalid)
    vblk = jnp.searchsorted(cum_valid, want, side="left").astype(jnp.int32)
    return (slot.astype(jnp.int32), item_e, item_blk, item_full, item_tail, vblk,
            nvalid.astype(jnp.int32).reshape(1))


def kernel(x, c, positions, w_ada, b_ada, w_in, g_q, g_kv, w_uq, w_ukv, rel_bias, w_o,
           ln1_g, ln1_b, w_router, b_router, w_gate, b_gate, w_up, b_up, w_down, b_down,
           ln2_g, ln2_b):
    batch, seq, d = x.shape
    depth = w_ada.shape[0]
    t = batch * seq
    n_heads = d // (2 * HEAD_DIM)
    dil_w = n_heads * HEAD_DIM
    alpha = (2.0 * depth) ** 0.25
    assert seq % (BLOCK * max(dd for _, dd in DIL_PATTERNS)) == 0
    assert all(win // dd == BLOCK for win, dd in DIL_PATTERNS)

    inv_freq = 1.0 / (ROPE_THETA ** (jnp.arange(0, QK_ROPE_DIM, 2, dtype=F32) / QK_ROPE_DIM))
    ang = positions.astype(F32)[..., None] * inv_freq
    cos, sin = jnp.cos(ang).reshape(t, -1), jnp.sin(ang).reshape(t, -1)
    zpad = jnp.zeros((t, LANES - QK_ROPE_DIM), F32)
    cos_t = jnp.concatenate([cos, cos, zpad], axis=1)
    sin_t = jnp.concatenate([-sin, sin, zpad], axis=1)

    bias_vec = _dilated_bias_vectors(rel_bias)
    c_pad = jnp.zeros((16, d), F32).at[:batch].set(c)

    x2 = x.reshape(t, d)
    n_items_max = (t * TOP_K) // MOE_ROWS + N_EXPERTS
    mla_cols = Q_LORA_RANK + KV_LORA_RANK

    for l in range(depth):
        mod = _ada(c_pad, w_ada, b_ada.reshape(depth, 1, -1), l)[:batch]
        mod3 = mod.reshape(batch, 6, d)

        w_in_t = jnp.swapaxes(w_in[l], 0, 1)
        wq = w_uq[l].reshape(Q_LORA_RANK, n_heads, QK_NOPE_DIM + QK_ROPE_DIM)
        w_uq_p = jnp.pad(wq, ((0, 0), (0, 0), (0, QK_PAD - QK_NOPE_DIM - QK_ROPE_DIM))
                         ).reshape(Q_LORA_RANK, n_heads * QK_PAD).astype(BF16)
        wkv = w_ukv[l].reshape(KV_LORA_RANK, n_heads, QK_NOPE_DIM + V_HEAD_DIM)
        w_uk = wkv[:, :, :QK_NOPE_DIM].reshape(KV_LORA_RANK, -1).astype(BF16)
        w_uv = wkv[:, :, QK_NOPE_DIM:].reshape(KV_LORA_RANK, -1).astype(BF16)

        u1 = _modulate(x2, mod3, seq, 0, 1)
        hm = _in_proj_mla(u1, w_in_t, mla_cols)
        qkv_d = _in_proj_dil(u1, w_in_t, mla_cols + QK_ROPE_DIM, batch, seq, n_heads)
        krr = _in_proj_krope(u1, w_in_t, mla_cols, cos_t, sin_t)
        q_a = _q_up(hm, g_q.reshape(depth, 1, -1), l, w_uq_p, cos_t, sin_t, n_heads)
        k_a, v_a = _kv_up(hm, g_kv.reshape(depth, 1, -1), l, w_uk, w_uv, krr, n_heads)
        o_a = _mla_attn(q_a, k_a, v_a, batch, seq, n_heads)
        o_b = _dil_attn(qkv_d, bias_vec, batch, seq, n_heads)
        mix = _out_proj(o_a, o_b, w_o, l)

        wr_p = jnp.pad(w_router[l], ((0, 0), (0, LANES - N_EXPERTS)))
        br_p = jnp.concatenate([b_router[l], jnp.full((LANES - N_EXPERTS,), NEG_BIG, F32)])[None, :]
        x1, u2p, ids, gates = _ln1_router(
            x2, mix, mod3, ln1_g.reshape(depth, 1, -1), ln1_b.reshape(depth, 1, -1),
            wr_p, br_p, l, seq, alpha)

        slot, item_e, item_blk, item_full, item_tail, vblk, nvalid = _routing_tables(
            ids[:, :TOP_K], n_items_max)
        tok = jnp.arange(t * TOP_K, dtype=jnp.int32) // TOP_K
        row_tok = jnp.zeros((n_items_max * MOE_ROWS,), jnp.int32).at[slot].set(tok)
        xs = _dispatch(u2p, d, row_tok, vblk, nvalid)
        hs = _moe_up(xs, w_gate, w_up, b_gate.reshape(depth, N_EXPERTS, 1, -1),
                     b_up.reshape(depth, N_EXPERTS, 1, -1), item_e, item_blk, item_full,
                     item_tail, l)
        ys = _moe_down(hs, w_down, b_down.reshape(depth, N_EXPERTS, 1, -1),
                       item_e, item_blk, item_full, item_tail, l)
        x2 = _combine_ln2(x1, ys, slot, gates, mod3, ln2_g.reshape(depth, 1, -1),
                          ln2_b.reshape(depth, 1, -1), l, seq, alpha)

    return x2.reshape(batch, seq, d)
```

```python
import functools
import math

import jax
import jax.numpy as jnp
from jax import lax
from jax.experimental import pallas as pl
from jax.experimental.pallas import tpu as pltpu

F32 = jnp.float32
BF16 = jnp.bfloat16

HEAD_DIM = 128
Q_LORA_RANK = 1536
KV_LORA_RANK = 512
QK_NOPE_DIM = 128
QK_ROPE_DIM = 64
V_HEAD_DIM = 128
ROPE_THETA = 10000.0
DIL_PATTERNS = ((128, 1), (512, 4), (2048, 16))
BLOCK = 128
REL_BUCKETS = 32
REL_MAX_EXACT = REL_BUCKETS // 2
REL_MAX_DISTANCE = 2048
N_EXPERTS = 32
TOP_K = 4
EXPERT_FF = 1536
SWIGLU_LIMIT = 7.0
SWIGLU_ALPHA = 1.702
LN_EPS = 1e-5
RMS_EPS = 1e-6
NEG_BIG = -1e30
LOG2E = math.log2(math.e)

LANES = 128
QK_PAD = 256
VMEM_LIMIT = 56 * 1024 * 1024
MOE_ROWS = 1280
MOE_SUB = 512
MOE_TAIL = 128
MOE_BLK = 256


def _cparams(sem):
    return pltpu.CompilerParams(dimension_semantics=sem, vmem_limit_bytes=VMEM_LIMIT)


def _ada_kernel(c_ref, w_ref, b_ref, o_ref):
    c = c_ref[...]
    cond = (c * jax.nn.sigmoid(c)).astype(BF16)
    o_ref[...] = jnp.dot(cond, w_ref[...].astype(BF16), preferred_element_type=F32) + b_ref[...]


def _ada(c_pad, w_ada, b_ada3, l, tn=512):
    rows, d = c_pad.shape
    n = w_ada.shape[-1]
    return pl.pallas_call(
        _ada_kernel,
        grid=(n // tn,),
        in_specs=[
            pl.BlockSpec((rows, d), lambda j: (0, 0)),
            pl.BlockSpec((None, d, tn), lambda j: (l, 0, j)),
            pl.BlockSpec((None, 1, tn), lambda j: (l, 0, j)),
        ],
        out_specs=pl.BlockSpec((rows, tn), lambda j: (0, j)),
        out_shape=jax.ShapeDtypeStruct((rows, n), F32),
        compiler_params=_cparams(("arbitrary",)),
        name="ada_mod",
    )(c_pad, w_ada, b_ada3)


def _modulate_kernel(x_ref, mod_ref, o_ref, *, shift_row, scale_row):
    sh = mod_ref[shift_row:shift_row + 1, :]
    sc = mod_ref[scale_row:scale_row + 1, :]
    o_ref[...] = (x_ref[...] * (1.0 + sc) + sh).astype(BF16)


def _modulate(x2, mod3, seq, shift_row, scale_row, tm=512):
    t, d = x2.shape
    per_b = seq // tm
    return pl.pallas_call(
        functools.partial(_modulate_kernel, shift_row=shift_row, scale_row=scale_row),
        grid=(t // tm,),
        in_specs=[
            pl.BlockSpec((tm, d), lambda i: (i, 0)),
            pl.BlockSpec((None, 6, d), lambda i: (i // per_b, 0, 0)),
        ],
        out_specs=pl.BlockSpec((tm, d), lambda i: (i, 0)),
        out_shape=jax.ShapeDtypeStruct((t, d), BF16),
        compiler_params=_cparams(("arbitrary",)),
        name="modulate",
    )(x2, mod3)


def _rope_block(x, cos_t, sin_t):
    lane = lax.broadcasted_iota(jnp.int32, x.shape, 1)
    half = QK_ROPE_DIM // 2
    swapped = jnp.where(lane < half, pltpu.roll(x, LANES - half, 1), pltpu.roll(x, half, 1))
    return x * cos_t + swapped * sin_t


def _load_weight_rows(wt_ref, row0, wf_s, wb_s):
    n_rows = wf_s.shape[0]
    pltpu.sync_copy(wt_ref.at[pl.ds(pl.multiple_of(row0, 8), n_rows), :], wf_s)
    wb_s[0:n_rows, :] = wf_s[...].astype(BF16)


def _dot_nt(a, w):
    return lax.dot_general(a, w, (((1,), (1,)), ((), ())), preferred_element_type=F32)


def _proj_plain_kernel(a_ref, wt_ref, o_ref, wf_s, wb_s, *, row_base):
    @pl.when(pl.program_id(1) == 0)
    def _():
        _load_weight_rows(wt_ref, row_base + pl.program_id(0) * wf_s.shape[0], wf_s, wb_s)

    o_ref[...] = _dot_nt(a_ref[...], wb_s[...]).astype(o_ref.dtype)


def _proj_heads_kernel(a_ref, wt_ref, o_ref, wf_s, wb_s, *, row_base, heads_per_tile):
    @pl.when(pl.program_id(1) == 0)
    def _():
        _load_weight_rows(wt_ref, row_base + pl.program_id(0) * wf_s.shape[0], wf_s, wb_s)

    res = _dot_nt(a_ref[...], wb_s[...])
    for hh in range(heads_per_tile):
        o_ref[hh] = res[:, hh * HEAD_DIM:(hh + 1) * HEAD_DIM].astype(o_ref.dtype)


def _proj_rope_kernel(a_ref, wt_ref, cos_ref, sin_ref, o_ref, wf_s, wb_s, *, row_base):
    @pl.when(pl.program_id(0) == 0)
    def _():
        wb_s[...] = jnp.zeros(wb_s.shape, BF16)
        _load_weight_rows(wt_ref, row_base, wf_s, wb_s)

    res = _dot_nt(a_ref[...], wb_s[...])
    o_ref[...] = _rope_block(res, cos_ref[...], sin_ref[...]).astype(o_ref.dtype)


def _in_proj_mla(u, wt, n_cols, tm=512, tn=512):
    t, d = u.shape
    return pl.pallas_call(
        functools.partial(_proj_plain_kernel, row_base=0),
        grid=(n_cols // tn, t // tm),
        in_specs=[
            pl.BlockSpec((tm, d), lambda j, i: (i, 0)),
            pl.BlockSpec(memory_space=pl.ANY),
        ],
        out_specs=pl.BlockSpec((tm, tn), lambda j, i: (i, j)),
        out_shape=jax.ShapeDtypeStruct((t, n_cols), BF16),
        scratch_shapes=[pltpu.VMEM((tn, d), F32), pltpu.VMEM((tn, d), BF16)],
        compiler_params=_cparams(("arbitrary", "arbitrary")),
        name="in_proj_mla",
    )(u, wt)


def _in_proj_dil(u, wt, col0, batch, seq, n_heads, tm=512, tn=512):
    t, d = u.shape
    hpt = tn // HEAD_DIM
    tiles_per_mat = n_heads // hpt
    per_b = seq // tm
    return pl.pallas_call(
        functools.partial(_proj_heads_kernel, row_base=col0, heads_per_tile=hpt),
        grid=(3 * tiles_per_mat, t // tm),
        in_specs=[
            pl.BlockSpec((tm, d), lambda j, i: (i, 0)),
            pl.BlockSpec(memory_space=pl.ANY),
        ],
        out_specs=pl.BlockSpec(
            (None, None, hpt, tm, HEAD_DIM),
            lambda j, i: (j // tiles_per_mat, i // per_b, j % tiles_per_mat, i % per_b, 0)),
        out_shape=jax.ShapeDtypeStruct((3, batch, n_heads, seq, HEAD_DIM), BF16),
        scratch_shapes=[pltpu.VMEM((tn, d), F32), pltpu.VMEM((tn, d), BF16)],
        compiler_params=_cparams(("arbitrary", "arbitrary")),
        name="in_proj_dil",
    )(u, wt)


def _in_proj_krope(u, wt, col0, cos_t, sin_t, tm=1024):
    t, d = u.shape
    return pl.pallas_call(
        functools.partial(_proj_rope_kernel, row_base=col0),
        grid=(t // tm,),
        in_specs=[
            pl.BlockSpec((tm, d), lambda i: (i, 0)),
            pl.BlockSpec(memory_space=pl.ANY),
            pl.BlockSpec((tm, LANES), lambda i: (i, 0)),
            pl.BlockSpec((tm, LANES), lambda i: (i, 0)),
        ],
        out_specs=pl.BlockSpec((tm, LANES), lambda i: (i, 0)),
        out_shape=jax.ShapeDtypeStruct((t, LANES), BF16),
        scratch_shapes=[pltpu.VMEM((QK_ROPE_DIM, d), F32), pltpu.VMEM((LANES, d), BF16)],
        compiler_params=_cparams(("arbitrary",)),
        name="in_proj_krope",
    )(u, wt, cos_t, sin_t)


def _rms_bf16(h_ref, g_ref):
    hf = h_ref[...].astype(F32)
    y = hf * lax.rsqrt(jnp.mean(hf * hf, axis=-1, keepdims=True) + RMS_EPS)
    return (y * g_ref[...]).astype(BF16)


def _q_up_kernel(h_ref, g_ref, w_ref, cos_ref, sin_ref, o_ref, *, n_heads, scale):
    yb = _rms_bf16(h_ref, g_ref)
    cos_t = cos_ref[...]
    sin_t = sin_ref[...]
    group = 4
    for h0 in range(0, n_heads, group):
        res = jnp.dot(yb, w_ref[:, h0 * QK_PAD:(h0 + group) * QK_PAD], preferred_element_type=F32)
        for hh in range(group):
            c0 = hh * QK_PAD
            nope = res[:, c0:c0 + QK_NOPE_DIM]
            rp = _rope_block(res[:, c0 + QK_NOPE_DIM:c0 + QK_PAD], cos_t, sin_t)
            o0 = (h0 + hh) * QK_PAD
            o_ref[:, o0:o0 + QK_NOPE_DIM] = (nope * scale).astype(BF16)
            o_ref[:, o0 + QK_NOPE_DIM:o0 + QK_PAD] = (rp * scale).astype(BF16)


def _q_up(hm, g3, l, w_uq_p, cos_t, sin_t, n_heads, tm=256):
    t = hm.shape[0]
    scale = float(QK_NOPE_DIM + QK_ROPE_DIM) ** -0.5 * LOG2E
    return pl.pallas_call(
        functools.partial(_q_up_kernel, n_heads=n_heads, scale=scale),
        grid=(t // tm,),
        in_specs=[
            pl.BlockSpec((tm, Q_LORA_RANK), lambda i: (i, 0)),
            pl.BlockSpec((None, 1, Q_LORA_RANK), lambda i: (l, 0, 0)),
            pl.BlockSpec((Q_LORA_RANK, n_heads * QK_PAD), lambda i: (0, 0)),
            pl.BlockSpec((tm, LANES), lambda i: (i, 0)),
            pl.BlockSpec((tm, LANES), lambda i: (i, 0)),
        ],
        out_specs=pl.BlockSpec((tm, n_heads * QK_PAD), lambda i: (i, 0)),
        out_shape=jax.ShapeDtypeStruct((t, n_heads * QK_PAD), BF16),
        compiler_params=_cparams(("arbitrary",)),
        name="mla_q_up",
    )(hm, g3, w_uq_p, cos_t, sin_t)


def _kv_up_kernel(h_ref, g_ref, wk_ref, wv_ref, kr_ref, k_ref, v_ref, *, n_heads):
    yb = _rms_bf16(h_ref, g_ref)
    kr = kr_ref[...]
    group = 4
    for h0 in range(0, n_heads, group):
        res = jnp.dot(yb, wk_ref[:, h0 * QK_NOPE_DIM:(h0 + group) * QK_NOPE_DIM],
                      preferred_element_type=F32)
        for hh in range(group):
            o0 = (h0 + hh) * QK_PAD
            k_ref[:, o0:o0 + QK_NOPE_DIM] = res[:, hh * QK_NOPE_DIM:(hh + 1) * QK_NOPE_DIM].astype(BF16)
            k_ref[:, o0 + QK_NOPE_DIM:o0 + QK_PAD] = kr
    v_ref[...] = jnp.dot(yb, wv_ref[...], preferred_element_type=F32).astype(BF16)


def _kv_up(hm, g3, l, w_uk, w_uv, krr, n_heads, tm=256):
    t = hm.shape[0]
    col_blk = Q_LORA_RANK // KV_LORA_RANK
    return pl.pallas_call(
        functools.partial(_kv_up_kernel, n_heads=n_heads),
        grid=(t // tm,),
        in_specs=[
            pl.BlockSpec((tm, KV_LORA_RANK), lambda i: (i, col_blk)),
            pl.BlockSpec((None, 1, KV_LORA_RANK), lambda i: (l, 0, 0)),
            pl.BlockSpec((KV_LORA_RANK, n_heads * QK_NOPE_DIM), lambda i: (0, 0)),
            pl.BlockSpec((KV_LORA_RANK, n_heads * V_HEAD_DIM), lambda i: (0, 0)),
            pl.BlockSpec((tm, LANES), lambda i: (i, 0)),
        ],
        out_specs=[
            pl.BlockSpec((tm, n_heads * QK_PAD), lambda i: (i, 0)),
            pl.BlockSpec((tm, n_heads * V_HEAD_DIM), lambda i: (i, 0)),
        ],
        out_shape=[
            jax.ShapeDtypeStruct((t, n_heads * QK_PAD), BF16),
            jax.ShapeDtypeStruct((t, n_heads * V_HEAD_DIM), BF16),
        ],
        compiler_params=_cparams(("arbitrary",)),
        name="mla_kv_up",
    )(hm, g3, w_uk, w_uv, krr)


MLA_HEADS_PER_STEP = 2


def _mla_attn_kernel(q_ref, k_ref, v_ref, o_ref, *, seq, tq):
    nq = seq // tq
    hp = MLA_HEADS_PER_STEP
    row = lax.broadcasted_iota(jnp.int32, (tq, tq), 0)
    col = lax.broadcasted_iota(jnp.int32, (tq, tq), 1)
    causal = col <= row

    def kv_step(qs, j, carry, masked):
        off = pl.multiple_of(j * tq, tq)
        out = []
        for hh in range(hp):
            m, l, acc = carry[hh]
            k = k_ref[pl.ds(off, tq), hh * QK_PAD:(hh + 1) * QK_PAD]
            v = v_ref[pl.ds(off, tq), hh * V_HEAD_DIM:(hh + 1) * V_HEAD_DIM]
            s = lax.dot_general(qs[hh], k, (((1,), (1,)), ((), ())), preferred_element_type=F32)
            if masked:
                s = jnp.where(causal, s, NEG_BIG)
            m_new = jnp.maximum(m, jnp.max(s, axis=-1, keepdims=True))
            p = jnp.exp2(s - m_new)
            alpha = jnp.exp2(m - m_new)
            l_new = alpha * l + jnp.sum(p, axis=-1, keepdims=True)
            acc_new = alpha * acc + jnp.dot(p.astype(BF16), v, preferred_element_type=F32)
            out.append((m_new, l_new, acc_new))
        return tuple(out)

    def q_loop(i, _):
        qoff = pl.multiple_of(i * tq, tq)
        qs = [q_ref[pl.ds(qoff, tq), hh * QK_PAD:(hh + 1) * QK_PAD] for hh in range(hp)]
        init = tuple((jnp.full((tq, 1), NEG_BIG, F32), jnp.zeros((tq, 1), F32),
                      jnp.zeros((tq, V_HEAD_DIM), F32)) for _ in range(hp))
        carry = lax.fori_loop(0, i, lambda j, c: kv_step(qs, j, c, False), init)
        carry = kv_step(qs, i, carry, True)
        for hh in range(hp):
            m, l, acc = carry[hh]
            o_ref[pl.ds(qoff, tq), hh * V_HEAD_DIM:(hh + 1) * V_HEAD_DIM] = (acc / l).astype(o_ref.dtype)
        return 0

    lax.fori_loop(0, nq, q_loop, 0)


def _mla_attn(q, k, v, batch, seq, n_heads, tq=512):
    t = q.shape[0]
    hp = MLA_HEADS_PER_STEP
    return pl.pallas_call(
        functools.partial(_mla_attn_kernel, seq=seq, tq=tq),
        grid=(batch, n_heads // hp),
        in_specs=[
            pl.BlockSpec((seq, hp * QK_PAD), lambda b, h: (b, h)),
            pl.BlockSpec((seq, hp * QK_PAD), lambda b, h: (b, h)),
            pl.BlockSpec((seq, hp * V_HEAD_DIM), lambda b, h: (b, h)),
        ],
        out_specs=pl.BlockSpec((seq, hp * V_HEAD_DIM), lambda b, h: (b, h)),
        out_shape=jax.ShapeDtypeStruct((t, n_heads * V_HEAD_DIM), BF16),
        compiler_params=_cparams(("arbitrary", "arbitrary")),
        name="mla_attn",
    )(q, k, v)


DIL_GROUP = 16


def _dil_attn_kernel(qkv_ref, bvec_ref, o_ref, m_s, l_s, acc_s, bias_s, nat_f, *view_s, seq):
    scale = float(HEAD_DIM) ** -0.5 * LOG2E
    views = (qkv_ref,) + tuple(view_s)

    for bi in range(len(DIL_PATTERNS)):
        full = jnp.broadcast_to(bvec_ref[bi] * LOG2E, (BLOCK, 2 * BLOCK))
        bias_s[bi] = pltpu.roll(full, 0, 1, stride=1, stride_axis=0)

    for which in range(3):
        nat_f[...] = qkv_ref[which].astype(F32)
        for bi, (_, d) in enumerate(DIL_PATTERNS):
            if d == 1:
                continue
            for r in range(d):
                views[bi][which, :, r * HEAD_DIM:(r + 1) * HEAD_DIM] = (
                    nat_f[pl.ds(r, seq // d, stride=d), :].astype(BF16))

    def scores(ref, bi, d, r, i, first):
        lanes = slice(r * HEAD_DIM, (r + 1) * HEAD_DIM)
        qoff = pl.multiple_of(i * BLOCK, BLOCK)
        q = ref[0, pl.ds(qoff, BLOCK), lanes]
        if first:
            kk = ref[1, pl.ds(0, BLOCK), lanes]
            vv = ref[2, pl.ds(0, BLOCK), lanes]
            bias = bias_s[bi, :, BLOCK:2 * BLOCK]
        else:
            koff = pl.multiple_of(i * BLOCK - BLOCK, BLOCK)
            kk = ref[1, pl.ds(koff, 2 * BLOCK), lanes]
            vv = ref[2, pl.ds(koff, 2 * BLOCK), lanes]
            bias = bias_s[bi]
        s = lax.dot_general(q, kk, (((1,), (1,)), ((), ())), preferred_element_type=F32)
        s = s * scale + bias
        m_b = jnp.max(s, axis=-1, keepdims=True)
        p = jnp.exp2(s - m_b)
        l_b = jnp.sum(p, axis=-1, keepdims=True)
        a_b = jnp.dot(p.astype(BF16), vv, preferred_element_type=F32)
        m_b = jnp.broadcast_to(m_b, (BLOCK, HEAD_DIM))
        l_b = jnp.broadcast_to(l_b, (BLOCK, HEAD_DIM))
        if d == 1:
            rows = pl.ds(qoff, BLOCK)
        else:
            rows = pl.ds(i * (BLOCK * d) + r, BLOCK, stride=d)
        return m_b, l_b, a_b, rows

    def group(ref, bi, d, blocks):
        parts = [scores(ref, bi, d, r, i, first) for r, i, first in blocks]
        if d == 1:
            for m_b, l_b, a_b, rows in parts:
                m_s[rows, :] = m_b
                l_s[rows, :] = l_b
                acc_s[rows, :] = a_b
            return
        old = [(m_s[rows, :], l_s[rows, :], acc_s[rows, :]) for _, _, _, rows in parts]
        new = []
        for (m_b, l_b, a_b, rows), (m_o, l_o, a_o) in zip(parts, old):
            m_n = jnp.maximum(m_o, m_b)
            e_o = jnp.exp2(m_o - m_n)
            e_b = jnp.exp2(m_b - m_n)
            new.append((m_n, e_o * l_o + e_b * l_b, e_o * a_o + e_b * a_b, rows))
        for m_n, l_n, a_n, rows in new:
            m_s[rows, :] = m_n
            l_s[rows, :] = l_n
            acc_s[rows, :] = a_n

    for bi, (_, d) in enumerate(DIL_PATTERNS):
        ref = views[bi]
        nblk = seq // d // BLOCK
        n_res = min(d, DIL_GROUP)
        n_seq = DIL_GROUP // n_res
        assert d % n_res == 0 and nblk % n_seq == 0
        for r0 in range(0, d, n_res):
            group(ref, bi, d, [(r0 + u, i, i == 0) for i in range(n_seq) for u in range(n_res)])

            def body(t, _, ref=ref, bi=bi, d=d, r0=r0, n_res=n_res, n_seq=n_seq):
                group(ref, bi, d, [(r0 + u, t * n_seq + ii, False)
                                   for ii in range(n_seq) for u in range(n_res)])
                return 0

            lax.fori_loop(1, nblk // n_seq, body, 0)

    o_ref[...] = (acc_s[...] / l_s[...]).astype(o_ref.dtype)


def _dil_attn(qkv, bias_vec, batch, seq, n_heads):
    t = batch * seq
    nbr = len(DIL_PATTERNS)
    assert DIL_PATTERNS[0][1] == 1
    view_scratch = [pltpu.VMEM((3, seq // d, d * HEAD_DIM), BF16) for _, d in DIL_PATTERNS[1:]]
    return pl.pallas_call(
        functools.partial(_dil_attn_kernel, seq=seq),
        grid=(batch, n_heads),
        in_specs=[
            pl.BlockSpec((3, None, None, seq, HEAD_DIM), lambda b, h: (0, b, h, 0, 0)),
            pl.BlockSpec((nbr, None, 1, 2 * BLOCK), lambda b, h: (0, h, 0, 0)),
        ],
        out_specs=pl.BlockSpec((seq, HEAD_DIM), lambda b, h: (b, h)),
        out_shape=jax.ShapeDtypeStruct((t, n_heads * HEAD_DIM), BF16),
        scratch_shapes=[pltpu.VMEM((seq, HEAD_DIM), F32)] * 3
        + [pltpu.VMEM((nbr, BLOCK, 2 * BLOCK), F32), pltpu.VMEM((seq, HEAD_DIM), F32)]
        + view_scratch,
        compiler_params=_cparams(("arbitrary", "arbitrary")),
        name="dil_attn",
    )(qkv, bias_vec)


def _out_proj_kernel(a1_ref, a2_ref, w_ref, o_ref, wb_ref):
    @pl.when(pl.program_id(1) == 0)
    def _():
        wb_ref[...] = w_ref[...].astype(BF16)

    k1 = a1_ref.shape[1]
    acc = jnp.dot(a1_ref[...], wb_ref[:k1, :], preferred_element_type=F32)
    acc = acc + jnp.dot(a2_ref[...], wb_ref[k1:, :], preferred_element_type=F32)
    o_ref[...] = acc.astype(o_ref.dtype)


def _out_proj(o_a, o_b, w_o, l, tm=512, tn=512):
    t, k1 = o_a.shape
    k2 = o_b.shape[1]
    n = w_o.shape[-1]
    return pl.pallas_call(
        _out_proj_kernel,
        grid=(n // tn, t // tm),
        in_specs=[
            pl.BlockSpec((tm, k1), lambda j, i: (i, 0)),
            pl.BlockSpec((tm, k2), lambda j, i: (i, 0)),
            pl.BlockSpec((None, k1 + k2, tn), lambda j, i: (l, 0, j)),
        ],
        out_specs=pl.BlockSpec((tm, tn), lambda j, i: (i, j)),
        out_shape=jax.ShapeDtypeStruct((t, n), BF16),
        scratch_shapes=[pltpu.VMEM((k1 + k2, tn), BF16)],
        compiler_params=_cparams(("arbitrary", "arbitrary")),
        name="out_proj",
    )(o_a, o_b, w_o)


def _layer_norm_rows(z, g, b):
    mu = jnp.mean(z, axis=-1, keepdims=True)
    zc = z - mu
    var = jnp.mean(zc * zc, axis=-1, keepdims=True)
    return zc * lax.rsqrt(var + LN_EPS) * g + b


SLAB_CHUNK = 1024
SLAB_Q = SLAB_CHUNK // (2 * LANES)


def _slab_sub(width):
    return width // (2 * LANES)


def _slab_index(j, first_tok, n_tok, sub):
    return (pl.ds(first_tok * sub + j, n_tok, stride=sub), slice(None))


def _pack_chunk(vals):
    bits = lax.bitcast_convert_type(vals.astype(BF16).astype(F32), jnp.uint32)
    half = SLAB_CHUNK // 2
    return [(bits[:, q * LANES:(q + 1) * LANES] >> 16)
            | bits[:, half + q * LANES:half + (q + 1) * LANES] for q in range(SLAB_Q)]


def _unpack_words(words):
    lo = lax.bitcast_convert_type(words << 16, F32)
    hi = lax.bitcast_convert_type(words & jnp.uint32(0xFFFF0000), F32)
    return lo, hi


def _ln1_router_kernel(x_ref, mix_ref, mod_ref, g_ref, b_ref, wr_ref, br_ref,
                       x1_ref, u2p_ref, ids_ref, gates_ref, *, alpha):
    gt1 = mod_ref[2:3, :]
    sh2 = mod_ref[3:4, :]
    sc2 = mod_ref[4:5, :]
    z = alpha * x_ref[...] + gt1 * mix_ref[...].astype(F32)
    x1 = _layer_norm_rows(z, g_ref[...], b_ref[...])
    x1_ref[...] = x1
    u2 = (x1 * (1.0 + sc2) + sh2).astype(BF16)
    tm, d = u2.shape
    for g in range(d // SLAB_CHUNK):
        for q, words in enumerate(_pack_chunk(u2[:, g * SLAB_CHUNK:(g + 1) * SLAB_CHUNK])):
            u2p_ref[_slab_index(g * SLAB_Q + q, 0, tm, _slab_sub(d))] = words
    logits = jnp.dot(u2, wr_ref[...].astype(BF16), preferred_element_type=F32) + br_ref[...]
    lane = lax.broadcasted_iota(jnp.int32, logits.shape, 1)
    lane_f = lane.astype(F32)
    vals = []
    ids = []
    for _ in range(TOP_K):
        mk = jnp.max(logits, axis=-1, keepdims=True)
        idx_f = jnp.min(jnp.where(logits == mk, lane_f, float(LANES)), axis=-1, keepdims=True)
        vals.append(mk)
        ids.append(idx_f.astype(jnp.int32))
        logits = jnp.where(lane_f == idx_f, -jnp.inf, logits)
    exps = [jnp.exp(v - vals[0]) for v in vals]
    denom = exps[0]
    for e in exps[1:]:
        denom = denom + e
    ids_out = jnp.zeros(lane.shape, jnp.int32)
    gates_out = jnp.zeros(lane.shape, F32)
    for k in range(TOP_K):
        ids_out = jnp.where(lane == k, ids[k], ids_out)
        gates_out = jnp.where(lane == k, exps[k] / denom, gates_out)
    ids_ref[...] = ids_out
    gates_ref[...] = gates_out


def _ln1_router(x2, mix, mod3, g3, b3, wr_p, br_p, l, seq, alpha, tm=256):
    t, d = x2.shape
    per_b = seq // tm
    return pl.pallas_call(
        functools.partial(_ln1_router_kernel, alpha=alpha),
        grid=(t // tm,),
        in_specs=[
            pl.BlockSpec((tm, d), lambda i: (i, 0)),
            pl.BlockSpec((tm, d), lambda i: (i, 0)),
            pl.BlockSpec((None, 6, d), lambda i: (i // per_b, 0, 0)),
            pl.BlockSpec((None, 1, d), lambda i: (l, 0, 0)),
            pl.BlockSpec((None, 1, d), lambda i: (l, 0, 0)),
            pl.BlockSpec((d, LANES), lambda i: (0, 0)),
            pl.BlockSpec((1, LANES), lambda i: (0, 0)),
        ],
        out_specs=[
            pl.BlockSpec((tm, d), lambda i: (i, 0)),
            pl.BlockSpec((tm * _slab_sub(d), LANES), lambda i: (i, 0)),
            pl.BlockSpec((tm, LANES), lambda i: (i, 0)),
            pl.BlockSpec((tm, LANES), lambda i: (i, 0)),
        ],
        out_shape=[
            jax.ShapeDtypeStruct((t, d), F32),
            jax.ShapeDtypeStruct((t * _slab_sub(d), LANES), jnp.uint32),
            jax.ShapeDtypeStruct((t, LANES), jnp.int32),
            jax.ShapeDtypeStruct((t, LANES), F32),
        ],
        compiler_params=_cparams(("arbitrary",)),
        name="ln1_router",
    )(x2, mix, mod3, g3, b3, wr_p, br_p)


def _moe_row_tiles(n_full, n_tail, tile_fn):
    def full(j, _):
        tile_fn(pl.multiple_of(j * MOE_SUB, MOE_SUB), MOE_SUB)
        return 0

    def tail(k, _):
        tile_fn(pl.multiple_of(n_full * MOE_SUB + k * MOE_TAIL, MOE_TAIL), MOE_TAIL)
        return 0

    tile_fn(0, MOE_SUB)
    lax.fori_loop(1, n_full, full, 0)
    lax.fori_loop(0, n_tail, tail, 0)


def _moe_up_kernel(item_e, item_blk, item_full, item_tail, x_ref, wg_ref, wu_ref, bg_ref, bu_ref,
                   h_ref, wg_s, wu_s):
    w = pl.program_id(0)
    n_full = item_full[w]
    n_tail = item_tail[w]

    @pl.when(n_full + n_tail > 0)
    def _():
        wg_s[...] = wg_ref[...].astype(BF16)
        wu_s[...] = wu_ref[...].astype(BF16)
        bg = bg_ref[...]
        bu = bu_ref[...]

        def tile(off, rows):
            xt = x_ref[pl.ds(off, rows), :]
            glu = jnp.dot(xt, wg_s[...], preferred_element_type=F32) + bg
            lin = jnp.dot(xt, wu_s[...], preferred_element_type=F32) + bu
            glu = jnp.minimum(glu, SWIGLU_LIMIT)
            lin = jnp.clip(lin, -SWIGLU_LIMIT, SWIGLU_LIMIT)
            act = glu * jax.nn.sigmoid(SWIGLU_ALPHA * glu) * (lin + 1.0)
            h_ref[pl.ds(off, rows), :] = act.astype(h_ref.dtype)

        _moe_row_tiles(n_full, n_tail, tile)


def _moe_up(xs, w_gate, w_up, b_gate4, b_up4, item_e, item_blk, item_full, item_tail, l, tf=256):
    r, d = xs.shape
    n_items = r // MOE_ROWS
    ff = w_gate.shape[-1]
    n_chunks = ff // tf

    def cmap(w, c, ifull, itail):
        return jnp.where(ifull[w] + itail[w] > 0, c, n_chunks - 1)

    def wmap(w, c, ie, ib, ifull, itail):
        return (l, ie[w], 0, cmap(w, c, ifull, itail))

    grid_spec = pltpu.PrefetchScalarGridSpec(
        num_scalar_prefetch=4,
        grid=(n_items, n_chunks),
        in_specs=[
            pl.BlockSpec((MOE_ROWS, d), lambda w, c, ie, ib, ifull, itail: (ib[w], 0)),
            pl.BlockSpec((None, None, d, tf), wmap),
            pl.BlockSpec((None, None, d, tf), wmap),
            pl.BlockSpec((None, None, 1, tf), wmap),
            pl.BlockSpec((None, None, 1, tf), wmap),
        ],
        out_specs=pl.BlockSpec(
            (MOE_ROWS, tf),
            lambda w, c, ie, ib, ifull, itail: (ib[w], cmap(w, c, ifull, itail))),
        scratch_shapes=[pltpu.VMEM((d, tf), BF16), pltpu.VMEM((d, tf), BF16)],
    )
    return pl.pallas_call(
        _moe_up_kernel,
        grid_spec=grid_spec,
        out_shape=jax.ShapeDtypeStruct((r, ff), BF16),
        compiler_params=_cparams(("arbitrary", "arbitrary")),
        name="moe_up",
    )(item_e, item_blk, item_full, item_tail, xs, w_gate, w_up, b_gate4, b_up4)


def _moe_down_kernel(item_e, item_blk, item_full, item_tail, h_ref, wd_ref, bd_ref, y_ref, wd_s,
                     *, sub):
    w = pl.program_id(0)
    c = pl.program_id(1)
    n_full = item_full[w]
    n_tail = item_tail[w]

    @pl.when(n_full + n_tail > 0)
    def _():
        wd_s[...] = wd_ref[...].astype(BF16)
        bd = bd_ref[...]

        def tile(off, rows):
            ht = h_ref[pl.ds(off, rows), :]
            y = jnp.dot(ht, wd_s[...], preferred_element_type=F32) + bd
            for q, words in enumerate(_pack_chunk(y)):
                y_ref[_slab_index(c * SLAB_Q + q, off, rows, sub)] = words

        _moe_row_tiles(n_full, n_tail, tile)


def _moe_down(hs, w_down, b_down4, item_e, item_blk, item_full, item_tail, l):
    r, ff = hs.shape
    n_items = r // MOE_ROWS
    d = w_down.shape[-1]
    tn = SLAB_CHUNK
    n_chunks = d // tn
    sub = _slab_sub(d)

    def cmap(w, c, ifull, itail):
        return jnp.where(ifull[w] + itail[w] > 0, c, n_chunks - 1)

    def wmap(w, c, ie, ib, ifull, itail):
        return (l, ie[w], 0, cmap(w, c, ifull, itail))

    grid_spec = pltpu.PrefetchScalarGridSpec(
        num_scalar_prefetch=4,
        grid=(n_items, n_chunks),
        in_specs=[
            pl.BlockSpec((MOE_ROWS, ff), lambda w, c, ie, ib, ifull, itail: (ib[w], 0)),
            pl.BlockSpec((None, None, ff, tn), wmap),
            pl.BlockSpec((None, None, 1, tn), wmap),
        ],
        out_specs=pl.BlockSpec((MOE_ROWS * sub, LANES),
                               lambda w, c, ie, ib, ifull, itail: (ib[w], 0)),
        scratch_shapes=[pltpu.VMEM((ff, tn), BF16)],
    )
    return pl.pallas_call(
        functools.partial(_moe_down_kernel, sub=sub),
        grid_spec=grid_spec,
        out_shape=jax.ShapeDtypeStruct((r * sub, LANES), jnp.uint32),
        compiler_params=_cparams(("arbitrary", "arbitrary")),
        name="moe_down",
    )(item_e, item_blk, item_full, item_tail, hs, w_down, b_down4)


GATHER_UNROLL = 8


def _buffer_pitch(sub):
    return sub + 8 if sub % 16 == 0 else sub


def _slab_gather(idx_ref, src_ref, buf, sem, base_slab, n, sub, wait):
    pitch = _buffer_pitch(sub)

    def copy(r):
        src_row = pl.multiple_of(idx_ref[0, r] * sub, sub)
        dst_row = pl.multiple_of((base_slab + r) * pitch, 8)
        return pltpu.make_async_copy(src_ref.at[pl.ds(src_row, sub), :],
                                     buf.at[pl.ds(dst_row, sub), :], sem)

    def body(r0, _):
        for u in range(GATHER_UNROLL):
            c = copy(r0 * GATHER_UNROLL + u)
            if wait:
                c.wait()
            else:
                c.start(priority=u % 2)
        return 0

    lax.fori_loop(0, n // GATHER_UNROLL, body, 0)


def _dispatch_kernel(vblk, nvalid, tok_ref, tok_next_ref, src_ref, o_ref, buf, sems, *, rows, d):
    s = pl.program_id(0)
    slot = s % 2
    sub = _slab_sub(d)

    @pl.when(s == 0)
    def _():
        _slab_gather(tok_ref, src_ref, buf, sems.at[0], 0, rows, sub, wait=False)

    @pl.when(s + 1 < nvalid[0])
    def _():
        _slab_gather(tok_next_ref, src_ref, buf, sems.at[1 - slot], (1 - slot) * rows, rows, sub,
                     wait=False)

    @pl.when((s < nvalid[0]) | (s == 0))
    def _():
        _slab_gather(tok_ref, src_ref, buf, sems.at[slot], slot * rows, rows, sub, wait=True)
        half = SLAB_CHUNK // 2
        for g in range(d // SLAB_CHUNK):
            for q in range(SLAB_Q):
                idx = _slab_index(g * SLAB_Q + q, slot * rows, rows, _buffer_pitch(sub))
                lo, hi = _unpack_words(buf[idx])
                c0 = g * SLAB_CHUNK + q * LANES
                o_ref[:, c0:c0 + LANES] = lo.astype(BF16)
                o_ref[:, c0 + half:c0 + half + LANES] = hi.astype(BF16)


def _dispatch(u2p, d, row_tok, vblk, nvalid, rows=MOE_BLK):
    r = row_tok.shape[0]
    n_blk = r // rows
    n_steps = vblk.shape[0]
    tok3 = row_tok.reshape(n_blk, 1, rows)
    grid_spec = pltpu.PrefetchScalarGridSpec(
        num_scalar_prefetch=2,
        grid=(n_steps,),
        in_specs=[
            pl.BlockSpec((None, 1, rows), lambda s, vb, nv: (vb[s], 0, 0), memory_space=pltpu.SMEM),
            pl.BlockSpec((None, 1, rows), lambda s, vb, nv: (vb[jnp.minimum(s + 1, n_steps - 1)], 0, 0),
                         memory_space=pltpu.SMEM),
            pl.BlockSpec(memory_space=pl.ANY),
        ],
        out_specs=pl.BlockSpec((rows, d), lambda s, vb, nv: (vb[s], 0)),
        scratch_shapes=[pltpu.VMEM((2 * rows * _buffer_pitch(_slab_sub(d)), LANES), jnp.uint32),
                        pltpu.SemaphoreType.DMA((2,))],
    )
    return pl.pallas_call(
        functools.partial(_dispatch_kernel, rows=rows, d=d),
        grid_spec=grid_spec,
        out_shape=jax.ShapeDtypeStruct((r, d), BF16),
        compiler_params=_cparams(("arbitrary",)),
        name="moe_dispatch",
    )(vblk, nvalid, tok3, tok3, u2p)


def _combine_ln2_kernel(slot_ref, slot_next_ref, x1_ref, ys_ref, gates_ref, mod_ref, g_ref, b_ref,
                        o_ref, buf, ffn_s, sems, *, alpha, tm):
    i = pl.program_id(0)
    n = pl.num_programs(0)
    cur = i % 2
    n_rows = TOP_K * tm
    d = ffn_s.shape[1]
    sub = _slab_sub(d)

    @pl.when(i == 0)
    def _():
        _slab_gather(slot_ref, ys_ref, buf, sems.at[0], 0, n_rows, sub, wait=False)

    @pl.when(i + 1 < n)
    def _():
        _slab_gather(slot_next_ref, ys_ref, buf, sems.at[1 - cur], (1 - cur) * n_rows, n_rows, sub,
                     wait=False)

    _slab_gather(slot_ref, ys_ref, buf, sems.at[cur], cur * n_rows, n_rows, sub, wait=True)

    gates = gates_ref[...]
    gk = [jnp.broadcast_to(gates[:, k:k + 1], (tm, LANES)) for k in range(TOP_K)]
    half = SLAB_CHUNK // 2
    for g in range(d // SLAB_CHUNK):
        for q in range(SLAB_Q):
            lo_acc = None
            hi_acc = None
            for k in range(TOP_K):
                idx = _slab_index(g * SLAB_Q + q, cur * n_rows + k * tm, tm, _buffer_pitch(sub))
                lo, hi = _unpack_words(buf[idx])
                lo_acc = gk[k] * lo if lo_acc is None else lo_acc + gk[k] * lo
                hi_acc = gk[k] * hi if hi_acc is None else hi_acc + gk[k] * hi
            c0 = g * SLAB_CHUNK + q * LANES
            ffn_s[:, c0:c0 + LANES] = lo_acc
            ffn_s[:, c0 + half:c0 + half + LANES] = hi_acc
    gt2 = mod_ref[5:6, :]
    z = alpha * x1_ref[...] + gt2 * ffn_s[...]
    o_ref[...] = _layer_norm_rows(z, g_ref[...], b_ref[...])


def _combine_ln2(x1, ys, slot, gates, mod3, g3, b3, l, seq, alpha, tm=256):
    t, d = x1.shape
    per_b = seq // tm
    n_tiles = t // tm
    slot3 = slot.reshape(n_tiles, tm, TOP_K).transpose(0, 2, 1).reshape(n_tiles, 1, TOP_K * tm)
    return pl.pallas_call(
        functools.partial(_combine_ln2_kernel, alpha=alpha, tm=tm),
        grid=(n_tiles,),
        in_specs=[
            pl.BlockSpec((None, 1, tm * TOP_K), lambda i: (i, 0, 0), memory_space=pltpu.SMEM),
            pl.BlockSpec((None, 1, tm * TOP_K), lambda i: (jnp.minimum(i + 1, n_tiles - 1), 0, 0),
                         memory_space=pltpu.SMEM),
            pl.BlockSpec((tm, d), lambda i: (i, 0)),
            pl.BlockSpec(memory_space=pl.ANY),
            pl.BlockSpec((tm, LANES), lambda i: (i, 0)),
            pl.BlockSpec((None, 6, d), lambda i: (i // per_b, 0, 0)),
            pl.BlockSpec((None, 1, d), lambda i: (l, 0, 0)),
            pl.BlockSpec((None, 1, d), lambda i: (l, 0, 0)),
        ],
        out_specs=pl.BlockSpec((tm, d), lambda i: (i, 0)),
        out_shape=jax.ShapeDtypeStruct((t, d), F32),
        scratch_shapes=[pltpu.VMEM((2 * TOP_K * tm * _buffer_pitch(_slab_sub(d)), LANES), jnp.uint32),
                        pltpu.VMEM((tm, d), F32),
                        pltpu.SemaphoreType.DMA((2,))],
        compiler_params=_cparams(("arbitrary",)),
        name="combine_ln2",
    )(slot3, slot3, x1, ys, gates, mod3, g3, b3)


def _t5_bucket(dist):
    is_small = dist < REL_MAX_EXACT
    nf = jnp.maximum(dist, REL_MAX_EXACT).astype(F32)
    large = REL_MAX_EXACT + (jnp.log(nf / REL_MAX_EXACT) / math.log(REL_MAX_DISTANCE / REL_MAX_EXACT)
                             * (REL_BUCKETS - REL_MAX_EXACT)).astype(jnp.int32)
    large = jnp.minimum(large, REL_BUCKETS - 1)
    return jnp.where(is_small, dist, large)


def _dilated_bias_vectors(rel_bias):
    m = jnp.arange(2 * BLOCK)
    rel = BLOCK - m
    vecs = []
    for window, d in DIL_PATTERNS:
        valid = (rel >= 0) & (rel <= window // d)
        bucket = _t5_bucket(jnp.clip(rel, 0) * d)
        bias = jnp.transpose(rel_bias[bucket].astype(F32), (1, 0))
        vecs.append(jnp.where(valid[None, :], bias, NEG_BIG))
    return jnp.stack(vecs, axis=0)[:, :, None, :]


def _routing_tables(ids4, n_items_max):
    e = ids4.reshape(-1)
    n_rows_real = e.shape[0]
    onehot =(e[:, None] == jnp.arange(N_EXPERTS, dtype=jnp.int32)[None, :]).astype(jnp.int32)
    csum = jnp.cumsum(onehot, axis=0)
    rank = jnp.sum(onehot * csum, axis=1) - 1
    counts = csum[-1]
    nit = (counts + MOE_ROWS - 1) // MOE_ROWS
    cum_items = jnp.cumsum(nit)
    first_item = cum_items - nit
    slot = first_item[e] * MOE_ROWS + rank
    total = cum_items[-1]
    w = jnp.arange(n_items_max, dtype=jnp.int32)
    e_w = jnp.minimum(jnp.searchsorted(cum_items, w, side="right"), N_EXPERTS - 1).astype(jnp.int32)
    j_w = w - first_item[e_w]
    rows_w = jnp.clip(counts[e_w] - j_w * MOE_ROWS, 0, MOE_ROWS)
    valid = w < total
    rows_w = jnp.where(valid, rows_w, 0)
    tails_per_full = MOE_SUB // MOE_TAIL
    n_full = rows_w // MOE_SUB
    n_tail = (rows_w - n_full * MOE_SUB + MOE_TAIL - 1) // MOE_TAIL
    item_full = jnp.where(n_tail == tails_per_full, n_full + 1, n_full).astype(jnp.int32)
    item_tail = jnp.where(n_tail == tails_per_full, 0, n_tail).astype(jnp.int32)
    last = total - 1
    item_e = jnp.where(valid, e_w, e_w[last]).astype(jnp.int32)
    item_blk = jnp.where(valid, w, last).astype(jnp.int32)
    per_item = MOE_ROWS // MOE_BLK
    blk = jnp.arange(n_items_max * per_item, dtype=jnp.int32)
    blk_valid = ((blk % per_item) * MOE_BLK < rows_w[blk // per_item]).astype(jnp.int32)
    cum_valid = jnp.cumsum(blk_valid)
    nvalid = cum_valid[-1]
    n_steps = n_rows_real // MOE_BLK + n_items_max
    want = jnp.minimum(jnp.arange(n_steps, dtype=jnp.int32) + 1, nvalid)
    vblk = jnp.searchsorted(cum_valid, want, side="left").astype(jnp.int32)
    return (slot.astype(jnp.int32), item_e, item_blk, item_full, item_tail, vblk,
            nvalid.astype(jnp.int32).reshape(1))


def kernel(x, c, positions, w_ada, b_ada, w_in, g_q, g_kv, w_uq, w_ukv, rel_bias, w_o,
           ln1_g, ln1_b, w_router, b_router, w_gate, b_gate, w_up, b_up, w_down, b_down,
           ln2_g, ln2_b):
    batch, seq, d = x.shape
    depth = w_ada.shape[0]
    t = batch * seq
    n_heads = d // (2 * HEAD_DIM)
    dil_w = n_heads * HEAD_DIM
    alpha = (2.0 * depth) ** 0.25
    assert seq % (BLOCK * max(dd for _, dd in DIL_PATTERNS)) == 0
    assert all(win // dd == BLOCK for win, dd in DIL_PATTERNS)

    inv_freq = 1.0 / (ROPE_THETA ** (jnp.arange(0, QK_ROPE_DIM, 2, dtype=F32) / QK_ROPE_DIM))
    ang = positions.astype(F32)[..., None] * inv_freq
    cos, sin = jnp.cos(ang).reshape(t, -1), jnp.sin(ang).reshape(t, -1)
    zpad = jnp.zeros((t, LANES - QK_ROPE_DIM), F32)
    cos_t = jnp.concatenate([cos, cos, zpad], axis=1)
    sin_t = jnp.concatenate([-sin, sin, zpad], axis=1)

    bias_vec = _dilated_bias_vectors(rel_bias)
    c_pad = jnp.zeros((16, d), F32).at[:batch].set(c)

    x2 = x.reshape(t, d)
    n_items_max = (t * TOP_K) // MOE_ROWS + N_EXPERTS
    mla_cols = Q_LORA_RANK + KV_LORA_RANK

    for l in range(depth):
        mod = _ada(c_pad, w_ada, b_ada.reshape(depth, 1, -1), l)[:batch]
        mod3 = mod.reshape(batch, 6, d)

        w_in_t = jnp.swapaxes(w_in[l], 0, 1)
        wq = w_uq[l].reshape(Q_LORA_RANK, n_heads, QK_NOPE_DIM + QK_ROPE_DIM)
        w_uq_p = jnp.pad(wq, ((0, 0), (0, 0), (0, QK_PAD - QK_NOPE_DIM - QK_ROPE_DIM))
                         ).reshape(Q_LORA_RANK, n_heads * QK_PAD).astype(BF16)
        wkv = w_ukv[l].reshape(KV_LORA_RANK, n_heads, QK_NOPE_DIM + V_HEAD_DIM)
        w_uk = wkv[:, :, :QK_NOPE_DIM].reshape(KV_LORA_RANK, -1).astype(BF16)
        w_uv = wkv[:, :, QK_NOPE_DIM:].reshape(KV_LORA_RANK, -1).astype(BF16)

        u1 = _modulate(x2, mod3, seq, 0, 1)
        hm = _in_proj_mla(u1, w_in_t, mla_cols)
        qkv_d = _in_proj_dil(u1, w_in_t, mla_cols + QK_ROPE_DIM, batch, seq, n_heads)
        krr = _in_proj_krope(u1, w_in_t, mla_cols, cos_t, sin_t)
        q_a = _q_up(hm, g_q.reshape(depth, 1, -1), l, w_uq_p, cos_t, sin_t, n_heads)
        k_a, v_a = _kv_up(hm, g_kv.reshape(depth, 1, -1), l, w_uk, w_uv, krr, n_heads)
        o_a = _mla_attn(q_a, k_a, v_a, batch, seq, n_heads)
        o_b = _dil_attn(qkv_d, bias_vec, batch, seq, n_heads)
        mix = _out_proj(o_a, o_b, w_o, l)

        wr_p = jnp.pad(w_router[l], ((0, 0), (0, LANES - N_EXPERTS)))
        br_p = jnp.concatenate([b_router[l], jnp.full((LANES - N_EXPERTS,), NEG_BIG, F32)])[None, :]
        x1, u2p, ids, gates = _ln1_router(
            x2, mix, mod3, ln1_g.reshape(depth, 1, -1), ln1_b.reshape(depth, 1, -1),
            wr_p, br_p, l, seq, alpha)

        slot, item_e, item_blk, item_full, item_tail, vblk, nvalid = _routing_tables(
            ids[:, :TOP_K], n_items_max)
        tok = jnp.arange(t * TOP_K, dtype=jnp.int32) // TOP_K
        row_tok = jnp.zeros((n_items_max * MOE_ROWS,), jnp.int32).at[slot].set(tok)
        xs = _dispatch(u2p, d, row_tok, vblk, nvalid)
        hs = _moe_up(xs, w_gate, w_up, b_gate.reshape(depth, N_EXPERTS, 1, -1),
                     b_up.reshape(depth, N_EXPERTS, 1, -1), item_e, item_blk, item_full,
                     item_tail, l)
        ys = _moe_down(hs, w_down, b_down.reshape(depth, N_EXPERTS, 1, -1),
                       item_e, item_blk, item_full, item_tail, l)
        x2 = _combine_ln2(x1, ys, slot, gates, mod3, ln2_g.reshape(depth, 1, -1),
                          ln2_b.reshape(depth, 1, -1), l, seq, alpha)

    return x2.reshape(batch, seq, d)
```

```python
import functools
import math

import jax
import jax.numpy as jnp
from jax import lax
from jax.experimental import pallas as pl
from jax.experimental.pallas import tpu as pltpu

F32 = jnp.float32
BF16 = jnp.bfloat16

HEAD_DIM = 128
Q_LORA_RANK = 1536
KV_LORA_RANK = 512
QK_NOPE_DIM = 128
QK_ROPE_DIM = 64
V_HEAD_DIM = 128
ROPE_THETA = 10000.0
DIL_PATTERNS = ((128, 1), (512, 4), (2048, 16))
BLOCK = 128
REL_BUCKETS = 32
REL_MAX_EXACT = REL_BUCKETS // 2
REL_MAX_DISTANCE = 2048
N_EXPERTS = 32
TOP_K = 4
EXPERT_FF = 1536
SWIGLU_LIMIT = 7.0
SWIGLU_ALPHA = 1.702
LN_EPS = 1e-5
RMS_EPS = 1e-6
NEG_BIG = -1e30
LOG2E = math.log2(math.e)

LANES = 128
QK_PAD = 256
VMEM_LIMIT = 56 * 1024 * 1024
MOE_ROWS = 1280
MOE_SUB = 512
MOE_TAIL = 128
MOE_BLK = 256
MOE_W_PARTS = 4


def _cparams(sem):
    return pltpu.CompilerParams(dimension_semantics=sem, vmem_limit_bytes=VMEM_LIMIT)


def _ada_kernel(c_ref, w_ref, b_ref, o_ref):
    c = c_ref[...]
    cond = (c * jax.nn.sigmoid(c)).astype(BF16)
    o_ref[...] = jnp.dot(cond, w_ref[...].astype(BF16), preferred_element_type=F32) + b_ref[...]


def _ada(c_pad, w_ada, b_ada3, l, tn=512):
    rows, d = c_pad.shape
    n = w_ada.shape[-1]
    return pl.pallas_call(
        _ada_kernel,
        grid=(n // tn,),
        in_specs=[
            pl.BlockSpec((rows, d), lambda j: (0, 0)),
            pl.BlockSpec((None, d, tn), lambda j: (l, 0, j)),
            pl.BlockSpec((None, 1, tn), lambda j: (l, 0, j)),
        ],
        out_specs=pl.BlockSpec((rows, tn), lambda j: (0, j)),
        out_shape=jax.ShapeDtypeStruct((rows, n), F32),
        compiler_params=_cparams(("arbitrary",)),
        name="ada_mod",
    )(c_pad, w_ada, b_ada3)


def _modulate_kernel(x_ref, mod_ref, o_ref, *, shift_row, scale_row):
    sh = mod_ref[shift_row:shift_row + 1, :]
    sc = mod_ref[scale_row:scale_row + 1, :]
    o_ref[...] = (x_ref[...] * (1.0 + sc) + sh).astype(BF16)


def _modulate(x2, mod3, seq, shift_row, scale_row, tm=512):
    t, d = x2.shape
    per_b = seq // tm
    return pl.pallas_call(
        functools.partial(_modulate_kernel, shift_row=shift_row, scale_row=scale_row),
        grid=(t // tm,),
        in_specs=[
            pl.BlockSpec((tm, d), lambda i: (i, 0)),
            pl.BlockSpec((None, 6, d), lambda i: (i // per_b, 0, 0)),
        ],
        out_specs=pl.BlockSpec((tm, d), lambda i: (i, 0)),
        out_shape=jax.ShapeDtypeStruct((t, d), BF16),
        compiler_params=_cparams(("arbitrary",)),
        name="modulate",
    )(x2, mod3)


def _rope_block(x, cos_t, sin_t):
    lane = lax.broadcasted_iota(jnp.int32, x.shape, 1)
    half = QK_ROPE_DIM // 2
    swapped = jnp.where(lane < half, pltpu.roll(x, LANES - half, 1), pltpu.roll(x, half, 1))
    return x * cos_t + swapped * sin_t


def _load_weight_rows(wt_ref, row0, wf_s, wb_s):
    n_rows = wf_s.shape[0]
    pltpu.sync_copy(wt_ref.at[pl.ds(pl.multiple_of(row0, 8), n_rows), :], wf_s)
    wb_s[0:n_rows, :] = wf_s[...].astype(BF16)


def _dot_nt(a, w):
    return lax.dot_general(a, w, (((1,), (1,)), ((), ())), preferred_element_type=F32)


def _proj_plain_kernel(a_ref, wt_ref, o_ref, wf_s, wb_s, *, row_base):
    @pl.when(pl.program_id(1) == 0)
    def _():
        _load_weight_rows(wt_ref, row_base + pl.program_id(0) * wf_s.shape[0], wf_s, wb_s)

    o_ref[...] = _dot_nt(a_ref[...], wb_s[...]).astype(o_ref.dtype)


def _proj_heads_kernel(a_ref, wt_ref, o_ref, wf_s, wb_s, *, row_base, heads_per_tile):
    @pl.when(pl.program_id(1) == 0)
    def _():
        _load_weight_rows(wt_ref, row_base + pl.program_id(0) * wf_s.shape[0], wf_s, wb_s)

    res = _dot_nt(a_ref[...], wb_s[...])
    for hh in range(heads_per_tile):
        o_ref[hh] = res[:, hh * HEAD_DIM:(hh + 1) * HEAD_DIM].astype(o_ref.dtype)


def _proj_rope_kernel(a_ref, wt_ref, cos_ref, sin_ref, o_ref, wf_s, wb_s, *, row_base):
    @pl.when(pl.program_id(0) == 0)
    def _():
        wb_s[...] = jnp.zeros(wb_s.shape, BF16)
        _load_weight_rows(wt_ref, row_base, wf_s, wb_s)

    res = _dot_nt(a_ref[...], wb_s[...])
    o_ref[...] = _rope_block(res, cos_ref[...], sin_ref[...]).astype(o_ref.dtype)


def _in_proj_mla(u, wt, n_cols, tm=512, tn=512):
    t, d = u.shape
    return pl.pallas_call(
        functools.partial(_proj_plain_kernel, row_base=0),
        grid=(n_cols // tn, t // tm),
        in_specs=[
            pl.BlockSpec((tm, d), lambda j, i: (i, 0)),
            pl.BlockSpec(memory_space=pl.ANY),
        ],
        out_specs=pl.BlockSpec((tm, tn), lambda j, i: (i, j)),
        out_shape=jax.ShapeDtypeStruct((t, n_cols), BF16),
        scratch_shapes=[pltpu.VMEM((tn, d), F32), pltpu.VMEM((tn, d), BF16)],
        compiler_params=_cparams(("arbitrary", "arbitrary")),
        name="in_proj_mla",
    )(u, wt)


def _in_proj_dil(u, wt, col0, batch, seq, n_heads, tm=512, tn=512):
    t, d = u.shape
    hpt = tn // HEAD_DIM
    tiles_per_mat = n_heads // hpt
    per_b = seq // tm
    return pl.pallas_call(
        functools.partial(_proj_heads_kernel, row_base=col0, heads_per_tile=hpt),
        grid=(3 * tiles_per_mat, t // tm),
        in_specs=[
            pl.BlockSpec((tm, d), lambda j, i: (i, 0)),
            pl.BlockSpec(memory_space=pl.ANY),
        ],
        out_specs=pl.BlockSpec(
            (None, None, hpt, tm, HEAD_DIM),
            lambda j, i: (j // tiles_per_mat, i // per_b, j % tiles_per_mat, i % per_b, 0)),
        out_shape=jax.ShapeDtypeStruct((3, batch, n_heads, seq, HEAD_DIM), BF16),
        scratch_shapes=[pltpu.VMEM((tn, d), F32), pltpu.VMEM((tn, d), BF16)],
        compiler_params=_cparams(("arbitrary", "arbitrary")),
        name="in_proj_dil",
    )(u, wt)


def _in_proj_krope(u, wt, col0, cos_t, sin_t, tm=1024):
    t, d = u.shape
    return pl.pallas_call(
        functools.partial(_proj_rope_kernel, row_base=col0),
        grid=(t // tm,),
        in_specs=[
            pl.BlockSpec((tm, d), lambda i: (i, 0)),
            pl.BlockSpec(memory_space=pl.ANY),
            pl.BlockSpec((tm, LANES), lambda i: (i, 0)),
            pl.BlockSpec((tm, LANES), lambda i: (i, 0)),
        ],
        out_specs=pl.BlockSpec((tm, LANES), lambda i: (i, 0)),
        out_shape=jax.ShapeDtypeStruct((t, LANES), BF16),
        scratch_shapes=[pltpu.VMEM((QK_ROPE_DIM, d), F32), pltpu.VMEM((LANES, d), BF16)],
        compiler_params=_cparams(("arbitrary",)),
        name="in_proj_krope",
    )(u, wt, cos_t, sin_t)


def _rms_bf16(h_ref, g_ref):
    hf = h_ref[...].astype(F32)
    y = hf * lax.rsqrt(jnp.mean(hf * hf, axis=-1, keepdims=True) + RMS_EPS)
    return (y * g_ref[...]).astype(BF16)


def _q_up_kernel(h_ref, g_ref, w_ref, cos_ref, sin_ref, o_ref, *, n_heads, scale):
    yb = _rms_bf16(h_ref, g_ref)
    cos_t = cos_ref[...]
    sin_t = sin_ref[...]
    group = 4
    for h0 in range(0, n_heads, group):
        res = jnp.dot(yb, w_ref[:, h0 * QK_PAD:(h0 + group) * QK_PAD], preferred_element_type=F32)
        for hh in range(group):
            c0 = hh * QK_PAD
            nope = res[:, c0:c0 + QK_NOPE_DIM]
            rp = _rope_block(res[:, c0 + QK_NOPE_DIM:c0 + QK_PAD], cos_t, sin_t)
            o0 = (h0 + hh) * QK_PAD
            o_ref[:, o0:o0 + QK_NOPE_DIM] = (nope * scale).astype(BF16)
            o_ref[:, o0 + QK_NOPE_DIM:o0 + QK_PAD] = (rp * scale).astype(BF16)


def _q_up(hm, g3, l, w_uq_p, cos_t, sin_t, n_heads, tm=256):
    t = hm.shape[0]
    scale = float(QK_NOPE_DIM + QK_ROPE_DIM) ** -0.5 * LOG2E
    return pl.pallas_call(
        functools.partial(_q_up_kernel, n_heads=n_heads, scale=scale),
        grid=(t // tm,),
        in_specs=[
            pl.BlockSpec((tm, Q_LORA_RANK), lambda i: (i, 0)),
            pl.BlockSpec((None, 1, Q_LORA_RANK), lambda i: (l, 0, 0)),
            pl.BlockSpec((Q_LORA_RANK, n_heads * QK_PAD), lambda i: (0, 0)),
            pl.BlockSpec((tm, LANES), lambda i: (i, 0)),
            pl.BlockSpec((tm, LANES), lambda i: (i, 0)),
        ],
        out_specs=pl.BlockSpec((tm, n_heads * QK_PAD), lambda i: (i, 0)),
        out_shape=jax.ShapeDtypeStruct((t, n_heads * QK_PAD), BF16),
        compiler_params=_cparams(("arbitrary",)),
        name="mla_q_up",
    )(hm, g3, w_uq_p, cos_t, sin_t)


def _kv_up_kernel(h_ref, g_ref, wk_ref, wv_ref, kr_ref, k_ref, v_ref, *, n_heads):
    yb = _rms_bf16(h_ref, g_ref)
    kr = kr_ref[...]
    group = 4
    for h0 in range(0, n_heads, group):
        res = jnp.dot(yb, wk_ref[:, h0 * QK_NOPE_DIM:(h0 + group) * QK_NOPE_DIM],
                      preferred_element_type=F32)
        for hh in range(group):
            o0 = (h0 + hh) * QK_PAD
            k_ref[:, o0:o0 + QK_NOPE_DIM] = res[:, hh * QK_NOPE_DIM:(hh + 1) * QK_NOPE_DIM].astype(BF16)
            k_ref[:, o0 + QK_NOPE_DIM:o0 + QK_PAD] = kr
    v_ref[...] = jnp.dot(yb, wv_ref[...], preferred_element_type=F32).astype(BF16)


def _kv_up(hm, g3, l, w_uk, w_uv, krr, n_heads, tm=256):
    t = hm.shape[0]
    col_blk = Q_LORA_RANK // KV_LORA_RANK
    return pl.pallas_call(
        functools.partial(_kv_up_kernel, n_heads=n_heads),
        grid=(t // tm,),
        in_specs=[
            pl.BlockSpec((tm, KV_LORA_RANK), lambda i: (i, col_blk)),
            pl.BlockSpec((None, 1, KV_LORA_RANK), lambda i: (l, 0, 0)),
            pl.BlockSpec((KV_LORA_RANK, n_heads * QK_NOPE_DIM), lambda i: (0, 0)),
            pl.BlockSpec((KV_LORA_RANK, n_heads * V_HEAD_DIM), lambda i: (0, 0)),
            pl.BlockSpec((tm, LANES), lambda i: (i, 0)),
        ],
        out_specs=[
            pl.BlockSpec((tm, n_heads * QK_PAD), lambda i: (i, 0)),
            pl.BlockSpec((tm, n_heads * V_HEAD_DIM), lambda i: (i, 0)),
        ],
        out_shape=[
            jax.ShapeDtypeStruct((t, n_heads * QK_PAD), BF16),
            jax.ShapeDtypeStruct((t, n_heads * V_HEAD_DIM), BF16),
        ],
        compiler_params=_cparams(("arbitrary",)),
        name="mla_kv_up",
    )(hm, g3, w_uk, w_uv, krr)


MLA_HEADS_PER_STEP = 2


def _mla_attn_kernel(q_ref, k_ref, v_ref, o_ref, *, seq, tq):
    nq = seq // tq
    hp = MLA_HEADS_PER_STEP
    row = lax.broadcasted_iota(jnp.int32, (tq, tq), 0)
    col = lax.broadcasted_iota(jnp.int32, (tq, tq), 1)
    causal = col <= row

    def kv_step(qs, j, carry, masked):
        off = pl.multiple_of(j * tq, tq)
        out = []
        for hh in range(hp):
            m, l, acc = carry[hh]
            k = k_ref[pl.ds(off, tq), hh * QK_PAD:(hh + 1) * QK_PAD]
            v = v_ref[pl.ds(off, tq), hh * V_HEAD_DIM:(hh + 1) * V_HEAD_DIM]
            s = lax.dot_general(qs[hh], k, (((1,), (1,)), ((), ())), preferred_element_type=F32)
            if masked:
                s = jnp.where(causal, s, NEG_BIG)
            m_new = jnp.maximum(m, jnp.max(s, axis=-1, keepdims=True))
            p = jnp.exp2(s - m_new)
            alpha = jnp.exp2(m - m_new)
            l_new = alpha * l + jnp.sum(p, axis=-1, keepdims=True)
            acc_new = alpha * acc + jnp.dot(p.astype(BF16), v, preferred_element_type=F32)
            out.append((m_new, l_new, acc_new))
        return tuple(out)

    def q_loop(i, _):
        qoff = pl.multiple_of(i * tq, tq)
        qs = [q_ref[pl.ds(qoff, tq), hh * QK_PAD:(hh + 1) * QK_PAD] for hh in range(hp)]
        init = tuple((jnp.full((tq, 1), NEG_BIG, F32), jnp.zeros((tq, 1), F32),
                      jnp.zeros((tq, V_HEAD_DIM), F32)) for _ in range(hp))
        carry = lax.fori_loop(0, i, lambda j, c: kv_step(qs, j, c, False), init)
        carry = kv_step(qs, i, carry, True)
        for hh in range(hp):
            m, l, acc = carry[hh]
            o_ref[pl.ds(qoff, tq), hh * V_HEAD_DIM:(hh + 1) * V_HEAD_DIM] = (acc / l).astype(o_ref.dtype)
        return 0

    lax.fori_loop(0, nq, q_loop, 0)


def _mla_attn(q, k, v, batch, seq, n_heads, tq=512):
    t = q.shape[0]
    hp = MLA_HEADS_PER_STEP
    return pl.pallas_call(
        functools.partial(_mla_attn_kernel, seq=seq, tq=tq),
        grid=(batch, n_heads // hp),
        in_specs=[
            pl.BlockSpec((seq, hp * QK_PAD), lambda b, h: (b, h)),
            pl.BlockSpec((seq, hp * QK_PAD), lambda b, h: (b, h)),
            pl.BlockSpec((seq, hp * V_HEAD_DIM), lambda b, h: (b, h)),
        ],
        out_specs=pl.BlockSpec((seq, hp * V_HEAD_DIM), lambda b, h: (b, h)),
        out_shape=jax.ShapeDtypeStruct((t, n_heads * V_HEAD_DIM), BF16),
        compiler_params=_cparams(("arbitrary", "arbitrary")),
        name="mla_attn",
    )(q, k, v)


DIL_GROUP = 16


def _dil_attn_kernel(qkv_ref, bvec_ref, o_ref, m_s, l_s, acc_s, bias_s, nat_f, *view_s, seq):
    scale = float(HEAD_DIM) ** -0.5 * LOG2E
    views = (qkv_ref,) + tuple(view_s)

    for bi in range(len(DIL_PATTERNS)):
        full = jnp.broadcast_to(bvec_ref[bi] * LOG2E, (BLOCK, 2 * BLOCK))
        bias_s[bi] = pltpu.roll(full, 0, 1, stride=1, stride_axis=0)

    for which in range(3):
        nat_f[...] = qkv_ref[which].astype(F32)
        for bi, (_, d) in enumerate(DIL_PATTERNS):
            if d == 1:
                continue
            for r in range(d):
                views[bi][which, :, r * HEAD_DIM:(r + 1) * HEAD_DIM] = (
                    nat_f[pl.ds(r, seq // d, stride=d), :].astype(BF16))

    def scores(ref, bi, d, r, i, first):
        lanes = slice(r * HEAD_DIM, (r + 1) * HEAD_DIM)
        qoff = pl.multiple_of(i * BLOCK, BLOCK)
        q = ref[0, pl.ds(qoff, BLOCK), lanes]
        if first:
            kk = ref[1, pl.ds(0, BLOCK), lanes]
            vv = ref[2, pl.ds(0, BLOCK), lanes]
            bias = bias_s[bi, :, BLOCK:2 * BLOCK]
        else:
            koff = pl.multiple_of(i * BLOCK - BLOCK, BLOCK)
            kk = ref[1, pl.ds(koff, 2 * BLOCK), lanes]
            vv = ref[2, pl.ds(koff, 2 * BLOCK), lanes]
            bias = bias_s[bi]
        s = lax.dot_general(q, kk, (((1,), (1,)), ((), ())), preferred_element_type=F32)
        s = s * scale + bias
        m_b = jnp.max(s, axis=-1, keepdims=True)
        p = jnp.exp2(s - m_b)
        l_b = jnp.sum(p, axis=-1, keepdims=True)
        a_b = jnp.dot(p.astype(BF16), vv, preferred_element_type=F32)
        m_b = jnp.broadcast_to(m_b, (BLOCK, HEAD_DIM))
        l_b = jnp.broadcast_to(l_b, (BLOCK, HEAD_DIM))
        if d == 1:
            rows = pl.ds(qoff, BLOCK)
        else:
            rows = pl.ds(i * (BLOCK * d) + r, BLOCK, stride=d)
        return m_b, l_b, a_b, rows

    def group(ref, bi, d, blocks):
        parts = [scores(ref, bi, d, r, i, first) for r, i, first in blocks]
        if d == 1:
            for m_b, l_b, a_b, rows in parts:
                m_s[rows, :] = m_b
                l_s[rows, :] = l_b
                acc_s[rows, :] = a_b
            return
        old = [(m_s[rows, :], l_s[rows, :], acc_s[rows, :]) for _, _, _, rows in parts]
        new = []
        for (m_b, l_b, a_b, rows), (m_o, l_o, a_o) in zip(parts, old):
            m_n = jnp.maximum(m_o, m_b)
            e_o = jnp.exp2(m_o - m_n)
            e_b = jnp.exp2(m_b - m_n)
            new.append((m_n, e_o * l_o + e_b * l_b, e_o * a_o + e_b * a_b, rows))
        for m_n, l_n, a_n, rows in new:
            m_s[rows, :] = m_n
            l_s[rows, :] = l_n
            acc_s[rows, :] = a_n

    for bi, (_, d) in enumerate(DIL_PATTERNS):
        ref = views[bi]
        nblk = seq // d // BLOCK
        n_res = min(d, DIL_GROUP)
        n_seq = DIL_GROUP // n_res
        assert d % n_res == 0 and nblk % n_seq == 0
        for r0 in range(0, d, n_res):
            group(ref, bi, d, [(r0 + u, i, i == 0) for i in range(n_seq) for u in range(n_res)])

            def body(t, _, ref=ref, bi=bi, d=d, r0=r0, n_res=n_res, n_seq=n_seq):
                group(ref, bi, d, [(r0 + u, t * n_seq + ii, False)
                                   for ii in range(n_seq) for u in range(n_res)])
                return 0

            lax.fori_loop(1, nblk // n_seq, body, 0)

    o_ref[...] = (acc_s[...] / l_s[...]).astype(o_ref.dtype)


def _dil_attn(qkv, bias_vec, batch, seq, n_heads):
    t = batch * seq
    nbr = len(DIL_PATTERNS)
    assert DIL_PATTERNS[0][1] == 1
    view_scratch = [pltpu.VMEM((3, seq // d, d * HEAD_DIM), BF16) for _, d in DIL_PATTERNS[1:]]
    return pl.pallas_call(
        functools.partial(_dil_attn_kernel, seq=seq),
        grid=(batch, n_heads),
        in_specs=[
            pl.BlockSpec((3, None, None, seq, HEAD_DIM), lambda b, h: (0, b, h, 0, 0)),
            pl.BlockSpec((nbr, None, 1, 2 * BLOCK), lambda b, h: (0, h, 0, 0)),
        ],
        out_specs=pl.BlockSpec((seq, HEAD_DIM), lambda b, h: (b, h)),
        out_shape=jax.ShapeDtypeStruct((t, n_heads * HEAD_DIM), BF16),
        scratch_shapes=[pltpu.VMEM((seq, HEAD_DIM), F32)] * 3
        + [pltpu.VMEM((nbr, BLOCK, 2 * BLOCK), F32), pltpu.VMEM((seq, HEAD_DIM), F32)]
        + view_scratch,
        compiler_params=_cparams(("arbitrary", "arbitrary")),
        name="dil_attn",
    )(qkv, bias_vec)


def _out_proj_kernel(a1_ref, a2_ref, w_ref, o_ref, wb_ref):
    @pl.when(pl.program_id(1) == 0)
    def _():
        wb_ref[...] = w_ref[...].astype(BF16)

    k1 = a1_ref.shape[1]
    acc = jnp.dot(a1_ref[...], wb_ref[:k1, :], preferred_element_type=F32)
    acc = acc + jnp.dot(a2_ref[...], wb_ref[k1:, :], preferred_element_type=F32)
    o_ref[...] = acc.astype(o_ref.dtype)


def _out_proj(o_a, o_b, w_o, l, tm=512, tn=512):
    t, k1 = o_a.shape
    k2 = o_b.shape[1]
    n = w_o.shape[-1]
    return pl.pallas_call(
        _out_proj_kernel,
        grid=(n // tn, t // tm),
        in_specs=[
            pl.BlockSpec((tm, k1), lambda j, i: (i, 0)),
            pl.BlockSpec((tm, k2), lambda j, i: (i, 0)),
            pl.BlockSpec((None, k1 + k2, tn), lambda j, i: (l, 0, j)),
        ],
        out_specs=pl.BlockSpec((tm, tn), lambda j, i: (i, j)),
        out_shape=jax.ShapeDtypeStruct((t, n), BF16),
        scratch_shapes=[pltpu.VMEM((k1 + k2, tn), BF16)],
        compiler_params=_cparams(("arbitrary", "arbitrary")),
        name="out_proj",
    )(o_a, o_b, w_o)


def _layer_norm_rows(z, g, b):
    mu = jnp.mean(z, axis=-1, keepdims=True)
    zc = z - mu
    var = jnp.mean(zc * zc, axis=-1, keepdims=True)
    return zc * lax.rsqrt(var + LN_EPS) * g + b


SLAB_CHUNK = 1024
SLAB_Q = SLAB_CHUNK // (2 * LANES)


def _slab_sub(width):
    return width // (2 * LANES)


def _slab_index(j, first_tok, n_tok, sub):
    return (pl.ds(first_tok * sub + j, n_tok, stride=sub), slice(None))


def _pack_chunk(vals):
    bits = lax.bitcast_convert_type(vals.astype(BF16).astype(F32), jnp.uint32)
    half = SLAB_CHUNK // 2
    return [(bits[:, q * LANES:(q + 1) * LANES] >> 16)
            | bits[:, half + q * LANES:half + (q + 1) * LANES] for q in range(SLAB_Q)]


def _unpack_words(words):
    lo = lax.bitcast_convert_type(words << 16, F32)
    hi = lax.bitcast_convert_type(words & jnp.uint32(0xFFFF0000), F32)
    return lo, hi


def _ln1_router_kernel(x_ref, mix_ref, mod_ref, g_ref, b_ref, wr_ref, br_ref,
                       x1_ref, u2p_ref, ids_ref, gates_ref, *, alpha):
    gt1 = mod_ref[2:3, :]
    sh2 = mod_ref[3:4, :]
    sc2 = mod_ref[4:5, :]
    z = alpha * x_ref[...] + gt1 * mix_ref[...].astype(F32)
    x1 = _layer_norm_rows(z, g_ref[...], b_ref[...])
    x1_ref[...] = x1
    u2 = (x1 * (1.0 + sc2) + sh2).astype(BF16)
    tm, d = u2.shape
    for g in range(d // SLAB_CHUNK):
        for q, words in enumerate(_pack_chunk(u2[:, g * SLAB_CHUNK:(g + 1) * SLAB_CHUNK])):
            u2p_ref[_slab_index(g * SLAB_Q + q, 0, tm, _slab_sub(d))] = words
    logits = jnp.dot(u2, wr_ref[...].astype(BF16), preferred_element_type=F32) + br_ref[...]
    lane = lax.broadcasted_iota(jnp.int32, logits.shape, 1)
    lane_f = lane.astype(F32)
    vals = []
    ids = []
    for _ in range(TOP_K):
        mk = jnp.max(logits, axis=-1, keepdims=True)
        idx_f = jnp.min(jnp.where(logits == mk, lane_f, float(LANES)), axis=-1, keepdims=True)
        vals.append(mk)
        ids.append(idx_f.astype(jnp.int32))
        logits = jnp.where(lane_f == idx_f, -jnp.inf, logits)
    exps = [jnp.exp(v - vals[0]) for v in vals]
    denom = exps[0]
    for e in exps[1:]:
        denom = denom + e
    ids_out = jnp.zeros(lane.shape, jnp.int32)
    gates_out = jnp.zeros(lane.shape, F32)
    for k in range(TOP_K):
        ids_out = jnp.where(lane == k, ids[k], ids_out)
        gates_out = jnp.where(lane == k, exps[k] / denom, gates_out)
    ids_ref[...] = ids_out
    gates_ref[...] = gates_out


def _ln1_router(x2, mix, mod3, g3, b3, wr_p, br_p, l, seq, alpha, tm=256):
    t, d = x2.shape
    per_b = seq // tm
    return pl.pallas_call(
        functools.partial(_ln1_router_kernel, alpha=alpha),
        grid=(t // tm,),
        in_specs=[
            pl.BlockSpec((tm, d), lambda i: (i, 0)),
            pl.BlockSpec((tm, d), lambda i: (i, 0)),
            pl.BlockSpec((None, 6, d), lambda i: (i // per_b, 0, 0)),
            pl.BlockSpec((None, 1, d), lambda i: (l, 0, 0)),
            pl.BlockSpec((None, 1, d), lambda i: (l, 0, 0)),
            pl.BlockSpec((d, LANES), lambda i: (0, 0)),
            pl.BlockSpec((1, LANES), lambda i: (0, 0)),
        ],
        out_specs=[
            pl.BlockSpec((tm, d), lambda i: (i, 0)),
            pl.BlockSpec((tm * _slab_sub(d), LANES), lambda i: (i, 0)),
            pl.BlockSpec((tm, LANES), lambda i: (i, 0)),
            pl.BlockSpec((tm, LANES), lambda i: (i, 0)),
        ],
        out_shape=[
            jax.ShapeDtypeStruct((t, d), F32),
            jax.ShapeDtypeStruct((t * _slab_sub(d), LANES), jnp.uint32),
            jax.ShapeDtypeStruct((t, LANES), jnp.int32),
            jax.ShapeDtypeStruct((t, LANES), F32),
        ],
        compiler_params=_cparams(("arbitrary",)),
        name="ln1_router",
    )(x2, mix, mod3, g3, b3, wr_p, br_p)


def _moe_row_tiles(n_full, n_tail, tile_fn):
    def full(j, _):
        tile_fn(pl.multiple_of(j * MOE_SUB, MOE_SUB), MOE_SUB)
        return 0

    def tail(k, _):
        tile_fn(pl.multiple_of(n_full * MOE_SUB + k * MOE_TAIL, MOE_TAIL), MOE_TAIL)
        return 0

    tile_fn(0, MOE_SUB)
    lax.fori_loop(1, n_full, full, 0)
    lax.fori_loop(0, n_tail, tail, 0)


def _cast_parts(part_refs, dst_ref):
    kp = dst_ref.shape[0] // len(part_refs)
    for p, ref in enumerate(part_refs):
        dst_ref[p * kp:(p + 1) * kp, :] = ref[...].astype(BF16)


def _moe_up_kernel(item_e, item_blk, item_full, item_tail, x_ref, *rest, n_parts):
    w_refs = rest[:2 * n_parts]
    bg_ref, bu_ref, h_ref, wg_s, wu_s = rest[2 * n_parts:]
    w = pl.program_id(0)
    n_full = item_full[w]
    n_tail = item_tail[w]

    @pl.when(n_full + n_tail > 0)
    def _():
        _cast_parts(w_refs[:n_parts], wg_s)
        _cast_parts(w_refs[n_parts:], wu_s)
        bg = bg_ref[...]
        bu = bu_ref[...]

        def tile(off, rows):
            xt = x_ref[pl.ds(off, rows), :]
            glu = jnp.dot(xt, wg_s[...], preferred_element_type=F32) + bg
            lin = jnp.dot(xt, wu_s[...], preferred_element_type=F32) + bu
            glu = jnp.minimum(glu, SWIGLU_LIMIT)
            lin = jnp.clip(lin, -SWIGLU_LIMIT, SWIGLU_LIMIT)
            act = glu * jax.nn.sigmoid(SWIGLU_ALPHA * glu) * (lin + 1.0)
            h_ref[pl.ds(off, rows), :] = act.astype(h_ref.dtype)

        _moe_row_tiles(n_full, n_tail, tile)


def _moe_up(xs, w_gate, w_up, b_gate4, b_up4, item_e, item_blk, item_full, item_tail, l, tf=256):
    r, d = xs.shape
    n_items = r // MOE_ROWS
    ff = w_gate.shape[-1]
    n_chunks = ff // tf

    def cmap(w, c, ifull, itail):
        return jnp.where(ifull[w] + itail[w] > 0, c, n_chunks - 1)

    def wmap(w, c, ie, ib, ifull, itail):
        return (l, ie[w], 0, cmap(w, c, ifull, itail))

    n_parts = MOE_W_PARTS
    kp = d // n_parts

    def part_spec(p):
        return pl.BlockSpec((None, None, kp, tf),
                            lambda w, c, ie, ib, ifull, itail: (l, ie[w], p, cmap(w, c, ifull, itail)))

    grid_spec = pltpu.PrefetchScalarGridSpec(
        num_scalar_prefetch=4,
        grid=(n_items, n_chunks),
        in_specs=[pl.BlockSpec((MOE_ROWS, d), lambda w, c, ie, ib, ifull, itail: (ib[w], 0))]
        + [part_spec(p) for p in range(n_parts)] * 2
        + [pl.BlockSpec((None, None, 1, tf), wmap), pl.BlockSpec((None, None, 1, tf), wmap)],
        out_specs=pl.BlockSpec(
            (MOE_ROWS, tf),
            lambda w, c, ie, ib, ifull, itail: (ib[w], cmap(w, c, ifull, itail))),
        scratch_shapes=[pltpu.VMEM((d, tf), BF16), pltpu.VMEM((d, tf), BF16)],
    )
    return pl.pallas_call(
        functools.partial(_moe_up_kernel, n_parts=n_parts),
        grid_spec=grid_spec,
        out_shape=jax.ShapeDtypeStruct((r, ff), BF16),
        compiler_params=_cparams(("arbitrary", "arbitrary")),
        name="moe_up",
    )(item_e, item_blk, item_full, item_tail, xs, *([w_gate] * n_parts), *([w_up] * n_parts),
      b_gate4, b_up4)


def _moe_down_kernel(item_e, item_blk, item_full, item_tail, h_ref, *rest, sub, n_parts):
    w_refs = rest[:n_parts]
    bd_ref, y_ref, wd_s = rest[n_parts:]
    w = pl.program_id(0)
    c = pl.program_id(1)
    n_full = item_full[w]
    n_tail = item_tail[w]

    @pl.when(n_full + n_tail > 0)
    def _():
        _cast_parts(w_refs, wd_s)
        bd = bd_ref[...]

        def tile(off, rows):
            ht = h_ref[pl.ds(off, rows), :]
            y = jnp.dot(ht, wd_s[...], preferred_element_type=F32) + bd
            for q, words in enumerate(_pack_chunk(y)):
                y_ref[_slab_index(c * SLAB_Q + q, off, rows, sub)] = words

        _moe_row_tiles(n_full, n_tail, tile)


def _moe_down(hs, w_down, b_down4, item_e, item_blk, item_full, item_tail, l):
    r, ff = hs.shape
    n_items = r // MOE_ROWS
    d = w_down.shape[-1]
    tn = SLAB_CHUNK
    n_chunks = d // tn
    sub = _slab_sub(d)

    def cmap(w, c, ifull, itail):
        return jnp.where(ifull[w] + itail[w] > 0, c, n_chunks - 1)

    def wmap(w, c, ie, ib, ifull, itail):
        return (l, ie[w], 0, cmap(w, c, ifull, itail))

    n_parts = MOE_W_PARTS
    kp = ff // n_parts

    def part_spec(p):
        return pl.BlockSpec((None, None, kp, tn),
                            lambda w, c, ie, ib, ifull, itail: (l, ie[w], p, cmap(w, c, ifull, itail)))

    grid_spec = pltpu.PrefetchScalarGridSpec(
        num_scalar_prefetch=4,
        grid=(n_items, n_chunks),
        in_specs=[pl.BlockSpec((MOE_ROWS, ff), lambda w, c, ie, ib, ifull, itail: (ib[w], 0))]
        + [part_spec(p) for p in range(n_parts)]
        + [pl.BlockSpec((None, None, 1, tn), wmap)],
        out_specs=pl.BlockSpec((MOE_ROWS * sub, LANES),
                               lambda w, c, ie, ib, ifull, itail: (ib[w], 0)),
        scratch_shapes=[pltpu.VMEM((ff, tn), BF16)],
    )
    return pl.pallas_call(
        functools.partial(_moe_down_kernel, sub=sub, n_parts=n_parts),
        grid_spec=grid_spec,
        out_shape=jax.ShapeDtypeStruct((r * sub, LANES), jnp.uint32),
        compiler_params=_cparams(("arbitrary", "arbitrary")),
        name="moe_down",
    )(item_e, item_blk, item_full, item_tail, hs, *([w_down] * n_parts), b_down4)


GATHER_UNROLL = 8


def _buffer_pitch(sub):
    return sub + 8 if sub % 16 == 0 else sub


def _slab_gather(idx_ref, src_ref, buf, sem, base_slab, n, sub, wait):
    pitch = _buffer_pitch(sub)

    def copy(r):
        src_row = pl.multiple_of(idx_ref[0, r] * sub, sub)
        dst_row = pl.multiple_of((base_slab + r) * pitch, 8)
        return pltpu.make_async_copy(src_ref.at[pl.ds(src_row, sub), :],
                                     buf.at[pl.ds(dst_row, sub), :], sem)

    def body(r0, _):
        for u in range(GATHER_UNROLL):
            c = copy(r0 * GATHER_UNROLL + u)
            if wait:
                c.wait()
            else:
                c.start(priority=u % 2)
        return 0

    lax.fori_loop(0, n // GATHER_UNROLL, body, 0)


def _dispatch_kernel(vblk, nvalid, tok_ref, tok_next_ref, src_ref, o_ref, buf, sems, *, rows, d):
    s = pl.program_id(0)
    slot = s % 2
    sub = _slab_sub(d)

    @pl.when(s == 0)
    def _():
        _slab_gather(tok_ref, src_ref, buf, sems.at[0], 0, rows, sub, wait=False)

    @pl.when(s + 1 < nvalid[0])
    def _():
        _slab_gather(tok_next_ref, src_ref, buf, sems.at[1 - slot], (1 - slot) * rows, rows, sub,
                     wait=False)

    @pl.when((s < nvalid[0]) | (s == 0))
    def _():
        _slab_gather(tok_ref, src_ref, buf, sems.at[slot], slot * rows, rows, sub, wait=True)
        half = SLAB_CHUNK // 2
        for g in range(d // SLAB_CHUNK):
            for q in range(SLAB_Q):
                idx = _slab_index(g * SLAB_Q + q, slot * rows, rows, _buffer_pitch(sub))
                lo, hi = _unpack_words(buf[idx])
                c0 = g * SLAB_CHUNK + q * LANES
                o_ref[:, c0:c0 + LANES] = lo.astype(BF16)
                o_ref[:, c0 + half:c0 + half + LANES] = hi.astype(BF16)


def _dispatch(u2p, d, row_tok, vblk, nvalid, rows=MOE_BLK):
    r = row_tok.shape[0]
    n_blk = r // rows
    n_steps = vblk.shape[0]
    tok3 = row_tok.reshape(n_blk, 1, rows)
    grid_spec = pltpu.PrefetchScalarGridSpec(
        num_scalar_prefetch=2,
        grid=(n_steps,),
        in_specs=[
            pl.BlockSpec((None, 1, rows), lambda s, vb, nv: (vb[s], 0, 0), memory_space=pltpu.SMEM),
            pl.BlockSpec((None, 1, rows), lambda s, vb, nv: (vb[jnp.minimum(s + 1, n_steps - 1)], 0, 0),
                         memory_space=pltpu.SMEM),
            pl.BlockSpec(memory_space=pl.ANY),
        ],
        out_specs=pl.BlockSpec((rows, d), lambda s, vb, nv: (vb[s], 0)),
        scratch_shapes=[pltpu.VMEM((2 * rows * _buffer_pitch(_slab_sub(d)), LANES), jnp.uint32),
                        pltpu.SemaphoreType.DMA((2,))],
    )
    return pl.pallas_call(
        functools.partial(_dispatch_kernel, rows=rows, d=d),
        grid_spec=grid_spec,
        out_shape=jax.ShapeDtypeStruct((r, d), BF16),
        compiler_params=_cparams(("arbitrary",)),
        name="moe_dispatch",
    )(vblk, nvalid, tok3, tok3, u2p)


def _combine_ln2_kernel(slot_ref, slot_next_ref, x1_ref, ys_ref, gates_ref, mod_ref, g_ref, b_ref,
                        o_ref, buf, ffn_s, sems, *, alpha, tm):
    i = pl.program_id(0)
    n = pl.num_programs(0)
    cur = i % 2
    n_rows = TOP_K * tm
    d = ffn_s.shape[1]
    sub = _slab_sub(d)

    @pl.when(i == 0)
    def _():
        _slab_gather(slot_ref, ys_ref, buf, sems.at[0], 0, n_rows, sub, wait=False)

    @pl.when(i + 1 < n)
    def _():
        _slab_gather(slot_next_ref, ys_ref, buf, sems.at[1 - cur], (1 - cur) * n_rows, n_rows, sub,
                     wait=False)

    _slab_gather(slot_ref, ys_ref, buf, sems.at[cur], cur * n_rows, n_rows, sub, wait=True)

    gates = gates_ref[...]
    gk = [jnp.broadcast_to(gates[:, k:k + 1], (tm, LANES)) for k in range(TOP_K)]
    half = SLAB_CHUNK // 2
    for g in range(d // SLAB_CHUNK):
        for q in range(SLAB_Q):
            lo_acc = None
            hi_acc = None
            for k in range(TOP_K):
                idx = _slab_index(g * SLAB_Q + q, cur * n_rows + k * tm, tm, _buffer_pitch(sub))
                lo, hi = _unpack_words(buf[idx])
                lo_acc = gk[k] * lo if lo_acc is None else lo_acc + gk[k] * lo
                hi_acc = gk[k] * hi if hi_acc is None else hi_acc + gk[k] * hi
            c0 = g * SLAB_CHUNK + q * LANES
            ffn_s[:, c0:c0 + LANES] = lo_acc
            ffn_s[:, c0 + half:c0 + half + LANES] = hi_acc
    gt2 = mod_ref[5:6, :]
    z = alpha * x1_ref[...] + gt2 * ffn_s[...]
    o_ref[...] = _layer_norm_rows(z, g_ref[...], b_ref[...])


def _combine_ln2(x1, ys, slot, gates, mod3, g3, b3, l, seq, alpha, tm=256):
    t, d = x1.shape
    per_b = seq // tm
    n_tiles = t // tm
    slot3 = slot.reshape(n_tiles, tm, TOP_K).transpose(0, 2, 1).reshape(n_tiles, 1, TOP_K * tm)
    return pl.pallas_call(
        functools.partial(_combine_ln2_kernel, alpha=alpha, tm=tm),
        grid=(n_tiles,),
        in_specs=[
            pl.BlockSpec((None, 1, tm * TOP_K), lambda i: (i, 0, 0), memory_space=pltpu.SMEM),
            pl.BlockSpec((None, 1, tm * TOP_K), lambda i: (jnp.minimum(i + 1, n_tiles - 1), 0, 0),
                         memory_space=pltpu.SMEM),
            pl.BlockSpec((tm, d), lambda i: (i, 0)),
            pl.BlockSpec(memory_space=pl.ANY),
            pl.BlockSpec((tm, LANES), lambda i: (i, 0)),
            pl.BlockSpec((None, 6, d), lambda i: (i // per_b, 0, 0)),
            pl.BlockSpec((None, 1, d), lambda i: (l, 0, 0)),
            pl.BlockSpec((None, 1, d), lambda i: (l, 0, 0)),
        ],
        out_specs=pl.BlockSpec((tm, d), lambda i: (i, 0)),
        out_shape=jax.ShapeDtypeStruct((t, d), F32),
        scratch_shapes=[pltpu.VMEM((2 * TOP_K * tm * _buffer_pitch(_slab_sub(d)), LANES), jnp.uint32),
                        pltpu.VMEM((tm, d), F32),
                        pltpu.SemaphoreType.DMA((2,))],
        compiler_params=_cparams(("arbitrary",)),
        name="combine_ln2",
    )(slot3, slot3, x1, ys, gates, mod3, g3, b3)


def _t5_bucket(dist):
    is_small = dist < REL_MAX_EXACT
    nf = jnp.maximum(dist, REL_MAX_EXACT).astype(F32)
    large = REL_MAX_EXACT + (jnp.log(nf / REL_MAX_EXACT) / math.log(REL_MAX_DISTANCE / REL_MAX_EXACT)
                             * (REL_BUCKETS - REL_MAX_EXACT)).astype(jnp.int32)
    large = jnp.minimum(large, REL_BUCKETS - 1)
    return jnp.where(is_small, dist, large)


def _dilated_bias_vectors(rel_bias):
    m = jnp.arange(2 * BLOCK)
    rel = BLOCK - m
    vecs = []
    for window, d in DIL_PATTERNS:
        valid = (rel >= 0) & (rel <= window // d)
        bucket = _t5_bucket(jnp.clip(rel, 0) * d)
        bias = jnp.transpose(rel_bias[bucket].astype(F32), (1, 0))
        vecs.append(jnp.where(valid[None, :], bias, NEG_BIG))
    return jnp.stack(vecs, axis=0)[:, :, None, :]


def _routing_tables(ids4, n_items_max):
    e = ids4.reshape(-1)
    n_rows_real = e.shape[0]
    onehot =(e[:, None] == jnp.arange(N_EXPERTS, dtype=jnp.int32)[None, :]).astype(jnp.int32)
    csum = jnp.cumsum(onehot, axis=0)
    rank = jnp.sum(onehot * csum, axis=1) - 1
    counts = csum[-1]
    nit = (counts + MOE_ROWS - 1) // MOE_ROWS
    cum_items = jnp.cumsum(nit)
    first_item = cum_items - nit
    slot = first_item[e] * MOE_ROWS + rank
    total = cum_items[-1]
    w = jnp.arange(n_items_max, dtype=jnp.int32)
    e_w = jnp.minimum(jnp.searchsorted(cum_items, w, side="right"), N_EXPERTS - 1).astype(jnp.int32)
    j_w = w - first_item[e_w]
    rows_w = jnp.clip(counts[e_w] - j_w * MOE_ROWS, 0, MOE_ROWS)
    valid = w < total
    rows_w = jnp.where(valid, rows_w, 0)
    tails_per_full = MOE_SUB // MOE_TAIL
    n_full = rows_w // MOE_SUB
    n_tail = (rows_w - n_full * MOE_SUB + MOE_TAIL - 1) // MOE_TAIL
    item_full = jnp.where(n_tail == tails_per_full, n_full + 1, n_full).astype(jnp.int32)
    item_tail = jnp.where(n_tail == tails_per_full, 0, n_tail).astype(jnp.int32)
    last = total - 1
    item_e = jnp.where(valid, e_w, e_w[last]).astype(jnp.int32)
    item_blk = jnp.where(valid, w, last).astype(jnp.int32)
    per_item = MOE_ROWS // MOE_BLK
    blk = jnp.arange(n_items_max * per_item, dtype=jnp.int32)
    blk_valid = ((blk % per_item) * MOE_BLK < rows_w[blk // per_item]).astype(jnp.int32)
    cum_valid = jnp.cumsum(blk_valid)
    nvalid = cum_valid[-1]
    n_steps = n_rows_real // MOE_BLK + n_items_max
    want = jnp.minimum(jnp.arange(n_steps, dtype=jnp.int32) + 1, nvalid)
    vblk = jnp.searchsorted(cum_valid, want, side="left").astype(jnp.int32)
    return (slot.astype(jnp.int32), item_e, item_blk, item_full, item_tail, vblk,
            nvalid.astype(jnp.int32).reshape(1))


def kernel(x, c, positions, w_ada, b_ada, w_in, g_q, g_kv, w_uq, w_ukv, rel_bias, w_o,
           ln1_g, ln1_b, w_router, b_router, w_gate, b_gate, w_up, b_up, w_down, b_down,
           ln2_g, ln2_b):
    batch, seq, d = x.shape
    depth = w_ada.shape[0]
    t = batch * seq
    n_heads = d // (2 * HEAD_DIM)
    dil_w = n_heads * HEAD_DIM
    alpha = (2.0 * depth) ** 0.25
    assert seq % (BLOCK * max(dd for _, dd in DIL_PATTERNS)) == 0
    assert all(win // dd == BLOCK for win, dd in DIL_PATTERNS)

    inv_freq = 1.0 / (ROPE_THETA ** (jnp.arange(0, QK_ROPE_DIM, 2, dtype=F32) / QK_ROPE_DIM))
    ang = positions.astype(F32)[..., None] * inv_freq
    cos, sin = jnp.cos(ang).reshape(t, -1), jnp.sin(ang).reshape(t, -1)
    zpad = jnp.zeros((t, LANES - QK_ROPE_DIM), F32)
    cos_t = jnp.concatenate([cos, cos, zpad], axis=1)
    sin_t = jnp.concatenate([-sin, sin, zpad], axis=1)

    bias_vec = _dilated_bias_vectors(rel_bias)
    c_pad = jnp.zeros((16, d), F32).at[:batch].set(c)

    x2 = x.reshape(t, d)
    n_items_max = (t * TOP_K) // MOE_ROWS + N_EXPERTS
    mla_cols = Q_LORA_RANK + KV_LORA_RANK

    for l in range(depth):
        mod = _ada(c_pad, w_ada, b_ada.reshape(depth, 1, -1), l)[:batch]
        mod3 = mod.reshape(batch, 6, d)

        w_in_t = jnp.swapaxes(w_in[l], 0, 1)
        wq = w_uq[l].reshape(Q_LORA_RANK, n_heads, QK_NOPE_DIM + QK_ROPE_DIM)
        w_uq_p = jnp.pad(wq, ((0, 0), (0, 0), (0, QK_PAD - QK_NOPE_DIM - QK_ROPE_DIM))
                         ).reshape(Q_LORA_RANK, n_heads * QK_PAD).astype(BF16)
        wkv = w_ukv[l].reshape(KV_LORA_RANK, n_heads, QK_NOPE_DIM + V_HEAD_DIM)
        w_uk = wkv[:, :, :QK_NOPE_DIM].reshape(KV_LORA_RANK, -1).astype(BF16)
        w_uv = wkv[:, :, QK_NOPE_DIM:].reshape(KV_LORA_RANK, -1).astype(BF16)

        u1 = _modulate(x2, mod3, seq, 0, 1)
        hm = _in_proj_mla(u1, w_in_t, mla_cols)
        qkv_d = _in_proj_dil(u1, w_in_t, mla_cols + QK_ROPE_DIM, batch, seq, n_heads)
        krr = _in_proj_krope(u1, w_in_t, mla_cols, cos_t, sin_t)
        q_a = _q_up(hm, g_q.reshape(depth, 1, -1), l, w_uq_p, cos_t, sin_t, n_heads)
        k_a, v_a = _kv_up(hm, g_kv.reshape(depth, 1, -1), l, w_uk, w_uv, krr, n_heads)
        o_a = _mla_attn(q_a, k_a, v_a, batch, seq, n_heads)
        o_b = _dil_attn(qkv_d, bias_vec, batch, seq, n_heads)
        mix = _out_proj(o_a, o_b, w_o, l)

        wr_p = jnp.pad(w_router[l], ((0, 0), (0, LANES - N_EXPERTS)))
        br_p = jnp.concatenate([b_router[l], jnp.full((LANES - N_EXPERTS,), NEG_BIG, F32)])[None, :]
        x1, u2p, ids, gates = _ln1_router(
            x2, mix, mod3, ln1_g.reshape(depth, 1, -1), ln1_b.reshape(depth, 1, -1),
            wr_p, br_p, l, seq, alpha)

        slot, item_e, item_blk, item_full, item_tail, vblk, nvalid = _routing_tables(
            ids[:, :TOP_K], n_items_max)
        tok = jnp.arange(t * TOP_K, dtype=jnp.int32) // TOP_K
        row_tok = jnp.zeros((n_items_max * MOE_ROWS,), jnp.int32).at[slot].set(tok)
        xs = _dispatch(u2p, d, row_tok, vblk, nvalid)
        hs = _moe_up(xs, w_gate, w_up, b_gate.reshape(depth, N_EXPERTS, 1, -1),
                     b_up.reshape(depth, N_EXPERTS, 1, -1), item_e, item_blk, item_full,
                     item_tail, l)
        ys = _moe_down(hs, w_down, b_down.reshape(depth, N_EXPERTS, 1, -1),
                       item_e, item_blk, item_full, item_tail, l)
        x2 = _combine_ln2(x1, ys, slot, gates, mod3, ln2_g.reshape(depth, 1, -1),
                          ln2_b.reshape(depth, 1, -1), l, seq, alpha)

    return x2.reshape(batch, seq, d)
```

```python
import functools
import math

import jax
import jax.numpy as jnp
from jax import lax
from jax.experimental import pallas as pl
from jax.experimental.pallas import tpu as pltpu

F32 = jnp.float32
BF16 = jnp.bfloat16

HEAD_DIM = 128
Q_LORA_RANK = 1536
KV_LORA_RANK = 512
QK_NOPE_DIM = 128
QK_ROPE_DIM = 64
V_HEAD_DIM = 128
ROPE_THETA = 10000.0
DIL_PATTERNS = ((128, 1), (512, 4), (2048, 16))
BLOCK = 128
REL_BUCKETS = 32
REL_MAX_EXACT = REL_BUCKETS // 2
REL_MAX_DISTANCE = 2048
N_EXPERTS = 32
TOP_K = 4
EXPERT_FF = 1536
SWIGLU_LIMIT = 7.0
SWIGLU_ALPHA = 1.702
LN_EPS = 1e-5
RMS_EPS = 1e-6
NEG_BIG = -1e30
LOG2E = math.log2(math.e)

LANES = 128
QK_PAD = 256
VMEM_LIMIT = 56 * 1024 * 1024
MOE_ROWS = 1280
MOE_SUB = 512
MOE_TAIL = 128
MOE_BLK = 256
MOE_W_PARTS = 1


def _cparams(sem):
    return pltpu.CompilerParams(dimension_semantics=sem, vmem_limit_bytes=VMEM_LIMIT)


def _ada_kernel(c_ref, w_ref, b_ref, o_ref):
    c = c_ref[...]
    cond = (c * jax.nn.sigmoid(c)).astype(BF16)
    o_ref[...] = jnp.dot(cond, w_ref[...].astype(BF16), preferred_element_type=F32) + b_ref[...]


def _ada(c_pad, w_ada, b_ada3, l, tn=512):
    rows, d = c_pad.shape
    n = w_ada.shape[-1]
    return pl.pallas_call(
        _ada_kernel,
        grid=(n // tn,),
        in_specs=[
            pl.BlockSpec((rows, d), lambda j: (0, 0)),
            pl.BlockSpec((None, d, tn), lambda j: (l, 0, j)),
            pl.BlockSpec((None, 1, tn), lambda j: (l, 0, j)),
        ],
        out_specs=pl.BlockSpec((rows, tn), lambda j: (0, j)),
        out_shape=jax.ShapeDtypeStruct((rows, n), F32),
        compiler_params=_cparams(("arbitrary",)),
        name="ada_mod",
    )(c_pad, w_ada, b_ada3)


def _modulate_kernel(x_ref, mod_ref, o_ref, *, shift_row, scale_row):
    sh = mod_ref[shift_row:shift_row + 1, :]
    sc = mod_ref[scale_row:scale_row + 1, :]
    o_ref[...] = (x_ref[...] * (1.0 + sc) + sh).astype(BF16)


def _modulate(x2, mod3, seq, shift_row, scale_row, tm=512):
    t, d = x2.shape
    per_b = seq // tm
    return pl.pallas_call(
        functools.partial(_modulate_kernel, shift_row=shift_row, scale_row=scale_row),
        grid=(t // tm,),
        in_specs=[
            pl.BlockSpec((tm, d), lambda i: (i, 0)),
            pl.BlockSpec((None, 6, d), lambda i: (i // per_b, 0, 0)),
        ],
        out_specs=pl.BlockSpec((tm, d), lambda i: (i, 0)),
        out_shape=jax.ShapeDtypeStruct((t, d), BF16),
        compiler_params=_cparams(("arbitrary",)),
        name="modulate",
    )(x2, mod3)


def _rope_block(x, cos_t, sin_t):
    lane = lax.broadcasted_iota(jnp.int32, x.shape, 1)
    half = QK_ROPE_DIM // 2
    swapped = jnp.where(lane < half, pltpu.roll(x, LANES - half, 1), pltpu.roll(x, half, 1))
    return x * cos_t + swapped * sin_t


def _load_weight_rows(wt_ref, row0, wf_s, wb_s):
    n_rows = wf_s.shape[0]
    pltpu.sync_copy(wt_ref.at[pl.ds(pl.multiple_of(row0, 8), n_rows), :], wf_s)
    wb_s[0:n_rows, :] = wf_s[...].astype(BF16)


def _dot_nt(a, w):
    return lax.dot_general(a, w, (((1,), (1,)), ((), ())), preferred_element_type=F32)


def _stream_weight_rows(wt_ref, wf_s, wb_s, sems, row_base):
    j = pl.program_id(0)
    n_j = pl.num_programs(0)
    tn = wb_s.shape[0]

    def rows_copy(jj, slot):
        row0 = pl.multiple_of(row_base + jj * tn, 8)
        return pltpu.make_async_copy(wt_ref.at[pl.ds(row0, tn), :], wf_s.at[slot], sems.at[slot])

    @pl.when(pl.program_id(1) == 0)
    def _():
        @pl.when(j == 0)
        def _():
            rows_copy(0, 0).start()

        rows_copy(j, j % 2).wait()

        @pl.when(j + 1 < n_j)
        def _():
            rows_copy(j + 1, (j + 1) % 2).start()

        wb_s[...] = wf_s[j % 2].astype(BF16)


def _proj_plain_kernel(a_ref, wt_ref, o_ref, wf_s, wb_s, sems, *, row_base):
    _stream_weight_rows(wt_ref, wf_s, wb_s, sems, row_base)
    o_ref[...] = _dot_nt(a_ref[...], wb_s[...]).astype(o_ref.dtype)


def _proj_heads_kernel(a_ref, wt_ref, o_ref, wf_s, wb_s, sems, *, row_base, heads_per_tile):
    _stream_weight_rows(wt_ref, wf_s, wb_s, sems, row_base)
    res = _dot_nt(a_ref[...], wb_s[...])
    for hh in range(heads_per_tile):
        o_ref[hh] = res[:, hh * HEAD_DIM:(hh + 1) * HEAD_DIM].astype(o_ref.dtype)


def _proj_rope_kernel(a_ref, wt_ref, cos_ref, sin_ref, o_ref, wf_s, wb_s, *, row_base):
    @pl.when(pl.program_id(0) == 0)
    def _():
        wb_s[...] = jnp.zeros(wb_s.shape, BF16)
        _load_weight_rows(wt_ref, row_base, wf_s, wb_s)

    res = _dot_nt(a_ref[...], wb_s[...])
    o_ref[...] = _rope_block(res, cos_ref[...], sin_ref[...]).astype(o_ref.dtype)


def _in_proj_mla(u, wt, n_cols, tm=512, tn=512):
    t, d = u.shape
    return pl.pallas_call(
        functools.partial(_proj_plain_kernel, row_base=0),
        grid=(n_cols // tn, t // tm),
        in_specs=[
            pl.BlockSpec((tm, d), lambda j, i: (i, 0)),
            pl.BlockSpec(memory_space=pl.ANY),
        ],
        out_specs=pl.BlockSpec((tm, tn), lambda j, i: (i, j)),
        out_shape=jax.ShapeDtypeStruct((t, n_cols), BF16),
        scratch_shapes=[pltpu.VMEM((2, tn, d), F32), pltpu.VMEM((tn, d), BF16),
                        pltpu.SemaphoreType.DMA((2,))],
        compiler_params=_cparams(("arbitrary", "arbitrary")),
        name="in_proj_mla",
    )(u, wt)


def _in_proj_dil(u, wt, col0, batch, seq, n_heads, tm=512, tn=512):
    t, d = u.shape
    hpt = tn // HEAD_DIM
    tiles_per_mat = n_heads // hpt
    per_b = seq // tm
    return pl.pallas_call(
        functools.partial(_proj_heads_kernel, row_base=col0, heads_per_tile=hpt),
        grid=(3 * tiles_per_mat, t // tm),
        in_specs=[
            pl.BlockSpec((tm, d), lambda j, i: (i, 0)),
            pl.BlockSpec(memory_space=pl.ANY),
        ],
        out_specs=pl.BlockSpec(
            (None, None, hpt, tm, HEAD_DIM),
            lambda j, i: (j // tiles_per_mat, i // per_b, j % tiles_per_mat, i % per_b, 0)),
        out_shape=jax.ShapeDtypeStruct((3, batch, n_heads, seq, HEAD_DIM), BF16),
        scratch_shapes=[pltpu.VMEM((2, tn, d), F32), pltpu.VMEM((tn, d), BF16),
                        pltpu.SemaphoreType.DMA((2,))],
        compiler_params=_cparams(("arbitrary", "arbitrary")),
        name="in_proj_dil",
    )(u, wt)


def _in_proj_krope(u, wt, col0, cos_t, sin_t, tm=1024):
    t, d = u.shape
    return pl.pallas_call(
        functools.partial(_proj_rope_kernel, row_base=col0),
        grid=(t // tm,),
        in_specs=[
            pl.BlockSpec((tm, d), lambda i: (i, 0)),
            pl.BlockSpec(memory_space=pl.ANY),
            pl.BlockSpec((tm, LANES), lambda i: (i, 0)),
            pl.BlockSpec((tm, LANES), lambda i: (i, 0)),
        ],
        out_specs=pl.BlockSpec((tm, LANES), lambda i: (i, 0)),
        out_shape=jax.ShapeDtypeStruct((t, LANES), BF16),
        scratch_shapes=[pltpu.VMEM((QK_ROPE_DIM, d), F32), pltpu.VMEM((LANES, d), BF16)],
        compiler_params=_cparams(("arbitrary",)),
        name="in_proj_krope",
    )(u, wt, cos_t, sin_t)


def _rms_bf16(h_ref, g_ref):
    hf = h_ref[...].astype(F32)
    y = hf * lax.rsqrt(jnp.mean(hf * hf, axis=-1, keepdims=True) + RMS_EPS)
    return (y * g_ref[...]).astype(BF16)


def _q_up_kernel(h_ref, g_ref, w_ref, cos_ref, sin_ref, o_ref, *, n_heads, scale):
    yb = _rms_bf16(h_ref, g_ref)
    cos_t = cos_ref[...]
    sin_t = sin_ref[...]
    group = 4
    for h0 in range(0, n_heads, group):
        res = jnp.dot(yb, w_ref[:, h0 * QK_PAD:(h0 + group) * QK_PAD], preferred_element_type=F32)
        for hh in range(group):
            c0 = hh * QK_PAD
            nope = res[:, c0:c0 + QK_NOPE_DIM]
            rp = _rope_block(res[:, c0 + QK_NOPE_DIM:c0 + QK_PAD], cos_t, sin_t)
            o0 = (h0 + hh) * QK_PAD
            o_ref[:, o0:o0 + QK_NOPE_DIM] = (nope * scale).astype(BF16)
            o_ref[:, o0 + QK_NOPE_DIM:o0 + QK_PAD] = (rp * scale).astype(BF16)


def _q_up(hm, g3, l, w_uq_p, cos_t, sin_t, n_heads, tm=256):
    t = hm.shape[0]
    scale = float(QK_NOPE_DIM + QK_ROPE_DIM) ** -0.5 * LOG2E
    return pl.pallas_call(
        functools.partial(_q_up_kernel, n_heads=n_heads, scale=scale),
        grid=(t // tm,),
        in_specs=[
            pl.BlockSpec((tm, Q_LORA_RANK), lambda i: (i, 0)),
            pl.BlockSpec((None, 1, Q_LORA_RANK), lambda i: (l, 0, 0)),
            pl.BlockSpec((Q_LORA_RANK, n_heads * QK_PAD), lambda i: (0, 0)),
            pl.BlockSpec((tm, LANES), lambda i: (i, 0)),
            pl.BlockSpec((tm, LANES), lambda i: (i, 0)),
        ],
        out_specs=pl.BlockSpec((tm, n_heads * QK_PAD), lambda i: (i, 0)),
        out_shape=jax.ShapeDtypeStruct((t, n_heads * QK_PAD), BF16),
        compiler_params=_cparams(("arbitrary",)),
        name="mla_q_up",
    )(hm, g3, w_uq_p, cos_t, sin_t)


def _kv_up_kernel(h_ref, g_ref, wk_ref, wv_ref, kr_ref, k_ref, v_ref, *, n_heads):
    yb = _rms_bf16(h_ref, g_ref)
    kr = kr_ref[...]
    group = 4
    for h0 in range(0, n_heads, group):
        res = jnp.dot(yb, wk_ref[:, h0 * QK_NOPE_DIM:(h0 + group) * QK_NOPE_DIM],
                      preferred_element_type=F32)
        for hh in range(group):
            o0 = (h0 + hh) * QK_PAD
            k_ref[:, o0:o0 + QK_NOPE_DIM] = res[:, hh * QK_NOPE_DIM:(hh + 1) * QK_NOPE_DIM].astype(BF16)
            k_ref[:, o0 + QK_NOPE_DIM:o0 + QK_PAD] = kr
    v_ref[...] = jnp.dot(yb, wv_ref[...], preferred_element_type=F32).astype(BF16)


def _kv_up(hm, g3, l, w_uk, w_uv, krr, n_heads, tm=256):
    t = hm.shape[0]
    col_blk = Q_LORA_RANK // KV_LORA_RANK
    return pl.pallas_call(
        functools.partial(_kv_up_kernel, n_heads=n_heads),
        grid=(t // tm,),
        in_specs=[
            pl.BlockSpec((tm, KV_LORA_RANK), lambda i: (i, col_blk)),
            pl.BlockSpec((None, 1, KV_LORA_RANK), lambda i: (l, 0, 0)),
            pl.BlockSpec((KV_LORA_RANK, n_heads * QK_NOPE_DIM), lambda i: (0, 0)),
            pl.BlockSpec((KV_LORA_RANK, n_heads * V_HEAD_DIM), lambda i: (0, 0)),
            pl.BlockSpec((tm, LANES), lambda i: (i, 0)),
        ],
        out_specs=[
            pl.BlockSpec((tm, n_heads * QK_PAD), lambda i: (i, 0)),
            pl.BlockSpec((tm, n_heads * V_HEAD_DIM), lambda i: (i, 0)),
        ],
        out_shape=[
            jax.ShapeDtypeStruct((t, n_heads * QK_PAD), BF16),
            jax.ShapeDtypeStruct((t, n_heads * V_HEAD_DIM), BF16),
        ],
        compiler_params=_cparams(("arbitrary",)),
        name="mla_kv_up",
    )(hm, g3, w_uk, w_uv, krr)


MLA_HEADS_PER_STEP = 2


def _mla_attn_kernel(q_ref, k_ref, v_ref, o_ref, *, seq, tq):
    nq = seq // tq
    hp = MLA_HEADS_PER_STEP
    row = lax.broadcasted_iota(jnp.int32, (tq, tq), 0)
    col = lax.broadcasted_iota(jnp.int32, (tq, tq), 1)
    causal = col <= row

    def kv_step(qs, j, carry, masked):
        off = pl.multiple_of(j * tq, tq)
        out = []
        for hh in range(hp):
            m, l, acc = carry[hh]
            k = k_ref[pl.ds(off, tq), hh * QK_PAD:(hh + 1) * QK_PAD]
            v = v_ref[pl.ds(off, tq), hh * V_HEAD_DIM:(hh + 1) * V_HEAD_DIM]
            s = lax.dot_general(qs[hh], k, (((1,), (1,)), ((), ())), preferred_element_type=F32)
            if masked:
                s = jnp.where(causal, s, NEG_BIG)
            m_new = jnp.maximum(m, jnp.max(s, axis=-1, keepdims=True))
            p = jnp.exp2(s - m_new)
            alpha = jnp.exp2(m - m_new)
            l_new = alpha * l + jnp.sum(p, axis=-1, keepdims=True)
            acc_new = alpha * acc + jnp.dot(p.astype(BF16), v, preferred_element_type=F32)
            out.append((m_new, l_new, acc_new))
        return tuple(out)

    def q_loop(i, _):
        qoff = pl.multiple_of(i * tq, tq)
        qs = [q_ref[pl.ds(qoff, tq), hh * QK_PAD:(hh + 1) * QK_PAD] for hh in range(hp)]
        init = tuple((jnp.full((tq, 1), NEG_BIG, F32), jnp.zeros((tq, 1), F32),
                      jnp.zeros((tq, V_HEAD_DIM), F32)) for _ in range(hp))
        carry = lax.fori_loop(0, i, lambda j, c: kv_step(qs, j, c, False), init)
        carry = kv_step(qs, i, carry, True)
        for hh in range(hp):
            m, l, acc = carry[hh]
            o_ref[pl.ds(qoff, tq), hh * V_HEAD_DIM:(hh + 1) * V_HEAD_DIM] = (acc / l).astype(o_ref.dtype)
        return 0

    lax.fori_loop(0, nq, q_loop, 0)


def _mla_attn(q, k, v, batch, seq, n_heads, tq=1024):
    t = q.shape[0]
    hp = MLA_HEADS_PER_STEP
    return pl.pallas_call(
        functools.partial(_mla_attn_kernel, seq=seq, tq=tq),
        grid=(batch, n_heads // hp),
        in_specs=[
            pl.BlockSpec((seq, hp * QK_PAD), lambda b, h: (b, h)),
            pl.BlockSpec((seq, hp * QK_PAD), lambda b, h: (b, h)),
            pl.BlockSpec((seq, hp * V_HEAD_DIM), lambda b, h: (b, h)),
        ],
        out_specs=pl.BlockSpec((seq, hp * V_HEAD_DIM), lambda b, h: (b, h)),
        out_shape=jax.ShapeDtypeStruct((t, n_heads * V_HEAD_DIM), BF16),
        compiler_params=_cparams(("arbitrary", "arbitrary")),
        name="mla_attn",
    )(q, k, v)


DIL_GROUP = 16


def _dil_attn_kernel(qkv_ref, bvec_ref, o_ref, m_s, l_s, acc_s, bias_s, nat_f, *view_s, seq):
    scale = float(HEAD_DIM) ** -0.5 * LOG2E
    views = (qkv_ref,) + tuple(view_s)

    for bi in range(len(DIL_PATTERNS)):
        full = jnp.broadcast_to(bvec_ref[bi] * LOG2E, (BLOCK, 2 * BLOCK))
        bias_s[bi] = pltpu.roll(full, 0, 1, stride=1, stride_axis=0)

    for which in range(3):
        nat_f[...] = qkv_ref[which].astype(F32)
        for bi, (_, d) in enumerate(DIL_PATTERNS):
            if d == 1:
                continue
            for r in range(d):
                views[bi][which, :, r * HEAD_DIM:(r + 1) * HEAD_DIM] = (
                    nat_f[pl.ds(r, seq // d, stride=d), :].astype(BF16))

    def scores(ref, bi, d, r, i, first):
        lanes = slice(r * HEAD_DIM, (r + 1) * HEAD_DIM)
        qoff = pl.multiple_of(i * BLOCK, BLOCK)
        q = ref[0, pl.ds(qoff, BLOCK), lanes]
        if first:
            kk = ref[1, pl.ds(0, BLOCK), lanes]
            vv = ref[2, pl.ds(0, BLOCK), lanes]
            bias = bias_s[bi, :, BLOCK:2 * BLOCK]
        else:
            koff = pl.multiple_of(i * BLOCK - BLOCK, BLOCK)
            kk = ref[1, pl.ds(koff, 2 * BLOCK), lanes]
            vv = ref[2, pl.ds(koff, 2 * BLOCK), lanes]
            bias = bias_s[bi]
        s = lax.dot_general(q, kk, (((1,), (1,)), ((), ())), preferred_element_type=F32)
        s = s * scale + bias
        m_b = jnp.max(s, axis=-1, keepdims=True)
        p = jnp.exp2(s - m_b)
        l_b = jnp.sum(p, axis=-1, keepdims=True)
        a_b = jnp.dot(p.astype(BF16), vv, preferred_element_type=F32)
        m_b = jnp.broadcast_to(m_b, (BLOCK, HEAD_DIM))
        l_b = jnp.broadcast_to(l_b, (BLOCK, HEAD_DIM))
        if d == 1:
            rows = pl.ds(qoff, BLOCK)
        else:
            rows = pl.ds(i * (BLOCK * d) + r, BLOCK, stride=d)
        return m_b, l_b, a_b, rows

    def group(ref, bi, d, blocks):
        parts = [scores(ref, bi, d, r, i, first) for r, i, first in blocks]
        if d == 1:
            for m_b, l_b, a_b, rows in parts:
                m_s[rows, :] = m_b
                l_s[rows, :] = l_b
                acc_s[rows, :] = a_b
            return
        old = [(m_s[rows, :], l_s[rows, :], acc_s[rows, :]) for _, _, _, rows in parts]
        new = []
        for (m_b, l_b, a_b, rows), (m_o, l_o, a_o) in zip(parts, old):
            m_n = jnp.maximum(m_o, m_b)
            e_o = jnp.exp2(m_o - m_n)
            e_b = jnp.exp2(m_b - m_n)
            new.append((m_n, e_o * l_o + e_b * l_b, e_o * a_o + e_b * a_b, rows))
        for m_n, l_n, a_n, rows in new:
            m_s[rows, :] = m_n
            l_s[rows, :] = l_n
            acc_s[rows, :] = a_n

    for bi, (_, d) in enumerate(DIL_PATTERNS):
        ref = views[bi]
        nblk = seq // d // BLOCK
        n_res = min(d, DIL_GROUP)
        n_seq = DIL_GROUP // n_res
        assert d % n_res == 0 and nblk % n_seq == 0
        for r0 in range(0, d, n_res):
            group(ref, bi, d, [(r0 + u, i, i == 0) for i in range(n_seq) for u in range(n_res)])

            def body(t, _, ref=ref, bi=bi, d=d, r0=r0, n_res=n_res, n_seq=n_seq):
                group(ref, bi, d, [(r0 + u, t * n_seq + ii, False)
                                   for ii in range(n_seq) for u in range(n_res)])
                return 0

            lax.fori_loop(1, nblk // n_seq, body, 0)

    o_ref[...] = (acc_s[...] / l_s[...]).astype(o_ref.dtype)


def _dil_attn(qkv, bias_vec, batch, seq, n_heads):
    t = batch * seq
    nbr = len(DIL_PATTERNS)
    assert DIL_PATTERNS[0][1] == 1
    view_scratch = [pltpu.VMEM((3, seq // d, d * HEAD_DIM), BF16) for _, d in DIL_PATTERNS[1:]]
    return pl.pallas_call(
        functools.partial(_dil_attn_kernel, seq=seq),
        grid=(batch, n_heads),
        in_specs=[
            pl.BlockSpec((3, None, None, seq, HEAD_DIM), lambda b, h: (0, b, h, 0, 0)),
            pl.BlockSpec((nbr, None, 1, 2 * BLOCK), lambda b, h: (0, h, 0, 0)),
        ],
        out_specs=pl.BlockSpec((seq, HEAD_DIM), lambda b, h: (b, h)),
        out_shape=jax.ShapeDtypeStruct((t, n_heads * HEAD_DIM), BF16),
        scratch_shapes=[pltpu.VMEM((seq, HEAD_DIM), F32)] * 3
        + [pltpu.VMEM((nbr, BLOCK, 2 * BLOCK), F32), pltpu.VMEM((seq, HEAD_DIM), F32)]
        + view_scratch,
        compiler_params=_cparams(("arbitrary", "arbitrary")),
        name="dil_attn",
    )(qkv, bias_vec)


def _out_proj_kernel(a1_ref, a2_ref, w_ref, o_ref, wb_ref):
    @pl.when(pl.program_id(1) == 0)
    def _():
        wb_ref[...] = w_ref[...].astype(BF16)

    k1 = a1_ref.shape[1]
    acc = jnp.dot(a1_ref[...], wb_ref[:k1, :], preferred_element_type=F32)
    acc = acc + jnp.dot(a2_ref[...], wb_ref[k1:, :], preferred_element_type=F32)
    o_ref[...] = acc.astype(o_ref.dtype)


def _out_proj(o_a, o_b, w_o, l, tm=512, tn=512):
    t, k1 = o_a.shape
    k2 = o_b.shape[1]
    n = w_o.shape[-1]
    return pl.pallas_call(
        _out_proj_kernel,
        grid=(n // tn, t // tm),
        in_specs=[
            pl.BlockSpec((tm, k1), lambda j, i: (i, 0)),
            pl.BlockSpec((tm, k2), lambda j, i: (i, 0)),
            pl.BlockSpec((None, k1 + k2, tn), lambda j, i: (l, 0, j)),
        ],
        out_specs=pl.BlockSpec((tm, tn), lambda j, i: (i, j)),
        out_shape=jax.ShapeDtypeStruct((t, n), BF16),
        scratch_shapes=[pltpu.VMEM((k1 + k2, tn), BF16)],
        compiler_params=_cparams(("arbitrary", "arbitrary")),
        name="out_proj",
    )(o_a, o_b, w_o)


def _layer_norm_rows(z, g, b):
    mu = jnp.mean(z, axis=-1, keepdims=True)
    zc = z - mu
    var = jnp.mean(zc * zc, axis=-1, keepdims=True)
    return zc * lax.rsqrt(var + LN_EPS) * g + b


SLAB_CHUNK = 1024
SLAB_Q = SLAB_CHUNK // (2 * LANES)


def _slab_sub(width):
    return width // (2 * LANES)


def _slab_index(j, first_tok, n_tok, sub):
    return (pl.ds(first_tok * sub + j, n_tok, stride=sub), slice(None))


def _pack_chunk(vals):
    bits = lax.bitcast_convert_type(vals.astype(BF16).astype(F32), jnp.uint32)
    half = SLAB_CHUNK // 2
    return [(bits[:, q * LANES:(q + 1) * LANES] >> 16)
            | bits[:, half + q * LANES:half + (q + 1) * LANES] for q in range(SLAB_Q)]


def _unpack_words(words):
    lo = lax.bitcast_convert_type(words << 16, F32)
    hi = lax.bitcast_convert_type(words & jnp.uint32(0xFFFF0000), F32)
    return lo, hi


def _ln1_router_kernel(x_ref, mix_ref, mod_ref, g_ref, b_ref, wr_ref, br_ref,
                       x1_ref, u2p_ref, ids_ref, gates_ref, *, alpha):
    gt1 = mod_ref[2:3, :]
    sh2 = mod_ref[3:4, :]
    sc2 = mod_ref[4:5, :]
    z = alpha * x_ref[...] + gt1 * mix_ref[...].astype(F32)
    x1 = _layer_norm_rows(z, g_ref[...], b_ref[...])
    x1_ref[...] = x1
    u2 = (x1 * (1.0 + sc2) + sh2).astype(BF16)
    tm, d = u2.shape
    for g in range(d // SLAB_CHUNK):
        for q, words in enumerate(_pack_chunk(u2[:, g * SLAB_CHUNK:(g + 1) * SLAB_CHUNK])):
            u2p_ref[_slab_index(g * SLAB_Q + q, 0, tm, _slab_sub(d))] = words
    logits = jnp.dot(u2, wr_ref[...].astype(BF16), preferred_element_type=F32) + br_ref[...]
    lane = lax.broadcasted_iota(jnp.int32, logits.shape, 1)
    lane_f = lane.astype(F32)
    vals = []
    ids = []
    for _ in range(TOP_K):
        mk = jnp.max(logits, axis=-1, keepdims=True)
        idx_f = jnp.min(jnp.where(logits == mk, lane_f, float(LANES)), axis=-1, keepdims=True)
        vals.append(mk)
        ids.append(idx_f.astype(jnp.int32))
        logits = jnp.where(lane_f == idx_f, -jnp.inf, logits)
    exps = [jnp.exp(v - vals[0]) for v in vals]
    denom = exps[0]
    for e in exps[1:]:
        denom = denom + e
    ids_out = jnp.zeros(lane.shape, jnp.int32)
    gates_out = jnp.zeros(lane.shape, F32)
    for k in range(TOP_K):
        ids_out = jnp.where(lane == k, ids[k], ids_out)
        gates_out = jnp.where(lane == k, exps[k] / denom, gates_out)
    ids_ref[...] = ids_out
    gates_ref[...] = gates_out


def _ln1_router(x2, mix, mod3, g3, b3, wr_p, br_p, l, seq, alpha, tm=256):
    t, d = x2.shape
    per_b = seq // tm
    return pl.pallas_call(
        functools.partial(_ln1_router_kernel, alpha=alpha),
        grid=(t // tm,),
        in_specs=[
            pl.BlockSpec((tm, d), lambda i: (i, 0)),
            pl.BlockSpec((tm, d), lambda i: (i, 0)),
            pl.BlockSpec((None, 6, d), lambda i: (i // per_b, 0, 0)),
            pl.BlockSpec((None, 1, d), lambda i: (l, 0, 0)),
            pl.BlockSpec((None, 1, d), lambda i: (l, 0, 0)),
            pl.BlockSpec((d, LANES), lambda i: (0, 0)),
            pl.BlockSpec((1, LANES), lambda i: (0, 0)),
        ],
        out_specs=[
            pl.BlockSpec((tm, d), lambda i: (i, 0)),
            pl.BlockSpec((tm * _slab_sub(d), LANES), lambda i: (i, 0)),
            pl.BlockSpec((tm, LANES), lambda i: (i, 0)),
            pl.BlockSpec((tm, LANES), lambda i: (i, 0)),
        ],
        out_shape=[
            jax.ShapeDtypeStruct((t, d), F32),
            jax.ShapeDtypeStruct((t * _slab_sub(d), LANES), jnp.uint32),
            jax.ShapeDtypeStruct((t, LANES), jnp.int32),
            jax.ShapeDtypeStruct((t, LANES), F32),
        ],
        compiler_params=_cparams(("arbitrary",)),
        name="ln1_router",
    )(x2, mix, mod3, g3, b3, wr_p, br_p)


def _moe_row_tiles(n_full, n_tail, tile_fn):
    def full(j, _):
        tile_fn(pl.multiple_of(j * MOE_SUB, MOE_SUB), MOE_SUB)
        return 0

    def tail(k, _):
        tile_fn(pl.multiple_of(n_full * MOE_SUB + k * MOE_TAIL, MOE_TAIL), MOE_TAIL)
        return 0

    tile_fn(0, MOE_SUB)
    lax.fori_loop(1, n_full, full, 0)
    lax.fori_loop(0, n_tail, tail, 0)


def _cast_parts(part_refs, dst_ref):
    kp = dst_ref.shape[0] // len(part_refs)
    for p, ref in enumerate(part_refs):
        dst_ref[p * kp:(p + 1) * kp, :] = ref[...].astype(BF16)


def _moe_up_kernel(item_e, item_blk, item_full, item_tail, x_ref, *rest, n_parts):
    w_refs = rest[:2 * n_parts]
    bg_ref, bu_ref, h_ref, wg_s, wu_s = rest[2 * n_parts:]
    w = pl.program_id(0)
    n_full = item_full[w]
    n_tail = item_tail[w]

    @pl.when(n_full + n_tail > 0)
    def _():
        _cast_parts(w_refs[:n_parts], wg_s)
        _cast_parts(w_refs[n_parts:], wu_s)
        bg = bg_ref[...]
        bu = bu_ref[...]

        def tile(off, rows):
            xt = x_ref[pl.ds(off, rows), :]
            glu = jnp.dot(xt, wg_s[...], preferred_element_type=F32) + bg
            lin = jnp.dot(xt, wu_s[...], preferred_element_type=F32) + bu
            glu = jnp.minimum(glu, SWIGLU_LIMIT)
            lin = jnp.clip(lin, -SWIGLU_LIMIT, SWIGLU_LIMIT)
            act = glu * jax.nn.sigmoid(SWIGLU_ALPHA * glu) * (lin + 1.0)
            h_ref[pl.ds(off, rows), :] = act.astype(h_ref.dtype)

        _moe_row_tiles(n_full, n_tail, tile)


def _moe_up(xs, w_gate, w_up, b_gate4, b_up4, item_e, item_blk, item_full, item_tail, l, tf=256):
    r, d = xs.shape
    n_items = r // MOE_ROWS
    ff = w_gate.shape[-1]
    n_chunks = ff // tf

    def cmap(w, c, ifull, itail):
        return jnp.where(ifull[w] + itail[w] > 0, c, n_chunks - 1)

    def wmap(w, c, ie, ib, ifull, itail):
        return (l, ie[w], 0, cmap(w, c, ifull, itail))

    n_parts = MOE_W_PARTS
    kp = d // n_parts

    def part_spec(p):
        return pl.BlockSpec((None, None, kp, tf),
                            lambda w, c, ie, ib, ifull, itail: (l, ie[w], p, cmap(w, c, ifull, itail)))

    grid_spec = pltpu.PrefetchScalarGridSpec(
        num_scalar_prefetch=4,
        grid=(n_items, n_chunks),
        in_specs=[pl.BlockSpec((MOE_ROWS, d), lambda w, c, ie, ib, ifull, itail: (ib[w], 0))]
        + [part_spec(p) for p in range(n_parts)] * 2
        + [pl.BlockSpec((None, None, 1, tf), wmap), pl.BlockSpec((None, None, 1, tf), wmap)],
        out_specs=pl.BlockSpec(
            (MOE_ROWS, tf),
            lambda w, c, ie, ib, ifull, itail: (ib[w], cmap(w, c, ifull, itail))),
        scratch_shapes=[pltpu.VMEM((d, tf), BF16), pltpu.VMEM((d, tf), BF16)],
    )
    return pl.pallas_call(
        functools.partial(_moe_up_kernel, n_parts=n_parts),
        grid_spec=grid_spec,
        out_shape=jax.ShapeDtypeStruct((r, ff), BF16),
        compiler_params=_cparams(("arbitrary", "arbitrary")),
        name="moe_up",
    )(item_e, item_blk, item_full, item_tail, xs, *([w_gate] * n_parts), *([w_up] * n_parts),
      b_gate4, b_up4)


def _moe_down_kernel(item_e, item_blk, item_full, item_tail, h_ref, *rest, sub, n_parts):
    w_refs = rest[:n_parts]
    bd_ref, y_ref, wd_s = rest[n_parts:]
    w = pl.program_id(0)
    c = pl.program_id(1)
    n_full = item_full[w]
    n_tail = item_tail[w]

    @pl.when(n_full + n_tail > 0)
    def _():
        _cast_parts(w_refs, wd_s)
        bd = bd_ref[...]

        def tile(off, rows):
            ht = h_ref[pl.ds(off, rows), :]
            y = jnp.dot(ht, wd_s[...], preferred_element_type=F32) + bd
            for q, words in enumerate(_pack_chunk(y)):
                y_ref[_slab_index(c * SLAB_Q + q, off, rows, sub)] = words

        _moe_row_tiles(n_full, n_tail, tile)


def _moe_down(hs, w_down, b_down4, item_e, item_blk, item_full, item_tail, l):
    r, ff = hs.shape
    n_items = r // MOE_ROWS
    d = w_down.shape[-1]
    tn = SLAB_CHUNK
    n_chunks = d // tn
    sub = _slab_sub(d)

    def cmap(w, c, ifull, itail):
        return jnp.where(ifull[w] + itail[w] > 0, c, n_chunks - 1)

    def wmap(w, c, ie, ib, ifull, itail):
        return (l, ie[w], 0, cmap(w, c, ifull, itail))

    n_parts = MOE_W_PARTS
    kp = ff // n_parts

    def part_spec(p):
        return pl.BlockSpec((None, None, kp, tn),
                            lambda w, c, ie, ib, ifull, itail: (l, ie[w], p, cmap(w, c, ifull, itail)))

    grid_spec = pltpu.PrefetchScalarGridSpec(
        num_scalar_prefetch=4,
        grid=(n_items, n_chunks),
        in_specs=[pl.BlockSpec((MOE_ROWS, ff), lambda w, c, ie, ib, ifull, itail: (ib[w], 0))]
        + [part_spec(p) for p in range(n_parts)]
        + [pl.BlockSpec((None, None, 1, tn), wmap)],
        out_specs=pl.BlockSpec((MOE_ROWS * sub, LANES),
                               lambda w, c, ie, ib, ifull, itail: (ib[w], 0)),
        scratch_shapes=[pltpu.VMEM((ff, tn), BF16)],
    )
    return pl.pallas_call(
        functools.partial(_moe_down_kernel, sub=sub, n_parts=n_parts),
        grid_spec=grid_spec,
        out_shape=jax.ShapeDtypeStruct((r * sub, LANES), jnp.uint32),
        compiler_params=_cparams(("arbitrary", "arbitrary")),
        name="moe_down",
    )(item_e, item_blk, item_full, item_tail, hs, *([w_down] * n_parts), b_down4)


GATHER_UNROLL = 8


def _buffer_pitch(sub):
    return sub + 8 if sub % 16 == 0 else sub


def _slab_gather(idx_ref, src_ref, buf, sem, base_slab, n, sub, wait):
    pitch = _buffer_pitch(sub)

    def copy(r):
        src_row = pl.multiple_of(idx_ref[0, r] * sub, sub)
        dst_row = pl.multiple_of((base_slab + r) * pitch, 8)
        return pltpu.make_async_copy(src_ref.at[pl.ds(src_row, sub), :],
                                     buf.at[pl.ds(dst_row, sub), :], sem)

    def body(r0, _):
        for u in range(GATHER_UNROLL):
            c = copy(r0 * GATHER_UNROLL + u)
            if wait:
                c.wait()
            else:
                c.start(priority=u % 2)
        return 0

    lax.fori_loop(0, n // GATHER_UNROLL, body, 0)


def _dispatch_kernel(vblk, nvalid, tok_ref, tok_next_ref, src_ref, o_ref, buf, sems, *, rows, d):
    s = pl.program_id(0)
    slot = s % 2
    sub = _slab_sub(d)

    @pl.when(s == 0)
    def _():
        _slab_gather(tok_ref, src_ref, buf, sems.at[0], 0, rows, sub, wait=False)

    @pl.when(s + 1 < nvalid[0])
    def _():
        _slab_gather(tok_next_ref, src_ref, buf, sems.at[1 - slot], (1 - slot) * rows, rows, sub,
                     wait=False)

    @pl.when((s < nvalid[0]) | (s == 0))
    def _():
        _slab_gather(tok_ref, src_ref, buf, sems.at[slot], slot * rows, rows, sub, wait=True)
        half = SLAB_CHUNK // 2
        for g in range(d // SLAB_CHUNK):
            for q in range(SLAB_Q):
                idx = _slab_index(g * SLAB_Q + q, slot * rows, rows, _buffer_pitch(sub))
                lo, hi = _unpack_words(buf[idx])
                c0 = g * SLAB_CHUNK + q * LANES
                o_ref[:, c0:c0 + LANES] = lo.astype(BF16)
                o_ref[:, c0 + half:c0 + half + LANES] = hi.astype(BF16)


def _dispatch(u2p, d, row_tok, vblk, nvalid, rows=MOE_BLK):
    r = row_tok.shape[0]
    n_blk = r // rows
    n_steps = vblk.shape[0]
    tok3 = row_tok.reshape(n_blk, 1, rows)
    grid_spec = pltpu.PrefetchScalarGridSpec(
        num_scalar_prefetch=2,
        grid=(n_steps,),
        in_specs=[
            pl.BlockSpec((None, 1, rows), lambda s, vb, nv: (vb[s], 0, 0), memory_space=pltpu.SMEM),
            pl.BlockSpec((None, 1, rows), lambda s, vb, nv: (vb[jnp.minimum(s + 1, n_steps - 1)], 0, 0),
                         memory_space=pltpu.SMEM),
            pl.BlockSpec(memory_space=pl.ANY),
        ],
        out_specs=pl.BlockSpec((rows, d), lambda s, vb, nv: (vb[s], 0)),
        scratch_shapes=[pltpu.VMEM((2 * rows * _buffer_pitch(_slab_sub(d)), LANES), jnp.uint32),
                        pltpu.SemaphoreType.DMA((2,))],
    )
    return pl.pallas_call(
        functools.partial(_dispatch_kernel, rows=rows, d=d),
        grid_spec=grid_spec,
        out_shape=jax.ShapeDtypeStruct((r, d), BF16),
        compiler_params=_cparams(("arbitrary",)),
        name="moe_dispatch",
    )(vblk, nvalid, tok3, tok3, u2p)


def _combine_ln2_kernel(slot_ref, slot_next_ref, x1_ref, ys_ref, gates_ref, mod_ref, g_ref, b_ref,
                        o_ref, buf, ffn_s, sems, *, alpha, tm):
    i = pl.program_id(0)
    n = pl.num_programs(0)
    cur = i % 2
    n_rows = TOP_K * tm
    d = ffn_s.shape[1]
    sub = _slab_sub(d)

    @pl.when(i == 0)
    def _():
        _slab_gather(slot_ref, ys_ref, buf, sems.at[0], 0, n_rows, sub, wait=False)

    @pl.when(i + 1 < n)
    def _():
        _slab_gather(slot_next_ref, ys_ref, buf, sems.at[1 - cur], (1 - cur) * n_rows, n_rows, sub,
                     wait=False)

    _slab_gather(slot_ref, ys_ref, buf, sems.at[cur], cur * n_rows, n_rows, sub, wait=True)

    gates = gates_ref[...]
    gk = [jnp.broadcast_to(gates[:, k:k + 1], (tm, LANES)) for k in range(TOP_K)]
    half = SLAB_CHUNK // 2
    for g in range(d // SLAB_CHUNK):
        for q in range(SLAB_Q):
            lo_acc = None
            hi_acc = None
            for k in range(TOP_K):
                idx = _slab_index(g * SLAB_Q + q, cur * n_rows + k * tm, tm, _buffer_pitch(sub))
                lo, hi = _unpack_words(buf[idx])
                lo_acc = gk[k] * lo if lo_acc is None else lo_acc + gk[k] * lo
                hi_acc = gk[k] * hi if hi_acc is None else hi_acc + gk[k] * hi
            c0 = g * SLAB_CHUNK + q * LANES
            ffn_s[:, c0:c0 + LANES] = lo_acc
            ffn_s[:, c0 + half:c0 + half + LANES] = hi_acc
    gt2 = mod_ref[5:6, :]
    z = alpha * x1_ref[...] + gt2 * ffn_s[...]
    o_ref[...] = _layer_norm_rows(z, g_ref[...], b_ref[...])


def _combine_ln2(x1, ys, slot, gates, mod3, g3, b3, l, seq, alpha, tm=256):
    t, d = x1.shape
    per_b = seq // tm
    n_tiles = t // tm
    slot3 = slot.reshape(n_tiles, tm, TOP_K).transpose(0, 2, 1).reshape(n_tiles, 1, TOP_K * tm)
    return pl.pallas_call(
        functools.partial(_combine_ln2_kernel, alpha=alpha, tm=tm),
        grid=(n_tiles,),
        in_specs=[
            pl.BlockSpec((None, 1, tm * TOP_K), lambda i: (i, 0, 0), memory_space=pltpu.SMEM),
            pl.BlockSpec((None, 1, tm * TOP_K), lambda i: (jnp.minimum(i + 1, n_tiles - 1), 0, 0),
                         memory_space=pltpu.SMEM),
            pl.BlockSpec((tm, d), lambda i: (i, 0)),
            pl.BlockSpec(memory_space=pl.ANY),
            pl.BlockSpec((tm, LANES), lambda i: (i, 0)),
            pl.BlockSpec((None, 6, d), lambda i: (i // per_b, 0, 0)),
            pl.BlockSpec((None, 1, d), lambda i: (l, 0, 0)),
            pl.BlockSpec((None, 1, d), lambda i: (l, 0, 0)),
        ],
        out_specs=pl.BlockSpec((tm, d), lambda i: (i, 0)),
        out_shape=jax.ShapeDtypeStruct((t, d), F32),
        scratch_shapes=[pltpu.VMEM((2 * TOP_K * tm * _buffer_pitch(_slab_sub(d)), LANES), jnp.uint32),
                        pltpu.VMEM((tm, d), F32),
                        pltpu.SemaphoreType.DMA((2,))],
        compiler_params=_cparams(("arbitrary",)),
        name="combine_ln2",
    )(slot3, slot3, x1, ys, gates, mod3, g3, b3)


def _t5_bucket(dist):
    is_small = dist < REL_MAX_EXACT
    nf = jnp.maximum(dist, REL_MAX_EXACT).astype(F32)
    large = REL_MAX_EXACT + (jnp.log(nf / REL_MAX_EXACT) / math.log(REL_MAX_DISTANCE / REL_MAX_EXACT)
                             * (REL_BUCKETS - REL_MAX_EXACT)).astype(jnp.int32)
    large = jnp.minimum(large, REL_BUCKETS - 1)
    return jnp.where(is_small, dist, large)


def _dilated_bias_vectors(rel_bias):
    m = jnp.arange(2 * BLOCK)
    rel = BLOCK - m
    vecs = []
    for window, d in DIL_PATTERNS:
        valid = (rel >= 0) & (rel <= window // d)
        bucket = _t5_bucket(jnp.clip(rel, 0) * d)
        bias = jnp.transpose(rel_bias[bucket].astype(F32), (1, 0))
        vecs.append(jnp.where(valid[None, :], bias, NEG_BIG))
    return jnp.stack(vecs, axis=0)[:, :, None, :]


def _routing_tables(ids4, n_items_max):
    e = ids4.reshape(-1)
    n_rows_real = e.shape[0]
    onehot =(e[:, None] == jnp.arange(N_EXPERTS, dtype=jnp.int32)[None, :]).astype(jnp.int32)
    csum = jnp.cumsum(onehot, axis=0)
    rank = jnp.sum(onehot * csum, axis=1) - 1
    counts = csum[-1]
    nit = (counts + MOE_ROWS - 1) // MOE_ROWS
    cum_items = jnp.cumsum(nit)
    first_item = cum_items - nit
    slot = first_item[e] * MOE_ROWS + rank
    total = cum_items[-1]
    w = jnp.arange(n_items_max, dtype=jnp.int32)
    e_w = jnp.minimum(jnp.searchsorted(cum_items, w, side="right", method="compare_all"),
                      N_EXPERTS - 1).astype(jnp.int32)
    j_w = w - first_item[e_w]
    rows_w = jnp.clip(counts[e_w] - j_w * MOE_ROWS, 0, MOE_ROWS)
    valid = w < total
    rows_w = jnp.where(valid, rows_w, 0)
    tails_per_full = MOE_SUB // MOE_TAIL
    n_full = rows_w // MOE_SUB
    n_tail = (rows_w - n_full * MOE_SUB + MOE_TAIL - 1) // MOE_TAIL
    item_full = jnp.where(n_tail == tails_per_full, n_full + 1, n_full).astype(jnp.int32)
    item_tail = jnp.where(n_tail == tails_per_full, 0, n_tail).astype(jnp.int32)
    last = total - 1
    item_e = jnp.where(valid, e_w, e_w[last]).astype(jnp.int32)
    item_blk = jnp.where(valid, w, last).astype(jnp.int32)
    per_item = MOE_ROWS // MOE_BLK
    blk = jnp.arange(n_items_max * per_item, dtype=jnp.int32)
    blk_valid = ((blk % per_item) * MOE_BLK < rows_w[blk // per_item]).astype(jnp.int32)
    cum_valid = jnp.cumsum(blk_valid)
    nvalid = cum_valid[-1]
    n_steps = n_rows_real // MOE_BLK + n_items_max
    want = jnp.minimum(jnp.arange(n_steps, dtype=jnp.int32) + 1, nvalid)
    vblk = jnp.searchsorted(cum_valid, want, side="left", method="compare_all").astype(jnp.int32)
    return (slot.astype(jnp.int32), item_e, item_blk, item_full, item_tail, vblk,
            nvalid.astype(jnp.int32).reshape(1))


def kernel(x, c, positions, w_ada, b_ada, w_in, g_q, g_kv, w_uq, w_ukv, rel_bias, w_o,
           ln1_g, ln1_b, w_router, b_router, w_gate, b_gate, w_up, b_up, w_down, b_down,
           ln2_g, ln2_b):
    batch, seq, d = x.shape
    depth = w_ada.shape[0]
    t = batch * seq
    n_heads = d // (2 * HEAD_DIM)
    dil_w = n_heads * HEAD_DIM
    alpha = (2.0 * depth) ** 0.25
    assert seq % (BLOCK * max(dd for _, dd in DIL_PATTERNS)) == 0
    assert all(win // dd == BLOCK for win, dd in DIL_PATTERNS)

    inv_freq = 1.0 / (ROPE_THETA ** (jnp.arange(0, QK_ROPE_DIM, 2, dtype=F32) / QK_ROPE_DIM))
    ang = positions.astype(F32)[..., None] * inv_freq
    cos, sin = jnp.cos(ang).reshape(t, -1), jnp.sin(ang).reshape(t, -1)
    zpad = jnp.zeros((t, LANES - QK_ROPE_DIM), F32)
    cos_t = jnp.concatenate([cos, cos, zpad], axis=1)
    sin_t = jnp.concatenate([-sin, sin, zpad], axis=1)

    bias_vec = _dilated_bias_vectors(rel_bias)
    c_pad = jnp.zeros((16, d), F32).at[:batch].set(c)

    x2 = x.reshape(t, d)
    n_items_max = (t * TOP_K) // MOE_ROWS + N_EXPERTS
    mla_cols = Q_LORA_RANK + KV_LORA_RANK

    for l in range(depth):
        mod = _ada(c_pad, w_ada, b_ada.reshape(depth, 1, -1), l)[:batch]
        mod3 = mod.reshape(batch, 6, d)

        w_in_t = jnp.swapaxes(w_in[l], 0, 1)
        wq = w_uq[l].reshape(Q_LORA_RANK, n_heads, QK_NOPE_DIM + QK_ROPE_DIM)
        w_uq_p = jnp.pad(wq, ((0, 0), (0, 0), (0, QK_PAD - QK_NOPE_DIM - QK_ROPE_DIM))
                         ).reshape(Q_LORA_RANK, n_heads * QK_PAD).astype(BF16)
        wkv = w_ukv[l].reshape(KV_LORA_RANK, n_heads, QK_NOPE_DIM + V_HEAD_DIM)
        w_uk = wkv[:, :, :QK_NOPE_DIM].reshape(KV_LORA_RANK, -1).astype(BF16)
        w_uv = wkv[:, :, QK_NOPE_DIM:].reshape(KV_LORA_RANK, -1).astype(BF16)

        u1 = _modulate(x2, mod3, seq, 0, 1)
        hm = _in_proj_mla(u1, w_in_t, mla_cols)
        qkv_d = _in_proj_dil(u1, w_in_t, mla_cols + QK_ROPE_DIM, batch, seq, n_heads)
        krr = _in_proj_krope(u1, w_in_t, mla_cols, cos_t, sin_t)
        q_a = _q_up(hm, g_q.reshape(depth, 1, -1), l, w_uq_p, cos_t, sin_t, n_heads)
        k_a, v_a = _kv_up(hm, g_kv.reshape(depth, 1, -1), l, w_uk, w_uv, krr, n_heads)
        o_a = _mla_attn(q_a, k_a, v_a, batch, seq, n_heads)
        o_b = _dil_attn(qkv_d, bias_vec, batch, seq, n_heads)
        mix = _out_proj(o_a, o_b, w_o, l)

        wr_p = jnp.pad(w_router[l], ((0, 0), (0, LANES - N_EXPERTS)))
        br_p = jnp.concatenate([b_router[l], jnp.full((LANES - N_EXPERTS,), NEG_BIG, F32)])[None, :]
        x1, u2p, ids, gates = _ln1_router(
            x2, mix, mod3, ln1_g.reshape(depth, 1, -1), ln1_b.reshape(depth, 1, -1),
            wr_p, br_p, l, seq, alpha)

        slot, item_e, item_blk, item_full, item_tail, vblk, nvalid = _routing_tables(
            ids[:, :TOP_K], n_items_max)
        tok = jnp.arange(t * TOP_K, dtype=jnp.int32) // TOP_K
        row_tok = jnp.zeros((n_items_max * MOE_ROWS,), jnp.int32).at[slot].set(tok)
        xs = _dispatch(u2p, d, row_tok, vblk, nvalid)
        hs = _moe_up(xs, w_gate, w_up, b_gate.reshape(depth, N_EXPERTS, 1, -1),
                     b_up.reshape(depth, N_EXPERTS, 1, -1), item_e, item_blk, item_full,
                     item_tail, l)
        ys = _moe_down(hs, w_down, b_down.reshape(depth, N_EXPERTS, 1, -1),
                       item_e, item_blk, item_full, item_tail, l)
        x2 = _combine_ln2(x1, ys, slot, gates, mod3, ln2_g.reshape(depth, 1, -1),
                          ln2_b.reshape(depth, 1, -1), l, seq, alpha)

    return x2.reshape(batch, seq, d)
```

```python
import functools
import math

import jax
import jax.numpy as jnp
from jax import lax
from jax.experimental import pallas as pl
from jax.experimental.pallas import tpu as pltpu

F32 = jnp.float32
BF16 = jnp.bfloat16

HEAD_DIM = 128
Q_LORA_RANK = 1536
KV_LORA_RANK = 512
QK_NOPE_DIM = 128
QK_ROPE_DIM = 64
V_HEAD_DIM = 128
ROPE_THETA = 10000.0
DIL_PATTERNS = ((128, 1), (512, 4), (2048, 16))
BLOCK = 128
REL_BUCKETS = 32
REL_MAX_EXACT = REL_BUCKETS // 2
REL_MAX_DISTANCE = 2048
N_EXPERTS = 32
TOP_K = 4
EXPERT_FF = 1536
SWIGLU_LIMIT = 7.0
SWIGLU_ALPHA = 1.702
LN_EPS = 1e-5
RMS_EPS = 1e-6
NEG_BIG = -1e30
LOG2E = math.log2(math.e)

LANES = 128
QK_PAD = 256
VMEM_LIMIT = 56 * 1024 * 1024
MOE_ROWS = 1280
MOE_SUB = 512
MOE_TAIL = 128
MOE_BLK = 256


def _cparams(sem):
    return pltpu.CompilerParams(dimension_semantics=sem, vmem_limit_bytes=VMEM_LIMIT)


def _ada_kernel(c_ref, w_ref, b_ref, o_ref):
    c = c_ref[...]
    cond = (c * jax.nn.sigmoid(c)).astype(BF16)
    o_ref[...] = jnp.dot(cond, w_ref[...].astype(BF16), preferred_element_type=F32) + b_ref[...]


def _ada(c_pad, w_ada, b_ada3, l, tn=512):
    rows, d = c_pad.shape
    n = w_ada.shape[-1]
    return pl.pallas_call(
        _ada_kernel,
        grid=(n // tn,),
        in_specs=[
            pl.BlockSpec((rows, d), lambda j: (0, 0)),
            pl.BlockSpec((None, d, tn), lambda j: (l, 0, j)),
            pl.BlockSpec((None, 1, tn), lambda j: (l, 0, j)),
        ],
        out_specs=pl.BlockSpec((rows, tn), lambda j: (0, j)),
        out_shape=jax.ShapeDtypeStruct((rows, n), F32),
        compiler_params=_cparams(("arbitrary",)),
        name="ada_mod",
    )(c_pad, w_ada, b_ada3)


def _modulate_kernel(x_ref, mod_ref, o_ref, *, shift_row, scale_row):
    sh = mod_ref[shift_row:shift_row + 1, :]
    sc = mod_ref[scale_row:scale_row + 1, :]
    o_ref[...] = (x_ref[...] * (1.0 + sc) + sh).astype(BF16)


def _modulate(x2, mod3, seq, shift_row, scale_row, tm=512):
    t, d = x2.shape
    per_b = seq // tm
    return pl.pallas_call(
        functools.partial(_modulate_kernel, shift_row=shift_row, scale_row=scale_row),
        grid=(t // tm,),
        in_specs=[
            pl.BlockSpec((tm, d), lambda i: (i, 0)),
            pl.BlockSpec((None, 6, d), lambda i: (i // per_b, 0, 0)),
        ],
        out_specs=pl.BlockSpec((tm, d), lambda i: (i, 0)),
        out_shape=jax.ShapeDtypeStruct((t, d), BF16),
        compiler_params=_cparams(("arbitrary",)),
        name="modulate",
    )(x2, mod3)


def _rope_block(x, cos_t, sin_t):
    lane = lax.broadcasted_iota(jnp.int32, x.shape, 1)
    half = QK_ROPE_DIM // 2
    swapped = jnp.where(lane < half, pltpu.roll(x, LANES - half, 1), pltpu.roll(x, half, 1))
    return x * cos_t + swapped * sin_t


def _load_weight_rows(wt_ref, row0, wf_s, wb_s):
    n_rows = wf_s.shape[0]
    pltpu.sync_copy(wt_ref.at[pl.ds(pl.multiple_of(row0, 8), n_rows), :], wf_s)
    wb_s[0:n_rows, :] = wf_s[...].astype(BF16)


def _dot_nt(a, w):
    return lax.dot_general(a, w, (((1,), (1,)), ((), ())), preferred_element_type=F32)


def _stream_weight_rows(wt_ref, wf_s, wb_s, sems, row_base):
    j = pl.program_id(0)
    n_j = pl.num_programs(0)
    tn = wb_s.shape[0]

    def rows_copy(jj, slot):
        row0 = pl.multiple_of(row_base + jj * tn, 8)
        return pltpu.make_async_copy(wt_ref.at[pl.ds(row0, tn), :], wf_s.at[slot], sems.at[slot])

    @pl.when(pl.program_id(1) == 0)
    def _():
        @pl.when(j == 0)
        def _():
            rows_copy(0, 0).start()

        rows_copy(j, j % 2).wait()

        @pl.when(j + 1 < n_j)
        def _():
            rows_copy(j + 1, (j + 1) % 2).start()

        wb_s[...] = wf_s[j % 2].astype(BF16)


def _proj_plain_kernel(a_ref, wt_ref, o_ref, wf_s, wb_s, sems, *, row_base):
    _stream_weight_rows(wt_ref, wf_s, wb_s, sems, row_base)
    o_ref[...] = _dot_nt(a_ref[...], wb_s[...]).astype(o_ref.dtype)


def _proj_heads_kernel(a_ref, wt_ref, o_ref, wf_s, wb_s, sems, *, row_base, heads_per_tile):
    _stream_weight_rows(wt_ref, wf_s, wb_s, sems, row_base)
    res = _dot_nt(a_ref[...], wb_s[...])
    for hh in range(heads_per_tile):
        o_ref[hh] = res[:, hh * HEAD_DIM:(hh + 1) * HEAD_DIM].astype(o_ref.dtype)


def _proj_rope_kernel(a_ref, wt_ref, cos_ref, sin_ref, o_ref, wf_s, wb_s, *, row_base):
    @pl.when(pl.program_id(0) == 0)
    def _():
        wb_s[...] = jnp.zeros(wb_s.shape, BF16)
        _load_weight_rows(wt_ref, row_base, wf_s, wb_s)

    res = _dot_nt(a_ref[...], wb_s[...])
    o_ref[...] = _rope_block(res, cos_ref[...], sin_ref[...]).astype(o_ref.dtype)


def _in_proj_mla(u, wt, n_cols, tm=1024, tn=512):
    t, d = u.shape
    return pl.pallas_call(
        functools.partial(_proj_plain_kernel, row_base=0),
        grid=(n_cols // tn, t // tm),
        in_specs=[
            pl.BlockSpec((tm, d), lambda j, i: (i, 0)),
            pl.BlockSpec(memory_space=pl.ANY),
        ],
        out_specs=pl.BlockSpec((tm, tn), lambda j, i: (i, j)),
        out_shape=jax.ShapeDtypeStruct((t, n_cols), BF16),
        scratch_shapes=[pltpu.VMEM((2, tn, d), F32), pltpu.VMEM((tn, d), BF16),
                        pltpu.SemaphoreType.DMA((2,))],
        compiler_params=_cparams(("arbitrary", "arbitrary")),
        name="in_proj_mla",
    )(u, wt)


def _in_proj_dil(u, wt, col0, batch, seq, n_heads, tm=1024, tn=512):
    t, d = u.shape
    hpt = tn // HEAD_DIM
    tiles_per_mat = n_heads // hpt
    per_b = seq // tm
    return pl.pallas_call(
        functools.partial(_proj_heads_kernel, row_base=col0, heads_per_tile=hpt),
        grid=(3 * tiles_per_mat, t // tm),
        in_specs=[
            pl.BlockSpec((tm, d), lambda j, i: (i, 0)),
            pl.BlockSpec(memory_space=pl.ANY),
        ],
        out_specs=pl.BlockSpec(
            (None, None, hpt, tm, HEAD_DIM),
            lambda j, i: (j // tiles_per_mat, i // per_b, j % tiles_per_mat, i % per_b, 0)),
        out_shape=jax.ShapeDtypeStruct((3, batch, n_heads, seq, HEAD_DIM), BF16),
        scratch_shapes=[pltpu.VMEM((2, tn, d), F32), pltpu.VMEM((tn, d), BF16),
                        pltpu.SemaphoreType.DMA((2,))],
        compiler_params=_cparams(("arbitrary", "arbitrary")),
        name="in_proj_dil",
    )(u, wt)


def _in_proj_krope(u, wt, col0, cos_t, sin_t, tm=1024):
    t, d = u.shape
    return pl.pallas_call(
        functools.partial(_proj_rope_kernel, row_base=col0),
        grid=(t // tm,),
        in_specs=[
            pl.BlockSpec((tm, d), lambda i: (i, 0)),
            pl.BlockSpec(memory_space=pl.ANY),
            pl.BlockSpec((tm, LANES), lambda i: (i, 0)),
            pl.BlockSpec((tm, LANES), lambda i: (i, 0)),
        ],
        out_specs=pl.BlockSpec((tm, LANES), lambda i: (i, 0)),
        out_shape=jax.ShapeDtypeStruct((t, LANES), BF16),
        scratch_shapes=[pltpu.VMEM((QK_ROPE_DIM, d), F32), pltpu.VMEM((LANES, d), BF16)],
        compiler_params=_cparams(("arbitrary",)),
        name="in_proj_krope",
    )(u, wt, cos_t, sin_t)


def _rms_bf16(h_ref, g_ref):
    hf = h_ref[...].astype(F32)
    y = hf * lax.rsqrt(jnp.mean(hf * hf, axis=-1, keepdims=True) + RMS_EPS)
    return (y * g_ref[...]).astype(BF16)


def _q_up_kernel(h_ref, g_ref, w_ref, cos_ref, sin_ref, o_ref, *, n_heads, scale):
    yb = _rms_bf16(h_ref, g_ref)
    cos_t = cos_ref[...]
    sin_t = sin_ref[...]
    group = 4
    for h0 in range(0, n_heads, group):
        res = jnp.dot(yb, w_ref[:, h0 * QK_PAD:(h0 + group) * QK_PAD], preferred_element_type=F32)
        for hh in range(group):
            c0 = hh * QK_PAD
            nope = res[:, c0:c0 + QK_NOPE_DIM]
            rp = _rope_block(res[:, c0 + QK_NOPE_DIM:c0 + QK_PAD], cos_t, sin_t)
            o0 = (h0 + hh) * QK_PAD
            o_ref[:, o0:o0 + QK_NOPE_DIM] = (nope * scale).astype(BF16)
            o_ref[:, o0 + QK_NOPE_DIM:o0 + QK_PAD] = (rp * scale).astype(BF16)


def _q_up(hm, g3, l, w_uq_p, cos_t, sin_t, n_heads, tm=256):
    t = hm.shape[0]
    scale = float(QK_NOPE_DIM + QK_ROPE_DIM) ** -0.5 * LOG2E
    return pl.pallas_call(
        functools.partial(_q_up_kernel, n_heads=n_heads, scale=scale),
        grid=(t // tm,),
        in_specs=[
            pl.BlockSpec((tm, Q_LORA_RANK), lambda i: (i, 0)),
            pl.BlockSpec((None, 1, Q_LORA_RANK), lambda i: (l, 0, 0)),
            pl.BlockSpec((Q_LORA_RANK, n_heads * QK_PAD), lambda i: (0, 0)),
            pl.BlockSpec((tm, LANES), lambda i: (i, 0)),
            pl.BlockSpec((tm, LANES), lambda i: (i, 0)),
        ],
        out_specs=pl.BlockSpec((tm, n_heads * QK_PAD), lambda i: (i, 0)),
        out_shape=jax.ShapeDtypeStruct((t, n_heads * QK_PAD), BF16),
        compiler_params=_cparams(("arbitrary",)),
        name="mla_q_up",
    )(hm, g3, w_uq_p, cos_t, sin_t)


def _kv_up_kernel(h_ref, g_ref, wk_ref, wv_ref, kr_ref, k_ref, v_ref, *, n_heads):
    yb = _rms_bf16(h_ref, g_ref)
    kr = kr_ref[...]
    group = 4
    for h0 in range(0, n_heads, group):
        res = jnp.dot(yb, wk_ref[:, h0 * QK_NOPE_DIM:(h0 + group) * QK_NOPE_DIM],
                      preferred_element_type=F32)
        for hh in range(group):
            o0 = (h0 + hh) * QK_PAD
            k_ref[:, o0:o0 + QK_NOPE_DIM] = res[:, hh * QK_NOPE_DIM:(hh + 1) * QK_NOPE_DIM].astype(BF16)
            k_ref[:, o0 + QK_NOPE_DIM:o0 + QK_PAD] = kr
    v_ref[...] = jnp.dot(yb, wv_ref[...], preferred_element_type=F32).astype(BF16)


def _kv_up(hm, g3, l, w_uk, w_uv, krr, n_heads, tm=256):
    t = hm.shape[0]
    col_blk = Q_LORA_RANK // KV_LORA_RANK
    return pl.pallas_call(
        functools.partial(_kv_up_kernel, n_heads=n_heads),
        grid=(t // tm,),
        in_specs=[
            pl.BlockSpec((tm, KV_LORA_RANK), lambda i: (i, col_blk)),
            pl.BlockSpec((None, 1, KV_LORA_RANK), lambda i: (l, 0, 0)),
            pl.BlockSpec((KV_LORA_RANK, n_heads * QK_NOPE_DIM), lambda i: (0, 0)),
            pl.BlockSpec((KV_LORA_RANK, n_heads * V_HEAD_DIM), lambda i: (0, 0)),
            pl.BlockSpec((tm, LANES), lambda i: (i, 0)),
        ],
        out_specs=[
            pl.BlockSpec((tm, n_heads * QK_PAD), lambda i: (i, 0)),
            pl.BlockSpec((tm, n_heads * V_HEAD_DIM), lambda i: (i, 0)),
        ],
        out_shape=[
            jax.ShapeDtypeStruct((t, n_heads * QK_PAD), BF16),
            jax.ShapeDtypeStruct((t, n_heads * V_HEAD_DIM), BF16),
        ],
        compiler_params=_cparams(("arbitrary",)),
        name="mla_kv_up",
    )(hm, g3, w_uk, w_uv, krr)


MLA_HEADS_PER_STEP = 2


def _mla_attn_kernel(q_ref, k_ref, v_ref, o_ref, *, seq, tq):
    nq = seq // tq
    hp = MLA_HEADS_PER_STEP
    row = lax.broadcasted_iota(jnp.int32, (tq, tq), 0)
    col = lax.broadcasted_iota(jnp.int32, (tq, tq), 1)
    causal = col <= row

    def kv_step(qs, j, carry, masked):
        off = pl.multiple_of(j * tq, tq)
        out = []
        for hh in range(hp):
            m, l, acc = carry[hh]
            k = k_ref[pl.ds(off, tq), hh * QK_PAD:(hh + 1) * QK_PAD]
            v = v_ref[pl.ds(off, tq), hh * V_HEAD_DIM:(hh + 1) * V_HEAD_DIM]
            s = lax.dot_general(qs[hh], k, (((1,), (1,)), ((), ())), preferred_element_type=F32)
            if masked:
                s = jnp.where(causal, s, NEG_BIG)
            m_new = jnp.maximum(m, jnp.max(s, axis=-1, keepdims=True))
            p = jnp.exp2(s - m_new)
            alpha = jnp.exp2(m - m_new)
            l_new = alpha * l + jnp.sum(p, axis=-1, keepdims=True)
            acc_new = alpha * acc + jnp.dot(p.astype(BF16), v, preferred_element_type=F32)
            out.append((m_new, l_new, acc_new))
        return tuple(out)

    def q_loop(i, _):
        qoff = pl.multiple_of(i * tq, tq)
        qs = [q_ref[pl.ds(qoff, tq), hh * QK_PAD:(hh + 1) * QK_PAD] for hh in range(hp)]
        init = tuple((jnp.full((tq, 1), NEG_BIG, F32), jnp.zeros((tq, 1), F32),
                      jnp.zeros((tq, V_HEAD_DIM), F32)) for _ in range(hp))
        carry = lax.fori_loop(0, i, lambda j, c: kv_step(qs, j, c, False), init)
        carry = kv_step(qs, i, carry, True)
        for hh in range(hp):
            m, l, acc = carry[hh]
            o_ref[pl.ds(qoff, tq), hh * V_HEAD_DIM:(hh + 1) * V_HEAD_DIM] = (acc / l).astype(o_ref.dtype)
        return 0

    lax.fori_loop(0, nq, q_loop, 0)


def _mla_attn(q, k, v, batch, seq, n_heads, tq=1024):
    t = q.shape[0]
    hp = MLA_HEADS_PER_STEP
    return pl.pallas_call(
        functools.partial(_mla_attn_kernel, seq=seq, tq=tq),
        grid=(batch, n_heads // hp),
        in_specs=[
            pl.BlockSpec((seq, hp * QK_PAD), lambda b, h: (b, h)),
            pl.BlockSpec((seq, hp * QK_PAD), lambda b, h: (b, h)),
            pl.BlockSpec((seq, hp * V_HEAD_DIM), lambda b, h: (b, h)),
        ],
        out_specs=pl.BlockSpec((seq, hp * V_HEAD_DIM), lambda b, h: (b, h)),
        out_shape=jax.ShapeDtypeStruct((t, n_heads * V_HEAD_DIM), BF16),
        compiler_params=_cparams(("arbitrary", "arbitrary")),
        name="mla_attn",
    )(q, k, v)


DIL_GROUP = 16


def _dil_attn_kernel(qkv_ref, bvec_ref, o_ref, m_s, l_s, acc_s, bias_s, nat_f, *view_s, seq):
    scale = float(HEAD_DIM) ** -0.5 * LOG2E
    views = (qkv_ref,) + tuple(view_s)

    for bi in range(len(DIL_PATTERNS)):
        full = jnp.broadcast_to(bvec_ref[bi] * LOG2E, (BLOCK, 2 * BLOCK))
        bias_s[bi] = pltpu.roll(full, 0, 1, stride=1, stride_axis=0)

    for which in range(3):
        nat_f[...] = qkv_ref[which].astype(F32)
        for bi, (_, d) in enumerate(DIL_PATTERNS):
            if d == 1:
                continue
            for r in range(d):
                views[bi][which, :, r * HEAD_DIM:(r + 1) * HEAD_DIM] = (
                    nat_f[pl.ds(r, seq // d, stride=d), :].astype(BF16))

    def scores(ref, bi, d, r, i, first):
        lanes = slice(r * HEAD_DIM, (r + 1) * HEAD_DIM)
        qoff = pl.multiple_of(i * BLOCK, BLOCK)
        q = ref[0, pl.ds(qoff, BLOCK), lanes]
        if first:
            kk = ref[1, pl.ds(0, BLOCK), lanes]
            vv = ref[2, pl.ds(0, BLOCK), lanes]
            bias = bias_s[bi, :, BLOCK:2 * BLOCK]
        else:
            koff = pl.multiple_of(i * BLOCK - BLOCK, BLOCK)
            kk = ref[1, pl.ds(koff, 2 * BLOCK), lanes]
            vv = ref[2, pl.ds(koff, 2 * BLOCK), lanes]
            bias = bias_s[bi]
        s = lax.dot_general(q, kk, (((1,), (1,)), ((), ())), preferred_element_type=F32)
        s = s * scale + bias
        m_b = jnp.max(s, axis=-1, keepdims=True)
        p = jnp.exp2(s - m_b)
        l_b = jnp.sum(p, axis=-1, keepdims=True)
        a_b = jnp.dot(p.astype(BF16), vv, preferred_element_type=F32)
        m_b = jnp.broadcast_to(m_b, (BLOCK, HEAD_DIM))
        l_b = jnp.broadcast_to(l_b, (BLOCK, HEAD_DIM))
        if d == 1:
            rows = pl.ds(qoff, BLOCK)
        else:
            rows = pl.ds(i * (BLOCK * d) + r, BLOCK, stride=d)
        return m_b, l_b, a_b, rows

    def group(ref, bi, d, blocks):
        parts = [scores(ref, bi, d, r, i, first) for r, i, first in blocks]
        if d == 1:
            for m_b, l_b, a_b, rows in parts:
                m_s[rows, :] = m_b
                l_s[rows, :] = l_b
                acc_s[rows, :] = a_b
            return
        old = [(m_s[rows, :], l_s[rows, :], acc_s[rows, :]) for _, _, _, rows in parts]
        new = []
        for (m_b, l_b, a_b, rows), (m_o, l_o, a_o) in zip(parts, old):
            m_n = jnp.maximum(m_o, m_b)
            e_o = jnp.exp2(m_o - m_n)
            e_b = jnp.exp2(m_b - m_n)
            new.append((m_n, e_o * l_o + e_b * l_b, e_o * a_o + e_b * a_b, rows))
        for m_n, l_n, a_n, rows in new:
            m_s[rows, :] = m_n
            l_s[rows, :] = l_n
            acc_s[rows, :] = a_n

    for bi, (_, d) in enumerate(DIL_PATTERNS):
        ref = views[bi]
        nblk = seq // d // BLOCK
        n_res = min(d, DIL_GROUP)
        n_seq = DIL_GROUP // n_res
        assert d % n_res == 0 and nblk % n_seq == 0
        for r0 in range(0, d, n_res):
            group(ref, bi, d, [(r0 + u, i, i == 0) for i in range(n_seq) for u in range(n_res)])

            def body(t, _, ref=ref, bi=bi, d=d, r0=r0, n_res=n_res, n_seq=n_seq):
                group(ref, bi, d, [(r0 + u, t * n_seq + ii, False)
                                   for ii in range(n_seq) for u in range(n_res)])
                return 0

            lax.fori_loop(1, nblk // n_seq, body, 0)

    o_ref[...] = (acc_s[...] / l_s[...]).astype(o_ref.dtype)


def _dil_attn(qkv, bias_vec, batch, seq, n_heads):
    t = batch * seq
    nbr = len(DIL_PATTERNS)
    assert DIL_PATTERNS[0][1] == 1
    view_scratch = [pltpu.VMEM((3, seq // d, d * HEAD_DIM), BF16) for _, d in DIL_PATTERNS[1:]]
    return pl.pallas_call(
        functools.partial(_dil_attn_kernel, seq=seq),
        grid=(batch, n_heads),
        in_specs=[
            pl.BlockSpec((3, None, None, seq, HEAD_DIM), lambda b, h: (0, b, h, 0, 0)),
            pl.BlockSpec((nbr, None, 1, 2 * BLOCK), lambda b, h: (0, h, 0, 0)),
        ],
        out_specs=pl.BlockSpec((seq, HEAD_DIM), lambda b, h: (b, h)),
        out_shape=jax.ShapeDtypeStruct((t, n_heads * HEAD_DIM), BF16),
        scratch_shapes=[pltpu.VMEM((seq, HEAD_DIM), F32)] * 3
        + [pltpu.VMEM((nbr, BLOCK, 2 * BLOCK), F32), pltpu.VMEM((seq, HEAD_DIM), F32)]
        + view_scratch,
        compiler_params=_cparams(("arbitrary", "arbitrary")),
        name="dil_attn",
    )(qkv, bias_vec)


def _out_proj_kernel(a1_ref, a2_ref, w_ref, o_ref, wb_ref):
    @pl.when(pl.program_id(1) == 0)
    def _():
        wb_ref[...] = w_ref[...].astype(BF16)

    k1 = a1_ref.shape[1]
    acc = jnp.dot(a1_ref[...], wb_ref[:k1, :], preferred_element_type=F32)
    acc = acc + jnp.dot(a2_ref[...], wb_ref[k1:, :], preferred_element_type=F32)
    o_ref[...] = acc.astype(o_ref.dtype)


def _out_proj(o_a, o_b, w_o, l, tm=1024, tn=512):
    t, k1 = o_a.shape
    k2 = o_b.shape[1]
    n = w_o.shape[-1]
    return pl.pallas_call(
        _out_proj_kernel,
        grid=(n // tn, t // tm),
        in_specs=[
            pl.BlockSpec((tm, k1), lambda j, i: (i, 0)),
            pl.BlockSpec((tm, k2), lambda j, i: (i, 0)),
            pl.BlockSpec((None, k1 + k2, tn), lambda j, i: (l, 0, j)),
        ],
        out_specs=pl.BlockSpec((tm, tn), lambda j, i: (i, j)),
        out_shape=jax.ShapeDtypeStruct((t, n), BF16),
        scratch_shapes=[pltpu.VMEM((k1 + k2, tn), BF16)],
        compiler_params=_cparams(("arbitrary", "arbitrary")),
        name="out_proj",
    )(o_a, o_b, w_o)


def _layer_norm_rows(z, g, b):
    mu = jnp.mean(z, axis=-1, keepdims=True)
    zc = z - mu
    var = jnp.mean(zc * zc, axis=-1, keepdims=True)
    return zc * lax.rsqrt(var + LN_EPS) * g + b


SLAB_CHUNK = 1024
SLAB_Q = SLAB_CHUNK // (2 * LANES)


def _slab_sub(width):
    return width // (2 * LANES)


def _slab_index(j, first_tok, n_tok, sub):
    return (pl.ds(first_tok * sub + j, n_tok, stride=sub), slice(None))


def _pack_chunk(vals):
    bits = lax.bitcast_convert_type(vals.astype(BF16).astype(F32), jnp.uint32)
    half = SLAB_CHUNK // 2
    return [(bits[:, q * LANES:(q + 1) * LANES] >> 16)
            | bits[:, half + q * LANES:half + (q + 1) * LANES] for q in range(SLAB_Q)]


def _unpack_words(words):
    lo = lax.bitcast_convert_type(words << 16, F32)
    hi = lax.bitcast_convert_type(words & jnp.uint32(0xFFFF0000), F32)
    return lo, hi


def _ln1_router_kernel(x_ref, mix_ref, mod_ref, g_ref, b_ref, wr_ref, br_ref,
                       x1_ref, u2p_ref, ids_ref, gates_ref, *, alpha):
    gt1 = mod_ref[2:3, :]
    sh2 = mod_ref[3:4, :]
    sc2 = mod_ref[4:5, :]
    z = alpha * x_ref[...] + gt1 * mix_ref[...].astype(F32)
    x1 = _layer_norm_rows(z, g_ref[...], b_ref[...])
    x1_ref[...] = x1
    u2 = (x1 * (1.0 + sc2) + sh2).astype(BF16)
    tm, d = u2.shape
    for g in range(d // SLAB_CHUNK):
        for q, words in enumerate(_pack_chunk(u2[:, g * SLAB_CHUNK:(g + 1) * SLAB_CHUNK])):
            u2p_ref[_slab_index(g * SLAB_Q + q, 0, tm, _slab_sub(d))] = words
    logits = jnp.dot(u2, wr_ref[...].astype(BF16), preferred_element_type=F32) + br_ref[...]
    lane = lax.broadcasted_iota(jnp.int32, logits.shape, 1)
    lane_f = lane.astype(F32)
    vals = []
    ids = []
    for _ in range(TOP_K):
        mk = jnp.max(logits, axis=-1, keepdims=True)
        idx_f = jnp.min(jnp.where(logits == mk, lane_f, float(LANES)), axis=-1, keepdims=True)
        vals.append(mk)
        ids.append(idx_f.astype(jnp.int32))
        logits = jnp.where(lane_f == idx_f, -jnp.inf, logits)
    exps = [jnp.exp(v - vals[0]) for v in vals]
    denom = exps[0]
    for e in exps[1:]:
        denom = denom + e
    ids_out = jnp.zeros(lane.shape, jnp.int32)
    gates_out = jnp.zeros(lane.shape, F32)
    for k in range(TOP_K):
        ids_out = jnp.where(lane == k, ids[k], ids_out)
        gates_out = jnp.where(lane == k, exps[k] / denom, gates_out)
    ids_ref[...] = ids_out
    gates_ref[...] = gates_out


def _ln1_router(x2, mix, mod3, g3, b3, wr_p, br_p, l, seq, alpha, tm=256):
    t, d = x2.shape
    per_b = seq // tm
    return pl.pallas_call(
        functools.partial(_ln1_router_kernel, alpha=alpha),
        grid=(t // tm,),
        in_specs=[
            pl.BlockSpec((tm, d), lambda i: (i, 0)),
            pl.BlockSpec((tm, d), lambda i: (i, 0)),
            pl.BlockSpec((None, 6, d), lambda i: (i // per_b, 0, 0)),
            pl.BlockSpec((None, 1, d), lambda i: (l, 0, 0)),
            pl.BlockSpec((None, 1, d), lambda i: (l, 0, 0)),
            pl.BlockSpec((d, LANES), lambda i: (0, 0)),
            pl.BlockSpec((1, LANES), lambda i: (0, 0)),
        ],
        out_specs=[
            pl.BlockSpec((tm, d), lambda i: (i, 0)),
            pl.BlockSpec((tm * _slab_sub(d), LANES), lambda i: (i, 0)),
            pl.BlockSpec((tm, LANES), lambda i: (i, 0)),
            pl.BlockSpec((tm, LANES), lambda i: (i, 0)),
        ],
        out_shape=[
            jax.ShapeDtypeStruct((t, d), F32),
            jax.ShapeDtypeStruct((t * _slab_sub(d), LANES), jnp.uint32),
            jax.ShapeDtypeStruct((t, LANES), jnp.int32),
            jax.ShapeDtypeStruct((t, LANES), F32),
        ],
        compiler_params=_cparams(("arbitrary",)),
        name="ln1_router",
    )(x2, mix, mod3, g3, b3, wr_p, br_p)


def _moe_row_tiles(n_full, n_tail, tile_fn):
    def full(j, _):
        tile_fn(pl.multiple_of(j * MOE_SUB, MOE_SUB), MOE_SUB)
        return 0

    def tail(k, _):
        tile_fn(pl.multiple_of(n_full * MOE_SUB + k * MOE_TAIL, MOE_TAIL), MOE_TAIL)
        return 0

    tile_fn(0, MOE_SUB)
    lax.fori_loop(1, n_full, full, 0)
    lax.fori_loop(0, n_tail, tail, 0)


def _moe_up_kernel(item_e, item_blk, item_full, item_tail, x_ref, wg_ref, wu_ref, bg_ref, bu_ref,
                   h_ref, wg_s, wu_s):
    w = pl.program_id(0)
    n_full = item_full[w]
    n_tail = item_tail[w]

    @pl.when(n_full + n_tail > 0)
    def _():
        wg_s[...] = wg_ref[...].astype(BF16)
        wu_s[...] = wu_ref[...].astype(BF16)
        bg = bg_ref[...]
        bu = bu_ref[...]

        def tile(off, rows):
            xt = x_ref[pl.ds(off, rows), :]
            glu = jnp.dot(xt, wg_s[...], preferred_element_type=F32) + bg
            lin = jnp.dot(xt, wu_s[...], preferred_element_type=F32) + bu
            glu = jnp.minimum(glu, SWIGLU_LIMIT)
            lin = jnp.clip(lin, -SWIGLU_LIMIT, SWIGLU_LIMIT)
            act = glu * jax.nn.sigmoid(SWIGLU_ALPHA * glu) * (lin + 1.0)
            h_ref[pl.ds(off, rows), :] = act.astype(h_ref.dtype)

        _moe_row_tiles(n_full, n_tail, tile)


def _moe_up(xs, w_gate, w_up, b_gate4, b_up4, item_e, item_blk, item_full, item_tail, l, tf=256):
    r, d = xs.shape
    n_items = r // MOE_ROWS
    ff = w_gate.shape[-1]
    n_chunks = ff // tf

    def cmap(w, c, ifull, itail):
        return jnp.where(ifull[w] + itail[w] > 0, c, n_chunks - 1)

    def wmap(w, c, ie, ib, ifull, itail):
        return (l, ie[w], 0, cmap(w, c, ifull, itail))

    grid_spec = pltpu.PrefetchScalarGridSpec(
        num_scalar_prefetch=4,
        grid=(n_items, n_chunks),
        in_specs=[
            pl.BlockSpec((MOE_ROWS, d), lambda w, c, ie, ib, ifull, itail: (ib[w], 0)),
            pl.BlockSpec((None, None, d, tf), wmap),
            pl.BlockSpec((None, None, d, tf), wmap),
            pl.BlockSpec((None, None, 1, tf), wmap),
            pl.BlockSpec((None, None, 1, tf), wmap),
        ],
        out_specs=pl.BlockSpec(
            (MOE_ROWS, tf),
            lambda w, c, ie, ib, ifull, itail: (ib[w], cmap(w, c, ifull, itail))),
        scratch_shapes=[pltpu.VMEM((d, tf), BF16), pltpu.VMEM((d, tf), BF16)],
    )
    return pl.pallas_call(
        _moe_up_kernel,
        grid_spec=grid_spec,
        out_shape=jax.ShapeDtypeStruct((r, ff), BF16),
        compiler_params=_cparams(("arbitrary", "arbitrary")),
        name="moe_up",
    )(item_e, item_blk, item_full, item_tail, xs, w_gate, w_up, b_gate4, b_up4)


def _moe_down_kernel(item_e, item_blk, item_full, item_tail, h_ref, wd_ref, bd_ref, y_ref, wd_s,
                     *, sub):
    w = pl.program_id(0)
    c = pl.program_id(1)
    n_full = item_full[w]
    n_tail = item_tail[w]

    @pl.when(n_full + n_tail > 0)
    def _():
        wd_s[...] = wd_ref[...].astype(BF16)
        bd = bd_ref[...]

        def tile(off, rows):
            ht = h_ref[pl.ds(off, rows), :]
            y = jnp.dot(ht, wd_s[...], preferred_element_type=F32) + bd
            for q, words in enumerate(_pack_chunk(y)):
                y_ref[_slab_index(c * SLAB_Q + q, off, rows, sub)] = words

        _moe_row_tiles(n_full, n_tail, tile)


def _moe_down(hs, w_down, b_down4, item_e, item_blk, item_full, item_tail, l):
    r, ff = hs.shape
    n_items = r // MOE_ROWS
    d = w_down.shape[-1]
    tn = SLAB_CHUNK
    n_chunks = d // tn
    sub = _slab_sub(d)

    def cmap(w, c, ifull, itail):
        return jnp.where(ifull[w] + itail[w] > 0, c, n_chunks - 1)

    def wmap(w, c, ie, ib, ifull, itail):
        return (l, ie[w], 0, cmap(w, c, ifull, itail))

    grid_spec = pltpu.PrefetchScalarGridSpec(
        num_scalar_prefetch=4,
        grid=(n_items, n_chunks),
        in_specs=[
            pl.BlockSpec((MOE_ROWS, ff), lambda w, c, ie, ib, ifull, itail: (ib[w], 0)),
            pl.BlockSpec((None, None, ff, tn), wmap),
            pl.BlockSpec((None, None, 1, tn), wmap),
        ],
        out_specs=pl.BlockSpec((MOE_ROWS * sub, LANES),
                               lambda w, c, ie, ib, ifull, itail: (ib[w], 0)),
        scratch_shapes=[pltpu.VMEM((ff, tn), BF16)],
    )
    return pl.pallas_call(
        functools.partial(_moe_down_kernel, sub=sub),
        grid_spec=grid_spec,
        out_shape=jax.ShapeDtypeStruct((r * sub, LANES), jnp.uint32),
        compiler_params=_cparams(("arbitrary", "arbitrary")),
        name="moe_down",
    )(item_e, item_blk, item_full, item_tail, hs, w_down, b_down4)


GATHER_UNROLL = 8


def _buffer_pitch(sub):
    return sub + 8 if sub % 16 == 0 else sub


def _slab_gather(idx_ref, src_ref, buf, sem, base_slab, n, sub, wait):
    pitch = _buffer_pitch(sub)

    def copy(r):
        src_row = pl.multiple_of(idx_ref[0, r] * sub, sub)
        dst_row = pl.multiple_of((base_slab + r) * pitch, 8)
        return pltpu.make_async_copy(src_ref.at[pl.ds(src_row, sub), :],
                                     buf.at[pl.ds(dst_row, sub), :], sem)

    def body(r0, _):
        for u in range(GATHER_UNROLL):
            c = copy(r0 * GATHER_UNROLL + u)
            if wait:
                c.wait()
            else:
                c.start(priority=u % 2)
        return 0

    lax.fori_loop(0, n // GATHER_UNROLL, body, 0)


def _dispatch_kernel(vblk, nvalid, tok_ref, tok_next_ref, src_ref, o_ref, buf, sems, *, rows, d):
    s = pl.program_id(0)
    slot = s % 2
    sub = _slab_sub(d)

    @pl.when(s == 0)
    def _():
        _slab_gather(tok_ref, src_ref, buf, sems.at[0], 0, rows, sub, wait=False)

    @pl.when(s + 1 < nvalid[0])
    def _():
        _slab_gather(tok_next_ref, src_ref, buf, sems.at[1 - slot], (1 - slot) * rows, rows, sub,
                     wait=False)

    @pl.when((s < nvalid[0]) | (s == 0))
    def _():
        _slab_gather(tok_ref, src_ref, buf, sems.at[slot], slot * rows, rows, sub, wait=True)
        half = SLAB_CHUNK // 2
        for g in range(d // SLAB_CHUNK):
            for q in range(SLAB_Q):
                idx = _slab_index(g * SLAB_Q + q, slot * rows, rows, _buffer_pitch(sub))
                lo, hi = _unpack_words(buf[idx])
                c0 = g * SLAB_CHUNK + q * LANES
                o_ref[:, c0:c0 + LANES] = lo.astype(BF16)
                o_ref[:, c0 + half:c0 + half + LANES] = hi.astype(BF16)


def _dispatch(u2p, d, row_tok, vblk, nvalid, rows=MOE_BLK):
    r = row_tok.shape[0]
    n_blk = r // rows
    n_steps = vblk.shape[0]
    tok3 = row_tok.reshape(n_blk, 1, rows)
    grid_spec = pltpu.PrefetchScalarGridSpec(
        num_scalar_prefetch=2,
        grid=(n_steps,),
        in_specs=[
            pl.BlockSpec((None, 1, rows), lambda s, vb, nv: (vb[s], 0, 0), memory_space=pltpu.SMEM),
            pl.BlockSpec((None, 1, rows), lambda s, vb, nv: (vb[jnp.minimum(s + 1, n_steps - 1)], 0, 0),
                         memory_space=pltpu.SMEM),
            pl.BlockSpec(memory_space=pl.ANY),
        ],
        out_specs=pl.BlockSpec((rows, d), lambda s, vb, nv: (vb[s], 0)),
        scratch_shapes=[pltpu.VMEM((2 * rows * _buffer_pitch(_slab_sub(d)), LANES), jnp.uint32),
                        pltpu.SemaphoreType.DMA((2,))],
    )
    return pl.pallas_call(
        functools.partial(_dispatch_kernel, rows=rows, d=d),
        grid_spec=grid_spec,
        out_shape=jax.ShapeDtypeStruct((r, d), BF16),
        compiler_params=_cparams(("arbitrary",)),
        name="moe_dispatch",
    )(vblk, nvalid, tok3, tok3, u2p)


def _combine_ln2_kernel(slot_ref, slot_next_ref, x1_ref, ys_ref, gates_ref, mod_ref, g_ref, b_ref,
                        o_ref, buf, ffn_s, sems, *, alpha, tm):
    i = pl.program_id(0)
    n = pl.num_programs(0)
    cur = i % 2
    n_rows = TOP_K * tm
    d = ffn_s.shape[1]
    sub = _slab_sub(d)

    @pl.when(i == 0)
    def _():
        _slab_gather(slot_ref, ys_ref, buf, sems.at[0], 0, n_rows, sub, wait=False)

    @pl.when(i + 1 < n)
    def _():
        _slab_gather(slot_next_ref, ys_ref, buf, sems.at[1 - cur], (1 - cur) * n_rows, n_rows, sub,
                     wait=False)

    _slab_gather(slot_ref, ys_ref, buf, sems.at[cur], cur * n_rows, n_rows, sub, wait=True)

    gates = gates_ref[...]
    gk = [jnp.broadcast_to(gates[:, k:k + 1], (tm, LANES)) for k in range(TOP_K)]
    half = SLAB_CHUNK // 2
    for g in range(d // SLAB_CHUNK):
        for q in range(SLAB_Q):
            lo_acc = None
            hi_acc = None
            for k in range(TOP_K):
                idx = _slab_index(g * SLAB_Q + q, cur * n_rows + k * tm, tm, _buffer_pitch(sub))
                lo, hi = _unpack_words(buf[idx])
                lo_acc = gk[k] * lo if lo_acc is None else lo_acc + gk[k] * lo
                hi_acc = gk[k] * hi if hi_acc is None else hi_acc + gk[k] * hi
            c0 = g * SLAB_CHUNK + q * LANES
            ffn_s[:, c0:c0 + LANES] = lo_acc
            ffn_s[:, c0 + half:c0 + half + LANES] = hi_acc
    gt2 = mod_ref[5:6, :]
    z = alpha * x1_ref[...] + gt2 * ffn_s[...]
    o_ref[...] = _layer_norm_rows(z, g_ref[...], b_ref[...])


def _combine_ln2(x1, ys, slot, gates, mod3, g3, b3, l, seq, alpha, tm=256):
    t, d = x1.shape
    per_b = seq // tm
    n_tiles = t // tm
    slot3 = slot.reshape(n_tiles, tm, TOP_K).transpose(0, 2, 1).reshape(n_tiles, 1, TOP_K * tm)
    return pl.pallas_call(
        functools.partial(_combine_ln2_kernel, alpha=alpha, tm=tm),
        grid=(n_tiles,),
        in_specs=[
            pl.BlockSpec((None, 1, tm * TOP_K), lambda i: (i, 0, 0), memory_space=pltpu.SMEM),
            pl.BlockSpec((None, 1, tm * TOP_K), lambda i: (jnp.minimum(i + 1, n_tiles - 1), 0, 0),
                         memory_space=pltpu.SMEM),
            pl.BlockSpec((tm, d), lambda i: (i, 0)),
            pl.BlockSpec(memory_space=pl.ANY),
            pl.BlockSpec((tm, LANES), lambda i: (i, 0)),
            pl.BlockSpec((None, 6, d), lambda i: (i // per_b, 0, 0)),
            pl.BlockSpec((None, 1, d), lambda i: (l, 0, 0)),
            pl.BlockSpec((None, 1, d), lambda i: (l, 0, 0)),
        ],
        out_specs=pl.BlockSpec((tm, d), lambda i: (i, 0)),
        out_shape=jax.ShapeDtypeStruct((t, d), F32),
        scratch_shapes=[pltpu.VMEM((2 * TOP_K * tm * _buffer_pitch(_slab_sub(d)), LANES), jnp.uint32),
                        pltpu.VMEM((tm, d), F32),
                        pltpu.SemaphoreType.DMA((2,))],
        compiler_params=_cparams(("arbitrary",)),
        name="combine_ln2",
    )(slot3, slot3, x1, ys, gates, mod3, g3, b3)


def _t5_bucket(dist):
    is_small = dist < REL_MAX_EXACT
    nf = jnp.maximum(dist, REL_MAX_EXACT).astype(F32)
    large = REL_MAX_EXACT + (jnp.log(nf / REL_MAX_EXACT) / math.log(REL_MAX_DISTANCE / REL_MAX_EXACT)
                             * (REL_BUCKETS - REL_MAX_EXACT)).astype(jnp.int32)
    large = jnp.minimum(large, REL_BUCKETS - 1)
    return jnp.where(is_small, dist, large)


def _dilated_bias_vectors(rel_bias):
    m = jnp.arange(2 * BLOCK)
    rel = BLOCK - m
    vecs = []
    for window, d in DIL_PATTERNS:
        valid = (rel >= 0) & (rel <= window // d)
        bucket = _t5_bucket(jnp.clip(rel, 0) * d)
        bias = jnp.transpose(rel_bias[bucket].astype(F32), (1, 0))
        vecs.append(jnp.where(valid[None, :], bias, NEG_BIG))
    return jnp.stack(vecs, axis=0)[:, :, None, :]


def _routing_tables(ids4, n_items_max):
    e = ids4.reshape(-1)
    n_rows_real = e.shape[0]
    onehot =(e[:, None] == jnp.arange(N_EXPERTS, dtype=jnp.int32)[None, :]).astype(jnp.int32)
    csum = jnp.cumsum(onehot, axis=0)
    rank = jnp.sum(onehot * csum, axis=1) - 1
    counts = csum[-1]
    nit = (counts + MOE_ROWS - 1) // MOE_ROWS
    cum_items = jnp.cumsum(nit)
    first_item = cum_items - nit
    slot = first_item[e] * MOE_ROWS + rank
    total = cum_items[-1]
    w = jnp.arange(n_items_max, dtype=jnp.int32)
    e_w = jnp.minimum(jnp.searchsorted(cum_items, w, side="right", method="compare_all"),
                      N_EXPERTS - 1).astype(jnp.int32)
    j_w = w - first_item[e_w]
    rows_w = jnp.clip(counts[e_w] - j_w * MOE_ROWS, 0, MOE_ROWS)
    valid = w < total
    rows_w = jnp.where(valid, rows_w, 0)
    tails_per_full = MOE_SUB // MOE_TAIL
    n_full = rows_w // MOE_SUB
    n_tail = (rows_w - n_full * MOE_SUB + MOE_TAIL - 1) // MOE_TAIL
    item_full = jnp.where(n_tail == tails_per_full, n_full + 1, n_full).astype(jnp.int32)
    item_tail = jnp.where(n_tail == tails_per_full, 0, n_tail).astype(jnp.int32)
    last = total - 1
    item_e = jnp.where(valid, e_w, e_w[last]).astype(jnp.int32)
    item_blk = jnp.where(valid, w, last).astype(jnp.int32)
    per_item = MOE_ROWS // MOE_BLK
    blk = jnp.arange(n_items_max * per_item, dtype=jnp.int32)
    blk_valid = ((blk % per_item) * MOE_BLK < rows_w[blk // per_item]).astype(jnp.int32)
    cum_valid = jnp.cumsum(blk_valid)
    nvalid = cum_valid[-1]
    n_steps = n_rows_real // MOE_BLK + n_items_max
    want = jnp.minimum(jnp.arange(n_steps, dtype=jnp.int32) + 1, nvalid)
    vblk = jnp.searchsorted(cum_valid, want, side="left", method="compare_all").astype(jnp.int32)
    return (slot.astype(jnp.int32), item_e, item_blk, item_full, item_tail, vblk,
            nvalid.astype(jnp.int32).reshape(1))


def kernel(x, c, positions, w_ada, b_ada, w_in, g_q, g_kv, w_uq, w_ukv, rel_bias, w_o,
           ln1_g, ln1_b, w_router, b_router, w_gate, b_gate, w_up, b_up, w_down, b_down,
           ln2_g, ln2_b):
    batch, seq, d = x.shape
    depth = w_ada.shape[0]
    t = batch * seq
    n_heads = d // (2 * HEAD_DIM)
    dil_w = n_heads * HEAD_DIM
    alpha = (2.0 * depth) ** 0.25
    assert seq % (BLOCK * max(dd for _, dd in DIL_PATTERNS)) == 0
    assert all(win // dd == BLOCK for win, dd in DIL_PATTERNS)

    inv_freq = 1.0 / (ROPE_THETA ** (jnp.arange(0, QK_ROPE_DIM, 2, dtype=F32) / QK_ROPE_DIM))
    ang = positions.astype(F32)[..., None] * inv_freq
    cos, sin = jnp.cos(ang).reshape(t, -1), jnp.sin(ang).reshape(t, -1)
    zpad = jnp.zeros((t, LANES - QK_ROPE_DIM), F32)
    cos_t = jnp.concatenate([cos, cos, zpad], axis=1)
    sin_t = jnp.concatenate([-sin, sin, zpad], axis=1)

    bias_vec = _dilated_bias_vectors(rel_bias)
    c_pad = jnp.zeros((16, d), F32).at[:batch].set(c)

    x2 = x.reshape(t, d)
    n_items_max = (t * TOP_K) // MOE_ROWS + N_EXPERTS
    mla_cols = Q_LORA_RANK + KV_LORA_RANK

    for l in range(depth):
        mod = _ada(c_pad, w_ada, b_ada.reshape(depth, 1, -1), l)[:batch]
        mod3 = mod.reshape(batch, 6, d)

        w_in_t = jnp.swapaxes(w_in[l], 0, 1)
        wq = w_uq[l].reshape(Q_LORA_RANK, n_heads, QK_NOPE_DIM + QK_ROPE_DIM)
        w_uq_p = jnp.pad(wq, ((0, 0), (0, 0), (0, QK_PAD - QK_NOPE_DIM - QK_ROPE_DIM))
                         ).reshape(Q_LORA_RANK, n_heads * QK_PAD).astype(BF16)
        wkv = w_ukv[l].reshape(KV_LORA_RANK, n_heads, QK_NOPE_DIM + V_HEAD_DIM)
        w_uk = wkv[:, :, :QK_NOPE_DIM].reshape(KV_LORA_RANK, -1).astype(BF16)
        w_uv = wkv[:, :, QK_NOPE_DIM:].reshape(KV_LORA_RANK, -1).astype(BF16)

        u1 = _modulate(x2, mod3, seq, 0, 1)
        hm = _in_proj_mla(u1, w_in_t, mla_cols)
        qkv_d = _in_proj_dil(u1, w_in_t, mla_cols + QK_ROPE_DIM, batch, seq, n_heads)
        krr = _in_proj_krope(u1, w_in_t, mla_cols, cos_t, sin_t)
        q_a = _q_up(hm, g_q.reshape(depth, 1, -1), l, w_uq_p, cos_t, sin_t, n_heads)
        k_a, v_a = _kv_up(hm, g_kv.reshape(depth, 1, -1), l, w_uk, w_uv, krr, n_heads)
        o_a = _mla_attn(q_a, k_a, v_a, batch, seq, n_heads)
        o_b = _dil_attn(qkv_d, bias_vec, batch, seq, n_heads)
        mix = _out_proj(o_a, o_b, w_o, l)

        wr_p = jnp.pad(w_router[l], ((0, 0), (0, LANES - N_EXPERTS)))
        br_p = jnp.concatenate([b_router[l], jnp.full((LANES - N_EXPERTS,), NEG_BIG, F32)])[None, :]
        x1, u2p, ids, gates = _ln1_router(
            x2, mix, mod3, ln1_g.reshape(depth, 1, -1), ln1_b.reshape(depth, 1, -1),
            wr_p, br_p, l, seq, alpha)

        slot, item_e, item_blk, item_full, item_tail, vblk, nvalid = _routing_tables(
            ids[:, :TOP_K], n_items_max)
        tok = jnp.arange(t * TOP_K, dtype=jnp.int32) // TOP_K
        row_tok = jnp.zeros((n_items_max * MOE_ROWS,), jnp.int32).at[slot].set(tok)
        xs = _dispatch(u2p, d, row_tok, vblk, nvalid)
        hs = _moe_up(xs, w_gate, w_up, b_gate.reshape(depth, N_EXPERTS, 1, -1),
                     b_up.reshape(depth, N_EXPERTS, 1, -1), item_e, item_blk, item_full,
                     item_tail, l)
        ys = _moe_down(hs, w_down, b_down.reshape(depth, N_EXPERTS, 1, -1),
                       item_e, item_blk, item_full, item_tail, l)
        x2 = _combine_ln2(x1, ys, slot, gates, mod3, ln2_g.reshape(depth, 1, -1),
                          ln2_b.reshape(depth, 1, -1), l, seq, alpha)

    return x2.reshape(batch, seq, d)
```

```python
import functools
import math

import jax
import jax.numpy as jnp
from jax import lax
from jax.experimental import pallas as pl
from jax.experimental.pallas import tpu as pltpu

F32 = jnp.float32
BF16 = jnp.bfloat16

HEAD_DIM = 128
Q_LORA_RANK = 1536
KV_LORA_RANK = 512
QK_NOPE_DIM = 128
QK_ROPE_DIM = 64
V_HEAD_DIM = 128
ROPE_THETA = 10000.0
DIL_PATTERNS = ((128, 1), (512, 4), (2048, 16))
BLOCK = 128
REL_BUCKETS = 32
REL_MAX_EXACT = REL_BUCKETS // 2
REL_MAX_DISTANCE = 2048
N_EXPERTS = 32
TOP_K = 4
EXPERT_FF = 1536
SWIGLU_LIMIT = 7.0
SWIGLU_ALPHA = 1.702
LN_EPS = 1e-5
RMS_EPS = 1e-6
NEG_BIG = -1e30
LOG2E = math.log2(math.e)

LANES = 128
BF16_SUBLANES = 16
QK_PAD = 256
VMEM_LIMIT = 56 * 1024 * 1024
MOE_ROWS = 1280
MOE_SUB = 512
MOE_TAIL = 128
MOE_BLK = 256


def _cparams(sem):
    return pltpu.CompilerParams(dimension_semantics=sem, vmem_limit_bytes=VMEM_LIMIT)


def _ada_kernel(c_ref, w_ref, b_ref, o_ref):
    c = c_ref[...]
    cond = (c * jax.nn.sigmoid(c)).astype(BF16)
    o_ref[...] = jnp.dot(cond, w_ref[...].astype(BF16), preferred_element_type=F32) + b_ref[...]


def _ada(c_pad, w_ada, b_ada3, l, tn=512):
    rows, d = c_pad.shape
    n = w_ada.shape[-1]
    return pl.pallas_call(
        _ada_kernel,
        grid=(n // tn,),
        in_specs=[
            pl.BlockSpec((rows, d), lambda j: (0, 0)),
            pl.BlockSpec((None, d, tn), lambda j: (l, 0, j)),
            pl.BlockSpec((None, 1, tn), lambda j: (l, 0, j)),
        ],
        out_specs=pl.BlockSpec((rows, tn), lambda j: (0, j)),
        out_shape=jax.ShapeDtypeStruct((rows, n), F32),
        compiler_params=_cparams(("arbitrary",)),
        name="ada_mod",
    )(c_pad, w_ada, b_ada3)


def _modulate_kernel(x_ref, mod_ref, o_ref, *, shift_row, scale_row):
    sh = mod_ref[shift_row:shift_row + 1, :]
    sc = mod_ref[scale_row:scale_row + 1, :]
    o_ref[...] = (x_ref[...] * (1.0 + sc) + sh).astype(BF16)


def _modulate(x2, mod3, seq, shift_row, scale_row, tm=512):
    t, d = x2.shape
    per_b = seq // tm
    return pl.pallas_call(
        functools.partial(_modulate_kernel, shift_row=shift_row, scale_row=scale_row),
        grid=(t // tm,),
        in_specs=[
            pl.BlockSpec((tm, d), lambda i: (i, 0)),
            pl.BlockSpec((None, 6, d), lambda i: (i // per_b, 0, 0)),
        ],
        out_specs=pl.BlockSpec((tm, d), lambda i: (i, 0)),
        out_shape=jax.ShapeDtypeStruct((t, d), BF16),
        compiler_params=_cparams(("arbitrary",)),
        name="modulate",
    )(x2, mod3)


def _rope_block(x, cos_t, sin_t):
    lane = lax.broadcasted_iota(jnp.int32, x.shape, 1)
    half = QK_ROPE_DIM // 2
    swapped = jnp.where(lane < half, pltpu.roll(x, LANES - half, 1), pltpu.roll(x, half, 1))
    return x * cos_t + swapped * sin_t


def _load_weight_rows(wt_ref, row0, wf_s, wb_s):
    n_rows = wf_s.shape[0]
    pltpu.sync_copy(wt_ref.at[pl.ds(pl.multiple_of(row0, 8), n_rows), :], wf_s)
    wb_s[0:n_rows, :] = wf_s[...].astype(BF16)


def _dot_nt(a, w):
    return lax.dot_general(a, w, (((1,), (1,)), ((), ())), preferred_element_type=F32)


def _stream_weight_rows(wt_ref, wf_s, wb_s, sems, row_base):
    j = pl.program_id(0)
    n_j = pl.num_programs(0)
    tn = wb_s.shape[0]

    def rows_copy(jj, slot):
        row0 = pl.multiple_of(row_base + jj * tn, 8)
        return pltpu.make_async_copy(wt_ref.at[pl.ds(row0, tn), :], wf_s.at[slot], sems.at[slot])

    @pl.when(pl.program_id(1) == 0)
    def _():
        @pl.when(j == 0)
        def _():
            rows_copy(0, 0).start()

        rows_copy(j, j % 2).wait()

        @pl.when(j + 1 < n_j)
        def _():
            rows_copy(j + 1, (j + 1) % 2).start()

        wb_s[...] = wf_s[j % 2].astype(BF16)


def _proj_plain_kernel(a_ref, wt_ref, o_ref, wf_s, wb_s, sems, *, row_base):
    _stream_weight_rows(wt_ref, wf_s, wb_s, sems, row_base)
    o_ref[...] = _dot_nt(a_ref[...], wb_s[...]).astype(o_ref.dtype)


def _proj_heads_kernel(a_ref, wt_ref, o_ref, wf_s, wb_s, sems, *, row_base, heads_per_tile):
    _stream_weight_rows(wt_ref, wf_s, wb_s, sems, row_base)
    res = _dot_nt(a_ref[...], wb_s[...])
    for hh in range(heads_per_tile):
        o_ref[hh] = res[:, hh * HEAD_DIM:(hh + 1) * HEAD_DIM].astype(o_ref.dtype)


def _proj_rope_kernel(a_ref, wt_ref, cos_ref, sin_ref, o_ref, wf_s, wb_s, *, row_base):
    @pl.when(pl.program_id(0) == 0)
    def _():
        wb_s[...] = jnp.zeros(wb_s.shape, BF16)
        _load_weight_rows(wt_ref, row_base, wf_s, wb_s)

    res = _dot_nt(a_ref[...], wb_s[...])
    o_ref[...] = _rope_block(res, cos_ref[...], sin_ref[...]).astype(o_ref.dtype)


def _in_proj_mla(u, wt, n_cols, tm=1024, tn=512):
    t, d = u.shape
    return pl.pallas_call(
        functools.partial(_proj_plain_kernel, row_base=0),
        grid=(n_cols // tn, t // tm),
        in_specs=[
            pl.BlockSpec((tm, d), lambda j, i: (i, 0)),
            pl.BlockSpec(memory_space=pl.ANY),
        ],
        out_specs=pl.BlockSpec((tm, tn), lambda j, i: (i, j)),
        out_shape=jax.ShapeDtypeStruct((t, n_cols), BF16),
        scratch_shapes=[pltpu.VMEM((2, tn, d), F32), pltpu.VMEM((tn, d), BF16),
                        pltpu.SemaphoreType.DMA((2,))],
        compiler_params=_cparams(("arbitrary", "arbitrary")),
        name="in_proj_mla",
    )(u, wt)


def _in_proj_dil(u, wt, col0, batch, seq, n_heads, tm=1024, tn=512):
    t, d = u.shape
    hpt = tn // HEAD_DIM
    tiles_per_mat = n_heads // hpt
    per_b = seq // tm
    return pl.pallas_call(
        functools.partial(_proj_heads_kernel, row_base=col0, heads_per_tile=hpt),
        grid=(3 * tiles_per_mat, t // tm),
        in_specs=[
            pl.BlockSpec((tm, d), lambda j, i: (i, 0)),
            pl.BlockSpec(memory_space=pl.ANY),
        ],
        out_specs=pl.BlockSpec(
            (None, None, hpt, tm, HEAD_DIM),
            lambda j, i: (j // tiles_per_mat, i // per_b, j % tiles_per_mat, i % per_b, 0)),
        out_shape=jax.ShapeDtypeStruct((3, batch, n_heads, seq, HEAD_DIM), BF16),
        scratch_shapes=[pltpu.VMEM((2, tn, d), F32), pltpu.VMEM((tn, d), BF16),
                        pltpu.SemaphoreType.DMA((2,))],
        compiler_params=_cparams(("arbitrary", "arbitrary")),
        name="in_proj_dil",
    )(u, wt)


def _in_proj_krope(u, wt, col0, cos_t, sin_t, tm=1024):
    t, d = u.shape
    return pl.pallas_call(
        functools.partial(_proj_rope_kernel, row_base=col0),
        grid=(t // tm,),
        in_specs=[
            pl.BlockSpec((tm, d), lambda i: (i, 0)),
            pl.BlockSpec(memory_space=pl.ANY),
            pl.BlockSpec((tm, LANES), lambda i: (i, 0)),
            pl.BlockSpec((tm, LANES), lambda i: (i, 0)),
        ],
        out_specs=pl.BlockSpec((tm, LANES), lambda i: (i, 0)),
        out_shape=jax.ShapeDtypeStruct((t, LANES), BF16),
        scratch_shapes=[pltpu.VMEM((QK_ROPE_DIM, d), F32), pltpu.VMEM((LANES, d), BF16)],
        compiler_params=_cparams(("arbitrary",)),
        name="in_proj_krope",
    )(u, wt, cos_t, sin_t)


def _rms_bf16(h_ref, g_ref):
    hf = h_ref[...].astype(F32)
    y = hf * lax.rsqrt(jnp.mean(hf * hf, axis=-1, keepdims=True) + RMS_EPS)
    return (y * g_ref[...]).astype(BF16)


def _q_up_kernel(h_ref, g_ref, w_ref, cos_ref, sin_ref, o_ref, *, n_heads, scale):
    yb = _rms_bf16(h_ref, g_ref)
    cos_t = cos_ref[...]
    sin_t = sin_ref[...]
    group = 4
    for h0 in range(0, n_heads, group):
        res = jnp.dot(yb, w_ref[:, h0 * QK_PAD:(h0 + group) * QK_PAD], preferred_element_type=F32)
        for hh in range(group):
            c0 = hh * QK_PAD
            nope = res[:, c0:c0 + QK_NOPE_DIM]
            rp = _rope_block(res[:, c0 + QK_NOPE_DIM:c0 + QK_PAD], cos_t, sin_t)
            o0 = (h0 + hh) * QK_PAD
            o_ref[:, o0:o0 + QK_NOPE_DIM] = (nope * scale).astype(BF16)
            o_ref[:, o0 + QK_NOPE_DIM:o0 + QK_PAD] = (rp * scale).astype(BF16)


def _q_up(hm, g3, l, w_uq_p, cos_t, sin_t, n_heads, tm=512):
    t = hm.shape[0]
    scale = float(QK_NOPE_DIM + QK_ROPE_DIM) ** -0.5 * LOG2E
    return pl.pallas_call(
        functools.partial(_q_up_kernel, n_heads=n_heads, scale=scale),
        grid=(t // tm,),
        in_specs=[
            pl.BlockSpec((tm, Q_LORA_RANK), lambda i: (i, 0)),
            pl.BlockSpec((None, 1, Q_LORA_RANK), lambda i: (l, 0, 0)),
            pl.BlockSpec((Q_LORA_RANK, n_heads * QK_PAD), lambda i: (0, 0)),
            pl.BlockSpec((tm, LANES), lambda i: (i, 0)),
            pl.BlockSpec((tm, LANES), lambda i: (i, 0)),
        ],
        out_specs=pl.BlockSpec((tm, n_heads * QK_PAD), lambda i: (i, 0)),
        out_shape=jax.ShapeDtypeStruct((t, n_heads * QK_PAD), BF16),
        compiler_params=_cparams(("arbitrary",)),
        name="mla_q_up",
    )(hm, g3, w_uq_p, cos_t, sin_t)


def _kv_up_kernel(h_ref, g_ref, wk_ref, wv_ref, kr_ref, k_ref, v_ref, *, n_heads):
    yb = _rms_bf16(h_ref, g_ref)
    kr = kr_ref[...]
    group = 4
    for h0 in range(0, n_heads, group):
        res = jnp.dot(yb, wk_ref[:, h0 * QK_NOPE_DIM:(h0 + group) * QK_NOPE_DIM],
                      preferred_element_type=F32)
        for hh in range(group):
            o0 = (h0 + hh) * QK_PAD
            k_ref[:, o0:o0 + QK_NOPE_DIM] = res[:, hh * QK_NOPE_DIM:(hh + 1) * QK_NOPE_DIM].astype(BF16)
            k_ref[:, o0 + QK_NOPE_DIM:o0 + QK_PAD] = kr
    v_ref[...] = jnp.dot(yb, wv_ref[...], preferred_element_type=F32).astype(BF16)


def _kv_up(hm, g3, l, w_uk, w_uv, krr, n_heads, tm=512):
    t = hm.shape[0]
    col_blk = Q_LORA_RANK // KV_LORA_RANK
    return pl.pallas_call(
        functools.partial(_kv_up_kernel, n_heads=n_heads),
        grid=(t // tm,),
        in_specs=[
            pl.BlockSpec((tm, KV_LORA_RANK), lambda i: (i, col_blk)),
            pl.BlockSpec((None, 1, KV_LORA_RANK), lambda i: (l, 0, 0)),
            pl.BlockSpec((KV_LORA_RANK, n_heads * QK_NOPE_DIM), lambda i: (0, 0)),
            pl.BlockSpec((KV_LORA_RANK, n_heads * V_HEAD_DIM), lambda i: (0, 0)),
            pl.BlockSpec((tm, LANES), lambda i: (i, 0)),
        ],
        out_specs=[
            pl.BlockSpec((tm, n_heads * QK_PAD), lambda i: (i, 0)),
            pl.BlockSpec((tm, n_heads * V_HEAD_DIM), lambda i: (i, 0)),
        ],
        out_shape=[
            jax.ShapeDtypeStruct((t, n_heads * QK_PAD), BF16),
            jax.ShapeDtypeStruct((t, n_heads * V_HEAD_DIM), BF16),
        ],
        compiler_params=_cparams(("arbitrary",)),
        name="mla_kv_up",
    )(hm, g3, w_uk, w_uv, krr)


MLA_HEADS_PER_STEP = 2


def _mla_attn_kernel(q_ref, k_ref, v_ref, o_ref, *, seq, tq):
    nq = seq // tq
    hp = MLA_HEADS_PER_STEP
    row = lax.broadcasted_iota(jnp.int32, (tq, tq), 0)
    col = lax.broadcasted_iota(jnp.int32, (tq, tq), 1)
    causal = col <= row

    def kv_step(qs, j, carry, masked):
        off = pl.multiple_of(j * tq, tq)
        out = []
        for hh in range(hp):
            m, l, acc = carry[hh]
            k = k_ref[pl.ds(off, tq), hh * QK_PAD:(hh + 1) * QK_PAD]
            v = v_ref[pl.ds(off, tq), hh * V_HEAD_DIM:(hh + 1) * V_HEAD_DIM]
            s = lax.dot_general(qs[hh], k, (((1,), (1,)), ((), ())), preferred_element_type=F32)
            if masked:
                s = jnp.where(causal, s, NEG_BIG)
            m_new = jnp.maximum(m, jnp.max(s, axis=-1, keepdims=True))
            p = jnp.exp2(s - m_new)
            alpha = jnp.exp2(m - m_new)
            l_new = alpha * l + jnp.sum(p, axis=-1, keepdims=True)
            acc_new = alpha * acc + jnp.dot(p.astype(BF16), v, preferred_element_type=F32)
            out.append((m_new, l_new, acc_new))
        return tuple(out)

    def q_loop(i, _):
        qoff = pl.multiple_of(i * tq, tq)
        qs = [q_ref[pl.ds(qoff, tq), hh * QK_PAD:(hh + 1) * QK_PAD] for hh in range(hp)]
        init = tuple((jnp.full((tq, 1), NEG_BIG, F32), jnp.zeros((tq, 1), F32),
                      jnp.zeros((tq, V_HEAD_DIM), F32)) for _ in range(hp))
        carry = lax.fori_loop(0, i, lambda j, c: kv_step(qs, j, c, False), init)
        carry = kv_step(qs, i, carry, True)
        for hh in range(hp):
            m, l, acc = carry[hh]
            o_ref[pl.ds(qoff, tq), hh * V_HEAD_DIM:(hh + 1) * V_HEAD_DIM] = (acc / l).astype(o_ref.dtype)
        return 0

    lax.fori_loop(0, nq, q_loop, 0)


def _mla_attn(q, k, v, batch, seq, n_heads, tq=1024):
    t = q.shape[0]
    hp = MLA_HEADS_PER_STEP
    return pl.pallas_call(
        functools.partial(_mla_attn_kernel, seq=seq, tq=tq),
        grid=(batch, n_heads // hp),
        in_specs=[
            pl.BlockSpec((seq, hp * QK_PAD), lambda b, h: (b, h)),
            pl.BlockSpec((seq, hp * QK_PAD), lambda b, h: (b, h)),
            pl.BlockSpec((seq, hp * V_HEAD_DIM), lambda b, h: (b, h)),
        ],
        out_specs=pl.BlockSpec((seq, hp * V_HEAD_DIM), lambda b, h: (b, h)),
        out_shape=jax.ShapeDtypeStruct((t, n_heads * V_HEAD_DIM), BF16),
        compiler_params=_cparams(("arbitrary", "arbitrary")),
        name="mla_attn",
    )(q, k, v)


DIL_GROUP = 16


def _dil_attn_kernel(qkv_ref, bvec_ref, o_ref, m_s, l_s, acc_s, bias_s, nat_f, *view_s, seq):
    scale = float(HEAD_DIM) ** -0.5 * LOG2E
    views = (qkv_ref,) + tuple(view_s)

    for bi in range(len(DIL_PATTERNS)):
        full = jnp.broadcast_to(bvec_ref[bi] * LOG2E, (BLOCK, 2 * BLOCK))
        bias_s[bi] = pltpu.roll(full, 0, 1, stride=1, stride_axis=0)

    for which in range(3):
        nat_f[...] = qkv_ref[which].astype(F32)
        for bi, (_, d) in enumerate(DIL_PATTERNS):
            if d == 1:
                continue
            for r in range(d):
                views[bi][which, :, r * HEAD_DIM:(r + 1) * HEAD_DIM] = (
                    nat_f[pl.ds(r, seq // d, stride=d), :].astype(BF16))

    def scores(ref, bi, d, r, i, first):
        lanes = slice(r * HEAD_DIM, (r + 1) * HEAD_DIM)
        qoff = pl.multiple_of(i * BLOCK, BLOCK)
        q = ref[0, pl.ds(qoff, BLOCK), lanes]
        if first:
            kk = ref[1, pl.ds(0, BLOCK), lanes]
            vv = ref[2, pl.ds(0, BLOCK), lanes]
            bias = bias_s[bi, :, BLOCK:2 * BLOCK]
        else:
            koff = pl.multiple_of(i * BLOCK - BLOCK, BLOCK)
            kk = ref[1, pl.ds(koff, 2 * BLOCK), lanes]
            vv = ref[2, pl.ds(koff, 2 * BLOCK), lanes]
            bias = bias_s[bi]
        s = lax.dot_general(q, kk, (((1,), (1,)), ((), ())), preferred_element_type=F32)
        s = s * scale + bias
        m_b = jnp.max(s, axis=-1, keepdims=True)
        p = jnp.exp2(s - m_b)
        l_b = jnp.sum(p, axis=-1, keepdims=True)
        a_b = jnp.dot(p.astype(BF16), vv, preferred_element_type=F32)
        m_b = jnp.broadcast_to(m_b, (BLOCK, HEAD_DIM))
        l_b = jnp.broadcast_to(l_b, (BLOCK, HEAD_DIM))
        if d == 1:
            rows = pl.ds(qoff, BLOCK)
        else:
            rows = pl.ds(i * (BLOCK * d) + r, BLOCK, stride=d)
        return m_b, l_b, a_b, rows

    def group(ref, bi, d, blocks):
        parts = [scores(ref, bi, d, r, i, first) for r, i, first in blocks]
        if d == 1:
            for m_b, l_b, a_b, rows in parts:
                m_s[rows, :] = m_b
                l_s[rows, :] = l_b
                acc_s[rows, :] = a_b
            return
        old = [(m_s[rows, :], l_s[rows, :], acc_s[rows, :]) for _, _, _, rows in parts]
        new = []
        for (m_b, l_b, a_b, rows), (m_o, l_o, a_o) in zip(parts, old):
            m_n = jnp.maximum(m_o, m_b)
            e_o = jnp.exp2(m_o - m_n)
            e_b = jnp.exp2(m_b - m_n)
            new.append((m_n, e_o * l_o + e_b * l_b, e_o * a_o + e_b * a_b, rows))
        for m_n, l_n, a_n, rows in new:
            m_s[rows, :] = m_n
            l_s[rows, :] = l_n
            acc_s[rows, :] = a_n

    for bi, (_, d) in enumerate(DIL_PATTERNS):
        ref = views[bi]
        nblk = seq // d // BLOCK
        n_res = min(d, DIL_GROUP)
        n_seq = DIL_GROUP // n_res
        assert d % n_res == 0 and nblk % n_seq == 0
        for r0 in range(0, d, n_res):
            group(ref, bi, d, [(r0 + u, i, i == 0) for i in range(n_seq) for u in range(n_res)])

            def body(t, _, ref=ref, bi=bi, d=d, r0=r0, n_res=n_res, n_seq=n_seq):
                group(ref, bi, d, [(r0 + u, t * n_seq + ii, False)
                                   for ii in range(n_seq) for u in range(n_res)])
                return 0

            lax.fori_loop(1, nblk // n_seq, body, 0)

    o_ref[...] = (acc_s[...] / l_s[...]).astype(o_ref.dtype)


def _dil_attn(qkv, bias_vec, batch, seq, n_heads):
    t = batch * seq
    nbr = len(DIL_PATTERNS)
    assert DIL_PATTERNS[0][1] == 1
    view_scratch = [pltpu.VMEM((3, seq // d, d * HEAD_DIM), BF16) for _, d in DIL_PATTERNS[1:]]
    return pl.pallas_call(
        functools.partial(_dil_attn_kernel, seq=seq),
        grid=(batch, n_heads),
        in_specs=[
            pl.BlockSpec((3, None, None, seq, HEAD_DIM), lambda b, h: (0, b, h, 0, 0)),
            pl.BlockSpec((nbr, None, 1, 2 * BLOCK), lambda b, h: (0, h, 0, 0)),
        ],
        out_specs=pl.BlockSpec((seq, HEAD_DIM), lambda b, h: (b, h)),
        out_shape=jax.ShapeDtypeStruct((t, n_heads * HEAD_DIM), BF16),
        scratch_shapes=[pltpu.VMEM((seq, HEAD_DIM), F32)] * 3
        + [pltpu.VMEM((nbr, BLOCK, 2 * BLOCK), F32), pltpu.VMEM((seq, HEAD_DIM), F32)]
        + view_scratch,
        compiler_params=_cparams(("arbitrary", "arbitrary")),
        name="dil_attn",
    )(qkv, bias_vec)


def _out_proj_kernel(a1_ref, a2_ref, w_ref, o_ref, wb_ref):
    @pl.when(pl.program_id(1) == 0)
    def _():
        wb_ref[...] = w_ref[...].astype(BF16)

    k1 = a1_ref.shape[1]
    acc = jnp.dot(a1_ref[...], wb_ref[:k1, :], preferred_element_type=F32)
    acc = acc + jnp.dot(a2_ref[...], wb_ref[k1:, :], preferred_element_type=F32)
    o_ref[...] = acc.astype(o_ref.dtype)


def _out_proj(o_a, o_b, w_o, l, tm=1024, tn=512):
    t, k1 = o_a.shape
    k2 = o_b.shape[1]
    n = w_o.shape[-1]
    return pl.pallas_call(
        _out_proj_kernel,
        grid=(n // tn, t // tm),
        in_specs=[
            pl.BlockSpec((tm, k1), lambda j, i: (i, 0)),
            pl.BlockSpec((tm, k2), lambda j, i: (i, 0)),
            pl.BlockSpec((None, k1 + k2, tn), lambda j, i: (l, 0, j)),
        ],
        out_specs=pl.BlockSpec((tm, tn), lambda j, i: (i, j)),
        out_shape=jax.ShapeDtypeStruct((t, n), BF16),
        scratch_shapes=[pltpu.VMEM((k1 + k2, tn), BF16)],
        compiler_params=_cparams(("arbitrary", "arbitrary")),
        name="out_proj",
    )(o_a, o_b, w_o)


def _layer_norm_rows(z, g, b):
    mu = jnp.mean(z, axis=-1, keepdims=True)
    zc = z - mu
    var = jnp.mean(zc * zc, axis=-1, keepdims=True)
    return zc * lax.rsqrt(var + LN_EPS) * g + b


SLAB_CHUNK = 1024
SLAB_Q = SLAB_CHUNK // (2 * LANES)


def _slab_sub(width):
    return width // (2 * LANES)


def _slab_index(j, first_tok, n_tok, sub):
    return (pl.ds(first_tok * sub + j, n_tok, stride=sub), slice(None))


def _pack_chunk(vals):
    bits = lax.bitcast_convert_type(vals.astype(BF16).astype(F32), jnp.uint32)
    half = SLAB_CHUNK // 2
    return [(bits[:, q * LANES:(q + 1) * LANES] >> 16)
            | bits[:, half + q * LANES:half + (q + 1) * LANES] for q in range(SLAB_Q)]


def _unpack_words(words):
    lo = lax.bitcast_convert_type(words << 16, F32)
    hi = lax.bitcast_convert_type(words & jnp.uint32(0xFFFF0000), F32)
    return lo, hi


def _ln1_router_kernel(x_ref, mix_ref, mod_ref, g_ref, b_ref, wr_ref, br_ref,
                       x1_ref, u2p_ref, ids_ref, gates_ref, *, alpha):
    gt1 = mod_ref[2:3, :]
    sh2 = mod_ref[3:4, :]
    sc2 = mod_ref[4:5, :]
    z = alpha * x_ref[...] + gt1 * mix_ref[...].astype(F32)
    x1 = _layer_norm_rows(z, g_ref[...], b_ref[...])
    x1_ref[...] = x1
    u2 = (x1 * (1.0 + sc2) + sh2).astype(BF16)
    tm, d = u2.shape
    for g in range(d // SLAB_CHUNK):
        for q, words in enumerate(_pack_chunk(u2[:, g * SLAB_CHUNK:(g + 1) * SLAB_CHUNK])):
            u2p_ref[_slab_index(g * SLAB_Q + q, 0, tm, _slab_sub(d))] = words
    logits = jnp.dot(u2, wr_ref[...].astype(BF16), preferred_element_type=F32) + br_ref[...]
    lane = lax.broadcasted_iota(jnp.int32, logits.shape, 1)
    lane_f = lane.astype(F32)
    vals = []
    ids = []
    for _ in range(TOP_K):
        mk = jnp.max(logits, axis=-1, keepdims=True)
        idx_f = jnp.min(jnp.where(logits == mk, lane_f, float(LANES)), axis=-1, keepdims=True)
        vals.append(mk)
        ids.append(idx_f.astype(jnp.int32))
        logits = jnp.where(lane_f == idx_f, -jnp.inf, logits)
    exps = [jnp.exp(v - vals[0]) for v in vals]
    denom = exps[0]
    for e in exps[1:]:
        denom = denom + e
    ids_out = jnp.zeros(lane.shape, jnp.int32)
    gates_out = jnp.zeros(lane.shape, F32)
    for k in range(TOP_K):
        ids_out = jnp.where(lane == k, ids[k], ids_out)
        gates_out = jnp.where(lane == k, exps[k] / denom, gates_out)
    ids_ref[...] = ids_out
    gates_ref[...] = gates_out


def _ln1_router(x2, mix, mod3, g3, b3, wr_p, br_p, l, seq, alpha, tm=256):
    t, d = x2.shape
    per_b = seq // tm
    return pl.pallas_call(
        functools.partial(_ln1_router_kernel, alpha=alpha),
        grid=(t // tm,),
        in_specs=[
            pl.BlockSpec((tm, d), lambda i: (i, 0)),
            pl.BlockSpec((tm, d), lambda i: (i, 0)),
            pl.BlockSpec((None, 6, d), lambda i: (i // per_b, 0, 0)),
            pl.BlockSpec((None, 1, d), lambda i: (l, 0, 0)),
            pl.BlockSpec((None, 1, d), lambda i: (l, 0, 0)),
            pl.BlockSpec((d, LANES), lambda i: (0, 0)),
            pl.BlockSpec((1, LANES), lambda i: (0, 0)),
        ],
        out_specs=[
            pl.BlockSpec((tm, d), lambda i: (i, 0)),
            pl.BlockSpec((tm * _slab_sub(d), LANES), lambda i: (i, 0)),
            pl.BlockSpec((tm, LANES), lambda i: (i, 0)),
            pl.BlockSpec((tm, LANES), lambda i: (i, 0)),
        ],
        out_shape=[
            jax.ShapeDtypeStruct((t, d), F32),
            jax.ShapeDtypeStruct((t * _slab_sub(d), LANES), jnp.uint32),
            jax.ShapeDtypeStruct((t, LANES), jnp.int32),
            jax.ShapeDtypeStruct((t, LANES), F32),
        ],
        compiler_params=_cparams(("arbitrary",)),
        name="ln1_router",
    )(x2, mix, mod3, g3, b3, wr_p, br_p)


def _moe_row_tiles(n_full, n_tail, tile_fn):
    def full(j, _):
        tile_fn(pl.multiple_of(j * MOE_SUB, MOE_SUB), MOE_SUB)
        return 0

    def tail(k, _):
        tile_fn(pl.multiple_of(n_full * MOE_SUB + k * MOE_TAIL, MOE_TAIL), MOE_TAIL)
        return 0

    tile_fn(0, MOE_SUB)
    lax.fori_loop(1, n_full, full, 0)
    lax.fori_loop(0, n_tail, tail, 0)


def _moe_up_kernel(item_e, item_blk, item_full, item_tail, x_ref, wg_ref, wu_ref, bg_ref, bu_ref,
                   h_ref, wg_s, wu_s):
    w = pl.program_id(0)
    n_full = item_full[w]
    n_tail = item_tail[w]

    @pl.when(n_full + n_tail > 0)
    def _():
        wg_s[...] = wg_ref[...].astype(BF16)
        wu_s[...] = wu_ref[...].astype(BF16)
        bg = bg_ref[...]
        bu = bu_ref[...]

        def tile(off, rows):
            xt = x_ref[pl.ds(off, rows), :]
            glu = jnp.dot(xt, wg_s[...], preferred_element_type=F32) + bg
            lin = jnp.dot(xt, wu_s[...], preferred_element_type=F32) + bu
            glu = jnp.minimum(glu, SWIGLU_LIMIT)
            lin = jnp.clip(lin, -SWIGLU_LIMIT, SWIGLU_LIMIT)
            act = glu * jax.nn.sigmoid(SWIGLU_ALPHA * glu) * (lin + 1.0)
            h_ref[pl.ds(off, rows), :] = act.astype(h_ref.dtype)

        _moe_row_tiles(n_full, n_tail, tile)


def _moe_up(xs, w_gate, w_up, b_gate4, b_up4, item_e, item_blk, item_full, item_tail, l, tf=256):
    r, d = xs.shape
    n_items = r // MOE_ROWS
    ff = w_gate.shape[-1]
    n_chunks = ff // tf

    def cmap(w, c, ifull, itail):
        return jnp.where(ifull[w] + itail[w] > 0, c, n_chunks - 1)

    def wmap(w, c, ie, ib, ifull, itail):
        return (l, ie[w], 0, cmap(w, c, ifull, itail))

    grid_spec = pltpu.PrefetchScalarGridSpec(
        num_scalar_prefetch=4,
        grid=(n_items, n_chunks),
        in_specs=[
            pl.BlockSpec((MOE_ROWS, d), lambda w, c, ie, ib, ifull, itail: (ib[w], 0)),
            pl.BlockSpec((None, None, d, tf), wmap),
            pl.BlockSpec((None, None, d, tf), wmap),
            pl.BlockSpec((None, None, 1, tf), wmap),
            pl.BlockSpec((None, None, 1, tf), wmap),
        ],
        out_specs=pl.BlockSpec(
            (MOE_ROWS, tf),
            lambda w, c, ie, ib, ifull, itail: (ib[w], cmap(w, c, ifull, itail))),
        scratch_shapes=[pltpu.VMEM((d, tf), BF16), pltpu.VMEM((d, tf), BF16)],
    )
    return pl.pallas_call(
        _moe_up_kernel,
        grid_spec=grid_spec,
        out_shape=jax.ShapeDtypeStruct((r, ff), BF16),
        compiler_params=_cparams(("arbitrary", "arbitrary")),
        name="moe_up",
    )(item_e, item_blk, item_full, item_tail, xs, w_gate, w_up, b_gate4, b_up4)


def _moe_down_kernel(item_e, item_blk, item_full, item_tail, h_ref, wd_ref, bd_ref, y_ref, wd_s,
                     *, sub):
    w = pl.program_id(0)
    c = pl.program_id(1)
    n_full = item_full[w]
    n_tail = item_tail[w]

    @pl.when(n_full + n_tail > 0)
    def _():
        wd_s[...] = wd_ref[...].astype(BF16)
        bd = bd_ref[...]

        def tile(off, rows):
            ht = h_ref[pl.ds(off, rows), :]
            y = jnp.dot(ht, wd_s[...], preferred_element_type=F32) + bd
            for q, words in enumerate(_pack_chunk(y)):
                y_ref[_slab_index(c * SLAB_Q + q, off, rows, sub)] = words

        _moe_row_tiles(n_full, n_tail, tile)


def _moe_down(hs, w_down, b_down4, item_e, item_blk, item_full, item_tail, l):
    r, ff = hs.shape
    n_items = r // MOE_ROWS
    d = w_down.shape[-1]
    tn = SLAB_CHUNK
    n_chunks = d // tn
    sub = _slab_sub(d)

    def cmap(w, c, ifull, itail):
        return jnp.where(ifull[w] + itail[w] > 0, c, n_chunks - 1)

    def wmap(w, c, ie, ib, ifull, itail):
        return (l, ie[w], 0, cmap(w, c, ifull, itail))

    grid_spec = pltpu.PrefetchScalarGridSpec(
        num_scalar_prefetch=4,
        grid=(n_items, n_chunks),
        in_specs=[
            pl.BlockSpec((MOE_ROWS, ff), lambda w, c, ie, ib, ifull, itail: (ib[w], 0)),
            pl.BlockSpec((None, None, ff, tn), wmap),
            pl.BlockSpec((None, None, 1, tn), wmap),
        ],
        out_specs=pl.BlockSpec((MOE_ROWS * sub, LANES),
                               lambda w, c, ie, ib, ifull, itail: (ib[w], 0)),
        scratch_shapes=[pltpu.VMEM((ff, tn), BF16)],
    )
    return pl.pallas_call(
        functools.partial(_moe_down_kernel, sub=sub),
        grid_spec=grid_spec,
        out_shape=jax.ShapeDtypeStruct((r * sub, LANES), jnp.uint32),
        compiler_params=_cparams(("arbitrary", "arbitrary")),
        name="moe_down",
    )(item_e, item_blk, item_full, item_tail, hs, w_down, b_down4)


GATHER_UNROLL = 8


def _buffer_pitch(sub):
    return sub + 8 if sub % 16 == 0 else sub


def _slab_gather(idx_ref, src_ref, buf, sem, base_slab, n, sub, wait):
    pitch = _buffer_pitch(sub)

    def copy(r):
        src_row = pl.multiple_of(idx_ref[0, r] * sub, sub)
        dst_row = pl.multiple_of((base_slab + r) * pitch, 8)
        return pltpu.make_async_copy(src_ref.at[pl.ds(src_row, sub), :],
                                     buf.at[pl.ds(dst_row, sub), :], sem)

    def body(r0, _):
        for u in range(GATHER_UNROLL):
            c = copy(r0 * GATHER_UNROLL + u)
            if wait:
                c.wait()
            else:
                c.start(priority=u % 2)
        return 0

    lax.fori_loop(0, n // GATHER_UNROLL, body, 0)


def _dispatch_kernel(vblk, nvalid, tok_ref, tok_next_ref, src_ref, o_ref, buf, sems, *, rows, d):
    s = pl.program_id(0)
    slot = s % 2
    sub = _slab_sub(d)

    @pl.when(s == 0)
    def _():
        _slab_gather(tok_ref, src_ref, buf, sems.at[0], 0, rows, sub, wait=False)

    @pl.when(s + 1 < nvalid[0])
    def _():
        _slab_gather(tok_next_ref, src_ref, buf, sems.at[1 - slot], (1 - slot) * rows, rows, sub,
                     wait=False)

    @pl.when((s < nvalid[0]) | (s == 0))
    def _():
        _slab_gather(tok_ref, src_ref, buf, sems.at[slot], slot * rows, rows, sub, wait=True)
        half = SLAB_CHUNK // 2
        for g in range(d // SLAB_CHUNK):
            for q in range(SLAB_Q):
                idx = _slab_index(g * SLAB_Q + q, slot * rows, rows, _buffer_pitch(sub))
                lo, hi = _unpack_words(buf[idx])
                c0 = g * SLAB_CHUNK + q * LANES
                o_ref[:, c0:c0 + LANES] = lo.astype(BF16)
                o_ref[:, c0 + half:c0 + half + LANES] = hi.astype(BF16)


def _dispatch(u2p, d, row_tok, vblk, nvalid, rows=MOE_BLK):
    r = row_tok.shape[0]
    n_blk = r // rows
    n_steps = vblk.shape[0]
    tok3 = row_tok.reshape(n_blk, 1, rows)
    grid_spec = pltpu.PrefetchScalarGridSpec(
        num_scalar_prefetch=2,
        grid=(n_steps,),
        in_specs=[
            pl.BlockSpec((None, 1, rows), lambda s, vb, nv: (vb[s], 0, 0), memory_space=pltpu.SMEM),
            pl.BlockSpec((None, 1, rows), lambda s, vb, nv: (vb[jnp.minimum(s + 1, n_steps - 1)], 0, 0),
                         memory_space=pltpu.SMEM),
            pl.BlockSpec(memory_space=pl.ANY),
        ],
        out_specs=pl.BlockSpec((rows, d), lambda s, vb, nv: (vb[s], 0)),
        scratch_shapes=[pltpu.VMEM((2 * rows * _buffer_pitch(_slab_sub(d)), LANES), jnp.uint32),
                        pltpu.SemaphoreType.DMA((2,))],
    )
    return pl.pallas_call(
        functools.partial(_dispatch_kernel, rows=rows, d=d),
        grid_spec=grid_spec,
        out_shape=jax.ShapeDtypeStruct((r, d), BF16),
        compiler_params=_cparams(("arbitrary",)),
        name="moe_dispatch",
    )(vblk, nvalid, tok3, tok3, u2p)


def _combine_ln2_kernel(slot_ref, slot_next_ref, x1_ref, ys_ref, gates_ref, mod_ref, g_ref, b_ref,
                        o_ref, buf, ffn_s, sems, *, alpha, tm):
    i = pl.program_id(0)
    n = pl.num_programs(0)
    cur = i % 2
    n_rows = TOP_K * tm
    d = ffn_s.shape[1]
    sub = _slab_sub(d)

    @pl.when(i == 0)
    def _():
        _slab_gather(slot_ref, ys_ref, buf, sems.at[0], 0, n_rows, sub, wait=False)

    @pl.when(i + 1 < n)
    def _():
        _slab_gather(slot_next_ref, ys_ref, buf, sems.at[1 - cur], (1 - cur) * n_rows, n_rows, sub,
                     wait=False)

    _slab_gather(slot_ref, ys_ref, buf, sems.at[cur], cur * n_rows, n_rows, sub, wait=True)

    gates = gates_ref[...]
    gk = [jnp.broadcast_to(gates[:, k:k + 1], (tm, LANES)) for k in range(TOP_K)]
    half = SLAB_CHUNK // 2
    for g in range(d // SLAB_CHUNK):
        for q in range(SLAB_Q):
            lo_acc = None
            hi_acc = None
            for k in range(TOP_K):
                idx = _slab_index(g * SLAB_Q + q, cur * n_rows + k * tm, tm, _buffer_pitch(sub))
                lo, hi = _unpack_words(buf[idx])
                lo_acc = gk[k] * lo if lo_acc is None else lo_acc + gk[k] * lo
                hi_acc = gk[k] * hi if hi_acc is None else hi_acc + gk[k] * hi
            c0 = g * SLAB_CHUNK + q * LANES
            ffn_s[:, c0:c0 + LANES] = lo_acc
            ffn_s[:, c0 + half:c0 + half + LANES] = hi_acc
    gt2 = mod_ref[5:6, :]
    z = alpha * x1_ref[...] + gt2 * ffn_s[...]
    o_ref[...] = _layer_norm_rows(z, g_ref[...], b_ref[...])


def _combine_ln2(x1, ys, slot, gates, mod3, g3, b3, l, seq, alpha, tm=256):
    t, d = x1.shape
    per_b = seq // tm
    n_tiles = t // tm
    slot3 = slot.reshape(n_tiles, tm, TOP_K).transpose(0, 2, 1).reshape(n_tiles, 1, TOP_K * tm)
    return pl.pallas_call(
        functools.partial(_combine_ln2_kernel, alpha=alpha, tm=tm),
        grid=(n_tiles,),
        in_specs=[
            pl.BlockSpec((None, 1, tm * TOP_K), lambda i: (i, 0, 0), memory_space=pltpu.SMEM),
            pl.BlockSpec((None, 1, tm * TOP_K), lambda i: (jnp.minimum(i + 1, n_tiles - 1), 0, 0),
                         memory_space=pltpu.SMEM),
            pl.BlockSpec((tm, d), lambda i: (i, 0)),
            pl.BlockSpec(memory_space=pl.ANY),
            pl.BlockSpec((tm, LANES), lambda i: (i, 0)),
            pl.BlockSpec((None, 6, d), lambda i: (i // per_b, 0, 0)),
            pl.BlockSpec((None, 1, d), lambda i: (l, 0, 0)),
            pl.BlockSpec((None, 1, d), lambda i: (l, 0, 0)),
        ],
        out_specs=pl.BlockSpec((tm, d), lambda i: (i, 0)),
        out_shape=jax.ShapeDtypeStruct((t, d), F32),
        scratch_shapes=[pltpu.VMEM((2 * TOP_K * tm * _buffer_pitch(_slab_sub(d)), LANES), jnp.uint32),
                        pltpu.VMEM((tm, d), F32),
                        pltpu.SemaphoreType.DMA((2,))],
        compiler_params=_cparams(("arbitrary",)),
        name="combine_ln2",
    )(slot3, slot3, x1, ys, gates, mod3, g3, b3)


def _t5_bucket(dist):
    is_small = dist < REL_MAX_EXACT
    nf = jnp.maximum(dist, REL_MAX_EXACT).astype(F32)
    large = REL_MAX_EXACT + (jnp.log(nf / REL_MAX_EXACT) / math.log(REL_MAX_DISTANCE / REL_MAX_EXACT)
                             * (REL_BUCKETS - REL_MAX_EXACT)).astype(jnp.int32)
    large = jnp.minimum(large, REL_BUCKETS - 1)
    return jnp.where(is_small, dist, large)


def _dilated_bias_vectors(rel_bias):
    m = jnp.arange(2 * BLOCK)
    rel = BLOCK - m
    vecs = []
    for window, d in DIL_PATTERNS:
        valid = (rel >= 0) & (rel <= window // d)
        bucket = _t5_bucket(jnp.clip(rel, 0) * d)
        bias = jnp.transpose(rel_bias[bucket].astype(F32), (1, 0))
        vecs.append(jnp.where(valid[None, :], bias, NEG_BIG))
    return jnp.stack(vecs, axis=0)[:, :, None, :]


def _routing_tables(ids4, n_items_max):
    e = ids4.reshape(-1)
    n_rows_real = e.shape[0]
    onehot =(e[:, None] == jnp.arange(N_EXPERTS, dtype=jnp.int32)[None, :]).astype(jnp.int32)
    csum = jnp.cumsum(onehot, axis=0)
    rank = jnp.sum(onehot * csum, axis=1) - 1
    counts = csum[-1]
    nit = (counts + MOE_ROWS - 1) // MOE_ROWS
    cum_items = jnp.cumsum(nit)
    first_item = cum_items - nit
    slot = first_item[e] * MOE_ROWS + rank
    total = cum_items[-1]
    w = jnp.arange(n_items_max, dtype=jnp.int32)
    e_w = jnp.minimum(jnp.searchsorted(cum_items, w, side="right", method="compare_all"),
                      N_EXPERTS - 1).astype(jnp.int32)
    j_w = w - first_item[e_w]
    rows_w = jnp.clip(counts[e_w] - j_w * MOE_ROWS, 0, MOE_ROWS)
    valid = w < total
    rows_w = jnp.where(valid, rows_w, 0)
    tails_per_full = MOE_SUB // MOE_TAIL
    n_full = rows_w // MOE_SUB
    n_tail = (rows_w - n_full * MOE_SUB + MOE_TAIL - 1) // MOE_TAIL
    item_full = jnp.where(n_tail == tails_per_full, n_full + 1, n_full).astype(jnp.int32)
    item_tail = jnp.where(n_tail == tails_per_full, 0, n_tail).astype(jnp.int32)
    last = total - 1
    item_e = jnp.where(valid, e_w, e_w[last]).astype(jnp.int32)
    item_blk = jnp.where(valid, w, last).astype(jnp.int32)
    per_item = MOE_ROWS // MOE_BLK
    blk = jnp.arange(n_items_max * per_item, dtype=jnp.int32)
    blk_valid = ((blk % per_item) * MOE_BLK < rows_w[blk // per_item]).astype(jnp.int32)
    cum_valid = jnp.cumsum(blk_valid)
    nvalid = cum_valid[-1]
    n_steps = n_rows_real // MOE_BLK + n_items_max
    want = jnp.minimum(jnp.arange(n_steps, dtype=jnp.int32) + 1, nvalid)
    vblk = jnp.searchsorted(cum_valid, want, side="left", method="compare_all").astype(jnp.int32)
    return (slot.astype(jnp.int32), item_e, item_blk, item_full, item_tail, vblk,
            nvalid.astype(jnp.int32).reshape(1))


def kernel(x, c, positions, w_ada, b_ada, w_in, g_q, g_kv, w_uq, w_ukv, rel_bias, w_o,
           ln1_g, ln1_b, w_router, b_router, w_gate, b_gate, w_up, b_up, w_down, b_down,
           ln2_g, ln2_b):
    batch, seq, d = x.shape
    depth = w_ada.shape[0]
    t = batch * seq
    n_heads = d // (2 * HEAD_DIM)
    dil_w = n_heads * HEAD_DIM
    alpha = (2.0 * depth) ** 0.25
    assert seq % (BLOCK * max(dd for _, dd in DIL_PATTERNS)) == 0
    assert all(win // dd == BLOCK for win, dd in DIL_PATTERNS)

    inv_freq = 1.0 / (ROPE_THETA ** (jnp.arange(0, QK_ROPE_DIM, 2, dtype=F32) / QK_ROPE_DIM))
    ang = positions.astype(F32)[..., None] * inv_freq
    cos, sin = jnp.cos(ang).reshape(t, -1), jnp.sin(ang).reshape(t, -1)
    zpad = jnp.zeros((t, LANES - QK_ROPE_DIM), F32)
    cos_t = jnp.concatenate([cos, cos, zpad], axis=1)
    sin_t = jnp.concatenate([-sin, sin, zpad], axis=1)

    bias_vec = _dilated_bias_vectors(rel_bias)
    c_pad = jnp.zeros((BF16_SUBLANES, d), F32).at[:batch].set(c)

    x2 = x.reshape(t, d)
    n_items_max = (t * TOP_K) // MOE_ROWS + N_EXPERTS
    mla_cols = Q_LORA_RANK + KV_LORA_RANK

    for l in range(depth):
        mod = _ada(c_pad, w_ada, b_ada.reshape(depth, 1, -1), l)[:batch]
        mod3 = mod.reshape(batch, 6, d)

        w_in_t = jnp.swapaxes(w_in[l], 0, 1)
        wq = w_uq[l].reshape(Q_LORA_RANK, n_heads, QK_NOPE_DIM + QK_ROPE_DIM)
        w_uq_p = jnp.pad(wq, ((0, 0), (0, 0), (0, QK_PAD - QK_NOPE_DIM - QK_ROPE_DIM))
                         ).reshape(Q_LORA_RANK, n_heads * QK_PAD).astype(BF16)
        wkv = w_ukv[l].reshape(KV_LORA_RANK, n_heads, QK_NOPE_DIM + V_HEAD_DIM)
        w_uk = wkv[:, :, :QK_NOPE_DIM].reshape(KV_LORA_RANK, -1).astype(BF16)
        w_uv = wkv[:, :, QK_NOPE_DIM:].reshape(KV_LORA_RANK, -1).astype(BF16)

        u1 = _modulate(x2, mod3, seq, 0, 1)
        hm = _in_proj_mla(u1, w_in_t, mla_cols)
        qkv_d = _in_proj_dil(u1, w_in_t, mla_cols + QK_ROPE_DIM, batch, seq, n_heads)
        krr = _in_proj_krope(u1, w_in_t, mla_cols, cos_t, sin_t)
        q_a = _q_up(hm, g_q.reshape(depth, 1, -1), l, w_uq_p, cos_t, sin_t, n_heads)
        k_a, v_a = _kv_up(hm, g_kv.reshape(depth, 1, -1), l, w_uk, w_uv, krr, n_heads)
        o_a = _mla_attn(q_a, k_a, v_a, batch, seq, n_heads)
        o_b = _dil_attn(qkv_d, bias_vec, batch, seq, n_heads)
        mix = _out_proj(o_a, o_b, w_o, l)

        wr_p = jnp.pad(w_router[l], ((0, 0), (0, LANES - N_EXPERTS)))
        br_p = jnp.concatenate([b_router[l], jnp.full((LANES - N_EXPERTS,), NEG_BIG, F32)])[None, :]
        x1, u2p, ids, gates = _ln1_router(
            x2, mix, mod3, ln1_g.reshape(depth, 1, -1), ln1_b.reshape(depth, 1, -1),
            wr_p, br_p, l, seq, alpha)

        slot, item_e, item_blk, item_full, item_tail, vblk, nvalid = _routing_tables(
            ids[:, :TOP_K], n_items_max)
        tok = jnp.arange(t * TOP_K, dtype=jnp.int32) // TOP_K
        row_tok = jnp.zeros((n_items_max * MOE_ROWS,), jnp.int32).at[slot].set(tok)
        xs = _dispatch(u2p, d, row_tok, vblk, nvalid)
        hs = _moe_up(xs, w_gate, w_up, b_gate.reshape(depth, N_EXPERTS, 1, -1),
                     b_up.reshape(depth, N_EXPERTS, 1, -1), item_e, item_blk, item_full,
                     item_tail, l)
        ys = _moe_down(hs, w_down, b_down.reshape(depth, N_EXPERTS, 1, -1),
                       item_e, item_blk, item_full, item_tail, l)
        x2 = _combine_ln2(x1, ys, slot, gates, mod3, ln2_g.reshape(depth, 1, -1),
                          ln2_b.reshape(depth, 1, -1), l, seq, alpha)

    return x2.reshape(batch, seq, d)
```

```python
import functools
import math

import jax
import jax.numpy as jnp
from jax import lax
from jax.experimental import pallas as pl
from jax.experimental.pallas import tpu as pltpu

F32 = jnp.float32
BF16 = jnp.bfloat16

HEAD_DIM = 128
Q_LORA_RANK = 1536
KV_LORA_RANK = 512
QK_NOPE_DIM = 128
QK_ROPE_DIM = 64
V_HEAD_DIM = 128
ROPE_THETA = 10000.0
DIL_PATTERNS = ((128, 1), (512, 4), (2048, 16))
BLOCK = 128
REL_BUCKETS = 32
REL_MAX_EXACT = REL_BUCKETS // 2
REL_MAX_DISTANCE = 2048
N_EXPERTS = 32
TOP_K = 4
EXPERT_FF = 1536
SWIGLU_LIMIT = 7.0
SWIGLU_ALPHA = 1.702
LN_EPS = 1e-5
RMS_EPS = 1e-6
NEG_BIG = -1e30
LOG2E = math.log2(math.e)

LANES = 128
BF16_SUBLANES = 16
QK_PAD = 256
VMEM_LIMIT = 56 * 1024 * 1024
MOE_ROWS = 1280
MOE_SUB = 512
MOE_TAIL = 128
MOE_BLK = 256


def _cparams(sem):
    return pltpu.CompilerParams(dimension_semantics=sem, vmem_limit_bytes=VMEM_LIMIT)


def _ada_kernel(c_ref, w_ref, b_ref, o_ref):
    c = c_ref[...]
    cond = (c * jax.nn.sigmoid(c)).astype(BF16)
    o_ref[...] = jnp.dot(cond, w_ref[...].astype(BF16), preferred_element_type=F32) + b_ref[...]


def _ada(c_pad, w_ada, b_ada3, l, tn=512):
    rows, d = c_pad.shape
    n = w_ada.shape[-1]
    return pl.pallas_call(
        _ada_kernel,
        grid=(n // tn,),
        in_specs=[
            pl.BlockSpec((rows, d), lambda j: (0, 0)),
            pl.BlockSpec((None, d, tn), lambda j: (l, 0, j)),
            pl.BlockSpec((None, 1, tn), lambda j: (l, 0, j)),
        ],
        out_specs=pl.BlockSpec((rows, tn), lambda j: (0, j)),
        out_shape=jax.ShapeDtypeStruct((rows, n), F32),
        compiler_params=_cparams(("arbitrary",)),
        name="ada_mod",
    )(c_pad, w_ada, b_ada3)


def _modulate_kernel(x_ref, mod_ref, o_ref, *, shift_row, scale_row):
    sh = mod_ref[shift_row:shift_row + 1, :]
    sc = mod_ref[scale_row:scale_row + 1, :]
    o_ref[...] = (x_ref[...] * (1.0 + sc) + sh).astype(BF16)


def _modulate(x2, mod3, seq, shift_row, scale_row, tm=512):
    t, d = x2.shape
    per_b = seq // tm
    return pl.pallas_call(
        functools.partial(_modulate_kernel, shift_row=shift_row, scale_row=scale_row),
        grid=(t // tm,),
        in_specs=[
            pl.BlockSpec((tm, d), lambda i: (i, 0)),
            pl.BlockSpec((None, 6, d), lambda i: (i // per_b, 0, 0)),
        ],
        out_specs=pl.BlockSpec((tm, d), lambda i: (i, 0)),
        out_shape=jax.ShapeDtypeStruct((t, d), BF16),
        compiler_params=_cparams(("arbitrary",)),
        name="modulate",
    )(x2, mod3)


def _rope_block(x, cos_t, sin_t):
    lane = lax.broadcasted_iota(jnp.int32, x.shape, 1)
    half = QK_ROPE_DIM // 2
    swapped = jnp.where(lane < half, pltpu.roll(x, LANES - half, 1), pltpu.roll(x, half, 1))
    return x * cos_t + swapped * sin_t


def _load_weight_rows(wt_ref, row0, wf_s, wb_s):
    n_rows = wf_s.shape[0]
    pltpu.sync_copy(wt_ref.at[pl.ds(pl.multiple_of(row0, 8), n_rows), :], wf_s)
    wb_s[0:n_rows, :] = wf_s[...].astype(BF16)


def _dot_nt(a, w):
    return lax.dot_general(a, w, (((1,), (1,)), ((), ())), preferred_element_type=F32)


def _stream_weight_rows(wt_ref, wf_s, wb_s, sems, row_base):
    j = pl.program_id(0)
    n_j = pl.num_programs(0)
    tn = wb_s.shape[0]

    def rows_copy(jj, slot):
        row0 = pl.multiple_of(row_base + jj * tn, 8)
        return pltpu.make_async_copy(wt_ref.at[pl.ds(row0, tn), :], wf_s.at[slot], sems.at[slot])

    @pl.when(pl.program_id(1) == 0)
    def _():
        @pl.when(j == 0)
        def _():
            rows_copy(0, 0).start()

        rows_copy(j, j % 2).wait()

        @pl.when(j + 1 < n_j)
        def _():
            rows_copy(j + 1, (j + 1) % 2).start()

        wb_s[...] = wf_s[j % 2].astype(BF16)


def _proj_plain_kernel(a_ref, wt_ref, o_ref, wf_s, wb_s, sems, *, row_base):
    _stream_weight_rows(wt_ref, wf_s, wb_s, sems, row_base)
    o_ref[...] = _dot_nt(a_ref[...], wb_s[...]).astype(o_ref.dtype)


def _proj_heads_kernel(a_ref, wt_ref, o_ref, wf_s, wb_s, sems, *, row_base, heads_per_tile):
    _stream_weight_rows(wt_ref, wf_s, wb_s, sems, row_base)
    res = _dot_nt(a_ref[...], wb_s[...])
    for hh in range(heads_per_tile):
        o_ref[hh] = res[:, hh * HEAD_DIM:(hh + 1) * HEAD_DIM].astype(o_ref.dtype)


def _proj_rope_kernel(a_ref, wt_ref, cos_ref, sin_ref, o_ref, wf_s, wb_s, *, row_base):
    @pl.when(pl.program_id(0) == 0)
    def _():
        wb_s[...] = jnp.zeros(wb_s.shape, BF16)
        _load_weight_rows(wt_ref, row_base, wf_s, wb_s)

    res = _dot_nt(a_ref[...], wb_s[...])
    o_ref[...] = _rope_block(res, cos_ref[...], sin_ref[...]).astype(o_ref.dtype)


def _in_proj_mla(u, wt, n_cols, tm=1024, tn=512):
    t, d = u.shape
    return pl.pallas_call(
        functools.partial(_proj_plain_kernel, row_base=0),
        grid=(n_cols // tn, t // tm),
        in_specs=[
            pl.BlockSpec((tm, d), lambda j, i: (i, 0)),
            pl.BlockSpec(memory_space=pl.ANY),
        ],
        out_specs=pl.BlockSpec((tm, tn), lambda j, i: (i, j)),
        out_shape=jax.ShapeDtypeStruct((t, n_cols), BF16),
        scratch_shapes=[pltpu.VMEM((2, tn, d), F32), pltpu.VMEM((tn, d), BF16),
                        pltpu.SemaphoreType.DMA((2,))],
        compiler_params=_cparams(("arbitrary", "arbitrary")),
        name="in_proj_mla",
    )(u, wt)


def _in_proj_dil(u, wt, col0, batch, seq, n_heads, tm=1024, tn=512):
    t, d = u.shape
    hpt = tn // HEAD_DIM
    tiles_per_mat = n_heads // hpt
    per_b = seq // tm
    return pl.pallas_call(
        functools.partial(_proj_heads_kernel, row_base=col0, heads_per_tile=hpt),
        grid=(3 * tiles_per_mat, t // tm),
        in_specs=[
            pl.BlockSpec((tm, d), lambda j, i: (i, 0)),
            pl.BlockSpec(memory_space=pl.ANY),
        ],
        out_specs=pl.BlockSpec(
            (None, None, hpt, tm, HEAD_DIM),
            lambda j, i: (j // tiles_per_mat, i // per_b, j % tiles_per_mat, i % per_b, 0)),
        out_shape=jax.ShapeDtypeStruct((3, batch, n_heads, seq, HEAD_DIM), BF16),
        scratch_shapes=[pltpu.VMEM((2, tn, d), F32), pltpu.VMEM((tn, d), BF16),
                        pltpu.SemaphoreType.DMA((2,))],
        compiler_params=_cparams(("arbitrary", "arbitrary")),
        name="in_proj_dil",
    )(u, wt)


def _in_proj_krope(u, wt, col0, cos_t, sin_t, tm=1024):
    t, d = u.shape
    return pl.pallas_call(
        functools.partial(_proj_rope_kernel, row_base=col0),
        grid=(t // tm,),
        in_specs=[
            pl.BlockSpec((tm, d), lambda i: (i, 0)),
            pl.BlockSpec(memory_space=pl.ANY),
            pl.BlockSpec((tm, LANES), lambda i: (i, 0)),
            pl.BlockSpec((tm, LANES), lambda i: (i, 0)),
        ],
        out_specs=pl.BlockSpec((tm, LANES), lambda i: (i, 0)),
        out_shape=jax.ShapeDtypeStruct((t, LANES), BF16),
        scratch_shapes=[pltpu.VMEM((QK_ROPE_DIM, d), F32), pltpu.VMEM((LANES, d), BF16)],
        compiler_params=_cparams(("arbitrary",)),
        name="in_proj_krope",
    )(u, wt, cos_t, sin_t)


def _rms_bf16(h_ref, g_ref):
    hf = h_ref[...].astype(F32)
    y = hf * lax.rsqrt(jnp.mean(hf * hf, axis=-1, keepdims=True) + RMS_EPS)
    return (y * g_ref[...]).astype(BF16)


def _q_up_kernel(h_ref, g_ref, w_ref, cos_ref, sin_ref, o_ref, *, n_heads, scale):
    yb = _rms_bf16(h_ref, g_ref)
    cos_t = cos_ref[...]
    sin_t = sin_ref[...]
    group = 4
    for h0 in range(0, n_heads, group):
        res = jnp.dot(yb, w_ref[:, h0 * QK_PAD:(h0 + group) * QK_PAD], preferred_element_type=F32)
        for hh in range(group):
            c0 = hh * QK_PAD
            nope = res[:, c0:c0 + QK_NOPE_DIM]
            rp = _rope_block(res[:, c0 + QK_NOPE_DIM:c0 + QK_PAD], cos_t, sin_t)
            o0 = (h0 + hh) * QK_PAD
            o_ref[:, o0:o0 + QK_NOPE_DIM] = (nope * scale).astype(BF16)
            o_ref[:, o0 + QK_NOPE_DIM:o0 + QK_PAD] = (rp * scale).astype(BF16)


def _q_up(hm, g3, l, w_uq_p, cos_t, sin_t, n_heads, tm=512):
    t = hm.shape[0]
    scale = float(QK_NOPE_DIM + QK_ROPE_DIM) ** -0.5 * LOG2E
    return pl.pallas_call(
        functools.partial(_q_up_kernel, n_heads=n_heads, scale=scale),
        grid=(t // tm,),
        in_specs=[
            pl.BlockSpec((tm, Q_LORA_RANK), lambda i: (i, 0)),
            pl.BlockSpec((None, 1, Q_LORA_RANK), lambda i: (l, 0, 0)),
            pl.BlockSpec((Q_LORA_RANK, n_heads * QK_PAD), lambda i: (0, 0)),
            pl.BlockSpec((tm, LANES), lambda i: (i, 0)),
            pl.BlockSpec((tm, LANES), lambda i: (i, 0)),
        ],
        out_specs=pl.BlockSpec((tm, n_heads * QK_PAD), lambda i: (i, 0)),
        out_shape=jax.ShapeDtypeStruct((t, n_heads * QK_PAD), BF16),
        compiler_params=_cparams(("arbitrary",)),
        name="mla_q_up",
    )(hm, g3, w_uq_p, cos_t, sin_t)


def _kv_up_kernel(h_ref, g_ref, wk_ref, wv_ref, kr_ref, k_ref, v_ref, *, n_heads):
    yb = _rms_bf16(h_ref, g_ref)
    kr = kr_ref[...]
    group = 4
    for h0 in range(0, n_heads, group):
        res = jnp.dot(yb, wk_ref[:, h0 * QK_NOPE_DIM:(h0 + group) * QK_NOPE_DIM],
                      preferred_element_type=F32)
        for hh in range(group):
            o0 = (h0 + hh) * QK_PAD
            k_ref[:, o0:o0 + QK_NOPE_DIM] = res[:, hh * QK_NOPE_DIM:(hh + 1) * QK_NOPE_DIM].astype(BF16)
            k_ref[:, o0 + QK_NOPE_DIM:o0 + QK_PAD] = kr
    v_ref[...] = jnp.dot(yb, wv_ref[...], preferred_element_type=F32).astype(BF16)


def _kv_up(hm, g3, l, w_uk, w_uv, krr, n_heads, tm=512):
    t = hm.shape[0]
    col_blk = Q_LORA_RANK // KV_LORA_RANK
    return pl.pallas_call(
        functools.partial(_kv_up_kernel, n_heads=n_heads),
        grid=(t // tm,),
        in_specs=[
            pl.BlockSpec((tm, KV_LORA_RANK), lambda i: (i, col_blk)),
            pl.BlockSpec((None, 1, KV_LORA_RANK), lambda i: (l, 0, 0)),
            pl.BlockSpec((KV_LORA_RANK, n_heads * QK_NOPE_DIM), lambda i: (0, 0)),
            pl.BlockSpec((KV_LORA_RANK, n_heads * V_HEAD_DIM), lambda i: (0, 0)),
            pl.BlockSpec((tm, LANES), lambda i: (i, 0)),
        ],
        out_specs=[
            pl.BlockSpec((tm, n_heads * QK_PAD), lambda i: (i, 0)),
            pl.BlockSpec((tm, n_heads * V_HEAD_DIM), lambda i: (i, 0)),
        ],
        out_shape=[
            jax.ShapeDtypeStruct((t, n_heads * QK_PAD), BF16),
            jax.ShapeDtypeStruct((t, n_heads * V_HEAD_DIM), BF16),
        ],
        compiler_params=_cparams(("arbitrary",)),
        name="mla_kv_up",
    )(hm, g3, w_uk, w_uv, krr)


MLA_HEADS_PER_STEP = 2


def _mla_attn_kernel(q_ref, k_ref, v_ref, o_ref, *, seq, tq):
    nq = seq // tq
    hp = MLA_HEADS_PER_STEP
    row = lax.broadcasted_iota(jnp.int32, (tq, tq), 0)
    col = lax.broadcasted_iota(jnp.int32, (tq, tq), 1)
    causal = col <= row

    def kv_step(qs, j, carry, masked):
        off = pl.multiple_of(j * tq, tq)
        out = []
        for hh in range(hp):
            m, l, acc = carry[hh]
            k = k_ref[pl.ds(off, tq), hh * QK_PAD:(hh + 1) * QK_PAD]
            v = v_ref[pl.ds(off, tq), hh * V_HEAD_DIM:(hh + 1) * V_HEAD_DIM]
            s = lax.dot_general(qs[hh], k, (((1,), (1,)), ((), ())), preferred_element_type=F32)
            if masked:
                s = jnp.where(causal, s, NEG_BIG)
            m_new = jnp.maximum(m, jnp.max(s, axis=-1, keepdims=True))
            p = jnp.exp2(s - m_new)
            alpha = jnp.exp2(m - m_new)
            l_new = alpha * l + jnp.sum(p, axis=-1, keepdims=True)
            acc_new = alpha * acc + jnp.dot(p.astype(BF16), v, preferred_element_type=F32)
            out.append((m_new, l_new, acc_new))
        return tuple(out)

    def q_loop(i, _):
        qoff = pl.multiple_of(i * tq, tq)
        qs = [q_ref[pl.ds(qoff, tq), hh * QK_PAD:(hh + 1) * QK_PAD] for hh in range(hp)]
        init = tuple((jnp.full((tq, 1), NEG_BIG, F32), jnp.zeros((tq, 1), F32),
                      jnp.zeros((tq, V_HEAD_DIM), F32)) for _ in range(hp))
        carry = lax.fori_loop(0, i, lambda j, c: kv_step(qs, j, c, False), init)
        carry = kv_step(qs, i, carry, True)
        for hh in range(hp):
            m, l, acc = carry[hh]
            o_ref[pl.ds(qoff, tq), hh * V_HEAD_DIM:(hh + 1) * V_HEAD_DIM] = (acc / l).astype(o_ref.dtype)
        return 0

    lax.fori_loop(0, nq, q_loop, 0)


def _mla_attn(q, k, v, batch, seq, n_heads, tq=1024):
    t = q.shape[0]
    hp = MLA_HEADS_PER_STEP
    return pl.pallas_call(
        functools.partial(_mla_attn_kernel, seq=seq, tq=tq),
        grid=(batch, n_heads // hp),
        in_specs=[
            pl.BlockSpec((seq, hp * QK_PAD), lambda b, h: (b, h)),
            pl.BlockSpec((seq, hp * QK_PAD), lambda b, h: (b, h)),
            pl.BlockSpec((seq, hp * V_HEAD_DIM), lambda b, h: (b, h)),
        ],
        out_specs=pl.BlockSpec((seq, hp * V_HEAD_DIM), lambda b, h: (b, h)),
        out_shape=jax.ShapeDtypeStruct((t, n_heads * V_HEAD_DIM), BF16),
        compiler_params=_cparams(("arbitrary", "arbitrary")),
        name="mla_attn",
    )(q, k, v)


DIL_GROUP = 16


def _dil_attn_kernel(qkv_ref, bvec_ref, o_ref, m_s, l_s, acc_s, bias_s, nat_f, *view_s, seq):
    scale = float(HEAD_DIM) ** -0.5 * LOG2E
    views = (qkv_ref,) + tuple(view_s)

    for bi in range(len(DIL_PATTERNS)):
        full = jnp.broadcast_to(bvec_ref[bi] * LOG2E, (BLOCK, 2 * BLOCK))
        bias_s[bi] = pltpu.roll(full, 0, 1, stride=1, stride_axis=0)

    for which in range(3):
        nat_f[...] = qkv_ref[which].astype(F32)
        for bi, (_, d) in enumerate(DIL_PATTERNS):
            if d == 1:
                continue
            for r in range(d):
                views[bi][which, :, r * HEAD_DIM:(r + 1) * HEAD_DIM] = (
                    nat_f[pl.ds(r, seq // d, stride=d), :].astype(BF16))

    def scores(ref, bi, d, r, i, first):
        lanes = slice(r * HEAD_DIM, (r + 1) * HEAD_DIM)
        qoff = pl.multiple_of(i * BLOCK, BLOCK)
        q = ref[0, pl.ds(qoff, BLOCK), lanes]
        if first:
            kk = ref[1, pl.ds(0, BLOCK), lanes]
            vv = ref[2, pl.ds(0, BLOCK), lanes]
            bias = bias_s[bi, :, BLOCK:2 * BLOCK]
        else:
            koff = pl.multiple_of(i * BLOCK - BLOCK, BLOCK)
            kk = ref[1, pl.ds(koff, 2 * BLOCK), lanes]
            vv = ref[2, pl.ds(koff, 2 * BLOCK), lanes]
            bias = bias_s[bi]
        s = lax.dot_general(q, kk, (((1,), (1,)), ((), ())), preferred_element_type=F32)
        s = s * scale + bias
        m_b = jnp.max(s, axis=-1, keepdims=True)
        p = jnp.exp2(s - m_b)
        l_b = jnp.sum(p, axis=-1, keepdims=True)
        a_b = jnp.dot(p.astype(BF16), vv, preferred_element_type=F32)
        m_b = jnp.broadcast_to(m_b, (BLOCK, HEAD_DIM))
        l_b = jnp.broadcast_to(l_b, (BLOCK, HEAD_DIM))
        if d == 1:
            rows = pl.ds(qoff, BLOCK)
        else:
            rows = pl.ds(i * (BLOCK * d) + r, BLOCK, stride=d)
        return m_b, l_b, a_b, rows

    def group(ref, bi, d, blocks):
        parts = [scores(ref, bi, d, r, i, first) for r, i, first in blocks]
        if d == 1:
            for m_b, l_b, a_b, rows in parts:
                m_s[rows, :] = m_b
                l_s[rows, :] = l_b
                acc_s[rows, :] = a_b
            return
        old = [(m_s[rows, :], l_s[rows, :], acc_s[rows, :]) for _, _, _, rows in parts]
        new = []
        for (m_b, l_b, a_b, rows), (m_o, l_o, a_o) in zip(parts, old):
            m_n = jnp.maximum(m_o, m_b)
            e_o = jnp.exp2(m_o - m_n)
            e_b = jnp.exp2(m_b - m_n)
            new.append((m_n, e_o * l_o + e_b * l_b, e_o * a_o + e_b * a_b, rows))
        for m_n, l_n, a_n, rows in new:
            m_s[rows, :] = m_n
            l_s[rows, :] = l_n
            acc_s[rows, :] = a_n

    for bi, (_, d) in enumerate(DIL_PATTERNS):
        ref = views[bi]
        nblk = seq // d // BLOCK
        n_res = min(d, DIL_GROUP)
        n_seq = DIL_GROUP // n_res
        assert d % n_res == 0 and nblk % n_seq == 0
        for r0 in range(0, d, n_res):
            group(ref, bi, d, [(r0 + u, i, i == 0) for i in range(n_seq) for u in range(n_res)])

            def body(t, _, ref=ref, bi=bi, d=d, r0=r0, n_res=n_res, n_seq=n_seq):
                group(ref, bi, d, [(r0 + u, t * n_seq + ii, False)
                                   for ii in range(n_seq) for u in range(n_res)])
                return 0

            lax.fori_loop(1, nblk // n_seq, body, 0)

    o_ref[...] = (acc_s[...] / l_s[...]).astype(o_ref.dtype)


def _dil_attn(qkv, bias_vec, batch, seq, n_heads):
    t = batch * seq
    nbr = len(DIL_PATTERNS)
    assert DIL_PATTERNS[0][1] == 1
    view_scratch = [pltpu.VMEM((3, seq // d, d * HEAD_DIM), BF16) for _, d in DIL_PATTERNS[1:]]
    return pl.pallas_call(
        functools.partial(_dil_attn_kernel, seq=seq),
        grid=(batch, n_heads),
        in_specs=[
            pl.BlockSpec((3, None, None, seq, HEAD_DIM), lambda b, h: (0, b, h, 0, 0)),
            pl.BlockSpec((nbr, None, 1, 2 * BLOCK), lambda b, h: (0, h, 0, 0)),
        ],
        out_specs=pl.BlockSpec((seq, HEAD_DIM), lambda b, h: (b, h)),
        out_shape=jax.ShapeDtypeStruct((t, n_heads * HEAD_DIM), BF16),
        scratch_shapes=[pltpu.VMEM((seq, HEAD_DIM), F32)] * 3
        + [pltpu.VMEM((nbr, BLOCK, 2 * BLOCK), F32), pltpu.VMEM((seq, HEAD_DIM), F32)]
        + view_scratch,
        compiler_params=_cparams(("arbitrary", "arbitrary")),
        name="dil_attn",
    )(qkv, bias_vec)


def _out_proj_kernel(a1_ref, a2_ref, w_ref, o_ref, wb_ref):
    @pl.when(pl.program_id(1) == 0)
    def _():
        wb_ref[...] = w_ref[...].astype(BF16)

    k1 = a1_ref.shape[1]
    acc = jnp.dot(a1_ref[...], wb_ref[:k1, :], preferred_element_type=F32)
    acc = acc + jnp.dot(a2_ref[...], wb_ref[k1:, :], preferred_element_type=F32)
    o_ref[...] = acc.astype(o_ref.dtype)


def _out_proj(o_a, o_b, w_o, l, tm=1024, tn=512):
    t, k1 = o_a.shape
    k2 = o_b.shape[1]
    n = w_o.shape[-1]
    return pl.pallas_call(
        _out_proj_kernel,
        grid=(n // tn, t // tm),
        in_specs=[
            pl.BlockSpec((tm, k1), lambda j, i: (i, 0)),
            pl.BlockSpec((tm, k2), lambda j, i: (i, 0)),
            pl.BlockSpec((None, k1 + k2, tn), lambda j, i: (l, 0, j)),
        ],
        out_specs=pl.BlockSpec((tm, tn), lambda j, i: (i, j)),
        out_shape=jax.ShapeDtypeStruct((t, n), BF16),
        scratch_shapes=[pltpu.VMEM((k1 + k2, tn), BF16)],
        compiler_params=_cparams(("arbitrary", "arbitrary")),
        name="out_proj",
    )(o_a, o_b, w_o)


def _layer_norm_rows(z, g, b):
    mu = jnp.mean(z, axis=-1, keepdims=True)
    zc = z - mu
    var = jnp.mean(zc * zc, axis=-1, keepdims=True)
    return zc * lax.rsqrt(var + LN_EPS) * g + b


SLAB_CHUNK = 1024
SLAB_Q = SLAB_CHUNK // (2 * LANES)


def _slab_sub(width):
    return width // (2 * LANES)


def _slab_index(j, first_tok, n_tok, sub):
    return (pl.ds(first_tok * sub + j, n_tok, stride=sub), slice(None))


def _pack_chunk(vals):
    bits = lax.bitcast_convert_type(vals.astype(BF16).astype(F32), jnp.uint32)
    half = SLAB_CHUNK // 2
    return [(bits[:, q * LANES:(q + 1) * LANES] >> 16)
            | bits[:, half + q * LANES:half + (q + 1) * LANES] for q in range(SLAB_Q)]


def _unpack_words(words):
    lo = lax.bitcast_convert_type(words << 16, F32)
    hi = lax.bitcast_convert_type(words & jnp.uint32(0xFFFF0000), F32)
    return lo, hi


def _ln1_router_kernel(x_ref, mix_ref, mod_ref, g_ref, b_ref, wr_ref, br_ref,
                       x1_ref, u2p_ref, ids_ref, gates_ref, *, alpha):
    gt1 = mod_ref[2:3, :]
    sh2 = mod_ref[3:4, :]
    sc2 = mod_ref[4:5, :]
    z = alpha * x_ref[...] + gt1 * mix_ref[...].astype(F32)
    x1 = _layer_norm_rows(z, g_ref[...], b_ref[...])
    x1_ref[...] = x1
    u2 = (x1 * (1.0 + sc2) + sh2).astype(BF16)
    tm, d = u2.shape
    for g in range(d // SLAB_CHUNK):
        for q, words in enumerate(_pack_chunk(u2[:, g * SLAB_CHUNK:(g + 1) * SLAB_CHUNK])):
            u2p_ref[_slab_index(g * SLAB_Q + q, 0, tm, _slab_sub(d))] = words
    logits = jnp.dot(u2, wr_ref[...].astype(BF16), preferred_element_type=F32) + br_ref[...]
    lane = lax.broadcasted_iota(jnp.int32, logits.shape, 1)
    lane_f = lane.astype(F32)
    vals = []
    ids = []
    for _ in range(TOP_K):
        mk = jnp.max(logits, axis=-1, keepdims=True)
        idx_f = jnp.min(jnp.where(logits == mk, lane_f, float(LANES)), axis=-1, keepdims=True)
        vals.append(mk)
        ids.append(idx_f.astype(jnp.int32))
        logits = jnp.where(lane_f == idx_f, -jnp.inf, logits)
    exps = [jnp.exp(v - vals[0]) for v in vals]
    denom = exps[0]
    for e in exps[1:]:
        denom = denom + e
    ids_out = jnp.zeros(lane.shape, jnp.int32)
    gates_out = jnp.zeros(lane.shape, F32)
    for k in range(TOP_K):
        ids_out = jnp.where(lane == k, ids[k], ids_out)
        gates_out = jnp.where(lane == k, exps[k] / denom, gates_out)
    ids_ref[...] = ids_out
    gates_ref[...] = gates_out


def _ln1_router(x2, mix, mod3, g3, b3, wr_p, br_p, l, seq, alpha, tm=256):
    t, d = x2.shape
    per_b = seq // tm
    return pl.pallas_call(
        functools.partial(_ln1_router_kernel, alpha=alpha),
        grid=(t // tm,),
        in_specs=[
            pl.BlockSpec((tm, d), lambda i: (i, 0)),
            pl.BlockSpec((tm, d), lambda i: (i, 0)),
            pl.BlockSpec((None, 6, d), lambda i: (i // per_b, 0, 0)),
            pl.BlockSpec((None, 1, d), lambda i: (l, 0, 0)),
            pl.BlockSpec((None, 1, d), lambda i: (l, 0, 0)),
            pl.BlockSpec((d, LANES), lambda i: (0, 0)),
            pl.BlockSpec((1, LANES), lambda i: (0, 0)),
        ],
        out_specs=[
            pl.BlockSpec((tm, d), lambda i: (i, 0)),
            pl.BlockSpec((tm * _slab_sub(d), LANES), lambda i: (i, 0)),
            pl.BlockSpec((tm, LANES), lambda i: (i, 0)),
            pl.BlockSpec((tm, LANES), lambda i: (i, 0)),
        ],
        out_shape=[
            jax.ShapeDtypeStruct((t, d), F32),
            jax.ShapeDtypeStruct((t * _slab_sub(d), LANES), jnp.uint32),
            jax.ShapeDtypeStruct((t, LANES), jnp.int32),
            jax.ShapeDtypeStruct((t, LANES), F32),
        ],
        compiler_params=_cparams(("arbitrary",)),
        name="ln1_router",
    )(x2, mix, mod3, g3, b3, wr_p, br_p)


def _moe_row_tiles(n_full, n_tail, tile_fn):
    def full(j, _):
        tile_fn(pl.multiple_of(j * MOE_SUB, MOE_SUB), MOE_SUB)
        return 0

    def tail(k, _):
        tile_fn(pl.multiple_of(n_full * MOE_SUB + k * MOE_TAIL, MOE_TAIL), MOE_TAIL)
        return 0

    tile_fn(0, MOE_SUB)
    lax.fori_loop(1, n_full, full, 0)
    lax.fori_loop(0, n_tail, tail, 0)


def _moe_up_kernel(item_e, item_blk, item_full, item_tail, x_ref, wg_ref, wu_ref, bg_ref, bu_ref,
                   h_ref, wg_s, wu_s):
    w = pl.program_id(0)
    n_full = item_full[w]
    n_tail = item_tail[w]

    @pl.when(n_full + n_tail > 0)
    def _():
        wg_s[...] = wg_ref[...].astype(BF16)
        wu_s[...] = wu_ref[...].astype(BF16)
        bg = bg_ref[...]
        bu = bu_ref[...]

        def tile(off, rows):
            xt = x_ref[pl.ds(off, rows), :]
            glu = jnp.dot(xt, wg_s[...], preferred_element_type=F32) + bg
            lin = jnp.dot(xt, wu_s[...], preferred_element_type=F32) + bu
            glu = jnp.minimum(glu, SWIGLU_LIMIT)
            lin = jnp.clip(lin, -SWIGLU_LIMIT, SWIGLU_LIMIT)
            act = glu * jax.nn.sigmoid(SWIGLU_ALPHA * glu) * (lin + 1.0)
            h_ref[pl.ds(off, rows), :] = act.astype(h_ref.dtype)

        _moe_row_tiles(n_full, n_tail, tile)


def _moe_up(xs, w_gate, w_up, b_gate4, b_up4, item_e, item_blk, item_full, item_tail, l, tf=256):
    r, d = xs.shape
    n_items = r // MOE_ROWS
    ff = w_gate.shape[-1]
    n_chunks = ff // tf

    def cmap(w, c, ifull, itail):
        return jnp.where(ifull[w] + itail[w] > 0, c, n_chunks - 1)

    def wmap(w, c, ie, ib, ifull, itail):
        return (l, ie[w], 0, cmap(w, c, ifull, itail))

    grid_spec = pltpu.PrefetchScalarGridSpec(
        num_scalar_prefetch=4,
        grid=(n_items, n_chunks),
        in_specs=[
            pl.BlockSpec((MOE_ROWS, d), lambda w, c, ie, ib, ifull, itail: (ib[w], 0)),
            pl.BlockSpec((None, None, d, tf), wmap),
            pl.BlockSpec((None, None, d, tf), wmap),
            pl.BlockSpec((None, None, 1, tf), wmap),
            pl.BlockSpec((None, None, 1, tf), wmap),
        ],
        out_specs=pl.BlockSpec(
            (MOE_ROWS, tf),
            lambda w, c, ie, ib, ifull, itail: (ib[w], cmap(w, c, ifull, itail))),
        scratch_shapes=[pltpu.VMEM((d, tf), BF16), pltpu.VMEM((d, tf), BF16)],
    )
    return pl.pallas_call(
        _moe_up_kernel,
        grid_spec=grid_spec,
        out_shape=jax.ShapeDtypeStruct((r, ff), BF16),
        compiler_params=_cparams(("arbitrary", "arbitrary")),
        name="moe_up",
    )(item_e, item_blk, item_full, item_tail, xs, w_gate, w_up, b_gate4, b_up4)


def _moe_down_kernel(item_e, item_blk, item_full, item_tail, h_ref, wd_ref, bd_ref, y_ref, wd_s,
                     *, sub):
    w = pl.program_id(0)
    c = pl.program_id(1)
    n_full = item_full[w]
    n_tail = item_tail[w]

    @pl.when(n_full + n_tail > 0)
    def _():
        wd_s[...] = wd_ref[...].astype(BF16)
        bd = bd_ref[...]

        def tile(off, rows):
            ht = h_ref[pl.ds(off, rows), :]
            y = jnp.dot(ht, wd_s[...], preferred_element_type=F32) + bd
            for q, words in enumerate(_pack_chunk(y)):
                y_ref[_slab_index(c * SLAB_Q + q, off, rows, sub)] = words

        _moe_row_tiles(n_full, n_tail, tile)


def _moe_down(hs, w_down, b_down4, item_e, item_blk, item_full, item_tail, l):
    r, ff = hs.shape
    n_items = r // MOE_ROWS
    d = w_down.shape[-1]
    tn = SLAB_CHUNK
    n_chunks = d // tn
    sub = _slab_sub(d)

    def cmap(w, c, ifull, itail):
        return jnp.where(ifull[w] + itail[w] > 0, c, n_chunks - 1)

    def wmap(w, c, ie, ib, ifull, itail):
        return (l, ie[w], 0, cmap(w, c, ifull, itail))

    grid_spec = pltpu.PrefetchScalarGridSpec(
        num_scalar_prefetch=4,
        grid=(n_items, n_chunks),
        in_specs=[
            pl.BlockSpec((MOE_ROWS, ff), lambda w, c, ie, ib, ifull, itail: (ib[w], 0)),
            pl.BlockSpec((None, None, ff, tn), wmap),
            pl.BlockSpec((None, None, 1, tn), wmap),
        ],
        out_specs=pl.BlockSpec((MOE_ROWS * sub, LANES),
                               lambda w, c, ie, ib, ifull, itail: (ib[w], 0)),
        scratch_shapes=[pltpu.VMEM((ff, tn), BF16)],
    )
    return pl.pallas_call(
        functools.partial(_moe_down_kernel, sub=sub),
        grid_spec=grid_spec,
        out_shape=jax.ShapeDtypeStruct((r * sub, LANES), jnp.uint32),
        compiler_params=_cparams(("arbitrary", "arbitrary")),
        name="moe_down",
    )(item_e, item_blk, item_full, item_tail, hs, w_down, b_down4)


GATHER_UNROLL = 8


def _buffer_pitch(sub):
    return sub + 8 if sub % 16 == 0 else sub


def _slab_gather(idx_ref, src_ref, buf, sem, base_slab, n, sub, wait):
    pitch = _buffer_pitch(sub)

    def copy(r):
        src_row = pl.multiple_of(idx_ref[0, r] * sub, sub)
        dst_row = pl.multiple_of((base_slab + r) * pitch, 8)
        return pltpu.make_async_copy(src_ref.at[pl.ds(src_row, sub), :],
                                     buf.at[pl.ds(dst_row, sub), :], sem)

    def body(r0, _):
        for u in range(GATHER_UNROLL):
            c = copy(r0 * GATHER_UNROLL + u)
            if wait:
                c.wait()
            else:
                c.start(priority=u % 2)
        return 0

    lax.fori_loop(0, n // GATHER_UNROLL, body, 0)


DISPATCH_SLOTS = 3


def _dispatch_kernel(vblk, nvalid, tok_ref, tok1_ref, tok2_ref, src_ref, o_ref, buf, sems,
                     *, rows, d):
    s = pl.program_id(0)
    slot = s % DISPATCH_SLOTS
    sub = _slab_sub(d)

    @pl.when(s == 0)
    def _():
        _slab_gather(tok_ref, src_ref, buf, sems.at[0], 0, rows, sub, wait=False)

        @pl.when(1 < nvalid[0])
        def _():
            _slab_gather(tok1_ref, src_ref, buf, sems.at[1], rows, rows, sub, wait=False)

    @pl.when(s + 2 < nvalid[0])
    def _():
        nxt = (s + 2) % DISPATCH_SLOTS
        _slab_gather(tok2_ref, src_ref, buf, sems.at[nxt], nxt * rows, rows, sub, wait=False)

    @pl.when((s < nvalid[0]) | (s == 0))
    def _():
        _slab_gather(tok_ref, src_ref, buf, sems.at[slot], slot * rows, rows, sub, wait=True)
        half = SLAB_CHUNK // 2
        for g in range(d // SLAB_CHUNK):
            for q in range(SLAB_Q):
                idx = _slab_index(g * SLAB_Q + q, slot * rows, rows, _buffer_pitch(sub))
                lo, hi = _unpack_words(buf[idx])
                c0 = g * SLAB_CHUNK + q * LANES
                o_ref[:, c0:c0 + LANES] = lo.astype(BF16)
                o_ref[:, c0 + half:c0 + half + LANES] = hi.astype(BF16)


def _dispatch(u2p, d, row_tok, vblk, nvalid, rows=MOE_BLK):
    r = row_tok.shape[0]
    n_blk = r // rows
    n_steps = vblk.shape[0]
    tok3 = row_tok.reshape(n_blk, 1, rows)
    grid_spec = pltpu.PrefetchScalarGridSpec(
        num_scalar_prefetch=2,
        grid=(n_steps,),
        in_specs=[
            pl.BlockSpec((None, 1, rows), lambda s, vb, nv: (vb[s], 0, 0), memory_space=pltpu.SMEM),
            pl.BlockSpec((None, 1, rows), lambda s, vb, nv: (vb[jnp.minimum(s + 1, n_steps - 1)], 0, 0),
                         memory_space=pltpu.SMEM),
            pl.BlockSpec((None, 1, rows), lambda s, vb, nv: (vb[jnp.minimum(s + 2, n_steps - 1)], 0, 0),
                         memory_space=pltpu.SMEM),
            pl.BlockSpec(memory_space=pl.ANY),
        ],
        out_specs=pl.BlockSpec((rows, d), lambda s, vb, nv: (vb[s], 0)),
        scratch_shapes=[
            pltpu.VMEM((DISPATCH_SLOTS * rows * _buffer_pitch(_slab_sub(d)), LANES), jnp.uint32),
            pltpu.SemaphoreType.DMA((DISPATCH_SLOTS,))],
    )
    return pl.pallas_call(
        functools.partial(_dispatch_kernel, rows=rows, d=d),
        grid_spec=grid_spec,
        out_shape=jax.ShapeDtypeStruct((r, d), BF16),
        compiler_params=_cparams(("arbitrary",)),
        name="moe_dispatch",
    )(vblk, nvalid, tok3, tok3, tok3, u2p)


def _combine_ln2_kernel(slot_ref, slot_next_ref, x1_ref, ys_ref, gates_ref, mod_ref, g_ref, b_ref,
                        o_ref, buf, ffn_s, sems, *, alpha, tm):
    i = pl.program_id(0)
    n = pl.num_programs(0)
    cur = i % 2
    n_rows = TOP_K * tm
    d = ffn_s.shape[1]
    sub = _slab_sub(d)

    @pl.when(i == 0)
    def _():
        _slab_gather(slot_ref, ys_ref, buf, sems.at[0], 0, n_rows, sub, wait=False)

    @pl.when(i + 1 < n)
    def _():
        _slab_gather(slot_next_ref, ys_ref, buf, sems.at[1 - cur], (1 - cur) * n_rows, n_rows, sub,
                     wait=False)

    _slab_gather(slot_ref, ys_ref, buf, sems.at[cur], cur * n_rows, n_rows, sub, wait=True)

    gates = gates_ref[...]
    gk = [jnp.broadcast_to(gates[:, k:k + 1], (tm, LANES)) for k in range(TOP_K)]
    half = SLAB_CHUNK // 2
    for g in range(d // SLAB_CHUNK):
        for q in range(SLAB_Q):
            lo_acc = None
            hi_acc = None
            for k in range(TOP_K):
                idx = _slab_index(g * SLAB_Q + q, cur * n_rows + k * tm, tm, _buffer_pitch(sub))
                lo, hi = _unpack_words(buf[idx])
                lo_acc = gk[k] * lo if lo_acc is None else lo_acc + gk[k] * lo
                hi_acc = gk[k] * hi if hi_acc is None else hi_acc + gk[k] * hi
            c0 = g * SLAB_CHUNK + q * LANES
            ffn_s[:, c0:c0 + LANES] = lo_acc
            ffn_s[:, c0 + half:c0 + half + LANES] = hi_acc
    gt2 = mod_ref[5:6, :]
    z = alpha * x1_ref[...] + gt2 * ffn_s[...]
    o_ref[...] = _layer_norm_rows(z, g_ref[...], b_ref[...])


def _combine_ln2(x1, ys, slot, gates, mod3, g3, b3, l, seq, alpha, tm=256):
    t, d = x1.shape
    per_b = seq // tm
    n_tiles = t // tm
    slot3 = slot.reshape(n_tiles, tm, TOP_K).transpose(0, 2, 1).reshape(n_tiles, 1, TOP_K * tm)
    return pl.pallas_call(
        functools.partial(_combine_ln2_kernel, alpha=alpha, tm=tm),
        grid=(n_tiles,),
        in_specs=[
            pl.BlockSpec((None, 1, tm * TOP_K), lambda i: (i, 0, 0), memory_space=pltpu.SMEM),
            pl.BlockSpec((None, 1, tm * TOP_K), lambda i: (jnp.minimum(i + 1, n_tiles - 1), 0, 0),
                         memory_space=pltpu.SMEM),
            pl.BlockSpec((tm, d), lambda i: (i, 0)),
            pl.BlockSpec(memory_space=pl.ANY),
            pl.BlockSpec((tm, LANES), lambda i: (i, 0)),
            pl.BlockSpec((None, 6, d), lambda i: (i // per_b, 0, 0)),
            pl.BlockSpec((None, 1, d), lambda i: (l, 0, 0)),
            pl.BlockSpec((None, 1, d), lambda i: (l, 0, 0)),
        ],
        out_specs=pl.BlockSpec((tm, d), lambda i: (i, 0)),
        out_shape=jax.ShapeDtypeStruct((t, d), F32),
        scratch_shapes=[pltpu.VMEM((2 * TOP_K * tm * _buffer_pitch(_slab_sub(d)), LANES), jnp.uint32),
                        pltpu.VMEM((tm, d), F32),
                        pltpu.SemaphoreType.DMA((2,))],
        compiler_params=_cparams(("arbitrary",)),
        name="combine_ln2",
    )(slot3, slot3, x1, ys, gates, mod3, g3, b3)


def _t5_bucket(dist):
    is_small = dist < REL_MAX_EXACT
    nf = jnp.maximum(dist, REL_MAX_EXACT).astype(F32)
    large = REL_MAX_EXACT + (jnp.log(nf / REL_MAX_EXACT) / math.log(REL_MAX_DISTANCE / REL_MAX_EXACT)
                             * (REL_BUCKETS - REL_MAX_EXACT)).astype(jnp.int32)
    large = jnp.minimum(large, REL_BUCKETS - 1)
    return jnp.where(is_small, dist, large)


def _dilated_bias_vectors(rel_bias):
    m = jnp.arange(2 * BLOCK)
    rel = BLOCK - m
    vecs = []
    for window, d in DIL_PATTERNS:
        valid = (rel >= 0) & (rel <= window // d)
        bucket = _t5_bucket(jnp.clip(rel, 0) * d)
        bias = jnp.transpose(rel_bias[bucket].astype(F32), (1, 0))
        vecs.append(jnp.where(valid[None, :], bias, NEG_BIG))
    return jnp.stack(vecs, axis=0)[:, :, None, :]


def _routing_tables(ids4, n_items_max):
    e = ids4.reshape(-1)
    n_rows_real = e.shape[0]
    onehot =(e[:, None] == jnp.arange(N_EXPERTS, dtype=jnp.int32)[None, :]).astype(jnp.int32)
    csum = jnp.cumsum(onehot, axis=0)
    rank = jnp.sum(onehot * csum, axis=1) - 1
    counts = csum[-1]
    nit = (counts + MOE_ROWS - 1) // MOE_ROWS
    cum_items = jnp.cumsum(nit)
    first_item = cum_items - nit
    slot = first_item[e] * MOE_ROWS + rank
    total = cum_items[-1]
    w = jnp.arange(n_items_max, dtype=jnp.int32)
    e_w = jnp.minimum(jnp.searchsorted(cum_items, w, side="right", method="compare_all"),
                      N_EXPERTS - 1).astype(jnp.int32)
    j_w = w - first_item[e_w]
    rows_w = jnp.clip(counts[e_w] - j_w * MOE_ROWS, 0, MOE_ROWS)
    valid = w < total
    rows_w = jnp.where(valid, rows_w, 0)
    tails_per_full = MOE_SUB // MOE_TAIL
    n_full = rows_w // MOE_SUB
    n_tail = (rows_w - n_full * MOE_SUB + MOE_TAIL - 1) // MOE_TAIL
    item_full = jnp.where(n_tail == tails_per_full, n_full + 1, n_full).astype(jnp.int32)
    item_tail = jnp.where(n_tail == tails_per_full, 0, n_tail).astype(jnp.int32)
    last = total - 1
    item_e = jnp.where(valid, e_w, e_w[last]).astype(jnp.int32)
    item_blk = jnp.where(valid, w, last).astype(jnp.int32)
    per_item = MOE_ROWS // MOE_BLK
    blk = jnp.arange(n_items_max * per_item, dtype=jnp.int32)
    blk_valid = ((blk % per_item) * MOE_BLK < rows_w[blk // per_item]).astype(jnp.int32)
    cum_valid = jnp.cumsum(blk_valid)
    nvalid = cum_valid[-1]
    n_steps = n_rows_real // MOE_BLK + n_items_max
    want = jnp.minimum(jnp.arange(n_steps, dtype=jnp.int32) + 1, nvalid)
    vblk = jnp.searchsorted(cum_valid, want, side="left", method="compare_all").astype(jnp.int32)
    return (slot.astype(jnp.int32), item_e, item_blk, item_full, item_tail, vblk,
            nvalid.astype(jnp.int32).reshape(1))


def kernel(x, c, positions, w_ada, b_ada, w_in, g_q, g_kv, w_uq, w_ukv, rel_bias, w_o,
           ln1_g, ln1_b, w_router, b_router, w_gate, b_gate, w_up, b_up, w_down, b_down,
           ln2_g, ln2_b):
    batch, seq, d = x.shape
    depth = w_ada.shape[0]
    t = batch * seq
    n_heads = d // (2 * HEAD_DIM)
    dil_w = n_heads * HEAD_DIM
    alpha = (2.0 * depth) ** 0.25
    assert seq % (BLOCK * max(dd for _, dd in DIL_PATTERNS)) == 0
    assert all(win // dd == BLOCK for win, dd in DIL_PATTERNS)

    inv_freq = 1.0 / (ROPE_THETA ** (jnp.arange(0, QK_ROPE_DIM, 2, dtype=F32) / QK_ROPE_DIM))
    ang = positions.astype(F32)[..., None] * inv_freq
    cos, sin = jnp.cos(ang).reshape(t, -1), jnp.sin(ang).reshape(t, -1)
    zpad = jnp.zeros((t, LANES - QK_ROPE_DIM), F32)
    cos_t = jnp.concatenate([cos, cos, zpad], axis=1)
    sin_t = jnp.concatenate([-sin, sin, zpad], axis=1)

    bias_vec = _dilated_bias_vectors(rel_bias)
    c_pad = jnp.zeros((BF16_SUBLANES, d), F32).at[:batch].set(c)

    x2 = x.reshape(t, d)
    n_items_max = (t * TOP_K) // MOE_ROWS + N_EXPERTS
    mla_cols = Q_LORA_RANK + KV_LORA_RANK

    for l in range(depth):
        mod = _ada(c_pad, w_ada, b_ada.reshape(depth, 1, -1), l)[:batch]
        mod3 = mod.reshape(batch, 6, d)

        w_in_t = jnp.swapaxes(w_in[l], 0, 1)
        wq = w_uq[l].reshape(Q_LORA_RANK, n_heads, QK_NOPE_DIM + QK_ROPE_DIM)
        w_uq_p = jnp.pad(wq, ((0, 0), (0, 0), (0, QK_PAD - QK_NOPE_DIM - QK_ROPE_DIM))
                         ).reshape(Q_LORA_RANK, n_heads * QK_PAD).astype(BF16)
        wkv = w_ukv[l].reshape(KV_LORA_RANK, n_heads, QK_NOPE_DIM + V_HEAD_DIM)
        w_uk = wkv[:, :, :QK_NOPE_DIM].reshape(KV_LORA_RANK, -1).astype(BF16)
        w_uv = wkv[:, :, QK_NOPE_DIM:].reshape(KV_LORA_RANK, -1).astype(BF16)

        u1 = _modulate(x2, mod3, seq, 0, 1)
        hm = _in_proj_mla(u1, w_in_t, mla_cols)
        qkv_d = _in_proj_dil(u1, w_in_t, mla_cols + QK_ROPE_DIM, batch, seq, n_heads)
        krr = _in_proj_krope(u1, w_in_t, mla_cols, cos_t, sin_t)
        q_a = _q_up(hm, g_q.reshape(depth, 1, -1), l, w_uq_p, cos_t, sin_t, n_heads)
        k_a, v_a = _kv_up(hm, g_kv.reshape(depth, 1, -1), l, w_uk, w_uv, krr, n_heads)
        o_a = _mla_attn(q_a, k_a, v_a, batch, seq, n_heads)
        o_b = _dil_attn(qkv_d, bias_vec, batch, seq, n_heads)
        mix = _out_proj(o_a, o_b, w_o, l)

        wr_p = jnp.pad(w_router[l], ((0, 0), (0, LANES - N_EXPERTS)))
        br_p = jnp.concatenate([b_router[l], jnp.full((LANES - N_EXPERTS,), NEG_BIG, F32)])[None, :]
        x1, u2p, ids, gates = _ln1_router(
            x2, mix, mod3, ln1_g.reshape(depth, 1, -1), ln1_b.reshape(depth, 1, -1),
            wr_p, br_p, l, seq, alpha)

        slot, item_e, item_blk, item_full, item_tail, vblk, nvalid = _routing_tables(
            ids[:, :TOP_K], n_items_max)
        tok = jnp.arange(t * TOP_K, dtype=jnp.int32) // TOP_K
        row_tok = jnp.zeros((n_items_max * MOE_ROWS,), jnp.int32).at[slot].set(tok)
        xs = _dispatch(u2p, d, row_tok, vblk, nvalid)
        hs = _moe_up(xs, w_gate, w_up, b_gate.reshape(depth, N_EXPERTS, 1, -1),
                     b_up.reshape(depth, N_EXPERTS, 1, -1), item_e, item_blk, item_full,
                     item_tail, l)
        ys = _moe_down(hs, w_down, b_down.reshape(depth, N_EXPERTS, 1, -1),
                       item_e, item_blk, item_full, item_tail, l)
        x2 = _combine_ln2(x1, ys, slot, gates, mod3, ln2_g.reshape(depth, 1, -1),
                          ln2_b.reshape(depth, 1, -1), l, seq, alpha)

    return x2.reshape(batch, seq, d)
```
